```python
import math
import jax, jax.numpy as jnp
from jax import lax
import numpy as np

D_MODEL = 1024
BATCH = 2
SEQ = 8192
DEPTH = 1

MEM_LEN = 256
MLA_NOPE = 128
MLA_ROPE = 64
MLA_V = 128
MLA_HEADS = D_MODEL // MLA_V
MLA_KV_RANK = 256
GLA_HEADS = 4
GLA_DK = D_MODEL // 2 // GLA_HEADS
GLA_DV = D_MODEL // GLA_HEADS
GLA_GATE_RANK = 16
GLA_GATE_NORMALIZER = 16.0
GLA_CHUNK = 64
MEM_HEADS = 4
MEM_DQK = 64
MEM_DV = D_MODEL // MEM_HEADS
N_BRANCHES = 3
D_FF = 4 * D_MODEL
ATTN_BLOCK = 128
ROPE_THETA = 10000.0
EPS = 1e-6

IN_SPLITS = (
    MLA_HEADS * (MLA_NOPE + MLA_ROPE),
    MLA_KV_RANK,
    MLA_ROPE,
    GLA_HEADS * GLA_DK,
    GLA_HEADS * GLA_DK,
    GLA_HEADS * GLA_DV,
    GLA_GATE_RANK,
    GLA_HEADS * GLA_DV,
    MEM_HEADS * MEM_DQK,
    N_BRANCHES * D_MODEL,
)
D_IN = sum(IN_SPLITS)

kernel_name = 'hybrid_mla_gla_mem_block'


def rms_norm(t, g):
    tf = t.astype(jnp.float32)
    y = tf * lax.rsqrt(jnp.mean(tf * tf, axis=-1, keepdims=True) + EPS)
    return (y * g.astype(jnp.float32)).astype(t.dtype)


def rope_tables(positions):
    inv = 1.0 / (ROPE_THETA ** (jnp.arange(0, MLA_ROPE, 2, dtype=jnp.float32) / MLA_ROPE))
    ang = positions.astype(jnp.float32)[..., None] * inv
    return jnp.cos(ang), jnp.sin(ang)


def apply_rope(t, cos, sin):
    tf = t.astype(jnp.float32)
    t1, t2 = jnp.split(tf, 2, axis=-1)
    c = cos[:, :, None, :]
    s = sin[:, :, None, :]
    return jnp.concatenate([t1 * c - t2 * s, t1 * s + t2 * c], axis=-1).astype(t.dtype)


def causal_block_attention(q, k, v, scale):
    B, S, H, D = q.shape
    Dv = v.shape[-1]
    nb = S // ATTN_BLOCK
    qb = q.reshape(B, nb, ATTN_BLOCK, H, D).transpose(1, 0, 3, 2, 4)
    kh = k.transpose(0, 2, 1, 3)
    vh = v.transpose(0, 2, 1, 3)
    kpos = jnp.arange(S)

    def one_block(args):
        qi, i = args
        s = jnp.einsum('bhqd,bhkd->bhqk', qi, kh).astype(jnp.float32) * scale
        qpos = i * ATTN_BLOCK + jnp.arange(ATTN_BLOCK)
        s = jnp.where(kpos[None, :] <= qpos[:, None], s, -1e30)
        p = jax.nn.softmax(s, axis=-1).astype(vh.dtype)
        return jnp.einsum('bhqk,bhkd->bhqd', p, vh)

    o = lax.map(one_block, (qb, jnp.arange(nb)))
    return o.transpose(1, 0, 3, 2, 4).reshape(B, S, H, Dv)


def gla_chunked(q, k, v, log_a):
    B, S, H, DK = q.shape
    DV = v.shape[-1]
    C = GLA_CHUNK
    nc = S // C

    def to_chunks(t):
        return t.astype(jnp.float32).reshape(B, nc, C, H, t.shape[-1]).transpose(1, 0, 3, 2, 4)

    qc, kc, vc = to_chunks(q), to_chunks(k), to_chunks(v)
    bc = jnp.cumsum(to_chunks(log_a), axis=3)
    mask = jnp.tril(jnp.ones((C, C), dtype=bool))[None, None, :, :, None]

    def step(state, inp):
        qi, ki, vi, bi = inp
        diff = bi[:, :, :, None, :] - bi[:, :, None, :, :]
        decay = jnp.where(mask, jnp.exp(jnp.where(mask, diff, 0.0)), 0.0)
        a = jnp.einsum('bhid,bhjd,bhijd->bhij', qi, ki, decay)
        o_intra = jnp.einsum('bhij,bhje->bhie', a, vi)
        o_inter = jnp.einsum('bhid,bhde->bhie', qi * jnp.exp(bi), state)
        b_last = bi[:, :, -1, :]
        k_dec = ki * jnp.exp(b_last[:, :, None, :] - bi)
        new_state = jnp.exp(b_last)[..., None] * state + jnp.einsum('bhjd,bhje->bhde', k_dec, vi)
        return new_state, o_intra + o_inter

    state0 = jnp.zeros((B, H, DK, DV), jnp.float32)
    _, o = lax.scan(step, state0, (qc, kc, vc, bc))
    return o.transpose(1, 0, 3, 2, 4).reshape(B, S, H, DV).astype(q.dtype)


def hybrid_layer(x, mem, cos, sin, g_mix, w_in, b_gate, g_ckv, w_ukv, g_q_nope, g_k_nope,
                 g_q_rope, g_k_rope, w_gla_gate, b_gla_gate, g_gla_out, g_mem, w_mem_kv,
                 g_q_mem, g_k_mem, w_o, g_ffn, w_up, w_down):
    B, S, _ = x.shape
    M = mem.shape[1]
    h = rms_norm(x, g_mix)
    proj = h @ w_in
    offsets = []
    acc = 0
    for n in IN_SPLITS[:-1]:
        acc += n
        offsets.append(acc)
    (q_mla, c_kv, k_rope, q_gla, k_gla, v_gla, a_gla, r_gla, q_mem,
     gate_logits) = jnp.split(proj, offsets, axis=-1)

    q_mla = q_mla.reshape(B, S, MLA_HEADS, MLA_NOPE + MLA_ROPE)
    q_nope = rms_norm(q_mla[..., :MLA_NOPE], g_q_nope)
    q_pe = apply_rope(rms_norm(q_mla[..., MLA_NOPE:], g_q_rope), cos, sin)
    kv = (rms_norm(c_kv, g_ckv) @ w_ukv).reshape(B, S, MLA_HEADS, MLA_NOPE + MLA_V)
    k_nope = rms_norm(kv[..., :MLA_NOPE], g_k_nope)
    v_mla = kv[..., MLA_NOPE:]
    k_pe = apply_rope(rms_norm(k_rope, g_k_rope)[:, :, None, :], cos, sin)
    q_full = jnp.concatenate([q_nope, q_pe], axis=-1)
    k_full = jnp.concatenate([k_nope, jnp.broadcast_to(k_pe, (B, S, MLA_HEADS, MLA_ROPE))], axis=-1)
    o_mla = causal_block_attention(q_full, k_full, v_mla,
                                   (MLA_NOPE + MLA_ROPE) ** -0.5).reshape(B, S, MLA_HEADS * MLA_V)

    qg = q_gla.reshape(B, S, GLA_HEADS, GLA_DK) * (GLA_DK ** -0.5)
    kg = k_gla.reshape(B, S, GLA_HEADS, GLA_DK)
    vg = v_gla.reshape(B, S, GLA_HEADS, GLA_DV)
    log_a = jax.nn.log_sigmoid((a_gla @ w_gla_gate + b_gla_gate).astype(jnp.float32)) / GLA_GATE_NORMALIZER
    log_a = log_a.reshape(B, S, GLA_HEADS, GLA_DK)
    og = gla_chunked(qg, kg, vg, log_a)
    o_gla = rms_norm(og, g_gla_out).reshape(B, S, GLA_HEADS * GLA_DV) * jax.nn.silu(r_gla)

    kv_m = rms_norm(mem, g_mem) @ w_mem_kv
    k_m = rms_norm(kv_m[..., :MEM_HEADS * MEM_DQK].reshape(B, M, MEM_HEADS, MEM_DQK), g_k_mem)
    v_m = kv_m[..., MEM_HEADS * MEM_DQK:].reshape(B, M, MEM_HEADS, MEM_DV)
    q_m = rms_norm(q_mem.reshape(B, S, MEM_HEADS, MEM_DQK), g_q_mem)
    s_m = jnp.einsum('bshd,bmhd->bhsm', q_m, k_m).astype(jnp.float32) * (MEM_DQK ** -0.5)
    p_m = jax.nn.softmax(s_m, axis=-1).astype(v_m.dtype)
    o_mem = jnp.einsum('bhsm,bmhd->bshd', p_m, v_m).reshape(B, S, MEM_HEADS * MEM_DV)

    gates = jax.nn.sigmoid(gate_logits + b_gate).reshape(B, S, N_BRANCHES, D_MODEL)
    y = gates[..., 0, :] * o_mla + gates[..., 1, :] * o_gla + gates[..., 2, :] * o_mem
    x = x + y @ w_o

    h2 = rms_norm(x, g_ffn)
    return x + jnp.square(jax.nn.relu(h2 @ w_up)) @ w_down


def setup_inputs(seed: int = 0) -> dict:
    key = jax.random.key(seed)
    ks = jax.random.split(key, 24)
    L = DEPTH

    def w(k, shape, fan_in):
        return jax.random.normal(k, (L,) + shape, jnp.float32) * (fan_in ** -0.5)

    def gain(k, n):
        return 1.0 + 0.02 * jax.random.normal(k, (L, n), jnp.float32)

    def bias(k, n):
        return 0.01 * jax.random.normal(k, (L, n), jnp.float32)

    x = jax.random.normal(ks[0], (BATCH, SEQ, D_MODEL), jnp.float32)
    mem = jax.random.normal(ks[1], (BATCH, MEM_LEN, D_MODEL), jnp.float32)
    positions = (jax.random.randint(ks[2], (BATCH, 1), 0, 4096, dtype=jnp.int32)
                 + jnp.arange(SEQ, dtype=jnp.int32)[None, :])
    return {
        'x': x,
        'mem': mem,
        'positions': positions,
        'g_mix': gain(ks[3], D_MODEL),
        'w_in': w(ks[4], (D_MODEL, D_IN), D_MODEL),
        'b_gate': bias(ks[5], N_BRANCHES * D_MODEL),
        'g_ckv': gain(ks[6], MLA_KV_RANK),
        'w_ukv': w(ks[7], (MLA_KV_RANK, MLA_HEADS * (MLA_NOPE + MLA_V)), MLA_KV_RANK),
        'g_q_nope': gain(ks[8], MLA_NOPE),
        'g_k_nope': gain(ks[9], MLA_NOPE),
        'g_q_rope': gain(ks[10], MLA_ROPE),
        'g_k_rope': gain(ks[11], MLA_ROPE),
        'w_gla_gate': w(ks[12], (GLA_GATE_RANK, GLA_HEADS * GLA_DK), GLA_GATE_RANK),
        'b_gla_gate': bias(ks[13], GLA_HEADS * GLA_DK),
        'g_gla_out': gain(ks[14], GLA_DV),
        'g_mem': gain(ks[15], D_MODEL),
        'w_mem_kv': w(ks[16], (D_MODEL, MEM_HEADS * (MEM_DQK + MEM_DV)), D_MODEL),
        'g_q_mem': gain(ks[17], MEM_DQK),
        'g_k_mem': gain(ks[18], MEM_DQK),
        'w_o': w(ks[19], (D_MODEL, D_MODEL), D_MODEL),
        'g_ffn': gain(ks[20], D_MODEL),
        'w_up': w(ks[21], (D_MODEL, D_FF), D_MODEL),
        'w_down': w(ks[22], (D_FF, D_MODEL), D_FF),
    }


def reference(x, mem, positions, g_mix, w_in, b_gate, g_ckv, w_ukv, g_q_nope, g_k_nope,
              g_q_rope, g_k_rope, w_gla_gate, b_gla_gate, g_gla_out, g_mem, w_mem_kv,
              g_q_mem, g_k_mem, w_o, g_ffn, w_up, w_down):
    cos, sin = rope_tables(positions)
    for l in range(DEPTH):
        x = hybrid_layer(x, mem, cos, sin, g_mix[l], w_in[l], b_gate[l], g_ckv[l], w_ukv[l],
                         g_q_nope[l], g_k_nope[l], g_q_rope[l], g_k_rope[l], w_gla_gate[l],
                         b_gla_gate[l], g_gla_out[l], g_mem[l], w_mem_kv[l], g_q_mem[l],
                         g_k_mem[l], w_o[l], g_ffn[l], w_up[l], w_down[l])
    return x
```

```python
import functools

import jax
import jax.numpy as jnp
import numpy as np
from jax import lax
from jax.experimental import pallas as pl
from jax.experimental.pallas import tpu as pltpu

F32 = jnp.float32
BF16 = jnp.bfloat16

EPS = 1e-6
ROPE_THETA = 10000.0
LANES = 128

MLA_HEADS, MLA_NOPE, MLA_ROPE, MLA_V = 8, 128, 64, 128
MLA_HEAD_PAD = 2 * LANES
GLA_HEADS, GLA_DK, GLA_DV = 4, 128, 256
GLA_GATE_RANK, GLA_GATE_NORMALIZER, GLA_CHUNK, GLA_SUB = 16, 16.0, 64, 16
MEM_HEADS, MEM_DQK, MEM_DV = 4, 64, 256
N_BRANCHES = 3

VMEM_LIMIT = 48 * 1024 * 1024

NT_DIMS = (((1,), (1,)), ((), ()))
TN_DIMS = (((0,), (0,)), ((), ()))


def _params(*sem):
    return pltpu.CompilerParams(dimension_semantics=sem, vmem_limit_bytes=VMEM_LIMIT)


def _rms(t, g, n=None):
    n = t.shape[-1] if n is None else n
    ss = jnp.sum(t * t, axis=-1, keepdims=True) * (1.0 / n)
    return t * lax.rsqrt(ss + EPS) * g


def _dot(a, b):
    return jnp.dot(a, b, preferred_element_type=F32)


def _split(a_f32):
    hi = a_f32.astype(BF16)
    return hi, (a_f32 - hi.astype(F32)).astype(BF16)


def _dot_split(a_f32, b_bf16):
    hi, lo = _split(a_f32)
    return _dot(hi, b_bf16) + _dot(lo, b_bf16)


def _const_spec(shape):
    return pl.BlockSpec(shape, lambda *_: (0,) * len(shape))


def _mla_prep_kernel(x_ref, gmix_ref, wq_ref, wckr_ref, wuk_ref, wuv_ref, cos_ref, sin_ref,
                     gqn_ref, gqr_ref, gckv_ref, gkn_ref, gkr_ref,
                     hb_ref, q_ref, k_ref, v_ref, *, q_scale):
    x = x_ref[...]
    hb = _rms(x, gmix_ref[...]).astype(BF16)
    hb_ref[...] = hb
    pq = _dot(hb, wq_ref[...])
    pc = _dot(hb, wckr_ref[...])
    cos = cos_ref[...]
    sin = sin_ref[...]
    lane = lax.broadcasted_iota(jnp.int32, cos.shape, 1)
    half = MLA_ROPE // 2

    def rope(t):
        rot = jnp.where(lane < half, pltpu.roll(t, LANES - half, 1), pltpu.roll(t, half, 1))
        return t * cos + rot * sin

    gqn = gqn_ref[...] * q_scale
    gqr = gqr_ref[...]
    for h in range(MLA_HEADS):
        c0 = h * MLA_HEAD_PAD
        a = _rms(pq[:, c0:c0 + LANES], gqn)
        b = rope(_rms(pq[:, c0 + LANES:c0 + 2 * LANES], gqr, MLA_ROPE)) * q_scale
        q_ref[:, c0:c0 + LANES] = a.astype(BF16)
        q_ref[:, c0 + LANES:c0 + 2 * LANES] = b.astype(BF16)

    rank = gckv_ref.shape[-1]
    ckv = _rms(pc[:, :rank], gckv_ref[...]).astype(BF16)
    kpe = rope(_rms(pc[:, rank:rank + LANES], gkr_ref[...], MLA_ROPE)).astype(BF16)
    kn = _dot(ckv, wuk_ref[...])
    v_ref[...] = _dot(ckv, wuv_ref[...]).astype(BF16)
    gkn = gkn_ref[...]
    for h in range(MLA_HEADS):
        c0 = h * MLA_HEAD_PAD
        k_ref[:, c0:c0 + LANES] = _rms(kn[:, h * LANES:(h + 1) * LANES], gkn).astype(BF16)
        k_ref[:, c0 + LANES:c0 + 2 * LANES] = kpe


def _mla_prep(x2, g_mix, wq, wckr, wuk, wuv, cos_t, sin_t, gqn, gqr, gckv, gkn, gkr, *, tm):
    T, D = x2.shape
    HP = MLA_HEADS * MLA_HEAD_PAD
    row = lambda w: pl.BlockSpec((tm, w), lambda i: (i, 0))
    return pl.pallas_call(
        functools.partial(_mla_prep_kernel, q_scale=float((MLA_NOPE + MLA_ROPE) ** -0.5)),
        grid=(T // tm,),
        in_specs=[row(D), _const_spec(g_mix.shape), _const_spec(wq.shape), _const_spec(wckr.shape),
                  _const_spec(wuk.shape), _const_spec(wuv.shape), row(LANES), row(LANES),
                  _const_spec(gqn.shape), _const_spec(gqr.shape), _const_spec(gckv.shape),
                  _const_spec(gkn.shape), _const_spec(gkr.shape)],
        out_specs=[row(D), row(HP), row(HP), row(MLA_HEADS * MLA_V)],
        out_shape=[jax.ShapeDtypeStruct((T, D), BF16), jax.ShapeDtypeStruct((T, HP), BF16),
                   jax.ShapeDtypeStruct((T, HP), BF16), jax.ShapeDtypeStruct((T, MLA_HEADS * MLA_V), BF16)],
        compiler_params=_params("parallel"),
        name="mla_prep",
    )(x2, g_mix, wq, wckr, wuk, wuv, cos_t, sin_t, gqn, gqr, gckv, gkn, gkr)


def _mla_attn_kernel(q_ref, k_ref, v_ref, o_ref, *, tq):
    qi = pl.program_id(2)
    q = q_ref[...]

    def step(ki, carry, masked):
        m, l, acc = carry
        r0 = pl.multiple_of(ki * tq, tq)
        k = k_ref[pl.ds(r0, tq), :]
        v = v_ref[pl.ds(r0, tq), :]
        s = lax.dot_general(q, k, NT_DIMS, preferred_element_type=F32)
        if masked:
            qpos = lax.broadcasted_iota(jnp.int32, s.shape, 0)
            kpos = lax.broadcasted_iota(jnp.int32, s.shape, 1)
            s = jnp.where(kpos <= qpos, s, -1e30)
        m_new = jnp.maximum(m, jnp.max(s, axis=-1, keepdims=True))
        p = jnp.exp(s - m_new)
        alpha = jnp.exp(m - m_new)
        l = alpha * l + jnp.sum(p, axis=-1, keepdims=True)
        acc = alpha * acc + _dot(p.astype(BF16), v)
        return m_new, l, acc

    init = (jnp.full((tq, 1), -1e30, F32), jnp.zeros((tq, 1), F32), jnp.zeros((tq, MLA_V), F32))
    carry = lax.fori_loop(0, qi, lambda ki, c: step(ki, c, False), init)
    m, l, acc = step(qi, carry, True)
    o_ref[...] = (acc / l).astype(o_ref.dtype)


def _mla_attn(q, k, v, *, batch, seq, tq):
    T = q.shape[0]
    nq = seq // tq
    return pl.pallas_call(
        functools.partial(_mla_attn_kernel, tq=tq),
        grid=(batch, MLA_HEADS, nq),
        in_specs=[pl.BlockSpec((tq, MLA_HEAD_PAD), lambda b, h, i: (b * nq + i, h)),
                  pl.BlockSpec((seq, MLA_HEAD_PAD), lambda b, h, i: (b, h)),
                  pl.BlockSpec((seq, MLA_V), lambda b, h, i: (b, h))],
        out_specs=pl.BlockSpec((tq, MLA_V), lambda b, h, i: (b * nq + i, h)),
        out_shape=jax.ShapeDtypeStruct((T, MLA_HEADS * MLA_V), BF16),
        compiler_params=_params("parallel", "parallel", "arbitrary"),
        name="mla_attn",
    )(q, k, v)


def _gla_proj_kernel(hb_ref, wq_ref, wk_ref, wv_ref, wa_ref, wr_ref, wg_ref, bg_ref,
                     q_ref, k_ref, v_ref, la_ref, sr_ref):
    hb = hb_ref[...]
    q_ref[...] = (_dot(hb, wq_ref[...]) * float(GLA_DK ** -0.5)).astype(BF16)
    k_ref[...] = _dot(hb, wk_ref[...]).astype(BF16)
    v_ref[...] = _dot(hb, wv_ref[...]).astype(BF16)
    a = _dot(hb, wa_ref[...])
    z = _dot(a.astype(BF16), wg_ref[...]) + bg_ref[...]
    log_sig = jnp.minimum(z, 0.0) - jnp.log1p(jnp.exp(-jnp.abs(z)))
    la_ref[...] = log_sig * (1.0 / GLA_GATE_NORMALIZER)
    r = _dot(hb, wr_ref[...])
    sr_ref[...] = (r / (1.0 + jnp.exp(-r))).astype(BF16)


def _gla_proj(hb, wq, wk, wv, wa, wr, wg, bg, *, tm):
    T, D = hb.shape
    nk, nv = wq.shape[1], wv.shape[1]
    row = lambda w: pl.BlockSpec((tm, w), lambda i: (i, 0))
    return pl.pallas_call(
        _gla_proj_kernel,
        grid=(T // tm,),
        in_specs=[row(D)] + [_const_spec(w.shape) for w in (wq, wk, wv, wa, wr, wg, bg)],
        out_specs=[row(nk), row(nk), row(nv), row(nk), row(nv)],
        out_shape=[jax.ShapeDtypeStruct((T, nk), BF16), jax.ShapeDtypeStruct((T, nk), BF16),
                   jax.ShapeDtypeStruct((T, nv), BF16), jax.ShapeDtypeStruct((T, nk), F32),
                   jax.ShapeDtypeStruct((T, nv), BF16)],
        compiler_params=_params("parallel"),
        name="gla_proj",
    )(hb, wq, wk, wv, wa, wr, wg, bg)


def _gla_chunk(q, k, v, la, st, tri):
    C = GLA_CHUNK
    la_hi, la_lo = _split(la)
    b = _dot(tri, la_hi) + _dot(tri, la_lo)
    b_last = b[C - 1:C, :]
    o = lax.dot_general((q * jnp.exp(b)).astype(BF16), st.astype(BF16), NT_DIMS,
                        preferred_element_type=F32)
    k_dec = (k * jnp.exp(b_last - b)).astype(BF16)
    st_new = st * jnp.exp(b_last) + lax.dot_general(v, k_dec, TN_DIMS, preferred_element_type=F32)

    lane_c = lax.broadcasted_iota(jnp.int32, (GLA_SUB, C), 1)
    sub = lax.broadcasted_iota(jnp.int32, (GLA_SUB, GLA_DK), 0)
    blocks = []
    for blk in range(C // GLA_SUB):
        i0 = blk * GLA_SUB
        b_i, q_i, k_i = b[i0:i0 + GLA_SUB], q[i0:i0 + GLA_SUB], k[i0:i0 + GLA_SUB]
        if blk == 0:
            a_blk = jnp.zeros((GLA_SUB, C), F32)
        else:
            ref_row = b[i0:i0 + 1, :]
            q_t = (q_i * jnp.exp(b_i - ref_row)).astype(BF16)
            k_t = (k * jnp.exp(jnp.minimum(ref_row - b, 0.0))).astype(BF16)
            a_blk = lax.dot_general(q_t, k_t, NT_DIMS, preferred_element_type=F32)
            a_blk = jnp.where(lane_c < i0, a_blk, 0.0)
        for j in range(GLA_SUB):
            keep = sub >= j
            diff = jnp.where(keep, b_i - b_i[j:j + 1, :], 0.0)
            w = jnp.where(keep, jnp.exp(diff), 0.0)
            col = jnp.sum(q_i * k_i[j:j + 1, :] * w, axis=-1, keepdims=True)
            a_blk = jnp.where(lane_c == i0 + j, col, a_blk)
        blocks.append(a_blk)
    a = jnp.concatenate(blocks, axis=0).astype(BF16)
    return o + _dot(a, v), st_new


def _gla_rec_kernel(q_ref, k_ref, v_ref, la_ref, sr_ref, g_ref, tri_ref, o_ref, st_ref, *, ts):
    @pl.when(pl.program_id(2) == 0)
    def _():
        st_ref[...] = jnp.zeros_like(st_ref)

    tri = tri_ref[...]
    g = g_ref[...]

    def body(c, _):
        r0 = pl.multiple_of(c * GLA_CHUNK, GLA_CHUNK)
        rows = pl.ds(r0, GLA_CHUNK)
        o, st_new = _gla_chunk(q_ref[rows, :].astype(F32), k_ref[rows, :].astype(F32), v_ref[rows, :],
                               la_ref[rows, :], st_ref[...], tri)
        st_ref[...] = st_new
        o_ref[rows, :] = (_rms(o, g) * sr_ref[rows, :].astype(F32)).astype(o_ref.dtype)
        return 0

    lax.fori_loop(0, ts // GLA_CHUNK, body, 0)


def _gla_rec(q, k, v, la, sr, g_out, tri, *, batch, seq, ts):
    T = q.shape[0]
    ns = seq // ts
    blk = lambda w: pl.BlockSpec((ts, w), lambda b, h, i: (b * ns + i, h))
    return pl.pallas_call(
        functools.partial(_gla_rec_kernel, ts=ts),
        grid=(batch, GLA_HEADS, ns),
        in_specs=[blk(GLA_DK), blk(GLA_DK), blk(GLA_DV), blk(GLA_DK), blk(GLA_DV),
                  _const_spec(g_out.shape), _const_spec(tri.shape)],
        out_specs=blk(GLA_DV),
        out_shape=jax.ShapeDtypeStruct((T, GLA_HEADS * GLA_DV), BF16),
        scratch_shapes=[pltpu.VMEM((GLA_DV, GLA_DK), F32)],
        compiler_params=_params("parallel", "parallel", "arbitrary"),
        name="gla_rec",
    )(q, k, v, la, sr, g_out, tri)


def _mem_kv_kernel(mem_ref, g_ref, w_ref, gk_ref, seg_ref, k_ref, v_ref):
    kv = _dot(_rms(mem_ref[...], g_ref[...]).astype(BF16), w_ref[...])
    nk = MEM_HEADS * MEM_DQK
    k = kv[:, :nk]
    ss = _dot_split(k * k, seg_ref[...]) * (1.0 / MEM_DQK)
    k_ref[...] = (k * lax.rsqrt(ss + EPS) * gk_ref[...]).astype(BF16)
    v_ref[...] = kv[:, nk:].astype(BF16)


def _mem_kv(mem2, g_mem, w, gk, seg, *, batch, mem_len):
    D = mem2.shape[1]
    nk, nv = MEM_HEADS * MEM_DQK, MEM_HEADS * MEM_DV
    row = lambda w_: pl.BlockSpec((mem_len, w_), lambda b: (b, 0))
    return pl.pallas_call(
        _mem_kv_kernel,
        grid=(batch,),
        in_specs=[row(D), _const_spec(g_mem.shape), _const_spec(w.shape), _const_spec(gk.shape),
                  _const_spec(seg.shape)],
        out_specs=[row(nk), row(nv)],
        out_shape=[jax.ShapeDtypeStruct((batch * mem_len, nk), BF16),
                   jax.ShapeDtypeStruct((batch * mem_len, nv), BF16)],
        compiler_params=_params("parallel"),
        name="mem_kv",
    )(mem2, g_mem, w, gk, seg)


def _merge_kernel(x_ref, hb_ref, omla_ref, ogla_ref, km_ref, vm_ref, wqm_ref, gqm_ref, seg_ref,
                  wgate_ref, bgate_ref, wo_ref, o_ref):
    hb = hb_ref[...]
    D = x_ref.shape[1]
    qm = _dot(hb, wqm_ref[...])
    ss = _dot_split(qm * qm, seg_ref[...]) * (1.0 / MEM_DQK)
    qn = qm * lax.rsqrt(ss + EPS) * (gqm_ref[...] * float(MEM_DQK ** -0.5))
    head_of_lane = lax.broadcasted_iota(jnp.int32, qn.shape, 1) // MEM_DQK
    km = km_ref[...]

    def gate(j):
        z = _dot(hb, wgate_ref[:, j * D:(j + 1) * D]) + bgate_ref[:, j * D:(j + 1) * D]
        return 1.0 / (1.0 + jnp.exp(-z))

    y = gate(0) * omla_ref[...].astype(F32) + gate(1) * ogla_ref[...].astype(F32)
    g_mem = gate(2)
    parts = []
    for h in range(MEM_HEADS):
        qh = jnp.where(head_of_lane == h, qn, 0.0).astype(BF16)
        s = lax.dot_general(qh, km, NT_DIMS, preferred_element_type=F32)
        p = jnp.exp(s - jnp.max(s, axis=-1, keepdims=True))
        o_h = _dot(p.astype(BF16), vm_ref[:, h * MEM_DV:(h + 1) * MEM_DV])
        parts.append(o_h / jnp.sum(p, axis=-1, keepdims=True))
    y = y + g_mem * jnp.concatenate(parts, axis=-1)
    o_ref[...] = x_ref[...] + _dot(y.astype(BF16), wo_ref[...])


def _merge(x2, hb, o_mla, o_gla, km, vm, wqm, gqm, seg, wgate, bgate, wo, *, seq, mem_len, tm):
    T, D = x2.shape
    per_batch = seq // tm
    row = lambda w: pl.BlockSpec((tm, w), lambda i: (i, 0))
    mem_blk = lambda w: pl.BlockSpec((mem_len, w), lambda i: (i // per_batch, 0))
    return pl.pallas_call(
        _merge_kernel,
        grid=(T // tm,),
        in_specs=[row(D), row(D), row(D), row(D), mem_blk(km.shape[1]), mem_blk(vm.shape[1])]
        + [_const_spec(w.shape) for w in (wqm, gqm, seg, wgate, bgate, wo)],
        out_specs=row(D),
        out_shape=jax.ShapeDtypeStruct((T, D), F32),
        compiler_params=_params("parallel"),
        name="merge",
    )(x2, hb, o_mla, o_gla, km, vm, wqm, gqm, seg, wgate, bgate, wo)


def _ffn_kernel(x_ref, g_ref, wup_ref, wdn_ref, o_ref, *, ff_chunk):
    x = x_ref[...]
    hb = _rms(x, g_ref[...]).astype(BF16)
    acc = x
    for c in range(wup_ref.shape[1] // ff_chunk):
        u = jnp.maximum(_dot(hb, wup_ref[:, c * ff_chunk:(c + 1) * ff_chunk]), 0.0)
        acc = acc + _dot((u * u).astype(BF16), wdn_ref[c * ff_chunk:(c + 1) * ff_chunk, :])
    o_ref[...] = acc


def _ffn(x1, g_ffn, wup, wdn, *, tm, ff_chunk):
    T, D = x1.shape
    row = pl.BlockSpec((tm, D), lambda i: (i, 0))
    return pl.pallas_call(
        functools.partial(_ffn_kernel, ff_chunk=ff_chunk),
        grid=(T // tm,),
        in_specs=[row, _const_spec(g_ffn.shape), _const_spec(wup.shape), _const_spec(wdn.shape)],
        out_specs=row,
        out_shape=jax.ShapeDtypeStruct((T, D), F32),
        compiler_params=_params("parallel"),
        name="ffn",
    )(x1, g_ffn, wup, wdn)


def _pad_cols(w, n):
    return jnp.pad(w, ((0, 0), (0, n - w.shape[-1])))


def _layer(x, mem, positions, g_mix, w_in, b_gate, g_ckv, w_ukv, g_q_nope, g_k_nope, g_q_rope, g_k_rope,
           w_gla_gate, b_gla_gate, g_gla_out, g_mem, w_mem_kv, g_q_mem, g_k_mem, w_o, g_ffn, w_up, w_down):
    B, S, D = x.shape
    M = mem.shape[1]
    T = B * S
    rank = g_ckv.shape[0]
    row = lambda g: g.reshape(1, -1).astype(F32)

    sizes = (MLA_HEADS * (MLA_NOPE + MLA_ROPE), rank, MLA_ROPE, GLA_HEADS * GLA_DK, GLA_HEADS * GLA_DK,
             GLA_HEADS * GLA_DV, GLA_GATE_RANK, GLA_HEADS * GLA_DV, MEM_HEADS * MEM_DQK, N_BRANCHES * D)
    offs = np.concatenate([[0], np.cumsum(sizes)])
    (w_q, w_ckv, w_kr, w_gq, w_gk, w_gv, w_ga, w_gr, w_qm, w_gate) = [
        w_in[:, offs[i]:offs[i + 1]] for i in range(len(sizes))]

    wq = jnp.pad(w_q.reshape(D, MLA_HEADS, MLA_NOPE + MLA_ROPE),
                 ((0, 0), (0, 0), (0, MLA_HEAD_PAD - MLA_NOPE - MLA_ROPE))).reshape(D, -1).astype(BF16)
    wckr = jnp.concatenate([w_ckv, _pad_cols(w_kr, LANES)], axis=1).astype(BF16)
    w_ukv3 = w_ukv.reshape(rank, MLA_HEADS, MLA_NOPE + MLA_V)
    wuk = w_ukv3[:, :, :MLA_NOPE].reshape(rank, -1).astype(BF16)
    wuv = w_ukv3[:, :, MLA_NOPE:].reshape(rank, -1).astype(BF16)

    inv = 1.0 / (ROPE_THETA ** (jnp.arange(0, MLA_ROPE, 2, dtype=F32) / MLA_ROPE))
    ang = positions.astype(F32).reshape(T, 1) * inv
    cos, sin = jnp.cos(ang), jnp.sin(ang)
    cos_t = _pad_cols(jnp.concatenate([cos, cos], axis=1), LANES)
    sin_t = _pad_cols(jnp.concatenate([-sin, sin], axis=1), LANES)

    x2 = x.reshape(T, D)
    hb, q, k, v = _mla_prep(
        x2, row(g_mix), wq, wckr, wuk, wuv, cos_t, sin_t, row(g_q_nope), _pad_cols(row(g_q_rope), LANES),
        row(g_ckv), row(g_k_nope), _pad_cols(row(g_k_rope), LANES), tm=min(256, S))
    o_mla = _mla_attn(q, k, v, batch=B, seq=S, tq=min(512, S))

    tri = jnp.asarray(np.tril(np.ones((GLA_CHUNK, GLA_CHUNK), np.float32)), BF16)
    wg = jnp.pad(w_gla_gate, ((0, LANES - GLA_GATE_RANK), (0, 0))).astype(BF16)
    qg, kg, vg, la, sr = _gla_proj(hb, w_gq.astype(BF16), w_gk.astype(BF16), w_gv.astype(BF16),
                                   _pad_cols(w_ga, LANES).astype(BF16), w_gr.astype(BF16), wg,
                                   row(b_gla_gate), tm=min(512, S))
    o_gla = _gla_rec(qg, kg, vg, la, sr, row(g_gla_out), tri, batch=B, seq=S, ts=min(512, S))

    nqk = MEM_HEADS * MEM_DQK
    seg = jnp.asarray(np.kron(np.eye(MEM_HEADS), np.ones((MEM_DQK, MEM_DQK))).astype(np.float32), BF16)
    km, vm = _mem_kv(mem.reshape(B * M, D), row(g_mem), w_mem_kv.astype(BF16),
                     jnp.tile(row(g_k_mem), (1, MEM_HEADS)), seg, batch=B, mem_len=M)
    x1 = _merge(x2, hb, o_mla, o_gla, km, vm, w_qm.astype(BF16), jnp.tile(row(g_q_mem), (1, MEM_HEADS)), seg,
                w_gate.astype(BF16), row(b_gate), w_o.astype(BF16), seq=S, mem_len=M, tm=min(256, S))
    out = _ffn(x1, row(g_ffn), w_up.astype(BF16), w_down.astype(BF16), tm=min(256, S), ff_chunk=1024)
    return out.reshape(B, S, D)


def kernel(x, mem, positions, g_mix, w_in, b_gate, g_ckv, w_ukv, g_q_nope, g_k_nope, g_q_rope, g_k_rope,
           w_gla_gate, b_gla_gate, g_gla_out, g_mem, w_mem_kv, g_q_mem, g_k_mem, w_o, g_ffn, w_up, w_down):
    for l in range(g_mix.shape[0]):
        x = _layer(x, mem, positions, g_mix[l], w_in[l], b_gate[l], g_ckv[l], w_ukv[l], g_q_nope[l],
                   g_k_nope[l], g_q_rope[l], g_k_rope[l], w_gla_gate[l], b_gla_gate[l], g_gla_out[l],
                   g_mem[l], w_mem_kv[l], g_q_mem[l], g_k_mem[l], w_o[l], g_ffn[l], w_up[l], w_down[l])
    return x
```

```python
import functools

import jax
import jax.numpy as jnp
import numpy as np
from jax import lax
from jax.experimental import pallas as pl
from jax.experimental.pallas import tpu as pltpu

F32 = jnp.float32
BF16 = jnp.bfloat16

EPS = 1e-6
ROPE_THETA = 10000.0
LANES = 128

MLA_HEADS, MLA_NOPE, MLA_ROPE, MLA_V = 8, 128, 64, 128
MLA_HEAD_PAD = 2 * LANES
GLA_HEADS, GLA_DK, GLA_DV = 4, 128, 256
GLA_GATE_RANK, GLA_GATE_NORMALIZER, GLA_CHUNK, GLA_SUB = 16, 16.0, 64, 16
MEM_HEADS, MEM_DQK, MEM_DV = 4, 64, 256
N_BRANCHES = 3

VMEM_LIMIT = 48 * 1024 * 1024

NT_DIMS = (((1,), (1,)), ((), ()))
TN_DIMS = (((0,), (0,)), ((), ()))


def _params(*sem):
    return pltpu.CompilerParams(dimension_semantics=sem, vmem_limit_bytes=VMEM_LIMIT)


def _rms(t, g, n=None):
    n = t.shape[-1] if n is None else n
    ss = jnp.sum(t * t, axis=-1, keepdims=True) * (1.0 / n)
    return t * lax.rsqrt(ss + EPS) * g


def _dot(a, b):
    return jnp.dot(a, b, preferred_element_type=F32)


def _split(a_f32):
    hi = a_f32.astype(BF16)
    return hi, (a_f32 - hi.astype(F32)).astype(BF16)


def _dot_split(a_f32, b_bf16):
    hi, lo = _split(a_f32)
    return _dot(hi, b_bf16) + _dot(lo, b_bf16)


def _const_spec(shape):
    return pl.BlockSpec(shape, lambda *_: (0,) * len(shape))


def _mla_prep_kernel(x_ref, gmix_ref, wq_ref, wckr_ref, wuk_ref, wuv_ref, cos_ref, sin_ref,
                     gqn_ref, gqr_ref, gckv_ref, gkn_ref, gkr_ref,
                     hb_ref, q_ref, k_ref, v_ref, *, q_scale):
    x = x_ref[...]
    hb = _rms(x, gmix_ref[...]).astype(BF16)
    hb_ref[...] = hb
    pq = _dot(hb, wq_ref[...])
    pc = _dot(hb, wckr_ref[...])
    cos = cos_ref[...]
    sin = sin_ref[...]
    lane = lax.broadcasted_iota(jnp.int32, cos.shape, 1)
    half = MLA_ROPE // 2

    def rope(t):
        rot = jnp.where(lane < half, pltpu.roll(t, LANES - half, 1), pltpu.roll(t, half, 1))
        return t * cos + rot * sin

    gqn = gqn_ref[...] * q_scale
    gqr = gqr_ref[...]
    for h in range(MLA_HEADS):
        c0 = h * MLA_HEAD_PAD
        a = _rms(pq[:, c0:c0 + LANES], gqn)
        b = rope(_rms(pq[:, c0 + LANES:c0 + 2 * LANES], gqr, MLA_ROPE)) * q_scale
        q_ref[:, c0:c0 + LANES] = a.astype(BF16)
        q_ref[:, c0 + LANES:c0 + 2 * LANES] = b.astype(BF16)

    rank = gckv_ref.shape[-1]
    ckv = _rms(pc[:, :rank], gckv_ref[...]).astype(BF16)
    kpe = rope(_rms(pc[:, rank:rank + LANES], gkr_ref[...], MLA_ROPE)).astype(BF16)
    kn = _dot(ckv, wuk_ref[...])
    v_ref[...] = _dot(ckv, wuv_ref[...]).astype(BF16)
    gkn = gkn_ref[...]
    for h in range(MLA_HEADS):
        c0 = h * MLA_HEAD_PAD
        k_ref[:, c0:c0 + LANES] = _rms(kn[:, h * LANES:(h + 1) * LANES], gkn).astype(BF16)
        k_ref[:, c0 + LANES:c0 + 2 * LANES] = kpe


def _mla_prep(x2, g_mix, wq, wckr, wuk, wuv, cos_t, sin_t, gqn, gqr, gckv, gkn, gkr, *, tm):
    T, D = x2.shape
    HP = MLA_HEADS * MLA_HEAD_PAD
    row = lambda w: pl.BlockSpec((tm, w), lambda i: (i, 0))
    return pl.pallas_call(
        functools.partial(_mla_prep_kernel, q_scale=float((MLA_NOPE + MLA_ROPE) ** -0.5 * np.log2(np.e))),
        grid=(T // tm,),
        in_specs=[row(D), _const_spec(g_mix.shape), _const_spec(wq.shape), _const_spec(wckr.shape),
                  _const_spec(wuk.shape), _const_spec(wuv.shape), row(LANES), row(LANES),
                  _const_spec(gqn.shape), _const_spec(gqr.shape), _const_spec(gckv.shape),
                  _const_spec(gkn.shape), _const_spec(gkr.shape)],
        out_specs=[row(D), row(HP), row(HP), row(MLA_HEADS * MLA_V)],
        out_shape=[jax.ShapeDtypeStruct((T, D), BF16), jax.ShapeDtypeStruct((T, HP), BF16),
                   jax.ShapeDtypeStruct((T, HP), BF16), jax.ShapeDtypeStruct((T, MLA_HEADS * MLA_V), BF16)],
        compiler_params=_params("parallel"),
        name="mla_prep",
    )(x2, g_mix, wq, wckr, wuk, wuv, cos_t, sin_t, gqn, gqr, gckv, gkn, gkr)


def _mla_attn_kernel(qt_ref, k_ref, vt_ref, o_ref, s_a, s_b, max_a, max_b, m_ref, l_ref, acc_ref, *, tq):
    qi = pl.program_id(2)

    def produce(j, s_ref, max_ref):
        r0 = pl.multiple_of(j * tq, tq)
        s = _dot(k_ref[pl.ds(r0, tq), :], qt_ref[...])
        s_ref[...] = s
        max_ref[...] = jnp.max(s, axis=0, keepdims=True)

    def consume(j, s, s_max):
        m = m_ref[...]
        m_new = jnp.maximum(m, s_max)
        alpha = jnp.exp2(m - m_new)
        p = jnp.exp2(s - m_new)
        m_ref[...] = m_new
        l_ref[...] = alpha * l_ref[...] + jnp.sum(p, axis=0, keepdims=True)
        acc_ref[...] = alpha * acc_ref[...] + _dot(vt_ref[j], p.astype(BF16))

    def finish(j, s_ref):
        s = s_ref[...]
        kpos = lax.broadcasted_iota(jnp.int32, s.shape, 0)
        qpos = lax.broadcasted_iota(jnp.int32, s.shape, 1)
        s = jnp.where(kpos <= qpos, s, -1e30)
        consume(j, s, jnp.max(s, axis=0, keepdims=True))
        o_ref[...] = (acc_ref[...] / l_ref[...]).T.astype(o_ref.dtype)

    m_ref[...] = jnp.full(m_ref.shape, -1e30, F32)
    l_ref[...] = jnp.zeros(l_ref.shape, F32)
    acc_ref[...] = jnp.zeros(acc_ref.shape, F32)
    produce(0, s_a, max_a)

    def pair(t, _):
        j = 2 * t
        produce(j + 1, s_b, max_b)
        consume(j, s_a[...], max_a[...])
        produce(j + 2, s_a, max_a)
        consume(j + 1, s_b[...], max_b[...])
        return 0

    lax.fori_loop(0, qi // 2, pair, 0)

    @pl.when(qi % 2 == 0)
    def _():
        finish(qi, s_a)

    @pl.when(qi % 2 == 1)
    def _():
        produce(qi, s_b, max_b)
        consume(qi - 1, s_a[...], max_a[...])
        finish(qi, s_b)


def _mla_attn(qt, k, vt, *, batch, seq, tq):
    T = k.shape[0]
    nq = seq // tq
    return pl.pallas_call(
        functools.partial(_mla_attn_kernel, tq=tq),
        grid=(batch, MLA_HEADS, nq),
        in_specs=[pl.BlockSpec((None, None, MLA_HEAD_PAD, tq), lambda b, h, i: (h, b * nq + i, 0, 0)),
                  pl.BlockSpec((seq, MLA_HEAD_PAD), lambda b, h, i: (b, h)),
                  pl.BlockSpec((None, nq, MLA_V, tq), lambda b, h, i: (h, b, 0, 0))],
        out_specs=pl.BlockSpec((tq, MLA_V), lambda b, h, i: (b * nq + i, h)),
        out_shape=jax.ShapeDtypeStruct((T, MLA_HEADS * MLA_V), BF16),
        scratch_shapes=[pltpu.VMEM((tq, tq), F32), pltpu.VMEM((tq, tq), F32),
                        pltpu.VMEM((1, tq), F32), pltpu.VMEM((1, tq), F32),
                        pltpu.VMEM((1, tq), F32), pltpu.VMEM((1, tq), F32),
                        pltpu.VMEM((MLA_V, tq), F32)],
        compiler_params=_params("parallel", "parallel", "arbitrary"),
        name="mla_attn",
    )(qt, k, vt)


def _gla_proj_kernel(hb_ref, wq_ref, wk_ref, wv_ref, wa_ref, wr_ref, wg_ref, bg_ref,
                     q_ref, k_ref, v_ref, la_ref, sr_ref):
    hb = hb_ref[...]
    q_ref[...] = (_dot(hb, wq_ref[...]) * float(GLA_DK ** -0.5)).astype(BF16)
    k_ref[...] = _dot(hb, wk_ref[...]).astype(BF16)
    v_ref[...] = _dot(hb, wv_ref[...]).astype(BF16)
    a = _dot(hb, wa_ref[...])
    z = _dot(a.astype(BF16), wg_ref[...]) + bg_ref[...]
    log_sig = jnp.minimum(z, 0.0) - jnp.log1p(jnp.exp(-jnp.abs(z)))
    la_ref[...] = log_sig * (1.0 / GLA_GATE_NORMALIZER)
    r = _dot(hb, wr_ref[...])
    sr_ref[...] = (r / (1.0 + jnp.exp(-r))).astype(BF16)


def _gla_proj(hb, wq, wk, wv, wa, wr, wg, bg, *, tm):
    T, D = hb.shape
    nk, nv = wq.shape[1], wv.shape[1]
    row = lambda w: pl.BlockSpec((tm, w), lambda i: (i, 0))
    return pl.pallas_call(
        _gla_proj_kernel,
        grid=(T // tm,),
        in_specs=[row(D)] + [_const_spec(w.shape) for w in (wq, wk, wv, wa, wr, wg, bg)],
        out_specs=[row(nk), row(nk), row(nv), row(nk), row(nv)],
        out_shape=[jax.ShapeDtypeStruct((T, nk), BF16), jax.ShapeDtypeStruct((T, nk), BF16),
                   jax.ShapeDtypeStruct((T, nv), BF16), jax.ShapeDtypeStruct((T, nk), F32),
                   jax.ShapeDtypeStruct((T, nv), BF16)],
        compiler_params=_params("parallel"),
        name="gla_proj",
    )(hb, wq, wk, wv, wa, wr, wg, bg)


def _gla_chunk(q, k, v, la, st, tri):
    C = GLA_CHUNK
    la_hi, la_lo = _split(la)
    b = _dot(tri, la_hi) + _dot(tri, la_lo)
    b_last = b[C - 1:C, :]
    o = lax.dot_general((q * jnp.exp(b)).astype(BF16), st.astype(BF16), NT_DIMS,
                        preferred_element_type=F32)
    k_dec = (k * jnp.exp(b_last - b)).astype(BF16)
    st_new = st * jnp.exp(b_last) + lax.dot_general(v, k_dec, TN_DIMS, preferred_element_type=F32)

    lane_c = lax.broadcasted_iota(jnp.int32, (GLA_SUB, C), 1)
    sub = lax.broadcasted_iota(jnp.int32, (GLA_SUB, GLA_DK), 0)
    blocks = []
    for blk in range(C // GLA_SUB):
        i0 = blk * GLA_SUB
        b_i, q_i, k_i = b[i0:i0 + GLA_SUB], q[i0:i0 + GLA_SUB], k[i0:i0 + GLA_SUB]
        if blk == 0:
            a_blk = jnp.zeros((GLA_SUB, C), F32)
        else:
            ref_row = b[i0:i0 + 1, :]
            q_t = (q_i * jnp.exp(b_i - ref_row)).astype(BF16)
            k_t = (k * jnp.exp(jnp.minimum(ref_row - b, 0.0))).astype(BF16)
            a_blk = lax.dot_general(q_t, k_t, NT_DIMS, preferred_element_type=F32)
            a_blk = jnp.where(lane_c < i0, a_blk, 0.0)
        for j in range(GLA_SUB):
            keep = sub >= j
            diff = jnp.where(keep, b_i - b_i[j:j + 1, :], 0.0)
            w = jnp.where(keep, jnp.exp(diff), 0.0)
            col = jnp.sum(q_i * k_i[j:j + 1, :] * w, axis=-1, keepdims=True)
            a_blk = jnp.where(lane_c == i0 + j, col, a_blk)
        blocks.append(a_blk)
    a = jnp.concatenate(blocks, axis=0).astype(BF16)
    return o + _dot(a, v), st_new


def _gla_rec_kernel(q_ref, k_ref, v_ref, la_ref, sr_ref, g_ref, tri_ref, o_ref, st_ref, *, ts):
    @pl.when(pl.program_id(2) == 0)
    def _():
        st_ref[...] = jnp.zeros_like(st_ref)

    tri = tri_ref[...]
    g = g_ref[...]

    def body(c, _):
        r0 = pl.multiple_of(c * GLA_CHUNK, GLA_CHUNK)
        rows = pl.ds(r0, GLA_CHUNK)
        o, st_new = _gla_chunk(q_ref[rows, :].astype(F32), k_ref[rows, :].astype(F32), v_ref[rows, :],
                               la_ref[rows, :], st_ref[...], tri)
        st_ref[...] = st_new
        o_ref[rows, :] = (_rms(o, g) * sr_ref[rows, :].astype(F32)).astype(o_ref.dtype)
        return 0

    lax.fori_loop(0, ts // GLA_CHUNK, body, 0)


def _gla_rec(q, k, v, la, sr, g_out, tri, *, batch, seq, ts):
    T = q.shape[0]
    ns = seq // ts
    blk = lambda w: pl.BlockSpec((ts, w), lambda b, h, i: (b * ns + i, h))
    return pl.pallas_call(
        functools.partial(_gla_rec_kernel, ts=ts),
        grid=(batch, GLA_HEADS, ns),
        in_specs=[blk(GLA_DK), blk(GLA_DK), blk(GLA_DV), blk(GLA_DK), blk(GLA_DV),
                  _const_spec(g_out.shape), _const_spec(tri.shape)],
        out_specs=blk(GLA_DV),
        out_shape=jax.ShapeDtypeStruct((T, GLA_HEADS * GLA_DV), BF16),
        scratch_shapes=[pltpu.VMEM((GLA_DV, GLA_DK), F32)],
        compiler_params=_params("parallel", "parallel", "arbitrary"),
        name="gla_rec",
    )(q, k, v, la, sr, g_out, tri)


def _mem_kv_kernel(mem_ref, g_ref, w_ref, gk_ref, seg_ref, k_ref, v_ref):
    kv = _dot(_rms(mem_ref[...], g_ref[...]).astype(BF16), w_ref[...])
    nk = MEM_HEADS * MEM_DQK
    k = kv[:, :nk]
    ss = _dot_split(k * k, seg_ref[...]) * (1.0 / MEM_DQK)
    k_ref[...] = (k * lax.rsqrt(ss + EPS) * gk_ref[...]).astype(BF16)
    v_ref[...] = kv[:, nk:].astype(BF16)


def _mem_kv(mem2, g_mem, w, gk, seg, *, batch, mem_len):
    D = mem2.shape[1]
    nk, nv = MEM_HEADS * MEM_DQK, MEM_HEADS * MEM_DV
    row = lambda w_: pl.BlockSpec((mem_len, w_), lambda b: (b, 0))
    return pl.pallas_call(
        _mem_kv_kernel,
        grid=(batch,),
        in_specs=[row(D), _const_spec(g_mem.shape), _const_spec(w.shape), _const_spec(gk.shape),
                  _const_spec(seg.shape)],
        out_specs=[row(nk), row(nv)],
        out_shape=[jax.ShapeDtypeStruct((batch * mem_len, nk), BF16),
                   jax.ShapeDtypeStruct((batch * mem_len, nv), BF16)],
        compiler_params=_params("parallel"),
        name="mem_kv",
    )(mem2, g_mem, w, gk, seg)


def _merge_kernel(x_ref, hb_ref, omla_ref, ogla_ref, km_ref, vm_ref, wqm_ref, gqm_ref, seg_ref,
                  wgate_ref, bgate_ref, wo_ref, o_ref):
    hb = hb_ref[...]
    D = x_ref.shape[1]
    qm = _dot(hb, wqm_ref[...])
    ss = _dot_split(qm * qm, seg_ref[...]) * (1.0 / MEM_DQK)
    qn = qm * lax.rsqrt(ss + EPS) * (gqm_ref[...] * float(MEM_DQK ** -0.5))
    head_of_lane = lax.broadcasted_iota(jnp.int32, qn.shape, 1) // MEM_DQK
    km = km_ref[...]

    def gate(j):
        z = _dot(hb, wgate_ref[:, j * D:(j + 1) * D]) + bgate_ref[:, j * D:(j + 1) * D]
        return 1.0 / (1.0 + jnp.exp(-z))

    y = gate(0) * omla_ref[...].astype(F32) + gate(1) * ogla_ref[...].astype(F32)
    g_mem = gate(2)
    parts = []
    for h in range(MEM_HEADS):
        qh = jnp.where(head_of_lane == h, qn, 0.0).astype(BF16)
        s = lax.dot_general(qh, km, NT_DIMS, preferred_element_type=F32)
        p = jnp.exp(s - jnp.max(s, axis=-1, keepdims=True))
        o_h = _dot(p.astype(BF16), vm_ref[:, h * MEM_DV:(h + 1) * MEM_DV])
        parts.append(o_h / jnp.sum(p, axis=-1, keepdims=True))
    y = y + g_mem * jnp.concatenate(parts, axis=-1)
    o_ref[...] = x_ref[...] + _dot(y.astype(BF16), wo_ref[...])


def _merge(x2, hb, o_mla, o_gla, km, vm, wqm, gqm, seg, wgate, bgate, wo, *, seq, mem_len, tm):
    T, D = x2.shape
    per_batch = seq // tm
    row = lambda w: pl.BlockSpec((tm, w), lambda i: (i, 0))
    mem_blk = lambda w: pl.BlockSpec((mem_len, w), lambda i: (i // per_batch, 0))
    return pl.pallas_call(
        _merge_kernel,
        grid=(T // tm,),
        in_specs=[row(D), row(D), row(D), row(D), mem_blk(km.shape[1]), mem_blk(vm.shape[1])]
        + [_const_spec(w.shape) for w in (wqm, gqm, seg, wgate, bgate, wo)],
        out_specs=row(D),
        out_shape=jax.ShapeDtypeStruct((T, D), F32),
        compiler_params=_params("parallel"),
        name="merge",
    )(x2, hb, o_mla, o_gla, km, vm, wqm, gqm, seg, wgate, bgate, wo)


def _ffn_kernel(x_ref, g_ref, wup_ref, wdn_ref, o_ref, *, ff_chunk):
    x = x_ref[...]
    hb = _rms(x, g_ref[...]).astype(BF16)
    acc = x
    for c in range(wup_ref.shape[1] // ff_chunk):
        u = jnp.maximum(_dot(hb, wup_ref[:, c * ff_chunk:(c + 1) * ff_chunk]), 0.0)
        acc = acc + _dot((u * u).astype(BF16), wdn_ref[c * ff_chunk:(c + 1) * ff_chunk, :])
    o_ref[...] = acc


def _ffn(x1, g_ffn, wup, wdn, *, tm, ff_chunk):
    T, D = x1.shape
    row = pl.BlockSpec((tm, D), lambda i: (i, 0))
    return pl.pallas_call(
        functools.partial(_ffn_kernel, ff_chunk=ff_chunk),
        grid=(T // tm,),
        in_specs=[row, _const_spec(g_ffn.shape), _const_spec(wup.shape), _const_spec(wdn.shape)],
        out_specs=row,
        out_shape=jax.ShapeDtypeStruct((T, D), F32),
        compiler_params=_params("parallel"),
        name="ffn",
    )(x1, g_ffn, wup, wdn)


def _pad_cols(w, n):
    return jnp.pad(w, ((0, 0), (0, n - w.shape[-1])))


def _layer(x, mem, positions, g_mix, w_in, b_gate, g_ckv, w_ukv, g_q_nope, g_k_nope, g_q_rope, g_k_rope,
           w_gla_gate, b_gla_gate, g_gla_out, g_mem, w_mem_kv, g_q_mem, g_k_mem, w_o, g_ffn, w_up, w_down):
    B, S, D = x.shape
    M = mem.shape[1]
    T = B * S
    rank = g_ckv.shape[0]
    row = lambda g: g.reshape(1, -1).astype(F32)

    sizes = (MLA_HEADS * (MLA_NOPE + MLA_ROPE), rank, MLA_ROPE, GLA_HEADS * GLA_DK, GLA_HEADS * GLA_DK,
             GLA_HEADS * GLA_DV, GLA_GATE_RANK, GLA_HEADS * GLA_DV, MEM_HEADS * MEM_DQK, N_BRANCHES * D)
    offs = np.concatenate([[0], np.cumsum(sizes)])
    (w_q, w_ckv, w_kr, w_gq, w_gk, w_gv, w_ga, w_gr, w_qm, w_gate) = [
        w_in[:, offs[i]:offs[i + 1]] for i in range(len(sizes))]

    wq = jnp.pad(w_q.reshape(D, MLA_HEADS, MLA_NOPE + MLA_ROPE),
                 ((0, 0), (0, 0), (0, MLA_HEAD_PAD - MLA_NOPE - MLA_ROPE))).reshape(D, -1).astype(BF16)
    wckr = jnp.concatenate([w_ckv, _pad_cols(w_kr, LANES)], axis=1).astype(BF16)
    w_ukv3 = w_ukv.reshape(rank, MLA_HEADS, MLA_NOPE + MLA_V)
    wuk = w_ukv3[:, :, :MLA_NOPE].reshape(rank, -1).astype(BF16)
    wuv = w_ukv3[:, :, MLA_NOPE:].reshape(rank, -1).astype(BF16)

    inv = 1.0 / (ROPE_THETA ** (jnp.arange(0, MLA_ROPE, 2, dtype=F32) / MLA_ROPE))
    ang = positions.astype(F32).reshape(T, 1) * inv
    cos, sin = jnp.cos(ang), jnp.sin(ang)
    cos_t = _pad_cols(jnp.concatenate([cos, cos], axis=1), LANES)
    sin_t = _pad_cols(jnp.concatenate([-sin, sin], axis=1), LANES)

    x2 = x.reshape(T, D)
    hb, q, k, v = _mla_prep(
        x2, row(g_mix), wq, wckr, wuk, wuv, cos_t, sin_t, row(g_q_nope), _pad_cols(row(g_q_rope), LANES),
        row(g_ckv), row(g_k_nope), _pad_cols(row(g_k_rope), LANES), tm=min(256, S))
    tq = min(512, S)
    qt = q.reshape(T // tq, tq, MLA_HEADS, MLA_HEAD_PAD).transpose(2, 0, 3, 1)
    vt = v.reshape(T // tq, tq, MLA_HEADS, MLA_V).transpose(2, 0, 3, 1)
    o_mla = _mla_attn(qt, k, vt, batch=B, seq=S, tq=tq)

    tri = jnp.asarray(np.tril(np.ones((GLA_CHUNK, GLA_CHUNK), np.float32)), BF16)
    wg = jnp.pad(w_gla_gate, ((0, LANES - GLA_GATE_RANK), (0, 0))).astype(BF16)
    qg, kg, vg, la, sr = _gla_proj(hb, w_gq.astype(BF16), w_gk.astype(BF16), w_gv.astype(BF16),
                                   _pad_cols(w_ga, LANES).astype(BF16), w_gr.astype(BF16), wg,
                                   row(b_gla_gate), tm=min(512, S))
    o_gla = _gla_rec(qg, kg, vg, la, sr, row(g_gla_out), tri, batch=B, seq=S, ts=min(512, S))

    nqk = MEM_HEADS * MEM_DQK
    seg = jnp.asarray(np.kron(np.eye(MEM_HEADS), np.ones((MEM_DQK, MEM_DQK))).astype(np.float32), BF16)
    km, vm = _mem_kv(mem.reshape(B * M, D), row(g_mem), w_mem_kv.astype(BF16),
                     jnp.tile(row(g_k_mem), (1, MEM_HEADS)), seg, batch=B, mem_len=M)
    x1 = _merge(x2, hb, o_mla, o_gla, km, vm, w_qm.astype(BF16), jnp.tile(row(g_q_mem), (1, MEM_HEADS)), seg,
                w_gate.astype(BF16), row(b_gate), w_o.astype(BF16), seq=S, mem_len=M, tm=min(256, S))
    out = _ffn(x1, row(g_ffn), w_up.astype(BF16), w_down.astype(BF16), tm=min(256, S), ff_chunk=1024)
    return out.reshape(B, S, D)


def kernel(x, mem, positions, g_mix, w_in, b_gate, g_ckv, w_ukv, g_q_nope, g_k_nope, g_q_rope, g_k_rope,
           w_gla_gate, b_gla_gate, g_gla_out, g_mem, w_mem_kv, g_q_mem, g_k_mem, w_o, g_ffn, w_up, w_down):
    for l in range(g_mix.shape[0]):
        x = _layer(x, mem, positions, g_mix[l], w_in[l], b_gate[l], g_ckv[l], w_ukv[l], g_q_nope[l],
                   g_k_nope[l], g_q_rope[l], g_k_rope[l], w_gla_gate[l], b_gla_gate[l], g_gla_out[l],
                   g_mem[l], w_mem_kv[l], g_q_mem[l], g_k_mem[l], w_o[l], g_ffn[l], w_up[l], w_down[l])
    return x
```

```python
import functools

import jax
import jax.numpy as jnp
import numpy as np
from jax import lax
from jax.experimental import pallas as pl
from jax.experimental.pallas import tpu as pltpu

F32 = jnp.float32
BF16 = jnp.bfloat16

EPS = 1e-6
ROPE_THETA = 10000.0
LANES = 128

MLA_HEADS, MLA_NOPE, MLA_ROPE, MLA_V = 8, 128, 64, 128
MLA_HEAD_PAD = 2 * LANES
GLA_HEADS, GLA_DK, GLA_DV = 4, 128, 256
GLA_GATE_RANK, GLA_GATE_NORMALIZER, GLA_CHUNK, GLA_SUB = 16, 16.0, 64, 16
MEM_HEADS, MEM_DQK, MEM_DV = 4, 64, 256
N_BRANCHES = 3

VMEM_LIMIT = 48 * 1024 * 1024

NT_DIMS = (((1,), (1,)), ((), ()))
TN_DIMS = (((0,), (0,)), ((), ()))


def _params(*sem):
    return pltpu.CompilerParams(dimension_semantics=sem, vmem_limit_bytes=VMEM_LIMIT)


def _rms(t, g, n=None):
    n = t.shape[-1] if n is None else n
    ss = jnp.sum(t * t, axis=-1, keepdims=True) * (1.0 / n)
    return t * lax.rsqrt(ss + EPS) * g


def _dot(a, b):
    return jnp.dot(a, b, preferred_element_type=F32)


def _split(a_f32):
    hi = a_f32.astype(BF16)
    return hi, (a_f32 - hi.astype(F32)).astype(BF16)


def _dot_split(a_f32, b_bf16):
    hi, lo = _split(a_f32)
    return _dot(hi, b_bf16) + _dot(lo, b_bf16)


def _const_spec(shape):
    return pl.BlockSpec(shape, lambda *_: (0,) * len(shape))


def _mla_prep_kernel(x_ref, gmix_ref, wq_ref, wckr_ref, wuk_ref, wuv_ref, cos_ref, sin_ref,
                     gqn_ref, gqr_ref, gckv_ref, gkn_ref, gkr_ref,
                     hb_ref, q_ref, k_ref, v_ref, *, q_scale):
    x = x_ref[...]
    hb = _rms(x, gmix_ref[...]).astype(BF16)
    hb_ref[...] = hb
    pq = _dot(hb, wq_ref[...])
    pc = _dot(hb, wckr_ref[...])
    cos = cos_ref[...]
    sin = sin_ref[...]
    lane = lax.broadcasted_iota(jnp.int32, cos.shape, 1)
    half = MLA_ROPE // 2

    def rope(t):
        rot = jnp.where(lane < half, pltpu.roll(t, LANES - half, 1), pltpu.roll(t, half, 1))
        return t * cos + rot * sin

    gqn = gqn_ref[...] * q_scale
    gqr = gqr_ref[...]
    for h in range(MLA_HEADS):
        c0 = h * MLA_HEAD_PAD
        a = _rms(pq[:, c0:c0 + LANES], gqn)
        b = rope(_rms(pq[:, c0 + LANES:c0 + 2 * LANES], gqr, MLA_ROPE)) * q_scale
        q_ref[:, c0:c0 + LANES] = a.astype(BF16)
        q_ref[:, c0 + LANES:c0 + 2 * LANES] = b.astype(BF16)

    rank = gckv_ref.shape[-1]
    ckv = _rms(pc[:, :rank], gckv_ref[...]).astype(BF16)
    kpe = rope(_rms(pc[:, rank:rank + LANES], gkr_ref[...], MLA_ROPE)).astype(BF16)
    kn = _dot(ckv, wuk_ref[...])
    v_ref[...] = _dot(ckv, wuv_ref[...]).astype(BF16)
    gkn = gkn_ref[...]
    for h in range(MLA_HEADS):
        c0 = h * MLA_HEAD_PAD
        k_ref[:, c0:c0 + LANES] = _rms(kn[:, h * LANES:(h + 1) * LANES], gkn).astype(BF16)
        k_ref[:, c0 + LANES:c0 + 2 * LANES] = kpe


def _mla_prep(x2, g_mix, wq, wckr, wuk, wuv, cos_t, sin_t, gqn, gqr, gckv, gkn, gkr, *, tm):
    T, D = x2.shape
    HP = MLA_HEADS * MLA_HEAD_PAD
    row = lambda w: pl.BlockSpec((tm, w), lambda i: (i, 0))
    return pl.pallas_call(
        functools.partial(_mla_prep_kernel, q_scale=float((MLA_NOPE + MLA_ROPE) ** -0.5 * np.log2(np.e))),
        grid=(T // tm,),
        in_specs=[row(D), _const_spec(g_mix.shape), _const_spec(wq.shape), _const_spec(wckr.shape),
                  _const_spec(wuk.shape), _const_spec(wuv.shape), row(LANES), row(LANES),
                  _const_spec(gqn.shape), _const_spec(gqr.shape), _const_spec(gckv.shape),
                  _const_spec(gkn.shape), _const_spec(gkr.shape)],
        out_specs=[row(D), row(HP), row(HP), row(MLA_HEADS * MLA_V)],
        out_shape=[jax.ShapeDtypeStruct((T, D), BF16), jax.ShapeDtypeStruct((T, HP), BF16),
                   jax.ShapeDtypeStruct((T, HP), BF16), jax.ShapeDtypeStruct((T, MLA_HEADS * MLA_V), BF16)],
        compiler_params=_params("parallel"),
        name="mla_prep",
    )(x2, g_mix, wq, wckr, wuk, wuv, cos_t, sin_t, gqn, gqr, gckv, gkn, gkr)


def _mla_attn_kernel(qt_ref, k_ref, vt_ref, o_ref, s_a, s_b, max_a, max_b, m_ref, l_ref, acc_ref, *, tq):
    qi = pl.program_id(2)

    def produce(j, s_ref, max_ref):
        r0 = pl.multiple_of(j * tq, tq)
        s = _dot(k_ref[pl.ds(r0, tq), :], qt_ref[...])
        s_ref[...] = s
        max_ref[...] = jnp.max(s, axis=0, keepdims=True)

    def consume(j, s, s_max):
        m = m_ref[...]
        m_new = jnp.maximum(m, s_max)
        alpha = jnp.exp2(m - m_new)
        p = jnp.exp2(s - m_new)
        m_ref[...] = m_new
        l_ref[...] = alpha * l_ref[...] + jnp.sum(p, axis=0, keepdims=True)
        acc_ref[...] = alpha * acc_ref[...] + _dot(vt_ref[j], p.astype(BF16))

    def finish(j, s_ref):
        s = s_ref[...]
        kpos = lax.broadcasted_iota(jnp.int32, s.shape, 0)
        qpos = lax.broadcasted_iota(jnp.int32, s.shape, 1)
        s = jnp.where(kpos <= qpos, s, -1e30)
        consume(j, s, jnp.max(s, axis=0, keepdims=True))
        o_ref[...] = (acc_ref[...] / l_ref[...]).T.astype(o_ref.dtype)

    m_ref[...] = jnp.full(m_ref.shape, -1e30, F32)
    l_ref[...] = jnp.zeros(l_ref.shape, F32)
    acc_ref[...] = jnp.zeros(acc_ref.shape, F32)
    produce(0, s_a, max_a)

    def pair(t, _):
        j = 2 * t
        produce(j + 1, s_b, max_b)
        consume(j, s_a[...], max_a[...])
        produce(j + 2, s_a, max_a)
        consume(j + 1, s_b[...], max_b[...])
        return 0

    lax.fori_loop(0, qi // 2, pair, 0)

    @pl.when(qi % 2 == 0)
    def _():
        finish(qi, s_a)

    @pl.when(qi % 2 == 1)
    def _():
        produce(qi, s_b, max_b)
        consume(qi - 1, s_a[...], max_a[...])
        finish(qi, s_b)


def _mla_attn(qt, k, vt, *, batch, seq, tq):
    T = k.shape[0]
    nq = seq // tq
    return pl.pallas_call(
        functools.partial(_mla_attn_kernel, tq=tq),
        grid=(batch, MLA_HEADS, nq),
        in_specs=[pl.BlockSpec((None, None, MLA_HEAD_PAD, tq), lambda b, h, i: (h, b * nq + i, 0, 0)),
                  pl.BlockSpec((seq, MLA_HEAD_PAD), lambda b, h, i: (b, h)),
                  pl.BlockSpec((None, nq, MLA_V, tq), lambda b, h, i: (h, b, 0, 0))],
        out_specs=pl.BlockSpec((tq, MLA_V), lambda b, h, i: (b * nq + i, h)),
        out_shape=jax.ShapeDtypeStruct((T, MLA_HEADS * MLA_V), BF16),
        scratch_shapes=[pltpu.VMEM((tq, tq), F32), pltpu.VMEM((tq, tq), F32),
                        pltpu.VMEM((1, tq), F32), pltpu.VMEM((1, tq), F32),
                        pltpu.VMEM((1, tq), F32), pltpu.VMEM((1, tq), F32),
                        pltpu.VMEM((MLA_V, tq), F32)],
        compiler_params=_params("parallel", "parallel", "arbitrary"),
        name="mla_attn",
    )(qt, k, vt)


def _gla_proj_kernel(hb_ref, wq_ref, wk_ref, wv_ref, wa_ref, wr_ref, wg_ref, bg_ref,
                     q_ref, k_ref, v_ref, la_ref, sr_ref):
    hb = hb_ref[...]
    q_ref[...] = (_dot(hb, wq_ref[...]) * float(GLA_DK ** -0.5)).astype(BF16)
    k_ref[...] = _dot(hb, wk_ref[...]).astype(BF16)
    v_ref[...] = _dot(hb, wv_ref[...]).astype(BF16)
    a = _dot(hb, wa_ref[...])
    z = _dot(a.astype(BF16), wg_ref[...]) + bg_ref[...]
    log_sig = jnp.minimum(z, 0.0) - jnp.log1p(jnp.exp(-jnp.abs(z)))
    la_ref[...] = log_sig * (1.0 / GLA_GATE_NORMALIZER)
    r = _dot(hb, wr_ref[...])
    sr_ref[...] = (r / (1.0 + jnp.exp(-r))).astype(BF16)


def _gla_proj(hb, wq, wk, wv, wa, wr, wg, bg, *, tm):
    T, D = hb.shape
    nk, nv = wq.shape[1], wv.shape[1]
    row = lambda w: pl.BlockSpec((tm, w), lambda i: (i, 0))
    return pl.pallas_call(
        _gla_proj_kernel,
        grid=(T // tm,),
        in_specs=[row(D)] + [_const_spec(w.shape) for w in (wq, wk, wv, wa, wr, wg, bg)],
        out_specs=[row(nk), row(nk), row(nv), row(nk), row(nv)],
        out_shape=[jax.ShapeDtypeStruct((T, nk), BF16), jax.ShapeDtypeStruct((T, nk), BF16),
                   jax.ShapeDtypeStruct((T, nv), BF16), jax.ShapeDtypeStruct((T, nk), F32),
                   jax.ShapeDtypeStruct((T, nv), BF16)],
        compiler_params=_params("parallel"),
        name="gla_proj",
    )(hb, wq, wk, wv, wa, wr, wg, bg)


GLA_LEVELS = tuple(GLA_CHUNK >> (i + 1) for i in range(GLA_CHUNK.bit_length() - 1))


def _gla_decay_matrix():
    r = np.arange(GLA_CHUNK)
    groups = [r[None, :] <= r[:, None], r[None, :] > r[:, None]]
    for s in GLA_LEVELS:
        mid = (r // (2 * s)) * (2 * s) + s
        upper = (r & s) != 0
        up = (r[None, :] > mid[:, None]) & (r[None, :] <= r[:, None])
        lo = (r[None, :] > r[:, None]) & (r[None, :] <= mid[:, None])
        groups.append(np.where(upper[:, None], up, lo))
    return np.concatenate(groups, axis=0).astype(np.float32)


def _gla_level_masks():
    r = np.arange(GLA_CHUNK)
    x = r[:, None] ^ r[None, :]
    lower = r[None, :] < r[:, None]
    return np.stack([(lower & (x >= s) & (x < 2 * s)) for s in GLA_LEVELS]).astype(np.float32)


def _gla_group(qs, ks, vs, las, st, nmat, masks_ref):
    C, n = GLA_CHUNK, len(qs)
    la2 = jnp.concatenate([jnp.concatenate(_split(la), axis=0) for la in las], axis=1)
    e_all = jnp.exp(_dot(nmat, la2))
    e = [e_all[:, u * GLA_DK:(u + 1) * GLA_DK] for u in range(n)]

    row = lax.broadcasted_iota(jnp.int32, qs[0].shape, 0)
    a = [jnp.zeros((C, C), F32) for _ in range(n)]
    for lvl, s in enumerate(GLA_LEVELS):
        upper = (row & s) != 0
        mask = masks_ref[lvl]
        for u in range(n):
            t = (jnp.where(upper, qs[u], ks[u]) * e[u][(2 + lvl) * C:(3 + lvl) * C]).astype(BF16)
            a[u] = a[u] + lax.dot_general(t, t, NT_DIMS, preferred_element_type=F32) * mask
    eye = (lax.broadcasted_iota(jnp.int32, (C, C), 0) == lax.broadcasted_iota(jnp.int32, (C, C), 1))
    o_intra, upd = [], []
    for u in range(n):
        a_u = jnp.where(eye, jnp.sum(qs[u] * ks[u], axis=-1, keepdims=True), a[u])
        o_intra.append(_dot(a_u.astype(BF16), vs[u]))
        k_dec = (ks[u] * e[u][C:2 * C]).astype(BF16)
        upd.append(lax.dot_general(vs[u], k_dec, TN_DIMS, preferred_element_type=F32))
    outs = []
    for u in range(n):
        eb = e[u][:C]
        outs.append(o_intra[u] + lax.dot_general((qs[u] * eb).astype(BF16), st.astype(BF16), NT_DIMS,
                                                 preferred_element_type=F32))
        st = st * eb[C - 1:C, :] + upd[u]
    return outs, st


def _gla_rec_kernel(q_ref, k_ref, v_ref, la_ref, sr_ref, g_ref, nmat_ref, masks_ref, o_ref, st_ref, *,
                    ts, group):
    @pl.when(pl.program_id(2) == 0)
    def _():
        st_ref[...] = jnp.zeros_like(st_ref)

    g = g_ref[...]
    nmat = nmat_ref[...]
    span = group * GLA_CHUNK

    def body(c, _):
        base = pl.multiple_of(c * span, span)
        rows = [pl.ds(base + u * GLA_CHUNK, GLA_CHUNK) for u in range(group)]
        outs, st = _gla_group([q_ref[r, :].astype(F32) for r in rows], [k_ref[r, :].astype(F32) for r in rows],
                              [v_ref[r, :] for r in rows], [la_ref[r, :] for r in rows], st_ref[...],
                              nmat, masks_ref)
        st_ref[...] = st
        for r, o in zip(rows, outs):
            o_ref[r, :] = (_rms(o, g) * sr_ref[r, :].astype(F32)).astype(o_ref.dtype)
        return 0

    lax.fori_loop(0, ts // span, body, 0)


def _gla_rec(q, k, v, la, sr, g_out, nmat, masks, *, batch, seq, ts, group):
    T = q.shape[0]
    ns = seq // ts
    blk = lambda w: pl.BlockSpec((ts, w), lambda b, h, i: (b * ns + i, h))
    return pl.pallas_call(
        functools.partial(_gla_rec_kernel, ts=ts, group=group),
        grid=(batch, GLA_HEADS, ns),
        in_specs=[blk(GLA_DK), blk(GLA_DK), blk(GLA_DV), blk(GLA_DK), blk(GLA_DV),
                  _const_spec(g_out.shape), _const_spec(nmat.shape), _const_spec(masks.shape)],
        out_specs=blk(GLA_DV),
        out_shape=jax.ShapeDtypeStruct((T, GLA_HEADS * GLA_DV), BF16),
        scratch_shapes=[pltpu.VMEM((GLA_DV, GLA_DK), F32)],
        compiler_params=_params("parallel", "parallel", "arbitrary"),
        name="gla_rec",
    )(q, k, v, la, sr, g_out, nmat, masks)


def _mem_kv_kernel(mem_ref, g_ref, w_ref, gk_ref, seg_ref, k_ref, v_ref):
    kv = _dot(_rms(mem_ref[...], g_ref[...]).astype(BF16), w_ref[...])
    nk = MEM_HEADS * MEM_DQK
    k = kv[:, :nk]
    ss = _dot_split(k * k, seg_ref[...]) * (1.0 / MEM_DQK)
    k_ref[...] = (k * lax.rsqrt(ss + EPS) * gk_ref[...]).astype(BF16)
    v_ref[...] = kv[:, nk:].astype(BF16)


def _mem_kv(mem2, g_mem, w, gk, seg, *, batch, mem_len):
    D = mem2.shape[1]
    nk, nv = MEM_HEADS * MEM_DQK, MEM_HEADS * MEM_DV
    row = lambda w_: pl.BlockSpec((mem_len, w_), lambda b: (b, 0))
    return pl.pallas_call(
        _mem_kv_kernel,
        grid=(batch,),
        in_specs=[row(D), _const_spec(g_mem.shape), _const_spec(w.shape), _const_spec(gk.shape),
                  _const_spec(seg.shape)],
        out_specs=[row(nk), row(nv)],
        out_shape=[jax.ShapeDtypeStruct((batch * mem_len, nk), BF16),
                   jax.ShapeDtypeStruct((batch * mem_len, nv), BF16)],
        compiler_params=_params("parallel"),
        name="mem_kv",
    )(mem2, g_mem, w, gk, seg)


def _merge_kernel(x_ref, hb_ref, omla_ref, ogla_ref, km_ref, vm_ref, wqm_ref, gqm_ref, seg_ref,
                  wgate_ref, bgate_ref, wo_ref, o_ref):
    hb = hb_ref[...]
    D = x_ref.shape[1]
    qm = _dot(hb, wqm_ref[...])
    ss = _dot_split(qm * qm, seg_ref[...]) * (1.0 / MEM_DQK)
    qn = qm * lax.rsqrt(ss + EPS) * (gqm_ref[...] * float(MEM_DQK ** -0.5))
    head_of_lane = lax.broadcasted_iota(jnp.int32, qn.shape, 1) // MEM_DQK
    km = km_ref[...]

    def gate(j):
        z = _dot(hb, wgate_ref[:, j * D:(j + 1) * D]) + bgate_ref[:, j * D:(j + 1) * D]
        return 1.0 / (1.0 + jnp.exp(-z))

    y = gate(0) * omla_ref[...].astype(F32) + gate(1) * ogla_ref[...].astype(F32)
    g_mem = gate(2)
    parts = []
    for h in range(MEM_HEADS):
        qh = jnp.where(head_of_lane == h, qn, 0.0).astype(BF16)
        s = lax.dot_general(qh, km, NT_DIMS, preferred_element_type=F32)
        p = jnp.exp(s - jnp.max(s, axis=-1, keepdims=True))
        o_h = _dot(p.astype(BF16), vm_ref[:, h * MEM_DV:(h + 1) * MEM_DV])
        parts.append(o_h / jnp.sum(p, axis=-1, keepdims=True))
    y = y + g_mem * jnp.concatenate(parts, axis=-1)
    o_ref[...] = x_ref[...] + _dot(y.astype(BF16), wo_ref[...])


def _merge(x2, hb, o_mla, o_gla, km, vm, wqm, gqm, seg, wgate, bgate, wo, *, seq, mem_len, tm):
    T, D = x2.shape
    per_batch = seq // tm
    row = lambda w: pl.BlockSpec((tm, w), lambda i: (i, 0))
    mem_blk = lambda w: pl.BlockSpec((mem_len, w), lambda i: (i // per_batch, 0))
    return pl.pallas_call(
        _merge_kernel,
        grid=(T // tm,),
        in_specs=[row(D), row(D), row(D), row(D), mem_blk(km.shape[1]), mem_blk(vm.shape[1])]
        + [_const_spec(w.shape) for w in (wqm, gqm, seg, wgate, bgate, wo)],
        out_specs=row(D),
        out_shape=jax.ShapeDtypeStruct((T, D), F32),
        compiler_params=_params("parallel"),
        name="merge",
    )(x2, hb, o_mla, o_gla, km, vm, wqm, gqm, seg, wgate, bgate, wo)


def _ffn_kernel(x_ref, g_ref, wup_ref, wdn_ref, o_ref, *, ff_chunk):
    x = x_ref[...]
    hb = _rms(x, g_ref[...]).astype(BF16)
    acc = x
    for c in range(wup_ref.shape[1] // ff_chunk):
        u = jnp.maximum(_dot(hb, wup_ref[:, c * ff_chunk:(c + 1) * ff_chunk]), 0.0)
        acc = acc + _dot((u * u).astype(BF16), wdn_ref[c * ff_chunk:(c + 1) * ff_chunk, :])
    o_ref[...] = acc


def _ffn(x1, g_ffn, wup, wdn, *, tm, ff_chunk):
    T, D = x1.shape
    row = pl.BlockSpec((tm, D), lambda i: (i, 0))
    return pl.pallas_call(
        functools.partial(_ffn_kernel, ff_chunk=ff_chunk),
        grid=(T // tm,),
        in_specs=[row, _const_spec(g_ffn.shape), _const_spec(wup.shape), _const_spec(wdn.shape)],
        out_specs=row,
        out_shape=jax.ShapeDtypeStruct((T, D), F32),
        compiler_params=_params("parallel"),
        name="ffn",
    )(x1, g_ffn, wup, wdn)


def _pad_cols(w, n):
    return jnp.pad(w, ((0, 0), (0, n - w.shape[-1])))


def _layer(x, mem, positions, g_mix, w_in, b_gate, g_ckv, w_ukv, g_q_nope, g_k_nope, g_q_rope, g_k_rope,
           w_gla_gate, b_gla_gate, g_gla_out, g_mem, w_mem_kv, g_q_mem, g_k_mem, w_o, g_ffn, w_up, w_down):
    B, S, D = x.shape
    M = mem.shape[1]
    T = B * S
    rank = g_ckv.shape[0]
    row = lambda g: g.reshape(1, -1).astype(F32)

    sizes = (MLA_HEADS * (MLA_NOPE + MLA_ROPE), rank, MLA_ROPE, GLA_HEADS * GLA_DK, GLA_HEADS * GLA_DK,
             GLA_HEADS * GLA_DV, GLA_GATE_RANK, GLA_HEADS * GLA_DV, MEM_HEADS * MEM_DQK, N_BRANCHES * D)
    offs = np.concatenate([[0], np.cumsum(sizes)])
    (w_q, w_ckv, w_kr, w_gq, w_gk, w_gv, w_ga, w_gr, w_qm, w_gate) = [
        w_in[:, offs[i]:offs[i + 1]] for i in range(len(sizes))]

    wq = jnp.pad(w_q.reshape(D, MLA_HEADS, MLA_NOPE + MLA_ROPE),
                 ((0, 0), (0, 0), (0, MLA_HEAD_PAD - MLA_NOPE - MLA_ROPE))).reshape(D, -1).astype(BF16)
    wckr = jnp.concatenate([w_ckv, _pad_cols(w_kr, LANES)], axis=1).astype(BF16)
    w_ukv3 = w_ukv.reshape(rank, MLA_HEADS, MLA_NOPE + MLA_V)
    wuk = w_ukv3[:, :, :MLA_NOPE].reshape(rank, -1).astype(BF16)
    wuv = w_ukv3[:, :, MLA_NOPE:].reshape(rank, -1).astype(BF16)

    inv = 1.0 / (ROPE_THETA ** (jnp.arange(0, MLA_ROPE, 2, dtype=F32) / MLA_ROPE))
    ang = positions.astype(F32).reshape(T, 1) * inv
    cos, sin = jnp.cos(ang), jnp.sin(ang)
    cos_t = _pad_cols(jnp.concatenate([cos, cos], axis=1), LANES)
    sin_t = _pad_cols(jnp.concatenate([-sin, sin], axis=1), LANES)

    x2 = x.reshape(T, D)
    hb, q, k, v = _mla_prep(
        x2, row(g_mix), wq, wckr, wuk, wuv, cos_t, sin_t, row(g_q_nope), _pad_cols(row(g_q_rope), LANES),
        row(g_ckv), row(g_k_nope), _pad_cols(row(g_k_rope), LANES), tm=min(256, S))
    tq = min(512, S)
    qt = q.reshape(T // tq, tq, MLA_HEADS, MLA_HEAD_PAD).transpose(2, 0, 3, 1)
    vt = v.reshape(T // tq, tq, MLA_HEADS, MLA_V).transpose(2, 0, 3, 1)
    o_mla = _mla_attn(qt, k, vt, batch=B, seq=S, tq=tq)

    nmat = jnp.asarray(np.tile(_gla_decay_matrix(), (1, 2)), BF16)
    masks = jnp.asarray(_gla_level_masks())
    wg = jnp.pad(w_gla_gate, ((0, LANES - GLA_GATE_RANK), (0, 0))).astype(BF16)
    qg, kg, vg, la, sr = _gla_proj(hb, w_gq.astype(BF16), w_gk.astype(BF16), w_gv.astype(BF16),
                                   _pad_cols(w_ga, LANES).astype(BF16), w_gr.astype(BF16), wg,
                                   row(b_gla_gate), tm=min(512, S))
    o_gla = _gla_rec(qg, kg, vg, la, sr, row(g_gla_out), nmat, masks, batch=B, seq=S, ts=min(512, S), group=8)

    nqk = MEM_HEADS * MEM_DQK
    seg = jnp.asarray(np.kron(np.eye(MEM_HEADS), np.ones((MEM_DQK, MEM_DQK))).astype(np.float32), BF16)
    km, vm = _mem_kv(mem.reshape(B * M, D), row(g_mem), w_mem_kv.astype(BF16),
                     jnp.tile(row(g_k_mem), (1, MEM_HEADS)), seg, batch=B, mem_len=M)
    x1 = _merge(x2, hb, o_mla, o_gla, km, vm, w_qm.astype(BF16), jnp.tile(row(g_q_mem), (1, MEM_HEADS)), seg,
                w_gate.astype(BF16), row(b_gate), w_o.astype(BF16), seq=S, mem_len=M, tm=min(256, S))
    out = _ffn(x1, row(g_ffn), w_up.astype(BF16), w_down.astype(BF16), tm=min(256, S), ff_chunk=1024)
    return out.reshape(B, S, D)


def kernel(x, mem, positions, g_mix, w_in, b_gate, g_ckv, w_ukv, g_q_nope, g_k_nope, g_q_rope, g_k_rope,
           w_gla_gate, b_gla_gate, g_gla_out, g_mem, w_mem_kv, g_q_mem, g_k_mem, w_o, g_ffn, w_up, w_down):
    for l in range(g_mix.shape[0]):
        x = _layer(x, mem, positions, g_mix[l], w_in[l], b_gate[l], g_ckv[l], w_ukv[l], g_q_nope[l],
                   g_k_nope[l], g_q_rope[l], g_k_rope[l], w_gla_gate[l], b_gla_gate[l], g_gla_out[l],
                   g_mem[l], w_mem_kv[l], g_q_mem[l], g_k_mem[l], w_o[l], g_ffn[l], w_up[l], w_down[l])
    return x
```

```python
import functools

import jax
import jax.numpy as jnp
import numpy as np
from jax import lax
from jax.experimental import pallas as pl
from jax.experimental.pallas import tpu as pltpu

F32 = jnp.float32
BF16 = jnp.bfloat16

EPS = 1e-6
ROPE_THETA = 10000.0
LANES = 128

MLA_HEADS, MLA_NOPE, MLA_ROPE, MLA_V = 8, 128, 64, 128
MLA_HEAD_PAD = 2 * LANES
MLA_VT_ROWS = MLA_V + 16
GLA_HEADS, GLA_DK, GLA_DV = 4, 128, 256
GLA_GATE_RANK, GLA_GATE_NORMALIZER, GLA_CHUNK, GLA_SUB = 16, 16.0, 64, 16
MEM_HEADS, MEM_DQK, MEM_DV = 4, 64, 256
N_BRANCHES = 3

VMEM_LIMIT = 48 * 1024 * 1024

NT_DIMS = (((1,), (1,)), ((), ()))
TN_DIMS = (((0,), (0,)), ((), ()))


def _params(*sem):
    return pltpu.CompilerParams(dimension_semantics=sem, vmem_limit_bytes=VMEM_LIMIT)


def _rms(t, g, n=None):
    n = t.shape[-1] if n is None else n
    ss = jnp.sum(t * t, axis=-1, keepdims=True) * (1.0 / n)
    return t * lax.rsqrt(ss + EPS) * g


def _dot(a, b):
    return jnp.dot(a, b, preferred_element_type=F32)


def _split(a_f32):
    hi = a_f32.astype(BF16)
    return hi, (a_f32 - hi.astype(F32)).astype(BF16)


def _dot_split(a_f32, b_bf16):
    hi, lo = _split(a_f32)
    return _dot(hi, b_bf16) + _dot(lo, b_bf16)


def _const_spec(shape):
    return pl.BlockSpec(shape, lambda *_: (0,) * len(shape))


def _mla_prep_kernel(x_ref, gmix_ref, wq_ref, wckr_ref, wuk_ref, wuv_ref, cos_ref, sin_ref,
                     gqn_ref, gqr_ref, gckv_ref, gkn_ref, gkr_ref,
                     hb_ref, qt_ref, k_ref, vt_ref, *, q_scale):
    x = x_ref[...]
    hb = _rms(x, gmix_ref[...]).astype(BF16)
    hb_ref[...] = hb
    pq = _dot(hb, wq_ref[...])
    pc = _dot(hb, wckr_ref[...])
    cos = cos_ref[...]
    sin = sin_ref[...]
    lane = lax.broadcasted_iota(jnp.int32, cos.shape, 1)
    half = MLA_ROPE // 2

    def rope(t):
        rot = jnp.where(lane < half, pltpu.roll(t, LANES - half, 1), pltpu.roll(t, half, 1))
        return t * cos + rot * sin

    gqn = gqn_ref[...] * q_scale
    gqr = gqr_ref[...]
    for h in range(MLA_HEADS):
        c0 = h * MLA_HEAD_PAD
        a = _rms(pq[:, c0:c0 + LANES], gqn)
        b = rope(_rms(pq[:, c0 + LANES:c0 + 2 * LANES], gqr, MLA_ROPE)) * q_scale
        qt_ref[h, :LANES, :] = a.T.astype(BF16)
        qt_ref[h, LANES:, :] = b.T.astype(BF16)

    rank = gckv_ref.shape[-1]
    ckv = _rms(pc[:, :rank], gckv_ref[...]).astype(BF16)
    kpe = rope(_rms(pc[:, rank:rank + LANES], gkr_ref[...], MLA_ROPE)).astype(BF16)
    kn = _dot(ckv, wuk_ref[...])
    v = _dot(ckv, wuv_ref[...])
    pad_row = lax.broadcasted_iota(jnp.int32, (MLA_VT_ROWS - MLA_V, v.shape[0]), 0)
    ones_rows = jnp.where(pad_row == 0, 1.0, 0.0).astype(BF16)
    gkn = gkn_ref[...]
    for h in range(MLA_HEADS):
        c0 = h * MLA_HEAD_PAD
        vt_ref[h, :MLA_V, :] = v[:, h * MLA_V:(h + 1) * MLA_V].T.astype(BF16)
        vt_ref[h, MLA_V:, :] = ones_rows
        k_ref[:, c0:c0 + LANES] = _rms(kn[:, h * LANES:(h + 1) * LANES], gkn).astype(BF16)
        k_ref[:, c0 + LANES:c0 + 2 * LANES] = kpe


def _mla_prep(x2, g_mix, wq, wckr, wuk, wuv, cos_t, sin_t, gqn, gqr, gckv, gkn, gkr, *, tm):
    T, D = x2.shape
    HP = MLA_HEADS * MLA_HEAD_PAD
    row = lambda w: pl.BlockSpec((tm, w), lambda i: (i, 0))
    return pl.pallas_call(
        functools.partial(_mla_prep_kernel, q_scale=float((MLA_NOPE + MLA_ROPE) ** -0.5 * np.log2(np.e))),
        grid=(T // tm,),
        in_specs=[row(D), _const_spec(g_mix.shape), _const_spec(wq.shape), _const_spec(wckr.shape),
                  _const_spec(wuk.shape), _const_spec(wuv.shape), row(LANES), row(LANES),
                  _const_spec(gqn.shape), _const_spec(gqr.shape), _const_spec(gckv.shape),
                  _const_spec(gkn.shape), _const_spec(gkr.shape)],
        out_specs=[row(D), pl.BlockSpec((MLA_HEADS, None, MLA_HEAD_PAD, tm), lambda i: (0, i, 0, 0)), row(HP),
                   pl.BlockSpec((MLA_HEADS, None, MLA_VT_ROWS, tm), lambda i: (0, i, 0, 0))],
        out_shape=[jax.ShapeDtypeStruct((T, D), BF16),
                   jax.ShapeDtypeStruct((MLA_HEADS, T // tm, MLA_HEAD_PAD, tm), BF16),
                   jax.ShapeDtypeStruct((T, HP), BF16),
                   jax.ShapeDtypeStruct((MLA_HEADS, T // tm, MLA_VT_ROWS, tm), BF16)],
        compiler_params=_params("parallel"),
        name="mla_prep",
    )(x2, g_mix, wq, wckr, wuk, wuv, cos_t, sin_t, gqn, gqr, gckv, gkn, gkr)


def _mla_attn_kernel(qt_ref, qt_next_ref, k_ref, vt_ref, o_ref, s_a, s_b, s_c, max_a, max_b, max_c,
                     m_ref, acc_ref, *, tq, heads, nq, chunks):
    i = pl.program_id(2)
    hs = range(heads)

    rows = tq // chunks

    def step(prod=None, cons=None):
        if cons is not None:
            jc, sc_ref, mc_ref, diagonal = cons
            m_new, alpha, pv = [], [], [None] * heads
            for h in hs:
                if diagonal:
                    s = sc_ref[h]
                    kpos = lax.broadcasted_iota(jnp.int32, s.shape, 0)
                    qpos = lax.broadcasted_iota(jnp.int32, s.shape, 1)
                    s_max = jnp.max(jnp.where(kpos <= qpos, s, -1e30), axis=0, keepdims=True)
                else:
                    s_max = mc_ref[h]
                m = m_ref[h]
                m_new.append(jnp.maximum(m, s_max))
                alpha.append(jnp.exp2(m - m_new[h]))
                m_ref[h] = m_new[h]
        if prod is not None:
            jp, q_ref, sp_ref, mp_ref = prod
            r0 = pl.multiple_of(jp * tq, tq)
            p_max = [None] * heads
        for c in range(chunks):
            lo = c * rows
            if prod is not None:
                for h in hs:
                    s = _dot(k_ref[pl.ds(r0 + lo, rows), h * MLA_HEAD_PAD:(h + 1) * MLA_HEAD_PAD], q_ref[h])
                    sp_ref[h, lo:lo + rows, :] = s
                    c_max = jnp.max(s, axis=0, keepdims=True)
                    p_max[h] = c_max if c == 0 else jnp.maximum(p_max[h], c_max)
            if cons is not None:
                for h in hs:
                    s = sc_ref[h, lo:lo + rows, :]
                    if diagonal:
                        kpos = lo + lax.broadcasted_iota(jnp.int32, s.shape, 0)
                        qpos = lax.broadcasted_iota(jnp.int32, s.shape, 1)
                        s = jnp.where(kpos <= qpos, s, -1e30)
                    p = jnp.exp2(s - m_new[h])
                    c_pv = _dot(vt_ref[h, jc, :, lo:lo + rows], p.astype(BF16))
                    pv[h] = c_pv if c == 0 else pv[h] + c_pv
        if prod is not None:
            for h in hs:
                mp_ref[h] = p_max[h]
        if cons is not None:
            for h in hs:
                acc_ref[h] = alpha[h] * acc_ref[h] + pv[h]

    ahead = (jnp.minimum(i + 1, nq - 1), qt_next_ref, s_c, max_c)

    def write_out():
        for h in hs:
            acc = acc_ref[h]
            o_ref[:, h * MLA_V:(h + 1) * MLA_V] = (acc[:MLA_V] / acc[MLA_V:MLA_V + 1]).T.astype(o_ref.dtype)

    @pl.when(i == 0)
    def _():
        step(prod=(0, qt_ref, s_c, max_c))

    m_ref[...] = jnp.full(m_ref.shape, -1e30, F32)
    acc_ref[...] = jnp.zeros(acc_ref.shape, F32)

    @pl.when(i == 0)
    def _():
        step(cons=(0, s_c, max_c, True))
        step(prod=ahead)
        write_out()

    @pl.when(i > 0)
    def _():
        step(prod=(0, qt_ref, s_a, max_a), cons=(i, s_c, max_c, True))

        def pair(t, _):
            j = 2 * t
            step(prod=(j + 1, qt_ref, s_b, max_b), cons=(j, s_a, max_a, False))
            step(prod=(j + 2, qt_ref, s_a, max_a), cons=(j + 1, s_b, max_b, False))
            return 0

        lax.fori_loop(0, (i - 1) // 2, pair, 0)

        @pl.when(i % 2 == 1)
        def _():
            step(prod=ahead, cons=(i - 1, s_a, max_a, False))
            write_out()

        @pl.when(i % 2 == 0)
        def _():
            step(prod=(i - 1, qt_ref, s_b, max_b), cons=(i - 2, s_a, max_a, False))
            step(prod=ahead, cons=(i - 1, s_b, max_b, False))
            write_out()


def _mla_attn(qt, k, vt, *, batch, seq, tq, heads, chunks):
    T = k.shape[0]
    nq = seq // tq
    q_spec = lambda step: pl.BlockSpec(
        (heads, None, MLA_HEAD_PAD, tq), lambda b, h, i: (h, b * nq + jnp.minimum(i + step, nq - 1), 0, 0))
    score = pltpu.VMEM((heads, tq, tq), F32)
    stat = pltpu.VMEM((heads, 1, tq), F32)
    return pl.pallas_call(
        functools.partial(_mla_attn_kernel, tq=tq, heads=heads, nq=nq, chunks=chunks),
        grid=(batch, MLA_HEADS // heads, nq),
        in_specs=[q_spec(0), q_spec(1),
                  pl.BlockSpec((seq, heads * MLA_HEAD_PAD), lambda b, h, i: (b, h)),
                  pl.BlockSpec((heads, nq, MLA_VT_ROWS, tq), lambda b, h, i: (h, b, 0, 0))],
        out_specs=pl.BlockSpec((tq, heads * MLA_V), lambda b, h, i: (b * nq + i, h)),
        out_shape=jax.ShapeDtypeStruct((T, MLA_HEADS * MLA_V), BF16),
        scratch_shapes=[score, score, score, stat, stat, stat, stat,
                        pltpu.VMEM((heads, MLA_VT_ROWS, tq), F32)],
        compiler_params=_params("parallel", "parallel", "arbitrary"),
        name="mla_attn",
    )(qt, qt, k, vt)


def _gla_proj_kernel(hb_ref, wq_ref, wk_ref, wv_ref, wa_ref, wr_ref, wg_ref, bg_ref,
                     q_ref, k_ref, v_ref, la_ref, sr_ref):
    hb = hb_ref[...]
    q_ref[...] = (_dot(hb, wq_ref[...]) * float(GLA_DK ** -0.5)).astype(BF16)
    k_ref[...] = _dot(hb, wk_ref[...]).astype(BF16)
    v_ref[...] = _dot(hb, wv_ref[...]).astype(BF16)
    a = _dot(hb, wa_ref[...])
    z = _dot(a.astype(BF16), wg_ref[...]) + bg_ref[...]
    log_sig = jnp.minimum(z, 0.0) - jnp.log1p(jnp.exp(-jnp.abs(z)))
    la_ref[...] = log_sig * (1.0 / GLA_GATE_NORMALIZER)
    r = _dot(hb, wr_ref[...])
    sr_ref[...] = (r / (1.0 + jnp.exp(-r))).astype(BF16)


def _gla_proj(hb, wq, wk, wv, wa, wr, wg, bg, *, tm):
    T, D = hb.shape
    nk, nv = wq.shape[1], wv.shape[1]
    row = lambda w: pl.BlockSpec((tm, w), lambda i: (i, 0))
    return pl.pallas_call(
        _gla_proj_kernel,
        grid=(T // tm,),
        in_specs=[row(D)] + [_const_spec(w.shape) for w in (wq, wk, wv, wa, wr, wg, bg)],
        out_specs=[row(nk), row(nk), row(nv), row(nk), row(nv)],
        out_shape=[jax.ShapeDtypeStruct((T, nk), BF16), jax.ShapeDtypeStruct((T, nk), BF16),
                   jax.ShapeDtypeStruct((T, nv), BF16), jax.ShapeDtypeStruct((T, nk), F32),
                   jax.ShapeDtypeStruct((T, nv), BF16)],
        compiler_params=_params("parallel"),
        name="gla_proj",
    )(hb, wq, wk, wv, wa, wr, wg, bg)


GLA_LEVELS = tuple(GLA_CHUNK >> (i + 1) for i in range(GLA_CHUNK.bit_length() - 1))


def _gla_decay_matrix():
    r = np.arange(GLA_CHUNK)
    groups = [r[None, :] <= r[:, None], r[None, :] > r[:, None]]
    for s in GLA_LEVELS:
        mid = (r // (2 * s)) * (2 * s) + s
        upper = (r & s) != 0
        up = (r[None, :] > mid[:, None]) & (r[None, :] <= r[:, None])
        lo = (r[None, :] > r[:, None]) & (r[None, :] <= mid[:, None])
        groups.append(np.where(upper[:, None], up, lo))
    return np.concatenate(groups, axis=0).astype(np.float32)


def _gla_level_masks():
    r = np.arange(GLA_CHUNK)
    x = r[:, None] ^ r[None, :]
    lower = r[None, :] < r[:, None]
    return np.stack([(lower & (x >= s) & (x < 2 * s)) for s in GLA_LEVELS]).astype(np.float32)


def _gla_group(qs, ks, vs, las, st, nmat, masks_ref):
    C, n = GLA_CHUNK, len(qs)
    la2 = jnp.concatenate([jnp.concatenate(_split(la), axis=0) for la in las], axis=1)
    e_all = jnp.exp(_dot(nmat, la2))
    e = [e_all[:, u * GLA_DK:(u + 1) * GLA_DK] for u in range(n)]

    row = lax.broadcasted_iota(jnp.int32, qs[0].shape, 0)
    a = [jnp.zeros((C, C), F32) for _ in range(n)]
    for lvl, s in enumerate(GLA_LEVELS):
        upper = (row & s) != 0
        mask = masks_ref[lvl]
        for u in range(n):
            t = (jnp.where(upper, qs[u], ks[u]) * e[u][(2 + lvl) * C:(3 + lvl) * C]).astype(BF16)
            a[u] = a[u] + lax.dot_general(t, t, NT_DIMS, preferred_element_type=F32) * mask
    eye = (lax.broadcasted_iota(jnp.int32, (C, C), 0) == lax.broadcasted_iota(jnp.int32, (C, C), 1))
    o_intra, upd = [], []
    for u in range(n):
        a_u = jnp.where(eye, jnp.sum(qs[u] * ks[u], axis=-1, keepdims=True), a[u])
        o_intra.append(_dot(a_u.astype(BF16), vs[u]))
        k_dec = (ks[u] * e[u][C:2 * C]).astype(BF16)
        upd.append(lax.dot_general(vs[u], k_dec, TN_DIMS, preferred_element_type=F32))
    outs = []
    for u in range(n):
        eb = e[u][:C]
        outs.append(o_intra[u] + lax.dot_general((qs[u] * eb).astype(BF16), st.astype(BF16), NT_DIMS,
                                                 preferred_element_type=F32))
        st = st * eb[C - 1:C, :] + upd[u]
    return outs, st


def _gla_rec_kernel(q_ref, k_ref, v_ref, la_ref, sr_ref, g_ref, nmat_ref, masks_ref, o_ref, st_ref, *,
                    ts, group):
    @pl.when(pl.program_id(2) == 0)
    def _():
        st_ref[...] = jnp.zeros_like(st_ref)

    g = g_ref[...]
    nmat = nmat_ref[...]
    span = group * GLA_CHUNK

    def body(c, _):
        base = pl.multiple_of(c * span, span)
        rows = [pl.ds(base + u * GLA_CHUNK, GLA_CHUNK) for u in range(group)]
        outs, st = _gla_group([q_ref[r, :].astype(F32) for r in rows], [k_ref[r, :].astype(F32) for r in rows],
                              [v_ref[r, :] for r in rows], [la_ref[r, :] for r in rows], st_ref[...],
                              nmat, masks_ref)
        st_ref[...] = st
        for r, o in zip(rows, outs):
            o_ref[r, :] = (_rms(o, g) * sr_ref[r, :].astype(F32)).astype(o_ref.dtype)
        return 0

    lax.fori_loop(0, ts // span, body, 0)


def _gla_rec(q, k, v, la, sr, g_out, nmat, masks, *, batch, seq, ts, group):
    T = q.shape[0]
    ns = seq // ts
    blk = lambda w: pl.BlockSpec((ts, w), lambda b, h, i: (b * ns + i, h))
    return pl.pallas_call(
        functools.partial(_gla_rec_kernel, ts=ts, group=group),
        grid=(batch, GLA_HEADS, ns),
        in_specs=[blk(GLA_DK), blk(GLA_DK), blk(GLA_DV), blk(GLA_DK), blk(GLA_DV),
                  _const_spec(g_out.shape), _const_spec(nmat.shape), _const_spec(masks.shape)],
        out_specs=blk(GLA_DV),
        out_shape=jax.ShapeDtypeStruct((T, GLA_HEADS * GLA_DV), BF16),
        scratch_shapes=[pltpu.VMEM((GLA_DV, GLA_DK), F32)],
        compiler_params=_params("parallel", "parallel", "arbitrary"),
        name="gla_rec",
    )(q, k, v, la, sr, g_out, nmat, masks)


def _mem_kv_kernel(mem_ref, g_ref, w_ref, gk_ref, seg_ref, k_ref, v_ref):
    kv = _dot(_rms(mem_ref[...], g_ref[...]).astype(BF16), w_ref[...])
    nk = MEM_HEADS * MEM_DQK
    k = kv[:, :nk]
    ss = _dot_split(k * k, seg_ref[...]) * (1.0 / MEM_DQK)
    k_ref[...] = (k * lax.rsqrt(ss + EPS) * gk_ref[...]).astype(BF16)
    v_ref[...] = kv[:, nk:].astype(BF16)


def _mem_kv(mem2, g_mem, w, gk, seg, *, batch, mem_len):
    D = mem2.shape[1]
    nk, nv = MEM_HEADS * MEM_DQK, MEM_HEADS * MEM_DV
    row = lambda w_: pl.BlockSpec((mem_len, w_), lambda b: (b, 0))
    return pl.pallas_call(
        _mem_kv_kernel,
        grid=(batch,),
        in_specs=[row(D), _const_spec(g_mem.shape), _const_spec(w.shape), _const_spec(gk.shape),
                  _const_spec(seg.shape)],
        out_specs=[row(nk), row(nv)],
        out_shape=[jax.ShapeDtypeStruct((batch * mem_len, nk), BF16),
                   jax.ShapeDtypeStruct((batch * mem_len, nv), BF16)],
        compiler_params=_params("parallel"),
        name="mem_kv",
    )(mem2, g_mem, w, gk, seg)


def _merge_kernel(x_ref, hb_ref, omla_ref, ogla_ref, km_ref, vm_ref, wqm_ref, gqm_ref, seg_ref,
                  wgate_ref, bgate_ref, wo_ref, o_ref):
    hb = hb_ref[...]
    D = x_ref.shape[1]
    qm = _dot(hb, wqm_ref[...])
    ss = _dot_split(qm * qm, seg_ref[...]) * (1.0 / MEM_DQK)
    qn = qm * lax.rsqrt(ss + EPS) * (gqm_ref[...] * float(MEM_DQK ** -0.5))
    head_of_lane = lax.broadcasted_iota(jnp.int32, qn.shape, 1) // MEM_DQK
    km = km_ref[...]

    def gate(j):
        z = _dot(hb, wgate_ref[:, j * D:(j + 1) * D]) + bgate_ref[:, j * D:(j + 1) * D]
        return 1.0 / (1.0 + jnp.exp(-z))

    y = gate(0) * omla_ref[...].astype(F32) + gate(1) * ogla_ref[...].astype(F32)
    g_mem = gate(2)
    parts = []
    for h in range(MEM_HEADS):
        qh = jnp.where(head_of_lane == h, qn, 0.0).astype(BF16)
        s = lax.dot_general(qh, km, NT_DIMS, preferred_element_type=F32)
        p = jnp.exp(s - jnp.max(s, axis=-1, keepdims=True))
        o_h = _dot(p.astype(BF16), vm_ref[:, h * MEM_DV:(h + 1) * MEM_DV])
        parts.append(o_h / jnp.sum(p, axis=-1, keepdims=True))
    y = y + g_mem * jnp.concatenate(parts, axis=-1)
    o_ref[...] = x_ref[...] + _dot(y.astype(BF16), wo_ref[...])


def _merge(x2, hb, o_mla, o_gla, km, vm, wqm, gqm, seg, wgate, bgate, wo, *, seq, mem_len, tm):
    T, D = x2.shape
    per_batch = seq // tm
    row = lambda w: pl.BlockSpec((tm, w), lambda i: (i, 0))
    mem_blk = lambda w: pl.BlockSpec((mem_len, w), lambda i: (i // per_batch, 0))
    return pl.pallas_call(
        _merge_kernel,
        grid=(T // tm,),
        in_specs=[row(D), row(D), row(D), row(D), mem_blk(km.shape[1]), mem_blk(vm.shape[1])]
        + [_const_spec(w.shape) for w in (wqm, gqm, seg, wgate, bgate, wo)],
        out_specs=row(D),
        out_shape=jax.ShapeDtypeStruct((T, D), F32),
        compiler_params=_params("parallel"),
        name="merge",
    )(x2, hb, o_mla, o_gla, km, vm, wqm, gqm, seg, wgate, bgate, wo)


def _ffn_kernel(x_ref, g_ref, wup_ref, wdn_ref, o_ref, *, ff_chunk):
    x = x_ref[...]
    hb = _rms(x, g_ref[...]).astype(BF16)
    acc = x
    for c in range(wup_ref.shape[1] // ff_chunk):
        u = jnp.maximum(_dot(hb, wup_ref[:, c * ff_chunk:(c + 1) * ff_chunk]), 0.0)
        acc = acc + _dot((u * u).astype(BF16), wdn_ref[c * ff_chunk:(c + 1) * ff_chunk, :])
    o_ref[...] = acc


def _ffn(x1, g_ffn, wup, wdn, *, tm, ff_chunk):
    T, D = x1.shape
    row = pl.BlockSpec((tm, D), lambda i: (i, 0))
    return pl.pallas_call(
        functools.partial(_ffn_kernel, ff_chunk=ff_chunk),
        grid=(T // tm,),
        in_specs=[row, _const_spec(g_ffn.shape), _const_spec(wup.shape), _const_spec(wdn.shape)],
        out_specs=row,
        out_shape=jax.ShapeDtypeStruct((T, D), F32),
        compiler_params=_params("parallel"),
        name="ffn",
    )(x1, g_ffn, wup, wdn)


def _pad_cols(w, n):
    return jnp.pad(w, ((0, 0), (0, n - w.shape[-1])))


def _layer(x, mem, positions, g_mix, w_in, b_gate, g_ckv, w_ukv, g_q_nope, g_k_nope, g_q_rope, g_k_rope,
           w_gla_gate, b_gla_gate, g_gla_out, g_mem, w_mem_kv, g_q_mem, g_k_mem, w_o, g_ffn, w_up, w_down):
    B, S, D = x.shape
    M = mem.shape[1]
    T = B * S
    rank = g_ckv.shape[0]
    row = lambda g: g.reshape(1, -1).astype(F32)

    sizes = (MLA_HEADS * (MLA_NOPE + MLA_ROPE), rank, MLA_ROPE, GLA_HEADS * GLA_DK, GLA_HEADS * GLA_DK,
             GLA_HEADS * GLA_DV, GLA_GATE_RANK, GLA_HEADS * GLA_DV, MEM_HEADS * MEM_DQK, N_BRANCHES * D)
    offs = np.concatenate([[0], np.cumsum(sizes)])
    (w_q, w_ckv, w_kr, w_gq, w_gk, w_gv, w_ga, w_gr, w_qm, w_gate) = [
        w_in[:, offs[i]:offs[i + 1]] for i in range(len(sizes))]

    wq = jnp.pad(w_q.reshape(D, MLA_HEADS, MLA_NOPE + MLA_ROPE),
                 ((0, 0), (0, 0), (0, MLA_HEAD_PAD - MLA_NOPE - MLA_ROPE))).reshape(D, -1).astype(BF16)
    wckr = jnp.concatenate([w_ckv, _pad_cols(w_kr, LANES)], axis=1).astype(BF16)
    w_ukv3 = w_ukv.reshape(rank, MLA_HEADS, MLA_NOPE + MLA_V)
    wuk = w_ukv3[:, :, :MLA_NOPE].reshape(rank, -1).astype(BF16)
    wuv = w_ukv3[:, :, MLA_NOPE:].reshape(rank, -1).astype(BF16)

    inv = 1.0 / (ROPE_THETA ** (jnp.arange(0, MLA_ROPE, 2, dtype=F32) / MLA_ROPE))
    ang = positions.astype(F32).reshape(T, 1) * inv
    cos, sin = jnp.cos(ang), jnp.sin(ang)
    cos_t = _pad_cols(jnp.concatenate([cos, cos], axis=1), LANES)
    sin_t = _pad_cols(jnp.concatenate([-sin, sin], axis=1), LANES)

    x2 = x.reshape(T, D)
    tq = min(512, S)
    hb, qt, k, vt = _mla_prep(
        x2, row(g_mix), wq, wckr, wuk, wuv, cos_t, sin_t, row(g_q_nope), _pad_cols(row(g_q_rope), LANES),
        row(g_ckv), row(g_k_nope), _pad_cols(row(g_k_rope), LANES), tm=tq)
    o_mla = _mla_attn(qt, k, vt, batch=B, seq=S, tq=tq, heads=2, chunks=1)

    nmat = jnp.asarray(np.tile(_gla_decay_matrix(), (1, 2)), BF16)
    masks = jnp.asarray(_gla_level_masks())
    wg = jnp.pad(w_gla_gate, ((0, LANES - GLA_GATE_RANK), (0, 0))).astype(BF16)
    qg, kg, vg, la, sr = _gla_proj(hb, w_gq.astype(BF16), w_gk.astype(BF16), w_gv.astype(BF16),
                                   _pad_cols(w_ga, LANES).astype(BF16), w_gr.astype(BF16), wg,
                                   row(b_gla_gate), tm=min(512, S))
    o_gla = _gla_rec(qg, kg, vg, la, sr, row(g_gla_out), nmat, masks, batch=B, seq=S, ts=min(512, S), group=8)

    nqk = MEM_HEADS * MEM_DQK
    seg = jnp.asarray(np.kron(np.eye(MEM_HEADS), np.ones((MEM_DQK, MEM_DQK))).astype(np.float32), BF16)
    km, vm = _mem_kv(mem.reshape(B * M, D), row(g_mem), w_mem_kv.astype(BF16),
                     jnp.tile(row(g_k_mem), (1, MEM_HEADS)), seg, batch=B, mem_len=M)
    x1 = _merge(x2, hb, o_mla, o_gla, km, vm, w_qm.astype(BF16), jnp.tile(row(g_q_mem), (1, MEM_HEADS)), seg,
                w_gate.astype(BF16), row(b_gate), w_o.astype(BF16), seq=S, mem_len=M, tm=min(256, S))
    out = _ffn(x1, row(g_ffn), w_up.astype(BF16), w_down.astype(BF16), tm=min(256, S), ff_chunk=1024)
    return out.reshape(B, S, D)


def kernel(x, mem, positions, g_mix, w_in, b_gate, g_ckv, w_ukv, g_q_nope, g_k_nope, g_q_rope, g_k_rope,
           w_gla_gate, b_gla_gate, g_gla_out, g_mem, w_mem_kv, g_q_mem, g_k_mem, w_o, g_ffn, w_up, w_down):
    for l in range(g_mix.shape[0]):
        x = _layer(x, mem, positions, g_mix[l], w_in[l], b_gate[l], g_ckv[l], w_ukv[l], g_q_nope[l],
                   g_k_nope[l], g_q_rope[l], g_k_rope[l], w_gla_gate[l], b_gla_gate[l], g_gla_out[l],
                   g_mem[l], w_mem_kv[l], g_q_mem[l], g_k_mem[l], w_o[l], g_ffn[l], w_up[l], w_down[l])
    return x
```

```python
import functools

import jax
import jax.numpy as jnp
import numpy as np
from jax import lax
from jax.experimental import pallas as pl
from jax.experimental.pallas import tpu as pltpu

F32 = jnp.float32
BF16 = jnp.bfloat16

EPS = 1e-6
ROPE_THETA = 10000.0
LANES = 128

MLA_HEADS, MLA_NOPE, MLA_ROPE, MLA_V = 8, 128, 64, 128
MLA_HEAD_PAD = 2 * LANES
MLA_VT_ROWS = MLA_V + 16
GLA_HEADS, GLA_DK, GLA_DV = 4, 128, 256
GLA_GATE_RANK, GLA_GATE_NORMALIZER, GLA_CHUNK, GLA_SUB = 16, 16.0, 64, 16
MEM_HEADS, MEM_DQK, MEM_DV = 4, 64, 256
N_BRANCHES = 3

VMEM_LIMIT = 48 * 1024 * 1024

NT_DIMS = (((1,), (1,)), ((), ()))
TN_DIMS = (((0,), (0,)), ((), ()))


def _params(*sem):
    return pltpu.CompilerParams(dimension_semantics=sem, vmem_limit_bytes=VMEM_LIMIT)


def _rms(t, g, n=None):
    n = t.shape[-1] if n is None else n
    ss = jnp.sum(t * t, axis=-1, keepdims=True) * (1.0 / n)
    return t * lax.rsqrt(ss + EPS) * g


def _dot(a, b):
    return jnp.dot(a, b, preferred_element_type=F32)


def _split(a_f32):
    hi = a_f32.astype(BF16)
    return hi, (a_f32 - hi.astype(F32)).astype(BF16)


def _dot_split(a_f32, b_bf16):
    hi, lo = _split(a_f32)
    return _dot(hi, b_bf16) + _dot(lo, b_bf16)


def _const_spec(shape):
    return pl.BlockSpec(shape, lambda *_: (0,) * len(shape))


def _rms_rows(t, g):
    ss = jnp.sum(t * t, axis=0, keepdims=True) * (1.0 / t.shape[0])
    return t * lax.rsqrt(ss + EPS) * g


def _mla_prep_kernel(x_ref, gmix_ref, wqt_ref, wckvt_ref, wuvt_ref, wckr_ref, wuk_ref,
                     cost_ref, sint_ref, cos_ref, sin_ref, gqnt_ref, gqrt_ref, gckvt_ref,
                     gckv_ref, gkn_ref, gkr_ref, hb_ref, qt_ref, k_ref, vt_ref):
    h = _rms(x_ref[...], gmix_ref[...])
    hb = h.astype(BF16)
    hb_ref[...] = hb
    hbt = h.T.astype(BF16)
    tm = hbt.shape[1]
    half = MLA_ROPE // 2
    dq = MLA_NOPE + MLA_ROPE

    pc = _dot(hb, wckr_ref[...])
    rank = gckv_ref.shape[-1]
    ckv = _rms(pc[:, :rank], gckv_ref[...]).astype(BF16)
    kn = _dot(ckv, wuk_ref[...])
    kr = _rms(pc[:, rank:rank + LANES], gkr_ref[...], MLA_ROPE)
    lane = lax.broadcasted_iota(jnp.int32, kr.shape, 1)
    rot = jnp.where(lane < half, pltpu.roll(kr, LANES - half, 1), pltpu.roll(kr, half, 1))
    kpe = (kr * cos_ref[...] + rot * sin_ref[...]).astype(BF16)
    gkn = gkn_ref[...]
    for hd in range(MLA_HEADS):
        c0 = hd * MLA_HEAD_PAD
        k_ref[:, c0:c0 + LANES] = _rms(kn[:, hd * LANES:(hd + 1) * LANES], gkn).astype(BF16)
        k_ref[:, c0 + LANES:c0 + 2 * LANES] = kpe

    ckvt = _rms_rows(_dot(wckvt_ref[...], hbt), gckvt_ref[...]).astype(BF16)
    vt = _dot(wuvt_ref[...], ckvt)
    pad_row = lax.broadcasted_iota(jnp.int32, (MLA_VT_ROWS - MLA_V, tm), 0)
    ones_rows = jnp.where(pad_row == 0, 1.0, 0.0).astype(BF16)
    for hd in range(MLA_HEADS):
        vt_ref[hd, :MLA_V, :] = vt[hd * MLA_V:(hd + 1) * MLA_V].astype(BF16)
        vt_ref[hd, MLA_V:, :] = ones_rows

    cos, sin = cost_ref[...], sint_ref[...]
    gqn, gqr = gqnt_ref[...], gqrt_ref[...]
    zeros = jnp.zeros((MLA_HEAD_PAD - dq, tm), BF16)
    for hd in range(MLA_HEADS):
        t = _dot(wqt_ref[hd * dq:(hd + 1) * dq, :], hbt)
        qt_ref[hd, :MLA_NOPE, :] = _rms_rows(t[:MLA_NOPE], gqn).astype(BF16)
        r = _rms_rows(t[MLA_NOPE:], gqr)
        t1, t2 = r[:half], r[half:]
        qt_ref[hd, MLA_NOPE:MLA_NOPE + half, :] = (t1 * cos - t2 * sin).astype(BF16)
        qt_ref[hd, MLA_NOPE + half:dq, :] = (t1 * sin + t2 * cos).astype(BF16)
        qt_ref[hd, dq:, :] = zeros


def _mla_prep(x2, g_mix, wqt, wckvt, wuvt, wckr, wuk, cos_tt, sin_tt, cos_t, sin_t, gqnt, gqrt, gckvt,
              gckv, gkn, gkr, *, tm):
    T, D = x2.shape
    HP = MLA_HEADS * MLA_HEAD_PAD
    row = lambda w: pl.BlockSpec((tm, w), lambda i: (i, 0))
    col = lambda r: pl.BlockSpec((r, tm), lambda i: (0, i))
    consts = (g_mix, wqt, wckvt, wuvt, wckr, wuk)
    gains = (gqnt, gqrt, gckvt, gckv, gkn, gkr)
    return pl.pallas_call(
        _mla_prep_kernel,
        grid=(T // tm,),
        in_specs=[row(D)] + [_const_spec(c.shape) for c in consts]
        + [col(cos_tt.shape[0]), col(sin_tt.shape[0]), row(LANES), row(LANES)]
        + [_const_spec(g.shape) for g in gains],
        out_specs=[row(D), pl.BlockSpec((MLA_HEADS, None, MLA_HEAD_PAD, tm), lambda i: (0, i, 0, 0)), row(HP),
                   pl.BlockSpec((MLA_HEADS, None, MLA_VT_ROWS, tm), lambda i: (0, i, 0, 0))],
        out_shape=[jax.ShapeDtypeStruct((T, D), BF16),
                   jax.ShapeDtypeStruct((MLA_HEADS, T // tm, MLA_HEAD_PAD, tm), BF16),
                   jax.ShapeDtypeStruct((T, HP), BF16),
                   jax.ShapeDtypeStruct((MLA_HEADS, T // tm, MLA_VT_ROWS, tm), BF16)],
        compiler_params=_params("parallel"),
        name="mla_prep",
    )(x2, *consts, cos_tt, sin_tt, cos_t, sin_t, *gains)


def _mla_attn_kernel(qt_ref, qt_next_ref, k_ref, vt_ref, o_ref, s_a, s_b, s_c, max_a, max_b, max_c,
                     m_ref, acc_ref, *, tq, heads, nq, chunks):
    i = pl.program_id(2)
    hs = range(heads)

    rows = tq // chunks

    def step(prod=None, cons=None):
        if cons is not None:
            jc, sc_ref, mc_ref, diagonal = cons
            m_new, alpha, pv = [], [], [None] * heads
            for h in hs:
                if diagonal:
                    s = sc_ref[h]
                    kpos = lax.broadcasted_iota(jnp.int32, s.shape, 0)
                    qpos = lax.broadcasted_iota(jnp.int32, s.shape, 1)
                    s_max = jnp.max(jnp.where(kpos <= qpos, s, -1e30), axis=0, keepdims=True)
                else:
                    s_max = mc_ref[h]
                m = m_ref[h]
                m_new.append(jnp.maximum(m, s_max))
                alpha.append(jnp.exp2(m - m_new[h]))
                m_ref[h] = m_new[h]
        if prod is not None:
            jp, q_ref, sp_ref, mp_ref = prod
            r0 = pl.multiple_of(jp * tq, tq)
            p_max = [None] * heads
        for c in range(chunks):
            lo = c * rows
            if prod is not None:
                for h in hs:
                    s = _dot(k_ref[pl.ds(r0 + lo, rows), h * MLA_HEAD_PAD:(h + 1) * MLA_HEAD_PAD], q_ref[h])
                    sp_ref[h, lo:lo + rows, :] = s
                    c_max = jnp.max(s, axis=0, keepdims=True)
                    p_max[h] = c_max if c == 0 else jnp.maximum(p_max[h], c_max)
            if cons is not None:
                for h in hs:
                    s = sc_ref[h, lo:lo + rows, :]
                    if diagonal:
                        kpos = lo + lax.broadcasted_iota(jnp.int32, s.shape, 0)
                        qpos = lax.broadcasted_iota(jnp.int32, s.shape, 1)
                        s = jnp.where(kpos <= qpos, s, -1e30)
                    p = jnp.exp2(s - m_new[h])
                    c_pv = _dot(vt_ref[h, jc, :, lo:lo + rows], p.astype(BF16))
                    pv[h] = c_pv if c == 0 else pv[h] + c_pv
        if prod is not None:
            for h in hs:
                mp_ref[h] = p_max[h]
        if cons is not None:
            for h in hs:
                acc_ref[h] = alpha[h] * acc_ref[h] + pv[h]

    ahead = (jnp.minimum(i + 1, nq - 1), qt_next_ref, s_c, max_c)

    def write_out():
        for h in hs:
            acc = acc_ref[h]
            o_ref[:, h * MLA_V:(h + 1) * MLA_V] = (acc[:MLA_V] / acc[MLA_V:MLA_V + 1]).T.astype(o_ref.dtype)

    @pl.when(i == 0)
    def _():
        step(prod=(0, qt_ref, s_c, max_c))

    m_ref[...] = jnp.full(m_ref.shape, -1e30, F32)
    acc_ref[...] = jnp.zeros(acc_ref.shape, F32)

    @pl.when(i == 0)
    def _():
        step(cons=(0, s_c, max_c, True))
        step(prod=ahead)
        write_out()

    @pl.when(i > 0)
    def _():
        step(prod=(0, qt_ref, s_a, max_a), cons=(i, s_c, max_c, True))

        def pair(t, _):
            j = 2 * t
            step(prod=(j + 1, qt_ref, s_b, max_b), cons=(j, s_a, max_a, False))
            step(prod=(j + 2, qt_ref, s_a, max_a), cons=(j + 1, s_b, max_b, False))
            return 0

        lax.fori_loop(0, (i - 1) // 2, pair, 0)

        @pl.when(i % 2 == 1)
        def _():
            step(prod=ahead, cons=(i - 1, s_a, max_a, False))
            write_out()

        @pl.when(i % 2 == 0)
        def _():
            step(prod=(i - 1, qt_ref, s_b, max_b), cons=(i - 2, s_a, max_a, False))
            step(prod=ahead, cons=(i - 1, s_b, max_b, False))
            write_out()


def _mla_attn(qt, k, vt, *, batch, seq, tq, heads, chunks):
    T = k.shape[0]
    nq = seq // tq
    q_spec = lambda step: pl.BlockSpec(
        (heads, None, MLA_HEAD_PAD, tq), lambda b, h, i: (h, b * nq + jnp.minimum(i + step, nq - 1), 0, 0))
    score = pltpu.VMEM((heads, tq, tq), F32)
    stat = pltpu.VMEM((heads, 1, tq), F32)
    return pl.pallas_call(
        functools.partial(_mla_attn_kernel, tq=tq, heads=heads, nq=nq, chunks=chunks),
        grid=(batch, MLA_HEADS // heads, nq),
        in_specs=[q_spec(0), q_spec(1),
                  pl.BlockSpec((seq, heads * MLA_HEAD_PAD), lambda b, h, i: (b, h)),
                  pl.BlockSpec((heads, nq, MLA_VT_ROWS, tq), lambda b, h, i: (h, b, 0, 0))],
        out_specs=pl.BlockSpec((tq, heads * MLA_V), lambda b, h, i: (b * nq + i, h)),
        out_shape=jax.ShapeDtypeStruct((T, MLA_HEADS * MLA_V), BF16),
        scratch_shapes=[score, score, score, stat, stat, stat, stat,
                        pltpu.VMEM((heads, MLA_VT_ROWS, tq), F32)],
        compiler_params=_params("parallel", "parallel", "arbitrary"),
        name="mla_attn",
    )(qt, qt, k, vt)


def _gla_proj_kernel(hb_ref, wq_ref, wk_ref, wv_ref, wa_ref, wr_ref, wg_ref, bg_ref,
                     q_ref, k_ref, v_ref, la_ref, sr_ref):
    hb = hb_ref[...]
    q_ref[...] = (_dot(hb, wq_ref[...]) * float(GLA_DK ** -0.5)).astype(BF16)
    k_ref[...] = _dot(hb, wk_ref[...]).astype(BF16)
    v_ref[...] = _dot(hb, wv_ref[...]).astype(BF16)
    a = _dot(hb, wa_ref[...])
    z = _dot(a.astype(BF16), wg_ref[...]) + bg_ref[...]
    log_sig = jnp.minimum(z, 0.0) - jnp.log1p(jnp.exp(-jnp.abs(z)))
    la_ref[...] = log_sig * (1.0 / GLA_GATE_NORMALIZER)
    r = _dot(hb, wr_ref[...])
    sr_ref[...] = (r / (1.0 + jnp.exp(-r))).astype(BF16)


def _gla_proj(hb, wq, wk, wv, wa, wr, wg, bg, *, tm):
    T, D = hb.shape
    nk, nv = wq.shape[1], wv.shape[1]
    row = lambda w: pl.BlockSpec((tm, w), lambda i: (i, 0))
    return pl.pallas_call(
        _gla_proj_kernel,
        grid=(T // tm,),
        in_specs=[row(D)] + [_const_spec(w.shape) for w in (wq, wk, wv, wa, wr, wg, bg)],
        out_specs=[row(nk), row(nk), row(nv), row(nk), row(nv)],
        out_shape=[jax.ShapeDtypeStruct((T, nk), BF16), jax.ShapeDtypeStruct((T, nk), BF16),
                   jax.ShapeDtypeStruct((T, nv), BF16), jax.ShapeDtypeStruct((T, nk), F32),
                   jax.ShapeDtypeStruct((T, nv), BF16)],
        compiler_params=_params("parallel"),
        name="gla_proj",
    )(hb, wq, wk, wv, wa, wr, wg, bg)


GLA_LEVELS = tuple(GLA_CHUNK >> (i + 1) for i in range(GLA_CHUNK.bit_length() - 1))


def _gla_decay_matrix():
    r = np.arange(GLA_CHUNK)
    groups = [r[None, :] <= r[:, None], r[None, :] > r[:, None]]
    for s in GLA_LEVELS:
        mid = (r // (2 * s)) * (2 * s) + s
        upper = (r & s) != 0
        up = (r[None, :] > mid[:, None]) & (r[None, :] <= r[:, None])
        lo = (r[None, :] > r[:, None]) & (r[None, :] <= mid[:, None])
        groups.append(np.where(upper[:, None], up, lo))
    return np.concatenate(groups, axis=0).astype(np.float32)


def _gla_level_masks():
    r = np.arange(GLA_CHUNK)
    x = r[:, None] ^ r[None, :]
    lower = r[None, :] < r[:, None]
    return np.stack([(lower & (x >= s) & (x < 2 * s)) for s in GLA_LEVELS]).astype(np.float32)


def _gla_group(qs, ks, vs, las, st, nmat, masks_ref):
    C, n = GLA_CHUNK, len(qs)
    la2 = jnp.concatenate([jnp.concatenate(_split(la), axis=0) for la in las], axis=1)
    e_all = jnp.exp(_dot(nmat, la2))
    e = [e_all[:, u * GLA_DK:(u + 1) * GLA_DK] for u in range(n)]

    row = lax.broadcasted_iota(jnp.int32, qs[0].shape, 0)
    a = [jnp.zeros((C, C), F32) for _ in range(n)]
    for lvl, s in enumerate(GLA_LEVELS):
        upper = (row & s) != 0
        mask = masks_ref[lvl]
        for u in range(n):
            t = (jnp.where(upper, qs[u], ks[u]) * e[u][(2 + lvl) * C:(3 + lvl) * C]).astype(BF16)
            a[u] = a[u] + lax.dot_general(t, t, NT_DIMS, preferred_element_type=F32) * mask
    eye = (lax.broadcasted_iota(jnp.int32, (C, C), 0) == lax.broadcasted_iota(jnp.int32, (C, C), 1))
    o_intra, upd = [], []
    for u in range(n):
        a_u = jnp.where(eye, jnp.sum(qs[u] * ks[u], axis=-1, keepdims=True), a[u])
        o_intra.append(_dot(a_u.astype(BF16), vs[u]))
        k_dec = (ks[u] * e[u][C:2 * C]).astype(BF16)
        upd.append(lax.dot_general(vs[u], k_dec, TN_DIMS, preferred_element_type=F32))
    outs = []
    for u in range(n):
        eb = e[u][:C]
        outs.append(o_intra[u] + lax.dot_general((qs[u] * eb).astype(BF16), st.astype(BF16), NT_DIMS,
                                                 preferred_element_type=F32))
        st = st * eb[C - 1:C, :] + upd[u]
    return outs, st


def _gla_rec_kernel(q_ref, k_ref, v_ref, la_ref, sr_ref, g_ref, nmat_ref, masks_ref, o_ref, st_ref, *,
                    ts, group):
    @pl.when(pl.program_id(2) == 0)
    def _():
        st_ref[...] = jnp.zeros_like(st_ref)

    g = g_ref[...]
    nmat = nmat_ref[...]
    span = group * GLA_CHUNK

    def body(c, _):
        base = pl.multiple_of(c * span, span)
        rows = [pl.ds(base + u * GLA_CHUNK, GLA_CHUNK) for u in range(group)]
        outs, st = _gla_group([q_ref[r, :].astype(F32) for r in rows], [k_ref[r, :].astype(F32) for r in rows],
                              [v_ref[r, :] for r in rows], [la_ref[r, :] for r in rows], st_ref[...],
                              nmat, masks_ref)
        st_ref[...] = st
        for r, o in zip(rows, outs):
            o_ref[r, :] = (_rms(o, g) * sr_ref[r, :].astype(F32)).astype(o_ref.dtype)
        return 0

    lax.fori_loop(0, ts // span, body, 0)


def _gla_rec(q, k, v, la, sr, g_out, nmat, masks, *, batch, seq, ts, group):
    T = q.shape[0]
    ns = seq // ts
    blk = lambda w: pl.BlockSpec((ts, w), lambda b, h, i: (b * ns + i, h))
    return pl.pallas_call(
        functools.partial(_gla_rec_kernel, ts=ts, group=group),
        grid=(batch, GLA_HEADS, ns),
        in_specs=[blk(GLA_DK), blk(GLA_DK), blk(GLA_DV), blk(GLA_DK), blk(GLA_DV),
                  _const_spec(g_out.shape), _const_spec(nmat.shape), _const_spec(masks.shape)],
        out_specs=blk(GLA_DV),
        out_shape=jax.ShapeDtypeStruct((T, GLA_HEADS * GLA_DV), BF16),
        scratch_shapes=[pltpu.VMEM((GLA_DV, GLA_DK), F32)],
        compiler_params=_params("parallel", "parallel", "arbitrary"),
        name="gla_rec",
    )(q, k, v, la, sr, g_out, nmat, masks)


def _mem_kv_kernel(mem_ref, g_ref, w_ref, gk_ref, seg_ref, k_ref, v_ref):
    kv = _dot(_rms(mem_ref[...], g_ref[...]).astype(BF16), w_ref[...])
    nk = MEM_HEADS * MEM_DQK
    k = kv[:, :nk]
    ss = _dot_split(k * k, seg_ref[...]) * (1.0 / MEM_DQK)
    k_ref[...] = (k * lax.rsqrt(ss + EPS) * gk_ref[...]).astype(BF16)
    v_ref[...] = kv[:, nk:].astype(BF16)


def _mem_kv(mem2, g_mem, w, gk, seg, *, batch, mem_len):
    D = mem2.shape[1]
    nk, nv = MEM_HEADS * MEM_DQK, MEM_HEADS * MEM_DV
    row = lambda w_: pl.BlockSpec((mem_len, w_), lambda b: (b, 0))
    return pl.pallas_call(
        _mem_kv_kernel,
        grid=(batch,),
        in_specs=[row(D), _const_spec(g_mem.shape), _const_spec(w.shape), _const_spec(gk.shape),
                  _const_spec(seg.shape)],
        out_specs=[row(nk), row(nv)],
        out_shape=[jax.ShapeDtypeStruct((batch * mem_len, nk), BF16),
                   jax.ShapeDtypeStruct((batch * mem_len, nv), BF16)],
        compiler_params=_params("parallel"),
        name="mem_kv",
    )(mem2, g_mem, w, gk, seg)


def _merge_kernel(x_ref, hb_ref, omla_ref, ogla_ref, km_ref, vm_ref, wqm_ref, gqm_ref, seg_ref,
                  wgate_ref, bgate_ref, wo_ref, o_ref):
    hb = hb_ref[...]
    D = x_ref.shape[1]
    qm = _dot(hb, wqm_ref[...])
    ss = _dot_split(qm * qm, seg_ref[...]) * (1.0 / MEM_DQK)
    qn = qm * lax.rsqrt(ss + EPS) * (gqm_ref[...] * float(MEM_DQK ** -0.5))
    head_of_lane = lax.broadcasted_iota(jnp.int32, qn.shape, 1) // MEM_DQK
    km = km_ref[...]

    def gate(j):
        z = _dot(hb, wgate_ref[:, j * D:(j + 1) * D]) + bgate_ref[:, j * D:(j + 1) * D]
        return 1.0 / (1.0 + jnp.exp(-z))

    y = gate(0) * omla_ref[...].astype(F32) + gate(1) * ogla_ref[...].astype(F32)
    g_mem = gate(2)
    parts = []
    for h in range(MEM_HEADS):
        qh = jnp.where(head_of_lane == h, qn, 0.0).astype(BF16)
        s = lax.dot_general(qh, km, NT_DIMS, preferred_element_type=F32)
        p = jnp.exp(s - jnp.max(s, axis=-1, keepdims=True))
        o_h = _dot(p.astype(BF16), vm_ref[:, h * MEM_DV:(h + 1) * MEM_DV])
        parts.append(o_h / jnp.sum(p, axis=-1, keepdims=True))
    y = y + g_mem * jnp.concatenate(parts, axis=-1)
    o_ref[...] = x_ref[...] + _dot(y.astype(BF16), wo_ref[...])


def _merge(x2, hb, o_mla, o_gla, km, vm, wqm, gqm, seg, wgate, bgate, wo, *, seq, mem_len, tm):
    T, D = x2.shape
    per_batch = seq // tm
    row = lambda w: pl.BlockSpec((tm, w), lambda i: (i, 0))
    mem_blk = lambda w: pl.BlockSpec((mem_len, w), lambda i: (i // per_batch, 0))
    return pl.pallas_call(
        _merge_kernel,
        grid=(T // tm,),
        in_specs=[row(D), row(D), row(D), row(D), mem_blk(km.shape[1]), mem_blk(vm.shape[1])]
        + [_const_spec(w.shape) for w in (wqm, gqm, seg, wgate, bgate, wo)],
        out_specs=row(D),
        out_shape=jax.ShapeDtypeStruct((T, D), F32),
        compiler_params=_params("parallel"),
        name="merge",
    )(x2, hb, o_mla, o_gla, km, vm, wqm, gqm, seg, wgate, bgate, wo)


def _ffn_kernel(x_ref, g_ref, wup_ref, wdn_ref, o_ref, *, ff_chunk):
    x = x_ref[...]
    hb = _rms(x, g_ref[...]).astype(BF16)
    acc = x
    for c in range(wup_ref.shape[1] // ff_chunk):
        u = jnp.maximum(_dot(hb, wup_ref[:, c * ff_chunk:(c + 1) * ff_chunk]), 0.0)
        acc = acc + _dot((u * u).astype(BF16), wdn_ref[c * ff_chunk:(c + 1) * ff_chunk, :])
    o_ref[...] = acc


def _ffn(x1, g_ffn, wup, wdn, *, tm, ff_chunk):
    T, D = x1.shape
    row = pl.BlockSpec((tm, D), lambda i: (i, 0))
    return pl.pallas_call(
        functools.partial(_ffn_kernel, ff_chunk=ff_chunk),
        grid=(T // tm,),
        in_specs=[row, _const_spec(g_ffn.shape), _const_spec(wup.shape), _const_spec(wdn.shape)],
        out_specs=row,
        out_shape=jax.ShapeDtypeStruct((T, D), F32),
        compiler_params=_params("parallel"),
        name="ffn",
    )(x1, g_ffn, wup, wdn)


def _pad_cols(w, n):
    return jnp.pad(w, ((0, 0), (0, n - w.shape[-1])))


def _layer(x, mem, positions, g_mix, w_in, b_gate, g_ckv, w_ukv, g_q_nope, g_k_nope, g_q_rope, g_k_rope,
           w_gla_gate, b_gla_gate, g_gla_out, g_mem, w_mem_kv, g_q_mem, g_k_mem, w_o, g_ffn, w_up, w_down):
    B, S, D = x.shape
    M = mem.shape[1]
    T = B * S
    rank = g_ckv.shape[0]
    row = lambda g: g.reshape(1, -1).astype(F32)

    sizes = (MLA_HEADS * (MLA_NOPE + MLA_ROPE), rank, MLA_ROPE, GLA_HEADS * GLA_DK, GLA_HEADS * GLA_DK,
             GLA_HEADS * GLA_DV, GLA_GATE_RANK, GLA_HEADS * GLA_DV, MEM_HEADS * MEM_DQK, N_BRANCHES * D)
    offs = np.concatenate([[0], np.cumsum(sizes)])
    (w_q, w_ckv, w_kr, w_gq, w_gk, w_gv, w_ga, w_gr, w_qm, w_gate) = [
        w_in[:, offs[i]:offs[i + 1]] for i in range(len(sizes))]

    tq = min(512, S)
    wckr = jnp.concatenate([w_ckv, _pad_cols(w_kr, LANES)], axis=1).astype(BF16)
    w_ukv3 = w_ukv.reshape(rank, MLA_HEADS, MLA_NOPE + MLA_V)
    wuk = w_ukv3[:, :, :MLA_NOPE].reshape(rank, -1).astype(BF16)
    wuvt = w_ukv3[:, :, MLA_NOPE:].reshape(rank, -1).T.astype(BF16)
    q_scale = float((MLA_NOPE + MLA_ROPE) ** -0.5 * np.log2(np.e))
    lanes_of = lambda g, scale=1.0: jnp.broadcast_to((g.astype(F32) * scale)[:, None], (g.shape[0], tq))

    inv = 1.0 / (ROPE_THETA ** (jnp.arange(0, MLA_ROPE, 2, dtype=F32) / MLA_ROPE))
    ang = positions.astype(F32).reshape(T, 1) * inv
    cos, sin = jnp.cos(ang), jnp.sin(ang)
    cos_t = _pad_cols(jnp.concatenate([cos, cos], axis=1), LANES)
    sin_t = _pad_cols(jnp.concatenate([-sin, sin], axis=1), LANES)

    x2 = x.reshape(T, D)
    hb, qt, k, vt = _mla_prep(
        x2, row(g_mix), w_q.T.astype(BF16), w_ckv.T.astype(BF16), wuvt, wckr, wuk, cos.T, sin.T, cos_t, sin_t,
        lanes_of(g_q_nope, q_scale), lanes_of(g_q_rope, q_scale), lanes_of(g_ckv),
        row(g_ckv), row(g_k_nope), _pad_cols(row(g_k_rope), LANES), tm=tq)
    o_mla = _mla_attn(qt, k, vt, batch=B, seq=S, tq=tq, heads=2, chunks=1)

    nmat = jnp.asarray(np.tile(_gla_decay_matrix(), (1, 2)), BF16)
    masks = jnp.asarray(_gla_level_masks())
    wg = jnp.pad(w_gla_gate, ((0, LANES - GLA_GATE_RANK), (0, 0))).astype(BF16)
    qg, kg, vg, la, sr = _gla_proj(hb, w_gq.astype(BF16), w_gk.astype(BF16), w_gv.astype(BF16),
                                   _pad_cols(w_ga, LANES).astype(BF16), w_gr.astype(BF16), wg,
                                   row(b_gla_gate), tm=min(512, S))
    o_gla = _gla_rec(qg, kg, vg, la, sr, row(g_gla_out), nmat, masks, batch=B, seq=S, ts=min(512, S), group=8)

    nqk = MEM_HEADS * MEM_DQK
    seg = jnp.asarray(np.kron(np.eye(MEM_HEADS), np.ones((MEM_DQK, MEM_DQK))).astype(np.float32), BF16)
    km, vm = _mem_kv(mem.reshape(B * M, D), row(g_mem), w_mem_kv.astype(BF16),
                     jnp.tile(row(g_k_mem), (1, MEM_HEADS)), seg, batch=B, mem_len=M)
    x1 = _merge(x2, hb, o_mla, o_gla, km, vm, w_qm.astype(BF16), jnp.tile(row(g_q_mem), (1, MEM_HEADS)), seg,
                w_gate.astype(BF16), row(b_gate), w_o.astype(BF16), seq=S, mem_len=M, tm=min(512, S))
    out = _ffn(x1, row(g_ffn), w_up.astype(BF16), w_down.astype(BF16), tm=min(512, S), ff_chunk=1024)
    return out.reshape(B, S, D)


def kernel(x, mem, positions, g_mix, w_in, b_gate, g_ckv, w_ukv, g_q_nope, g_k_nope, g_q_rope, g_k_rope,
           w_gla_gate, b_gla_gate, g_gla_out, g_mem, w_mem_kv, g_q_mem, g_k_mem, w_o, g_ffn, w_up, w_down):
    for l in range(g_mix.shape[0]):
        x = _layer(x, mem, positions, g_mix[l], w_in[l], b_gate[l], g_ckv[l], w_ukv[l], g_q_nope[l],
                   g_k_nope[l], g_q_rope[l], g_k_rope[l], w_gla_gate[l], b_gla_gate[l], g_gla_out[l],
                   g_mem[l], w_mem_kv[l], g_q_mem[l], g_k_mem[l], w_o[l], g_ffn[l], w_up[l], w_down[l])
    return x
```

```python
import functools

import jax
import jax.numpy as jnp
import numpy as np
from jax import lax
from jax.experimental import pallas as pl
from jax.experimental.pallas import tpu as pltpu

F32 = jnp.float32
BF16 = jnp.bfloat16

EPS = 1e-6
ROPE_THETA = 10000.0
LANES = 128

MLA_HEADS, MLA_NOPE, MLA_ROPE, MLA_V = 8, 128, 64, 128
MLA_HEAD_PAD = 2 * LANES
MLA_VT_ROWS = MLA_V + 16
GLA_HEADS, GLA_DK, GLA_DV = 4, 128, 256
GLA_GATE_RANK, GLA_GATE_NORMALIZER, GLA_CHUNK, GLA_SUB = 16, 16.0, 64, 16
MEM_HEADS, MEM_DQK, MEM_DV = 4, 64, 256
N_BRANCHES = 3

VMEM_LIMIT = 48 * 1024 * 1024

NT_DIMS = (((1,), (1,)), ((), ()))
TN_DIMS = (((0,), (0,)), ((), ()))


def _params(*sem):
    return pltpu.CompilerParams(dimension_semantics=sem, vmem_limit_bytes=VMEM_LIMIT)


def _rms(t, g, n=None):
    n = t.shape[-1] if n is None else n
    ss = jnp.sum(t * t, axis=-1, keepdims=True) * (1.0 / n)
    return t * lax.rsqrt(ss + EPS) * g


def _dot(a, b):
    return jnp.dot(a, b, preferred_element_type=F32)


def _split(a_f32):
    hi = a_f32.astype(BF16)
    return hi, (a_f32 - hi.astype(F32)).astype(BF16)


def _dot_split(a_f32, b_bf16):
    hi, lo = _split(a_f32)
    return _dot(hi, b_bf16) + _dot(lo, b_bf16)


def _const_spec(shape):
    return pl.BlockSpec(shape, lambda *_: (0,) * len(shape))


def _rms_rows(t, g):
    ss = jnp.sum(t * t, axis=0, keepdims=True) * (1.0 / t.shape[0])
    return t * lax.rsqrt(ss + EPS) * g


def _rope_rows(r, cos, sin):
    half = r.shape[0] // 2
    t1, t2 = r[:half], r[half:]
    return t1 * cos - t2 * sin, t1 * sin + t2 * cos


def _mla_prep_kernel(x_ref, gmix_ref, wqt_ref, wlatt_ref, wuvt_ref, wckv_ref, wuk_ref,
                     cost_ref, sint_ref, gqnt_ref, gqrt_ref, gckvt_ref, gkrt_ref,
                     gckv_ref, gkn_ref, hb_ref, qt_ref, k_ref, vt_ref):
    h = _rms(x_ref[...], gmix_ref[...])
    hb = h.astype(BF16)
    hb_ref[...] = hb
    hbt = h.T.astype(BF16)
    tm = hbt.shape[1]
    half = MLA_ROPE // 2
    dq = MLA_NOPE + MLA_ROPE

    cos, sin = cost_ref[...], sint_ref[...]
    rank = gckv_ref.shape[-1]
    latt = _dot(wlatt_ref[...], hbt)

    ckv = _rms(_dot(hb, wckv_ref[...]), gckv_ref[...]).astype(BF16)
    kn = _dot(ckv, wuk_ref[...])
    kr1, kr2 = _rope_rows(_rms_rows(latt[rank:], gkrt_ref[...]), cos, sin)
    kpe = jnp.concatenate([kr1, kr2, jnp.zeros((LANES - MLA_ROPE, tm), F32)], axis=0).T.astype(BF16)
    gkn = gkn_ref[...]
    for hd in range(MLA_HEADS):
        c0 = hd * MLA_HEAD_PAD
        k_ref[:, c0:c0 + LANES] = _rms(kn[:, hd * LANES:(hd + 1) * LANES], gkn).astype(BF16)
        k_ref[:, c0 + LANES:c0 + 2 * LANES] = kpe

    ckvt = _rms_rows(latt[:rank], gckvt_ref[...]).astype(BF16)
    vt = _dot(wuvt_ref[...], ckvt)
    pad_row = lax.broadcasted_iota(jnp.int32, (MLA_VT_ROWS - MLA_V, tm), 0)
    ones_rows = jnp.where(pad_row == 0, 1.0, 0.0).astype(BF16)
    for hd in range(MLA_HEADS):
        vt_ref[hd, :MLA_V, :] = vt[hd * MLA_V:(hd + 1) * MLA_V].astype(BF16)
        vt_ref[hd, MLA_V:, :] = ones_rows

    gqn, gqr = gqnt_ref[...], gqrt_ref[...]
    zeros = jnp.zeros((MLA_HEAD_PAD - dq, tm), BF16)
    for hd in range(MLA_HEADS):
        t = _dot(wqt_ref[hd * dq:(hd + 1) * dq, :], hbt)
        qt_ref[hd, :MLA_NOPE, :] = _rms_rows(t[:MLA_NOPE], gqn).astype(BF16)
        r1, r2 = _rope_rows(_rms_rows(t[MLA_NOPE:], gqr), cos, sin)
        qt_ref[hd, MLA_NOPE:MLA_NOPE + half, :] = r1.astype(BF16)
        qt_ref[hd, MLA_NOPE + half:dq, :] = r2.astype(BF16)
        qt_ref[hd, dq:, :] = zeros


def _mla_prep(x2, g_mix, wqt, wlatt, wuvt, wckv, wuk, cos_tt, sin_tt, gqnt, gqrt, gckvt, gkrt,
              gckv, gkn, *, tm):
    T, D = x2.shape
    HP = MLA_HEADS * MLA_HEAD_PAD
    row = lambda w: pl.BlockSpec((tm, w), lambda i: (i, 0))
    col = lambda r: pl.BlockSpec((r, tm), lambda i: (0, i))
    consts = (g_mix, wqt, wlatt, wuvt, wckv, wuk)
    gains = (gqnt, gqrt, gckvt, gkrt, gckv, gkn)
    return pl.pallas_call(
        _mla_prep_kernel,
        grid=(T // tm,),
        in_specs=[row(D)] + [_const_spec(c.shape) for c in consts]
        + [col(cos_tt.shape[0]), col(sin_tt.shape[0])]
        + [_const_spec(g.shape) for g in gains],
        out_specs=[row(D), pl.BlockSpec((MLA_HEADS, None, MLA_HEAD_PAD, tm), lambda i: (0, i, 0, 0)), row(HP),
                   pl.BlockSpec((MLA_HEADS, None, MLA_VT_ROWS, tm), lambda i: (0, i, 0, 0))],
        out_shape=[jax.ShapeDtypeStruct((T, D), BF16),
                   jax.ShapeDtypeStruct((MLA_HEADS, T // tm, MLA_HEAD_PAD, tm), BF16),
                   jax.ShapeDtypeStruct((T, HP), BF16),
                   jax.ShapeDtypeStruct((MLA_HEADS, T // tm, MLA_VT_ROWS, tm), BF16)],
        compiler_params=_params("parallel"),
        name="mla_prep",
    )(x2, *consts, cos_tt, sin_tt, *gains)


def _mla_attn_kernel(qt_ref, qt_next_ref, k_ref, vt_ref, o_ref, s_a, s_b, s_c, max_a, max_b, max_c,
                     m_ref, acc_ref, *, tq, heads, nq, chunks):
    i = pl.program_id(2)
    hs = range(heads)

    rows = tq // chunks

    def step(prod=None, cons=None):
        if cons is not None:
            jc, sc_ref, mc_ref, diagonal = cons
            m_new, alpha, pv = [], [], [None] * heads
            for h in hs:
                if diagonal:
                    s = sc_ref[h]
                    kpos = lax.broadcasted_iota(jnp.int32, s.shape, 0)
                    qpos = lax.broadcasted_iota(jnp.int32, s.shape, 1)
                    s_max = jnp.max(jnp.where(kpos <= qpos, s, -1e30), axis=0, keepdims=True)
                else:
                    s_max = mc_ref[h]
                m = m_ref[h]
                m_new.append(jnp.maximum(m, s_max))
                alpha.append(jnp.exp2(m - m_new[h]))
                m_ref[h] = m_new[h]
        if prod is not None:
            jp, q_ref, sp_ref, mp_ref = prod
            r0 = pl.multiple_of(jp * tq, tq)
            p_max = [None] * heads
        for c in range(chunks):
            lo = c * rows
            if prod is not None:
                for h in hs:
                    s = _dot(k_ref[pl.ds(r0 + lo, rows), h * MLA_HEAD_PAD:(h + 1) * MLA_HEAD_PAD], q_ref[h])
                    sp_ref[h, lo:lo + rows, :] = s
                    c_max = jnp.max(s, axis=0, keepdims=True)
                    p_max[h] = c_max if c == 0 else jnp.maximum(p_max[h], c_max)
            if cons is not None:
                for h in hs:
                    s = sc_ref[h, lo:lo + rows, :]
                    if diagonal:
                        kpos = lo + lax.broadcasted_iota(jnp.int32, s.shape, 0)
                        qpos = lax.broadcasted_iota(jnp.int32, s.shape, 1)
                        s = jnp.where(kpos <= qpos, s, -1e30)
                    p = jnp.exp2(s - m_new[h])
                    c_pv = _dot(vt_ref[h, jc, :, lo:lo + rows], p.astype(BF16))
                    pv[h] = c_pv if c == 0 else pv[h] + c_pv
        if prod is not None:
            for h in hs:
                mp_ref[h] = p_max[h]
        if cons is not None:
            for h in hs:
                acc_ref[h] = alpha[h] * acc_ref[h] + pv[h]

    ahead = (jnp.minimum(i + 1, nq - 1), qt_next_ref, s_c, max_c)

    def write_out():
        for h in hs:
            acc = acc_ref[h]
            o_ref[:, h * MLA_V:(h + 1) * MLA_V] = (acc[:MLA_V] / acc[MLA_V:MLA_V + 1]).T.astype(o_ref.dtype)

    @pl.when(i == 0)
    def _():
        step(prod=(0, qt_ref, s_c, max_c))

    m_ref[...] = jnp.full(m_ref.shape, -1e30, F32)
    acc_ref[...] = jnp.zeros(acc_ref.shape, F32)

    @pl.when(i == 0)
    def _():
        step(cons=(0, s_c, max_c, True))
        step(prod=ahead)
        write_out()

    @pl.when(i > 0)
    def _():
        step(prod=(0, qt_ref, s_a, max_a), cons=(i, s_c, max_c, True))

        def pair(t, _):
            j = 2 * t
            step(prod=(j + 1, qt_ref, s_b, max_b), cons=(j, s_a, max_a, False))
            step(prod=(j + 2, qt_ref, s_a, max_a), cons=(j + 1, s_b, max_b, False))
            return 0

        lax.fori_loop(0, (i - 1) // 2, pair, 0)

        @pl.when(i % 2 == 1)
        def _():
            step(prod=ahead, cons=(i - 1, s_a, max_a, False))
            write_out()

        @pl.when(i % 2 == 0)
        def _():
            step(prod=(i - 1, qt_ref, s_b, max_b), cons=(i - 2, s_a, max_a, False))
            step(prod=ahead, cons=(i - 1, s_b, max_b, False))
            write_out()


def _mla_attn(qt, k, vt, *, batch, seq, tq, heads, chunks):
    T = k.shape[0]
    nq = seq // tq
    q_spec = lambda step: pl.BlockSpec(
        (heads, None, MLA_HEAD_PAD, tq), lambda b, h, i: (h, b * nq + jnp.minimum(i + step, nq - 1), 0, 0))
    score = pltpu.VMEM((heads, tq, tq), F32)
    stat = pltpu.VMEM((heads, 1, tq), F32)
    return pl.pallas_call(
        functools.partial(_mla_attn_kernel, tq=tq, heads=heads, nq=nq, chunks=chunks),
        grid=(batch, MLA_HEADS // heads, nq),
        in_specs=[q_spec(0), q_spec(1),
                  pl.BlockSpec((seq, heads * MLA_HEAD_PAD), lambda b, h, i: (b, h)),
                  pl.BlockSpec((heads, nq, MLA_VT_ROWS, tq), lambda b, h, i: (h, b, 0, 0))],
        out_specs=pl.BlockSpec((tq, heads * MLA_V), lambda b, h, i: (b * nq + i, h)),
        out_shape=jax.ShapeDtypeStruct((T, MLA_HEADS * MLA_V), BF16),
        scratch_shapes=[score, score, score, stat, stat, stat, stat,
                        pltpu.VMEM((heads, MLA_VT_ROWS, tq), F32)],
        compiler_params=_params("parallel", "parallel", "arbitrary"),
        name="mla_attn",
    )(qt, qt, k, vt)


def _gla_proj_kernel(hb_ref, wq_ref, wk_ref, wv_ref, wa_ref, wr_ref, wg_ref, bg_ref,
                     q_ref, k_ref, v_ref, la_ref, sr_ref):
    hb = hb_ref[...]
    q_ref[...] = (_dot(hb, wq_ref[...]) * float(GLA_DK ** -0.5)).astype(BF16)
    k_ref[...] = _dot(hb, wk_ref[...]).astype(BF16)
    v_ref[...] = _dot(hb, wv_ref[...]).astype(BF16)
    a = _dot(hb, wa_ref[...])
    z = _dot(a.astype(BF16), wg_ref[...]) + bg_ref[...]
    log_sig = jnp.minimum(z, 0.0) - jnp.log1p(jnp.exp(-jnp.abs(z)))
    la_ref[...] = log_sig * (1.0 / GLA_GATE_NORMALIZER)
    r = _dot(hb, wr_ref[...])
    sr_ref[...] = (r / (1.0 + jnp.exp(-r))).astype(BF16)


def _gla_proj(hb, wq, wk, wv, wa, wr, wg, bg, *, tm):
    T, D = hb.shape
    nk, nv = wq.shape[1], wv.shape[1]
    row = lambda w: pl.BlockSpec((tm, w), lambda i: (i, 0))
    return pl.pallas_call(
        _gla_proj_kernel,
        grid=(T // tm,),
        in_specs=[row(D)] + [_const_spec(w.shape) for w in (wq, wk, wv, wa, wr, wg, bg)],
        out_specs=[row(nk), row(nk), row(nv), row(nk), row(nv)],
        out_shape=[jax.ShapeDtypeStruct((T, nk), BF16), jax.ShapeDtypeStruct((T, nk), BF16),
                   jax.ShapeDtypeStruct((T, nv), BF16), jax.ShapeDtypeStruct((T, nk), F32),
                   jax.ShapeDtypeStruct((T, nv), BF16)],
        compiler_params=_params("parallel"),
        name="gla_proj",
    )(hb, wq, wk, wv, wa, wr, wg, bg)


GLA_LEVELS = tuple(GLA_CHUNK >> (i + 1) for i in range(GLA_CHUNK.bit_length() - 1))


def _gla_decay_matrix():
    r = np.arange(GLA_CHUNK)
    groups = [r[None, :] <= r[:, None], r[None, :] > r[:, None]]
    for s in GLA_LEVELS:
        mid = (r // (2 * s)) * (2 * s) + s
        upper = (r & s) != 0
        up = (r[None, :] > mid[:, None]) & (r[None, :] <= r[:, None])
        lo = (r[None, :] > r[:, None]) & (r[None, :] <= mid[:, None])
        groups.append(np.where(upper[:, None], up, lo))
    return np.concatenate(groups, axis=0).astype(np.float32)


def _gla_level_masks():
    r = np.arange(GLA_CHUNK)
    x = r[:, None] ^ r[None, :]
    lower = r[None, :] < r[:, None]
    return np.stack([(lower & (x >= s) & (x < 2 * s)) for s in GLA_LEVELS]).astype(np.float32)


def _gla_group(qs, ks, vs, las, st, nmat, masks_ref):
    C, n = GLA_CHUNK, len(qs)
    la2 = jnp.concatenate([jnp.concatenate(_split(la), axis=0) for la in las], axis=1)
    e_all = jnp.exp(_dot(nmat, la2))
    e = [e_all[:, u * GLA_DK:(u + 1) * GLA_DK] for u in range(n)]

    row = lax.broadcasted_iota(jnp.int32, qs[0].shape, 0)
    a = [jnp.zeros((C, C), F32) for _ in range(n)]
    for lvl, s in enumerate(GLA_LEVELS):
        upper = (row & s) != 0
        mask = masks_ref[lvl]
        for u in range(n):
            t = (jnp.where(upper, qs[u], ks[u]) * e[u][(2 + lvl) * C:(3 + lvl) * C]).astype(BF16)
            a[u] = a[u] + lax.dot_general(t, t, NT_DIMS, preferred_element_type=F32) * mask
    eye = (lax.broadcasted_iota(jnp.int32, (C, C), 0) == lax.broadcasted_iota(jnp.int32, (C, C), 1))
    o_intra, upd = [], []
    for u in range(n):
        a_u = jnp.where(eye, jnp.sum(qs[u] * ks[u], axis=-1, keepdims=True), a[u])
        o_intra.append(_dot(a_u.astype(BF16), vs[u]))
        k_dec = (ks[u] * e[u][C:2 * C]).astype(BF16)
        upd.append(lax.dot_general(k_dec, vs[u], TN_DIMS, preferred_element_type=F32))
    outs = []
    for u in range(n):
        eb = e[u][:C]
        outs.append(o_intra[u] + _dot((qs[u] * eb).astype(BF16), st.astype(BF16)))
        decay = jnp.broadcast_to(eb[C - 1:C, :], (GLA_DK, GLA_DK)).T
        st = st * jnp.concatenate([decay] * (GLA_DV // GLA_DK), axis=1) + upd[u]
    return outs, st


def _gla_rec_kernel(q_ref, k_ref, v_ref, la_ref, sr_ref, g_ref, nmat_ref, masks_ref, o_ref, st_ref, *,
                    ts, group):
    @pl.when(pl.program_id(2) == 0)
    def _():
        st_ref[...] = jnp.zeros_like(st_ref)

    g = g_ref[...]
    nmat = nmat_ref[...]
    span = group * GLA_CHUNK

    def body(c, _):
        base = pl.multiple_of(c * span, span)
        rows = [pl.ds(base + u * GLA_CHUNK, GLA_CHUNK) for u in range(group)]
        outs, st = _gla_group([q_ref[r, :].astype(F32) for r in rows], [k_ref[r, :].astype(F32) for r in rows],
                              [v_ref[r, :] for r in rows], [la_ref[r, :] for r in rows], st_ref[...],
                              nmat, masks_ref)
        st_ref[...] = st
        for r, o in zip(rows, outs):
            o_ref[r, :] = (_rms(o, g) * sr_ref[r, :].astype(F32)).astype(o_ref.dtype)
        return 0

    lax.fori_loop(0, ts // span, body, 0)


def _gla_rec(q, k, v, la, sr, g_out, nmat, masks, *, batch, seq, ts, group):
    T = q.shape[0]
    ns = seq // ts
    blk = lambda w: pl.BlockSpec((ts, w), lambda b, h, i: (b * ns + i, h))
    return pl.pallas_call(
        functools.partial(_gla_rec_kernel, ts=ts, group=group),
        grid=(batch, GLA_HEADS, ns),
        in_specs=[blk(GLA_DK), blk(GLA_DK), blk(GLA_DV), blk(GLA_DK), blk(GLA_DV),
                  _const_spec(g_out.shape), _const_spec(nmat.shape), _const_spec(masks.shape)],
        out_specs=blk(GLA_DV),
        out_shape=jax.ShapeDtypeStruct((T, GLA_HEADS * GLA_DV), BF16),
        scratch_shapes=[pltpu.VMEM((GLA_DK, GLA_DV), F32)],
        compiler_params=_params("parallel", "parallel", "arbitrary"),
        name="gla_rec",
    )(q, k, v, la, sr, g_out, nmat, masks)


def _mem_kv_kernel(mem_ref, g_ref, w_ref, gk_ref, seg_ref, k_ref, v_ref):
    kv = _dot(_rms(mem_ref[...], g_ref[...]).astype(BF16), w_ref[...])
    nk = MEM_HEADS * MEM_DQK
    k = kv[:, :nk]
    ss = _dot_split(k * k, seg_ref[...]) * (1.0 / MEM_DQK)
    k_ref[...] = (k * lax.rsqrt(ss + EPS) * gk_ref[...]).astype(BF16)
    v_ref[...] = kv[:, nk:].astype(BF16)


def _mem_kv(mem2, g_mem, w, gk, seg, *, batch, mem_len):
    D = mem2.shape[1]
    nk, nv = MEM_HEADS * MEM_DQK, MEM_HEADS * MEM_DV
    row = lambda w_: pl.BlockSpec((mem_len, w_), lambda b: (b, 0))
    return pl.pallas_call(
        _mem_kv_kernel,
        grid=(batch,),
        in_specs=[row(D), _const_spec(g_mem.shape), _const_spec(w.shape), _const_spec(gk.shape),
                  _const_spec(seg.shape)],
        out_specs=[row(nk), row(nv)],
        out_shape=[jax.ShapeDtypeStruct((batch * mem_len, nk), BF16),
                   jax.ShapeDtypeStruct((batch * mem_len, nv), BF16)],
        compiler_params=_params("parallel"),
        name="mem_kv",
    )(mem2, g_mem, w, gk, seg)


def _merge_kernel(x_ref, hb_ref, omla_ref, ogla_ref, km_ref, vm_ref, wqm_ref, gqm_ref, seg_ref,
                  wgate_ref, bgate_ref, wo_ref, o_ref):
    hb = hb_ref[...]
    D = x_ref.shape[1]
    qm = _dot(hb, wqm_ref[...])
    ss = _dot_split(qm * qm, seg_ref[...]) * (1.0 / MEM_DQK)
    qn = qm * lax.rsqrt(ss + EPS) * (gqm_ref[...] * float(MEM_DQK ** -0.5))
    head_of_lane = lax.broadcasted_iota(jnp.int32, qn.shape, 1) // MEM_DQK
    km = km_ref[...]

    def gate(j):
        z = _dot(hb, wgate_ref[:, j * D:(j + 1) * D]) + bgate_ref[:, j * D:(j + 1) * D]
        return 1.0 / (1.0 + jnp.exp(-z))

    y = gate(0) * omla_ref[...].astype(F32) + gate(1) * ogla_ref[...].astype(F32)
    g_mem = gate(2)
    parts = []
    for h in range(MEM_HEADS):
        qh = jnp.where(head_of_lane == h, qn, 0.0).astype(BF16)
        s = lax.dot_general(qh, km, NT_DIMS, preferred_element_type=F32)
        p = jnp.exp(s - jnp.max(s, axis=-1, keepdims=True))
        o_h = _dot(p.astype(BF16), vm_ref[:, h * MEM_DV:(h + 1) * MEM_DV])
        parts.append(o_h / jnp.sum(p, axis=-1, keepdims=True))
    y = y + g_mem * jnp.concatenate(parts, axis=-1)
    o_ref[...] = x_ref[...] + _dot(y.astype(BF16), wo_ref[...])


def _merge(x2, hb, o_mla, o_gla, km, vm, wqm, gqm, seg, wgate, bgate, wo, *, seq, mem_len, tm):
    T, D = x2.shape
    per_batch = seq // tm
    row = lambda w: pl.BlockSpec((tm, w), lambda i: (i, 0))
    mem_blk = lambda w: pl.BlockSpec((mem_len, w), lambda i: (i // per_batch, 0))
    return pl.pallas_call(
        _merge_kernel,
        grid=(T // tm,),
        in_specs=[row(D), row(D), row(D), row(D), mem_blk(km.shape[1]), mem_blk(vm.shape[1])]
        + [_const_spec(w.shape) for w in (wqm, gqm, seg, wgate, bgate, wo)],
        out_specs=row(D),
        out_shape=jax.ShapeDtypeStruct((T, D), F32),
        compiler_params=_params("parallel"),
        name="merge",
    )(x2, hb, o_mla, o_gla, km, vm, wqm, gqm, seg, wgate, bgate, wo)


def _ffn_kernel(x_ref, g_ref, wup_ref, wdn_ref, o_ref, *, ff_chunk):
    x = x_ref[...]
    hb = _rms(x, g_ref[...]).astype(BF16)
    acc = x
    for c in range(wup_ref.shape[1] // ff_chunk):
        u = jnp.maximum(_dot(hb, wup_ref[:, c * ff_chunk:(c + 1) * ff_chunk]), 0.0)
        acc = acc + _dot((u * u).astype(BF16), wdn_ref[c * ff_chunk:(c + 1) * ff_chunk, :])
    o_ref[...] = acc


def _ffn(x1, g_ffn, wup, wdn, *, tm, ff_chunk):
    T, D = x1.shape
    row = pl.BlockSpec((tm, D), lambda i: (i, 0))
    return pl.pallas_call(
        functools.partial(_ffn_kernel, ff_chunk=ff_chunk),
        grid=(T // tm,),
        in_specs=[row, _const_spec(g_ffn.shape), _const_spec(wup.shape), _const_spec(wdn.shape)],
        out_specs=row,
        out_shape=jax.ShapeDtypeStruct((T, D), F32),
        compiler_params=_params("parallel"),
        name="ffn",
    )(x1, g_ffn, wup, wdn)


def _pad_cols(w, n):
    return jnp.pad(w, ((0, 0), (0, n - w.shape[-1])))


def _layer(x, mem, positions, g_mix, w_in, b_gate, g_ckv, w_ukv, g_q_nope, g_k_nope, g_q_rope, g_k_rope,
           w_gla_gate, b_gla_gate, g_gla_out, g_mem, w_mem_kv, g_q_mem, g_k_mem, w_o, g_ffn, w_up, w_down):
    B, S, D = x.shape
    M = mem.shape[1]
    T = B * S
    rank = g_ckv.shape[0]
    row = lambda g: g.reshape(1, -1).astype(F32)

    sizes = (MLA_HEADS * (MLA_NOPE + MLA_ROPE), rank, MLA_ROPE, GLA_HEADS * GLA_DK, GLA_HEADS * GLA_DK,
             GLA_HEADS * GLA_DV, GLA_GATE_RANK, GLA_HEADS * GLA_DV, MEM_HEADS * MEM_DQK, N_BRANCHES * D)
    offs = np.concatenate([[0], np.cumsum(sizes)])
    (w_q, w_ckv, w_kr, w_gq, w_gk, w_gv, w_ga, w_gr, w_qm, w_gate) = [
        w_in[:, offs[i]:offs[i + 1]] for i in range(len(sizes))]

    tq = min(512, S)
    w_ukv3 = w_ukv.reshape(rank, MLA_HEADS, MLA_NOPE + MLA_V)
    wuk = w_ukv3[:, :, :MLA_NOPE].reshape(rank, -1).astype(BF16)
    wuvt = w_ukv3[:, :, MLA_NOPE:].reshape(rank, -1).T.astype(BF16)
    q_scale = float((MLA_NOPE + MLA_ROPE) ** -0.5 * np.log2(np.e))
    lanes_of = lambda g, scale=1.0: jnp.broadcast_to((g.astype(F32) * scale)[:, None], (g.shape[0], tq))

    inv = 1.0 / (ROPE_THETA ** (jnp.arange(0, MLA_ROPE, 2, dtype=F32) / MLA_ROPE))
    ang = inv[:, None] * positions.astype(F32).reshape(1, T)

    x2 = x.reshape(T, D)
    hb, qt, k, vt = _mla_prep(
        x2, row(g_mix), w_q.T.astype(BF16), jnp.concatenate([w_ckv, w_kr], axis=1).T.astype(BF16), wuvt,
        w_ckv.astype(BF16), wuk,
        jnp.cos(ang), jnp.sin(ang), lanes_of(g_q_nope, q_scale), lanes_of(g_q_rope, q_scale), lanes_of(g_ckv),
        lanes_of(g_k_rope), row(g_ckv), row(g_k_nope), tm=tq)
    o_mla = _mla_attn(qt, k, vt, batch=B, seq=S, tq=tq, heads=2, chunks=1)

    nmat = jnp.asarray(np.tile(_gla_decay_matrix(), (1, 2)), BF16)
    masks = jnp.asarray(_gla_level_masks())
    wg = jnp.pad(w_gla_gate, ((0, LANES - GLA_GATE_RANK), (0, 0))).astype(BF16)
    qg, kg, vg, la, sr = _gla_proj(hb, w_gq.astype(BF16), w_gk.astype(BF16), w_gv.astype(BF16),
                                   _pad_cols(w_ga, LANES).astype(BF16), w_gr.astype(BF16), wg,
                                   row(b_gla_gate), tm=min(512, S))
    o_gla = _gla_rec(qg, kg, vg, la, sr, row(g_gla_out), nmat, masks, batch=B, seq=S, ts=min(1024, S), group=16)

    nqk = MEM_HEADS * MEM_DQK
    seg = jnp.asarray(np.kron(np.eye(MEM_HEADS), np.ones((MEM_DQK, MEM_DQK))).astype(np.float32), BF16)
    km, vm = _mem_kv(mem.reshape(B * M, D), row(g_mem), w_mem_kv.astype(BF16),
                     jnp.tile(row(g_k_mem), (1, MEM_HEADS)), seg, batch=B, mem_len=M)
    x1 = _merge(x2, hb, o_mla, o_gla, km, vm, w_qm.astype(BF16), jnp.tile(row(g_q_mem), (1, MEM_HEADS)), seg,
                w_gate.astype(BF16), row(b_gate), w_o.astype(BF16), seq=S, mem_len=M, tm=min(512, S))
    out = _ffn(x1, row(g_ffn), w_up.astype(BF16), w_down.astype(BF16), tm=min(512, S), ff_chunk=1024)
    return out.reshape(B, S, D)


def kernel(x, mem, positions, g_mix, w_in, b_gate, g_ckv, w_ukv, g_q_nope, g_k_nope, g_q_rope, g_k_rope,
           w_gla_gate, b_gla_gate, g_gla_out, g_mem, w_mem_kv, g_q_mem, g_k_mem, w_o, g_ffn, w_up, w_down):
    for l in range(g_mix.shape[0]):
        x = _layer(x, mem, positions, g_mix[l], w_in[l], b_gate[l], g_ckv[l], w_ukv[l], g_q_nope[l],
                   g_k_nope[l], g_q_rope[l], g_k_rope[l], w_gla_gate[l], b_gla_gate[l], g_gla_out[l],
                   g_mem[l], w_mem_kv[l], g_q_mem[l], g_k_mem[l], w_o[l], g_ffn[l], w_up[l], w_down[l])
    return x
```

```python
import functools

import jax
import jax.numpy as jnp
import numpy as np
from jax import lax
from jax.experimental import pallas as pl
from jax.experimental.pallas import tpu as pltpu

F32 = jnp.float32
BF16 = jnp.bfloat16

EPS = 1e-6
ROPE_THETA = 10000.0
LANES = 128

MLA_HEADS, MLA_NOPE, MLA_ROPE, MLA_V = 8, 128, 64, 128
MLA_HEAD_PAD = 2 * LANES
MLA_VT_ROWS = MLA_V + 16
GLA_HEADS, GLA_DK, GLA_DV = 4, 128, 256
GLA_GATE_RANK, GLA_GATE_NORMALIZER, GLA_CHUNK, GLA_SUB = 16, 16.0, 64, 16
MEM_HEADS, MEM_DQK, MEM_DV = 4, 64, 256
N_BRANCHES = 3

VMEM_LIMIT = 48 * 1024 * 1024

NT_DIMS = (((1,), (1,)), ((), ()))
TN_DIMS = (((0,), (0,)), ((), ()))


def _params(*sem):
    return pltpu.CompilerParams(dimension_semantics=sem, vmem_limit_bytes=VMEM_LIMIT)


def _rms(t, g, n=None):
    n = t.shape[-1] if n is None else n
    ss = jnp.sum(t * t, axis=-1, keepdims=True) * (1.0 / n)
    return t * lax.rsqrt(ss + EPS) * g


def _dot(a, b):
    return jnp.dot(a, b, preferred_element_type=F32)


def _split(a_f32):
    hi = a_f32.astype(BF16)
    return hi, (a_f32 - hi.astype(F32)).astype(BF16)


def _dot_split(a_f32, b_bf16):
    hi, lo = _split(a_f32)
    return _dot(hi, b_bf16) + _dot(lo, b_bf16)


def _const_spec(shape):
    return pl.BlockSpec(shape, lambda *_: (0,) * len(shape))


def _rms_rows(t, g):
    ss = jnp.sum(t * t, axis=0, keepdims=True) * (1.0 / t.shape[0])
    return t * lax.rsqrt(ss + EPS) * g


def _rope_rows(r, cos, sin):
    half = r.shape[0] // 2
    t1, t2 = r[:half], r[half:]
    return t1 * cos - t2 * sin, t1 * sin + t2 * cos


def _mla_prep_kernel(x_ref, gmix_ref, wqt_ref, wlatt_ref, wuvt_ref, wckv_ref, wuk_ref,
                     cost_ref, sint_ref, gqnt_ref, gqrt_ref, gckvt_ref, gkrt_ref,
                     gckv_ref, gkn_ref, hb_ref, qt_ref, k_ref, vt_ref):
    h = _rms(x_ref[...], gmix_ref[...])
    hb = h.astype(BF16)
    hb_ref[...] = hb
    hbt = h.T.astype(BF16)
    tm = hbt.shape[1]
    half = MLA_ROPE // 2
    dq = MLA_NOPE + MLA_ROPE

    cos, sin = cost_ref[...], sint_ref[...]
    rank = gckv_ref.shape[-1]
    latt = _dot(wlatt_ref[...], hbt)

    ckv = _rms(_dot(hb, wckv_ref[...]), gckv_ref[...]).astype(BF16)
    kn = _dot(ckv, wuk_ref[...])
    kr1, kr2 = _rope_rows(_rms_rows(latt[rank:], gkrt_ref[...]), cos, sin)
    kpe = jnp.concatenate([kr1, kr2, jnp.zeros((LANES - MLA_ROPE, tm), F32)], axis=0).T.astype(BF16)
    gkn = gkn_ref[...]
    for hd in range(MLA_HEADS):
        c0 = hd * MLA_HEAD_PAD
        k_ref[:, c0:c0 + LANES] = _rms(kn[:, hd * LANES:(hd + 1) * LANES], gkn).astype(BF16)
        k_ref[:, c0 + LANES:c0 + 2 * LANES] = kpe

    ckvt = _rms_rows(latt[:rank], gckvt_ref[...]).astype(BF16)
    vt = _dot(wuvt_ref[...], ckvt)
    pad_row = lax.broadcasted_iota(jnp.int32, (MLA_VT_ROWS - MLA_V, tm), 0)
    ones_rows = jnp.where(pad_row == 0, 1.0, 0.0).astype(BF16)
    for hd in range(MLA_HEADS):
        vt_ref[hd, :MLA_V, :] = vt[hd * MLA_V:(hd + 1) * MLA_V].astype(BF16)
        vt_ref[hd, MLA_V:, :] = ones_rows

    gqn, gqr = gqnt_ref[...], gqrt_ref[...]
    zeros = jnp.zeros((MLA_HEAD_PAD - dq, tm), BF16)
    for hd in range(MLA_HEADS):
        t = _dot(wqt_ref[hd * dq:(hd + 1) * dq, :], hbt)
        qt_ref[hd, :MLA_NOPE, :] = _rms_rows(t[:MLA_NOPE], gqn).astype(BF16)
        r1, r2 = _rope_rows(_rms_rows(t[MLA_NOPE:], gqr), cos, sin)
        qt_ref[hd, MLA_NOPE:MLA_NOPE + half, :] = r1.astype(BF16)
        qt_ref[hd, MLA_NOPE + half:dq, :] = r2.astype(BF16)
        qt_ref[hd, dq:, :] = zeros


def _mla_prep(x2, g_mix, wqt, wlatt, wuvt, wckv, wuk, cos_tt, sin_tt, gqnt, gqrt, gckvt, gkrt,
              gckv, gkn, *, tm):
    T, D = x2.shape
    HP = MLA_HEADS * MLA_HEAD_PAD
    row = lambda w: pl.BlockSpec((tm, w), lambda i: (i, 0))
    col = lambda r: pl.BlockSpec((r, tm), lambda i: (0, i))
    consts = (g_mix, wqt, wlatt, wuvt, wckv, wuk)
    gains = (gqnt, gqrt, gckvt, gkrt, gckv, gkn)
    return pl.pallas_call(
        _mla_prep_kernel,
        grid=(T // tm,),
        in_specs=[row(D)] + [_const_spec(c.shape) for c in consts]
        + [col(cos_tt.shape[0]), col(sin_tt.shape[0])]
        + [_const_spec(g.shape) for g in gains],
        out_specs=[row(D), pl.BlockSpec((MLA_HEADS, None, MLA_HEAD_PAD, tm), lambda i: (0, i, 0, 0)), row(HP),
                   pl.BlockSpec((MLA_HEADS, None, MLA_VT_ROWS, tm), lambda i: (0, i, 0, 0))],
        out_shape=[jax.ShapeDtypeStruct((T, D), BF16),
                   jax.ShapeDtypeStruct((MLA_HEADS, T // tm, MLA_HEAD_PAD, tm), BF16),
                   jax.ShapeDtypeStruct((T, HP), BF16),
                   jax.ShapeDtypeStruct((MLA_HEADS, T // tm, MLA_VT_ROWS, tm), BF16)],
        compiler_params=_params("parallel"),
        name="mla_prep",
    )(x2, *consts, cos_tt, sin_tt, *gains)


def _mla_attn_kernel(qt_ref, qt_next_ref, k_ref, vt_ref, o_ref, s_a, s_b, s_c, max_a, max_b, max_c,
                     m_even, acc_even, m_odd, acc_odd, *, tq, heads, nq):
    pair = pl.program_id(2)
    hs = range(heads)
    buf_a, buf_b, buf_c = (s_a, max_a), (s_b, max_b), (s_c, max_c)
    even, odd = (m_even, acc_even), (m_odd, acc_odd)

    def step(prod=None, cons=None):
        if prod is not None:
            jp, q_ref, q_tile, (sp_ref, mp_ref) = prod
            r0 = pl.multiple_of(jp * tq, tq)
            for h in hs:
                q = q_ref[h] if q_tile is None else q_ref[h, q_tile]
                s = _dot(k_ref[pl.ds(r0, tq), h * MLA_HEAD_PAD:(h + 1) * MLA_HEAD_PAD], q)
                sp_ref[h] = s
                mp_ref[h] = jnp.max(s, axis=0, keepdims=True)
        if cons is not None:
            jc, (sc_ref, mc_ref), diagonal, (m_ref, acc_ref) = cons
            for h in hs:
                s = sc_ref[h]
                if diagonal:
                    kpos = lax.broadcasted_iota(jnp.int32, s.shape, 0)
                    qpos = lax.broadcasted_iota(jnp.int32, s.shape, 1)
                    s = jnp.where(kpos <= qpos, s, -1e30)
                    s_max = jnp.max(s, axis=0, keepdims=True)
                else:
                    s_max = mc_ref[h]
                m = m_ref[h]
                m_new = jnp.maximum(m, s_max)
                alpha = jnp.exp2(m - m_new)
                p = jnp.exp2(s - m_new)
                m_ref[h] = m_new
                acc_ref[h] = alpha * acc_ref[h] + _dot(vt_ref[h, jc], p.astype(BF16))

    def start(stats):
        m_ref, acc_ref = stats
        m_ref[...] = jnp.full(m_ref.shape, -1e30, F32)
        acc_ref[...] = jnp.zeros(acc_ref.shape, F32)

    def write_out(stats, tile):
        for h in hs:
            acc = stats[1][h]
            o_ref[tile * tq:(tile + 1) * tq, h * MLA_V:(h + 1) * MLA_V] = (
                acc[:MLA_V] / acc[MLA_V:MLA_V + 1]).T.astype(o_ref.dtype)

    def two_blocks(t, q_tile, stats):
        j = 2 * t
        step(prod=(j + 1, qt_ref, q_tile, buf_b), cons=(j, buf_a, False, stats))
        step(prod=(j + 2, qt_ref, q_tile, buf_a), cons=(j + 1, buf_b, False, stats))

    i_even, i_odd = 2 * pair, 2 * pair + 1
    ahead = (jnp.minimum(i_odd + 1, nq - 1), qt_next_ref, None, buf_c)
    start(even)
    start(odd)

    @pl.when(pair == 0)
    def _():
        step(prod=(0, qt_ref, 0, buf_c))
        step(prod=(1, qt_ref, 1, buf_b), cons=(0, buf_c, True, even))
        write_out(even, 0)
        step(prod=(0, qt_ref, 1, buf_a), cons=(1, buf_b, True, odd))
        step(prod=ahead, cons=(0, buf_a, False, odd))
        write_out(odd, 1)

    @pl.when(pair > 0)
    def _():
        step(prod=(0, qt_ref, 0, buf_a), cons=(i_even, buf_c, True, even))
        lax.fori_loop(0, pair - 1, lambda t, c: (two_blocks(t, 0, even), c)[1], 0)
        step(prod=(i_even - 1, qt_ref, 0, buf_b), cons=(i_even - 2, buf_a, False, even))
        step(prod=(i_odd, qt_ref, 1, buf_c), cons=(i_even - 1, buf_b, False, even))
        write_out(even, 0)
        step(prod=(0, qt_ref, 1, buf_a), cons=(i_odd, buf_c, True, odd))
        lax.fori_loop(0, pair, lambda t, c: (two_blocks(t, 1, odd), c)[1], 0)
        step(prod=ahead, cons=(i_odd - 1, buf_a, False, odd))
        write_out(odd, 1)


def _mla_attn(qt, k, vt, *, batch, seq, tq, heads):
    T = k.shape[0]
    nq = seq // tq
    assert nq % 2 == 0, "q tiles are processed in (even, odd) pairs"
    half = nq // 2
    score = pltpu.VMEM((heads, tq, tq), F32)
    stat = pltpu.VMEM((heads, 1, tq), F32)
    acc = pltpu.VMEM((heads, MLA_VT_ROWS, tq), F32)
    return pl.pallas_call(
        functools.partial(_mla_attn_kernel, tq=tq, heads=heads, nq=nq),
        grid=(batch, MLA_HEADS // heads, half),
        in_specs=[pl.BlockSpec((heads, 2, MLA_HEAD_PAD, tq), lambda b, h, p: (h, b * half + p, 0, 0)),
                  pl.BlockSpec((heads, None, MLA_HEAD_PAD, tq),
                               lambda b, h, p: (h, b * nq + jnp.minimum(2 * p + 2, nq - 1), 0, 0)),
                  pl.BlockSpec((seq, heads * MLA_HEAD_PAD), lambda b, h, p: (b, h)),
                  pl.BlockSpec((heads, nq, MLA_VT_ROWS, tq), lambda b, h, p: (h, b, 0, 0))],
        out_specs=pl.BlockSpec((2 * tq, heads * MLA_V), lambda b, h, p: (b * half + p, h)),
        out_shape=jax.ShapeDtypeStruct((T, MLA_HEADS * MLA_V), BF16),
        scratch_shapes=[score, score, score, stat, stat, stat, stat, acc, stat, acc],
        compiler_params=_params("parallel", "parallel", "arbitrary"),
        name="mla_attn",
    )(qt, qt, k, vt)


def _gla_proj_kernel(hb_ref, wq_ref, wk_ref, wv_ref, wa_ref, wr_ref, wg_ref, bg_ref,
                     q_ref, k_ref, v_ref, la_ref, sr_ref):
    hb = hb_ref[...]
    a = _dot(hb, wa_ref[...])
    z = _dot(a.astype(BF16), wg_ref[...]) + bg_ref[...]
    log_sig = jnp.minimum(z, 0.0) - jnp.log1p(jnp.exp(-jnp.abs(z)))
    la_ref[...] = log_sig * (1.0 / GLA_GATE_NORMALIZER)
    r = _dot(hb, wr_ref[...])
    sr_ref[...] = (r / (1.0 + jnp.exp(-r))).astype(BF16)
    v_ref[...] = _dot(hb, wv_ref[...]).astype(BF16)
    q_ref[...] = (_dot(hb, wq_ref[...]) * float(GLA_DK ** -0.5)).astype(BF16)
    k_ref[...] = _dot(hb, wk_ref[...]).astype(BF16)


def _gla_proj(hb, wq, wk, wv, wa, wr, wg, bg, *, tm):
    T, D = hb.shape
    nk, nv = wq.shape[1], wv.shape[1]
    row = lambda w: pl.BlockSpec((tm, w), lambda i: (i, 0))
    return pl.pallas_call(
        _gla_proj_kernel,
        grid=(T // tm,),
        in_specs=[row(D)] + [_const_spec(w.shape) for w in (wq, wk, wv, wa, wr, wg, bg)],
        out_specs=[row(nk), row(nk), row(nv), row(nk), row(nv)],
        out_shape=[jax.ShapeDtypeStruct((T, nk), BF16), jax.ShapeDtypeStruct((T, nk), BF16),
                   jax.ShapeDtypeStruct((T, nv), BF16), jax.ShapeDtypeStruct((T, nk), F32),
                   jax.ShapeDtypeStruct((T, nv), BF16)],
        compiler_params=_params("parallel"),
        name="gla_proj",
    )(hb, wq, wk, wv, wa, wr, wg, bg)


GLA_LEVELS = tuple(GLA_CHUNK >> (i + 1) for i in range(GLA_CHUNK.bit_length() - 1))


def _gla_decay_matrix():
    r = np.arange(GLA_CHUNK)
    groups = [r[None, :] <= r[:, None], r[None, :] > r[:, None]]
    for s in GLA_LEVELS:
        mid = (r // (2 * s)) * (2 * s) + s
        upper = (r & s) != 0
        up = (r[None, :] > mid[:, None]) & (r[None, :] <= r[:, None])
        lo = (r[None, :] > r[:, None]) & (r[None, :] <= mid[:, None])
        groups.append(np.where(upper[:, None], up, lo))
    return np.concatenate(groups, axis=0).astype(np.float32)


def _gla_level_masks():
    r = np.arange(GLA_CHUNK)
    x = r[:, None] ^ r[None, :]
    lower = r[None, :] < r[:, None]
    return np.stack([(lower & (x >= s) & (x < 2 * s)) for s in GLA_LEVELS]).astype(np.float32)


def _gla_group(qs, ks, vs, las, st, nmat, masks_ref):
    C, n = GLA_CHUNK, len(qs)
    la2 = jnp.concatenate([jnp.concatenate(_split(la), axis=0) for la in las], axis=1)
    e_all = jnp.exp(_dot(nmat, la2))
    e = [e_all[:, u * GLA_DK:(u + 1) * GLA_DK] for u in range(n)]

    row = lax.broadcasted_iota(jnp.int32, qs[0].shape, 0)
    a = [jnp.zeros((C, C), F32) for _ in range(n)]
    for lvl, s in enumerate(GLA_LEVELS):
        upper = (row & s) != 0
        mask = masks_ref[lvl]
        for u in range(n):
            t = (jnp.where(upper, qs[u], ks[u]) * e[u][(2 + lvl) * C:(3 + lvl) * C]).astype(BF16)
            a[u] = a[u] + lax.dot_general(t, t, NT_DIMS, preferred_element_type=F32) * mask
    eye = (lax.broadcasted_iota(jnp.int32, (C, C), 0) == lax.broadcasted_iota(jnp.int32, (C, C), 1))
    o_intra, upd = [], []
    for u in range(n):
        a_u = jnp.where(eye, jnp.sum(qs[u] * ks[u], axis=-1, keepdims=True), a[u])
        o_intra.append(_dot(a_u.astype(BF16), vs[u]))
        k_dec = (ks[u] * e[u][C:2 * C]).astype(BF16)
        upd.append(lax.dot_general(k_dec, vs[u], TN_DIMS, preferred_element_type=F32))
    outs = []
    for u in range(n):
        eb = e[u][:C]
        outs.append(o_intra[u] + _dot((qs[u] * eb).astype(BF16), st.astype(BF16)))
        decay = jnp.broadcast_to(eb[C - 1:C, :], (GLA_DK, GLA_DK)).T
        st = st * jnp.concatenate([decay] * (GLA_DV // GLA_DK), axis=1) + upd[u]
    return outs, st


def _gla_rec_kernel(q_ref, k_ref, v_ref, la_ref, sr_ref, g_ref, nmat_ref, masks_ref, o_ref, st_ref, *,
                    ts, group):
    @pl.when(pl.program_id(2) == 0)
    def _():
        st_ref[...] = jnp.zeros_like(st_ref)

    g = g_ref[...]
    nmat = nmat_ref[...]
    span = group * GLA_CHUNK

    def body(c, _):
        base = pl.multiple_of(c * span, span)
        rows = [pl.ds(base + u * GLA_CHUNK, GLA_CHUNK) for u in range(group)]
        outs, st = _gla_group([q_ref[r, :].astype(F32) for r in rows], [k_ref[r, :].astype(F32) for r in rows],
                              [v_ref[r, :] for r in rows], [la_ref[r, :] for r in rows], st_ref[...],
                              nmat, masks_ref)
        st_ref[...] = st
        for r, o in zip(rows, outs):
            o_ref[r, :] = (_rms(o, g) * sr_ref[r, :].astype(F32)).astype(o_ref.dtype)
        return 0

    lax.fori_loop(0, ts // span, body, 0)


def _gla_rec(q, k, v, la, sr, g_out, nmat, masks, *, batch, seq, ts, group):
    T = q.shape[0]
    ns = seq // ts
    blk = lambda w: pl.BlockSpec((ts, w), lambda b, h, i: (b * ns + i, h))
    return pl.pallas_call(
        functools.partial(_gla_rec_kernel, ts=ts, group=group),
        grid=(batch, GLA_HEADS, ns),
        in_specs=[blk(GLA_DK), blk(GLA_DK), blk(GLA_DV), blk(GLA_DK), blk(GLA_DV),
                  _const_spec(g_out.shape), _const_spec(nmat.shape), _const_spec(masks.shape)],
        out_specs=blk(GLA_DV),
        out_shape=jax.ShapeDtypeStruct((T, GLA_HEADS * GLA_DV), BF16),
        scratch_shapes=[pltpu.VMEM((GLA_DK, GLA_DV), F32)],
        compiler_params=_params("parallel", "parallel", "arbitrary"),
        name="gla_rec",
    )(q, k, v, la, sr, g_out, nmat, masks)


def _mem_kv_kernel(mem_ref, g_ref, w_ref, gk_ref, seg_ref, k_ref, v_ref):
    kv = _dot(_rms(mem_ref[...], g_ref[...]).astype(BF16), w_ref[...])
    nk = MEM_HEADS * MEM_DQK
    k = kv[:, :nk]
    ss = _dot_split(k * k, seg_ref[...]) * (1.0 / MEM_DQK)
    k_ref[...] = (k * lax.rsqrt(ss + EPS) * gk_ref[...]).astype(BF16)
    v_ref[...] = kv[:, nk:].astype(BF16)


def _mem_kv(mem2, g_mem, w, gk, seg, *, batch, mem_len):
    D = mem2.shape[1]
    nk, nv = MEM_HEADS * MEM_DQK, MEM_HEADS * MEM_DV
    row = lambda w_: pl.BlockSpec((mem_len, w_), lambda b: (b, 0))
    return pl.pallas_call(
        _mem_kv_kernel,
        grid=(batch,),
        in_specs=[row(D), _const_spec(g_mem.shape), _const_spec(w.shape), _const_spec(gk.shape),
                  _const_spec(seg.shape)],
        out_specs=[row(nk), row(nv)],
        out_shape=[jax.ShapeDtypeStruct((batch * mem_len, nk), BF16),
                   jax.ShapeDtypeStruct((batch * mem_len, nv), BF16)],
        compiler_params=_params("parallel"),
        name="mem_kv",
    )(mem2, g_mem, w, gk, seg)


def _merge_kernel(x_ref, hb_ref, omla_ref, ogla_ref, km_ref, vm_ref, wqm_ref, gqm_ref, seg_ref,
                  wgate_ref, bgate_ref, wo_ref, o_ref):
    hb = hb_ref[...]
    D = x_ref.shape[1]
    qm = _dot(hb, wqm_ref[...])
    ss = _dot_split(qm * qm, seg_ref[...]) * (1.0 / MEM_DQK)
    qn = qm * lax.rsqrt(ss + EPS) * (gqm_ref[...] * float(MEM_DQK ** -0.5))
    head_of_lane = lax.broadcasted_iota(jnp.int32, qn.shape, 1) // MEM_DQK
    km = km_ref[...]

    def gate(j):
        z = _dot(hb, wgate_ref[:, j * D:(j + 1) * D]) + bgate_ref[:, j * D:(j + 1) * D]
        return 1.0 / (1.0 + jnp.exp(-z))

    y = gate(0) * omla_ref[...].astype(F32) + gate(1) * ogla_ref[...].astype(F32)
    g_mem = gate(2)
    parts = []
    for h in range(MEM_HEADS):
        qh = jnp.where(head_of_lane == h, qn, 0.0).astype(BF16)
        s = lax.dot_general(qh, km, NT_DIMS, preferred_element_type=F32)
        p = jnp.exp(s - jnp.max(s, axis=-1, keepdims=True))
        o_h = _dot(p.astype(BF16), vm_ref[:, h * MEM_DV:(h + 1) * MEM_DV])
        parts.append(o_h / jnp.sum(p, axis=-1, keepdims=True))
    y = y + g_mem * jnp.concatenate(parts, axis=-1)
    o_ref[...] = x_ref[...] + _dot(y.astype(BF16), wo_ref[...])


def _merge(x2, hb, o_mla, o_gla, km, vm, wqm, gqm, seg, wgate, bgate, wo, *, seq, mem_len, tm):
    T, D = x2.shape
    per_batch = seq // tm
    row = lambda w: pl.BlockSpec((tm, w), lambda i: (i, 0))
    mem_blk = lambda w: pl.BlockSpec((mem_len, w), lambda i: (i // per_batch, 0))
    return pl.pallas_call(
        _merge_kernel,
        grid=(T // tm,),
        in_specs=[row(D), row(D), row(D), row(D), mem_blk(km.shape[1]), mem_blk(vm.shape[1])]
        + [_const_spec(w.shape) for w in (wqm, gqm, seg, wgate, bgate, wo)],
        out_specs=row(D),
        out_shape=jax.ShapeDtypeStruct((T, D), F32),
        compiler_params=_params("parallel"),
        name="merge",
    )(x2, hb, o_mla, o_gla, km, vm, wqm, gqm, seg, wgate, bgate, wo)


def _ffn_kernel(x_ref, g_ref, wup_ref, wdn_ref, o_ref, *, ff_chunk):
    x = x_ref[...]
    hb = _rms(x, g_ref[...]).astype(BF16)
    acc = x
    for c in range(wup_ref.shape[1] // ff_chunk):
        u = jnp.maximum(_dot(hb, wup_ref[:, c * ff_chunk:(c + 1) * ff_chunk]), 0.0)
        acc = acc + _dot((u * u).astype(BF16), wdn_ref[c * ff_chunk:(c + 1) * ff_chunk, :])
    o_ref[...] = acc


def _ffn(x1, g_ffn, wup, wdn, *, tm, ff_chunk):
    T, D = x1.shape
    row = pl.BlockSpec((tm, D), lambda i: (i, 0))
    return pl.pallas_call(
        functools.partial(_ffn_kernel, ff_chunk=ff_chunk),
        grid=(T // tm,),
        in_specs=[row, _const_spec(g_ffn.shape), _const_spec(wup.shape), _const_spec(wdn.shape)],
        out_specs=row,
        out_shape=jax.ShapeDtypeStruct((T, D), F32),
        compiler_params=_params("parallel"),
        name="ffn",
    )(x1, g_ffn, wup, wdn)


def _pad_cols(w, n):
    return jnp.pad(w, ((0, 0), (0, n - w.shape[-1])))


def _layer(x, mem, positions, g_mix, w_in, b_gate, g_ckv, w_ukv, g_q_nope, g_k_nope, g_q_rope, g_k_rope,
           w_gla_gate, b_gla_gate, g_gla_out, g_mem, w_mem_kv, g_q_mem, g_k_mem, w_o, g_ffn, w_up, w_down):
    B, S, D = x.shape
    M = mem.shape[1]
    T = B * S
    rank = g_ckv.shape[0]
    row = lambda g: g.reshape(1, -1).astype(F32)

    sizes = (MLA_HEADS * (MLA_NOPE + MLA_ROPE), rank, MLA_ROPE, GLA_HEADS * GLA_DK, GLA_HEADS * GLA_DK,
             GLA_HEADS * GLA_DV, GLA_GATE_RANK, GLA_HEADS * GLA_DV, MEM_HEADS * MEM_DQK, N_BRANCHES * D)
    offs = np.concatenate([[0], np.cumsum(sizes)])
    (w_q, w_ckv, w_kr, w_gq, w_gk, w_gv, w_ga, w_gr, w_qm, w_gate) = [
        w_in[:, offs[i]:offs[i + 1]] for i in range(len(sizes))]

    tq = min(512, S)
    w_ukv3 = w_ukv.reshape(rank, MLA_HEADS, MLA_NOPE + MLA_V)
    wuk = w_ukv3[:, :, :MLA_NOPE].reshape(rank, -1).astype(BF16)
    wuvt = w_ukv3[:, :, MLA_NOPE:].reshape(rank, -1).T.astype(BF16)
    q_scale = float((MLA_NOPE + MLA_ROPE) ** -0.5 * np.log2(np.e))
    lanes_of = lambda g, scale=1.0: jnp.broadcast_to((g.astype(F32) * scale)[:, None], (g.shape[0], tq))

    inv = 1.0 / (ROPE_THETA ** (jnp.arange(0, MLA_ROPE, 2, dtype=F32) / MLA_ROPE))
    ang = inv[:, None] * positions.astype(F32).reshape(1, T)

    x2 = x.reshape(T, D)
    hb, qt, k, vt = _mla_prep(
        x2, row(g_mix), w_q.T.astype(BF16), jnp.concatenate([w_ckv, w_kr], axis=1).T.astype(BF16), wuvt,
        w_ckv.astype(BF16), wuk,
        jnp.cos(ang), jnp.sin(ang), lanes_of(g_q_nope, q_scale), lanes_of(g_q_rope, q_scale), lanes_of(g_ckv),
        lanes_of(g_k_rope), row(g_ckv), row(g_k_nope), tm=tq)
    o_mla = _mla_attn(qt, k, vt, batch=B, seq=S, tq=tq, heads=2)

    nmat = jnp.asarray(np.tile(_gla_decay_matrix(), (1, 2)), BF16)
    masks = jnp.asarray(_gla_level_masks())
    wg = jnp.pad(w_gla_gate, ((0, LANES - GLA_GATE_RANK), (0, 0))).astype(BF16)
    qg, kg, vg, la, sr = _gla_proj(hb, w_gq.astype(BF16), w_gk.astype(BF16), w_gv.astype(BF16),
                                   _pad_cols(w_ga, LANES).astype(BF16), w_gr.astype(BF16), wg,
                                   row(b_gla_gate), tm=min(512, S))
    o_gla = _gla_rec(qg, kg, vg, la, sr, row(g_gla_out), nmat, masks, batch=B, seq=S, ts=min(1024, S), group=16)

    nqk = MEM_HEADS * MEM_DQK
    seg = jnp.asarray(np.kron(np.eye(MEM_HEADS), np.ones((MEM_DQK, MEM_DQK))).astype(np.float32), BF16)
    km, vm = _mem_kv(mem.reshape(B * M, D), row(g_mem), w_mem_kv.astype(BF16),
                     jnp.tile(row(g_k_mem), (1, MEM_HEADS)), seg, batch=B, mem_len=M)
    x1 = _merge(x2, hb, o_mla, o_gla, km, vm, w_qm.astype(BF16), jnp.tile(row(g_q_mem), (1, MEM_HEADS)), seg,
                w_gate.astype(BF16), row(b_gate), w_o.astype(BF16), seq=S, mem_len=M, tm=min(512, S))
    out = _ffn(x1, row(g_ffn), w_up.astype(BF16), w_down.astype(BF16), tm=min(512, S), ff_chunk=1024)
    return out.reshape(B, S, D)


def kernel(x, mem, positions, g_mix, w_in, b_gate, g_ckv, w_ukv, g_q_nope, g_k_nope, g_q_rope, g_k_rope,
           w_gla_gate, b_gla_gate, g_gla_out, g_mem, w_mem_kv, g_q_mem, g_k_mem, w_o, g_ffn, w_up, w_down):
    for l in range(g_mix.shape[0]):
        x = _layer(x, mem, positions, g_mix[l], w_in[l], b_gate[l], g_ckv[l], w_ukv[l], g_q_nope[l],
                   g_k_nope[l], g_q_rope[l], g_k_rope[l], w_gla_gate[l], b_gla_gate[l], g_gla_out[l],
                   g_mem[l], w_mem_kv[l], g_q_mem[l], g_k_mem[l], w_o[l], g_ffn[l], w_up[l], w_down[l])
    return x
```

```python
import functools

import jax
import jax.numpy as jnp
import numpy as np
from jax import lax
from jax.experimental import pallas as pl
from jax.experimental.pallas import tpu as pltpu

F32 = jnp.float32
BF16 = jnp.bfloat16

EPS = 1e-6
ROPE_THETA = 10000.0
LANES = 128

MLA_HEADS, MLA_NOPE, MLA_ROPE, MLA_V = 8, 128, 64, 128
MLA_HEAD_PAD = 2 * LANES
MLA_VT_ROWS = MLA_V + 16
GLA_HEADS, GLA_DK, GLA_DV = 4, 128, 256
GLA_GATE_RANK, GLA_GATE_NORMALIZER, GLA_CHUNK, GLA_SUB = 16, 16.0, 64, 16
MEM_HEADS, MEM_DQK, MEM_DV = 4, 64, 256
N_BRANCHES = 3

VMEM_LIMIT = 48 * 1024 * 1024

NT_DIMS = (((1,), (1,)), ((), ()))
TN_DIMS = (((0,), (0,)), ((), ()))


def _params(*sem):
    return pltpu.CompilerParams(dimension_semantics=sem, vmem_limit_bytes=VMEM_LIMIT)


def _rms(t, g, n=None):
    n = t.shape[-1] if n is None else n
    ss = jnp.sum(t * t, axis=-1, keepdims=True) * (1.0 / n)
    return t * lax.rsqrt(ss + EPS) * g


def _dot(a, b):
    return jnp.dot(a, b, preferred_element_type=F32)


def _split(a_f32):
    hi = a_f32.astype(BF16)
    return hi, (a_f32 - hi.astype(F32)).astype(BF16)


def _dot_split(a_f32, b_bf16):
    hi, lo = _split(a_f32)
    return _dot(hi, b_bf16) + _dot(lo, b_bf16)


def _const_spec(shape):
    return pl.BlockSpec(shape, lambda *_: (0,) * len(shape))


def _rms_rows(t, g):
    ss = jnp.sum(t * t, axis=0, keepdims=True) * (1.0 / t.shape[0])
    return t * lax.rsqrt(ss + EPS) * g


def _rope_rows(r, cos, sin):
    half = r.shape[0] // 2
    t1, t2 = r[:half], r[half:]
    return t1 * cos - t2 * sin, t1 * sin + t2 * cos


def _mla_prep_kernel(x_ref, gmix_ref, wqt_ref, wlatt_ref, wuvt_ref, wckv_ref, wuk_ref,
                     cost_ref, sint_ref, gqnt_ref, gqrt_ref, gckvt_ref, gkrt_ref,
                     gckv_ref, gkn_ref, hb_ref, qt_ref, k_ref, vt_ref):
    h = _rms(x_ref[...], gmix_ref[...])
    hb = h.astype(BF16)
    hb_ref[...] = hb
    hbt = h.T.astype(BF16)
    tm = hbt.shape[1]
    half = MLA_ROPE // 2
    dq = MLA_NOPE + MLA_ROPE

    cos, sin = cost_ref[...], sint_ref[...]
    rank = gckv_ref.shape[-1]
    latt = _dot(wlatt_ref[...], hbt)

    ckv = _rms(_dot(hb, wckv_ref[...]), gckv_ref[...]).astype(BF16)
    kn = _dot(ckv, wuk_ref[...])
    kr1, kr2 = _rope_rows(_rms_rows(latt[rank:], gkrt_ref[...]), cos, sin)
    kpe = jnp.concatenate([kr1, kr2, jnp.zeros((LANES - MLA_ROPE, tm), F32)], axis=0).T.astype(BF16)
    gkn = gkn_ref[...]
    for hd in range(MLA_HEADS):
        c0 = hd * MLA_HEAD_PAD
        k_ref[:, c0:c0 + LANES] = _rms(kn[:, hd * LANES:(hd + 1) * LANES], gkn).astype(BF16)
        k_ref[:, c0 + LANES:c0 + 2 * LANES] = kpe

    ckvt = _rms_rows(latt[:rank], gckvt_ref[...]).astype(BF16)
    vt = _dot(wuvt_ref[...], ckvt)
    pad_row = lax.broadcasted_iota(jnp.int32, (MLA_VT_ROWS - MLA_V, tm), 0)
    ones_rows = jnp.where(pad_row == 0, 1.0, 0.0).astype(BF16)
    for hd in range(MLA_HEADS):
        vt_ref[hd, :MLA_V, :] = vt[hd * MLA_V:(hd + 1) * MLA_V].astype(BF16)
        vt_ref[hd, MLA_V:, :] = ones_rows

    gqn, gqr = gqnt_ref[...], gqrt_ref[...]
    zeros = jnp.zeros((MLA_HEAD_PAD - dq, tm), BF16)
    for hd in range(MLA_HEADS):
        t = _dot(wqt_ref[hd * dq:(hd + 1) * dq, :], hbt)
        qt_ref[hd, :MLA_NOPE, :] = _rms_rows(t[:MLA_NOPE], gqn).astype(BF16)
        r1, r2 = _rope_rows(_rms_rows(t[MLA_NOPE:], gqr), cos, sin)
        qt_ref[hd, MLA_NOPE:MLA_NOPE + half, :] = r1.astype(BF16)
        qt_ref[hd, MLA_NOPE + half:dq, :] = r2.astype(BF16)
        qt_ref[hd, dq:, :] = zeros


def _mla_prep(x2, g_mix, wqt, wlatt, wuvt, wckv, wuk, cos_tt, sin_tt, gqnt, gqrt, gckvt, gkrt,
              gckv, gkn, *, tm):
    T, D = x2.shape
    HP = MLA_HEADS * MLA_HEAD_PAD
    row = lambda w: pl.BlockSpec((tm, w), lambda i: (i, 0))
    col = lambda r: pl.BlockSpec((r, tm), lambda i: (0, i))
    consts = (g_mix, wqt, wlatt, wuvt, wckv, wuk)
    gains = (gqnt, gqrt, gckvt, gkrt, gckv, gkn)
    return pl.pallas_call(
        _mla_prep_kernel,
        grid=(T // tm,),
        in_specs=[row(D)] + [_const_spec(c.shape) for c in consts]
        + [col(cos_tt.shape[0]), col(sin_tt.shape[0])]
        + [_const_spec(g.shape) for g in gains],
        out_specs=[row(D), pl.BlockSpec((MLA_HEADS, None, MLA_HEAD_PAD, tm), lambda i: (0, i, 0, 0)), row(HP),
                   pl.BlockSpec((MLA_HEADS, None, MLA_VT_ROWS, tm), lambda i: (0, i, 0, 0))],
        out_shape=[jax.ShapeDtypeStruct((T, D), BF16),
                   jax.ShapeDtypeStruct((MLA_HEADS, T // tm, MLA_HEAD_PAD, tm), BF16),
                   jax.ShapeDtypeStruct((T, HP), BF16),
                   jax.ShapeDtypeStruct((MLA_HEADS, T // tm, MLA_VT_ROWS, tm), BF16)],
        compiler_params=_params("parallel"),
        name="mla_prep",
    )(x2, *consts, cos_tt, sin_tt, *gains)


def _mla_attn_kernel(qt_ref, qt_next_ref, k_ref, vt_ref, o_ref, s_a, s_b, s_c, max_a, max_b, max_c,
                     m_even, acc_even, m_odd, acc_odd, *, tq, heads, nq):
    pair = pl.program_id(2)
    hs = range(heads)
    buf_a, buf_b, buf_c = (s_a, max_a), (s_b, max_b), (s_c, max_c)
    even, odd = (m_even, acc_even), (m_odd, acc_odd)

    def step(prod=None, cons=None):
        if prod is not None:
            jp, q_ref, q_tile, (sp_ref, mp_ref) = prod
            r0 = pl.multiple_of(jp * tq, tq)
            for h in hs:
                q = q_ref[h] if q_tile is None else q_ref[h, q_tile]
                s = _dot(k_ref[pl.ds(r0, tq), h * MLA_HEAD_PAD:(h + 1) * MLA_HEAD_PAD], q)
                sp_ref[h] = s
                mp_ref[h] = jnp.max(s, axis=0, keepdims=True)
        if cons is not None:
            jc, (sc_ref, mc_ref), diagonal, (m_ref, acc_ref) = cons
            for h in hs:
                s = sc_ref[h]
                if diagonal:
                    kpos = lax.broadcasted_iota(jnp.int32, s.shape, 0)
                    qpos = lax.broadcasted_iota(jnp.int32, s.shape, 1)
                    s = jnp.where(kpos <= qpos, s, -1e30)
                    s_max = jnp.max(s, axis=0, keepdims=True)
                else:
                    s_max = mc_ref[h]
                m = m_ref[h]
                m_new = jnp.maximum(m, s_max)
                alpha = jnp.exp2(m - m_new)
                p = jnp.exp2(s - m_new)
                m_ref[h] = m_new
                acc_ref[h] = alpha * acc_ref[h] + _dot(vt_ref[h, jc], p.astype(BF16))

    def start(stats):
        m_ref, acc_ref = stats
        m_ref[...] = jnp.full(m_ref.shape, -1e30, F32)
        acc_ref[...] = jnp.zeros(acc_ref.shape, F32)

    def write_out(stats, tile):
        for h in hs:
            acc = stats[1][h]
            o_ref[tile * tq:(tile + 1) * tq, h * MLA_V:(h + 1) * MLA_V] = (
                acc[:MLA_V] / acc[MLA_V:MLA_V + 1]).T.astype(o_ref.dtype)

    def two_blocks(t, q_tile, stats):
        j = 2 * t
        step(prod=(j + 1, qt_ref, q_tile, buf_b), cons=(j, buf_a, False, stats))
        step(prod=(j + 2, qt_ref, q_tile, buf_a), cons=(j + 1, buf_b, False, stats))

    def full_blocks(n, q_tile, stats):
        def four_blocks(t, carry):
            two_blocks(2 * t, q_tile, stats)
            two_blocks(2 * t + 1, q_tile, stats)
            return carry

        lax.fori_loop(0, n // 2, four_blocks, 0)

        @pl.when(n % 2 == 1)
        def _():
            two_blocks(n - 1, q_tile, stats)

    i_even, i_odd = 2 * pair, 2 * pair + 1
    ahead = (jnp.minimum(i_odd + 1, nq - 1), qt_next_ref, None, buf_c)
    start(even)
    start(odd)

    @pl.when(pair == 0)
    def _():
        step(prod=(0, qt_ref, 0, buf_c))
        step(prod=(1, qt_ref, 1, buf_b), cons=(0, buf_c, True, even))
        write_out(even, 0)
        step(prod=(0, qt_ref, 1, buf_a), cons=(1, buf_b, True, odd))
        step(prod=ahead, cons=(0, buf_a, False, odd))
        write_out(odd, 1)

    @pl.when(pair > 0)
    def _():
        step(prod=(0, qt_ref, 0, buf_a), cons=(i_even, buf_c, True, even))
        full_blocks(pair - 1, 0, even)
        step(prod=(i_even - 1, qt_ref, 0, buf_b), cons=(i_even - 2, buf_a, False, even))
        step(prod=(i_odd, qt_ref, 1, buf_c), cons=(i_even - 1, buf_b, False, even))
        write_out(even, 0)
        step(prod=(0, qt_ref, 1, buf_a), cons=(i_odd, buf_c, True, odd))
        full_blocks(pair, 1, odd)
        step(prod=ahead, cons=(i_odd - 1, buf_a, False, odd))
        write_out(odd, 1)


def _mla_attn(qt, k, vt, *, batch, seq, tq, heads):
    T = k.shape[0]
    nq = seq // tq
    assert nq % 2 == 0, "q tiles are processed in (even, odd) pairs"
    half = nq // 2
    score = pltpu.VMEM((heads, tq, tq), F32)
    stat = pltpu.VMEM((heads, 1, tq), F32)
    acc = pltpu.VMEM((heads, MLA_VT_ROWS, tq), F32)
    return pl.pallas_call(
        functools.partial(_mla_attn_kernel, tq=tq, heads=heads, nq=nq),
        grid=(batch, MLA_HEADS // heads, half),
        in_specs=[pl.BlockSpec((heads, 2, MLA_HEAD_PAD, tq), lambda b, h, p: (h, b * half + p, 0, 0)),
                  pl.BlockSpec((heads, None, MLA_HEAD_PAD, tq),
                               lambda b, h, p: (h, b * nq + jnp.minimum(2 * p + 2, nq - 1), 0, 0)),
                  pl.BlockSpec((seq, heads * MLA_HEAD_PAD), lambda b, h, p: (b, h)),
                  pl.BlockSpec((heads, nq, MLA_VT_ROWS, tq), lambda b, h, p: (h, b, 0, 0))],
        out_specs=pl.BlockSpec((2 * tq, heads * MLA_V), lambda b, h, p: (b * half + p, h)),
        out_shape=jax.ShapeDtypeStruct((T, MLA_HEADS * MLA_V), BF16),
        scratch_shapes=[score, score, score, stat, stat, stat, stat, acc, stat, acc],
        compiler_params=_params("parallel", "parallel", "arbitrary"),
        name="mla_attn",
    )(qt, qt, k, vt)


def _gla_proj_kernel(hb_ref, wq_ref, wk_ref, wv_ref, wa_ref, wr_ref, wg_ref, bg_ref,
                     q_ref, k_ref, v_ref, la_ref, sr_ref):
    hb = hb_ref[...]
    a = _dot(hb, wa_ref[...])
    z = _dot(a.astype(BF16), wg_ref[...]) + bg_ref[...]
    log_sig = jnp.minimum(z, 0.0) - jnp.log1p(jnp.exp(-jnp.abs(z)))
    la_ref[...] = log_sig * (1.0 / GLA_GATE_NORMALIZER)
    r = _dot(hb, wr_ref[...])
    sr_ref[...] = (r / (1.0 + jnp.exp(-r))).astype(BF16)
    v_ref[...] = _dot(hb, wv_ref[...]).astype(BF16)
    q_ref[...] = (_dot(hb, wq_ref[...]) * float(GLA_DK ** -0.5)).astype(BF16)
    k_ref[...] = _dot(hb, wk_ref[...]).astype(BF16)


def _gla_proj(hb, wq, wk, wv, wa, wr, wg, bg, *, tm):
    T, D = hb.shape
    nk, nv = wq.shape[1], wv.shape[1]
    row = lambda w: pl.BlockSpec((tm, w), lambda i: (i, 0))
    return pl.pallas_call(
        _gla_proj_kernel,
        grid=(T // tm,),
        in_specs=[row(D)] + [_const_spec(w.shape) for w in (wq, wk, wv, wa, wr, wg, bg)],
        out_specs=[row(nk), row(nk), row(nv), row(nk), row(nv)],
        out_shape=[jax.ShapeDtypeStruct((T, nk), BF16), jax.ShapeDtypeStruct((T, nk), BF16),
                   jax.ShapeDtypeStruct((T, nv), BF16), jax.ShapeDtypeStruct((T, nk), F32),
                   jax.ShapeDtypeStruct((T, nv), BF16)],
        compiler_params=_params("parallel"),
        name="gla_proj",
    )(hb, wq, wk, wv, wa, wr, wg, bg)


GLA_LEVELS = tuple(GLA_CHUNK >> (i + 1) for i in range(GLA_CHUNK.bit_length() - 1))


def _gla_decay_matrix():
    r = np.arange(GLA_CHUNK)
    groups = [r[None, :] <= r[:, None], r[None, :] > r[:, None]]
    for s in GLA_LEVELS:
        mid = (r // (2 * s)) * (2 * s) + s
        upper = (r & s) != 0
        up = (r[None, :] > mid[:, None]) & (r[None, :] <= r[:, None])
        lo = (r[None, :] > r[:, None]) & (r[None, :] <= mid[:, None])
        groups.append(np.where(upper[:, None], up, lo))
    return np.concatenate(groups, axis=0).astype(np.float32)


def _gla_level_masks():
    r = np.arange(GLA_CHUNK)
    x = r[:, None] ^ r[None, :]
    lower = r[None, :] < r[:, None]
    return np.stack([(lower & (x >= s) & (x < 2 * s)) for s in GLA_LEVELS]).astype(np.float32)


def _gla_group(qs, ks, vs, las, st, nmat, masks_ref):
    C, n = GLA_CHUNK, len(qs)
    la2 = jnp.concatenate([jnp.concatenate(_split(la), axis=0) for la in las], axis=1)
    e_all = jnp.exp(_dot(nmat, la2))
    e = [e_all[:, u * GLA_DK:(u + 1) * GLA_DK] for u in range(n)]

    row = lax.broadcasted_iota(jnp.int32, qs[0].shape, 0)
    a = [jnp.zeros((C, C), F32) for _ in range(n)]
    for lvl, s in enumerate(GLA_LEVELS):
        upper = (row & s) != 0
        mask = masks_ref[lvl]
        for u in range(n):
            t = (jnp.where(upper, qs[u], ks[u]) * e[u][(2 + lvl) * C:(3 + lvl) * C]).astype(BF16)
            a[u] = a[u] + lax.dot_general(t, t, NT_DIMS, preferred_element_type=F32) * mask
    eye = (lax.broadcasted_iota(jnp.int32, (C, C), 0) == lax.broadcasted_iota(jnp.int32, (C, C), 1))
    o_intra, upd = [], []
    for u in range(n):
        a_u = jnp.where(eye, jnp.sum(qs[u] * ks[u], axis=-1, keepdims=True), a[u])
        o_intra.append(_dot(a_u.astype(BF16), vs[u]))
        k_dec = (ks[u] * e[u][C:2 * C]).astype(BF16)
        upd.append(lax.dot_general(k_dec, vs[u], TN_DIMS, preferred_element_type=F32))
    outs = []
    for u in range(n):
        eb = e[u][:C]
        outs.append(o_intra[u] + _dot((qs[u] * eb).astype(BF16), st.astype(BF16)))
        decay = jnp.broadcast_to(eb[C - 1:C, :], (GLA_DK, GLA_DK)).T
        st = st * jnp.concatenate([decay] * (GLA_DV // GLA_DK), axis=1) + upd[u]
    return outs, st


def _gla_rec_kernel(q_ref, k_ref, v_ref, la_ref, sr_ref, g_ref, nmat_ref, masks_ref, o_ref, st_ref, *,
                    ts, group):
    @pl.when(pl.program_id(2) == 0)
    def _():
        st_ref[...] = jnp.zeros_like(st_ref)

    g = g_ref[...]
    nmat = nmat_ref[...]
    span = group * GLA_CHUNK

    def body(c, _):
        base = pl.multiple_of(c * span, span)
        rows = [pl.ds(base + u * GLA_CHUNK, GLA_CHUNK) for u in range(group)]
        outs, st = _gla_group([q_ref[r, :].astype(F32) for r in rows], [k_ref[r, :].astype(F32) for r in rows],
                              [v_ref[r, :] for r in rows], [la_ref[r, :] for r in rows], st_ref[...],
                              nmat, masks_ref)
        st_ref[...] = st
        for r, o in zip(rows, outs):
            o_ref[r, :] = (_rms(o, g) * sr_ref[r, :].astype(F32)).astype(o_ref.dtype)
        return 0

    lax.fori_loop(0, ts // span, body, 0)


def _gla_rec(q, k, v, la, sr, g_out, nmat, masks, *, batch, seq, ts, group):
    T = q.shape[0]
    ns = seq // ts
    blk = lambda w: pl.BlockSpec((ts, w), lambda b, h, i: (b * ns + i, h))
    return pl.pallas_call(
        functools.partial(_gla_rec_kernel, ts=ts, group=group),
        grid=(batch, GLA_HEADS, ns),
        in_specs=[blk(GLA_DK), blk(GLA_DK), blk(GLA_DV), blk(GLA_DK), blk(GLA_DV),
                  _const_spec(g_out.shape), _const_spec(nmat.shape), _const_spec(masks.shape)],
        out_specs=blk(GLA_DV),
        out_shape=jax.ShapeDtypeStruct((T, GLA_HEADS * GLA_DV), BF16),
        scratch_shapes=[pltpu.VMEM((GLA_DK, GLA_DV), F32)],
        compiler_params=_params("parallel", "parallel", "arbitrary"),
        name="gla_rec",
    )(q, k, v, la, sr, g_out, nmat, masks)


def _mem_kv_kernel(mem_ref, g_ref, w_ref, gk_ref, seg_ref, k_ref, v_ref):
    kv = _dot(_rms(mem_ref[...], g_ref[...]).astype(BF16), w_ref[...])
    nk = MEM_HEADS * MEM_DQK
    k = kv[:, :nk]
    ss = _dot_split(k * k, seg_ref[...]) * (1.0 / MEM_DQK)
    k_ref[...] = (k * lax.rsqrt(ss + EPS) * gk_ref[...]).astype(BF16)
    v_ref[...] = kv[:, nk:].astype(BF16)


def _mem_kv(mem2, g_mem, w, gk, seg, *, batch, mem_len):
    D = mem2.shape[1]
    nk, nv = MEM_HEADS * MEM_DQK, MEM_HEADS * MEM_DV
    row = lambda w_: pl.BlockSpec((mem_len, w_), lambda b: (b, 0))
    return pl.pallas_call(
        _mem_kv_kernel,
        grid=(batch,),
        in_specs=[row(D), _const_spec(g_mem.shape), _const_spec(w.shape), _const_spec(gk.shape),
                  _const_spec(seg.shape)],
        out_specs=[row(nk), row(nv)],
        out_shape=[jax.ShapeDtypeStruct((batch * mem_len, nk), BF16),
                   jax.ShapeDtypeStruct((batch * mem_len, nv), BF16)],
        compiler_params=_params("parallel"),
        name="mem_kv",
    )(mem2, g_mem, w, gk, seg)


def _merge_kernel(x_ref, hb_ref, omla_ref, ogla_ref, km_ref, vm_ref, wqm_ref, gqm_ref, seg_ref,
                  wgate_ref, bgate_ref, wo_ref, o_ref):
    hb = hb_ref[...]
    D = x_ref.shape[1]
    qm = _dot(hb, wqm_ref[...])
    ss = _dot_split(qm * qm, seg_ref[...]) * (1.0 / MEM_DQK)
    qn = qm * lax.rsqrt(ss + EPS) * (gqm_ref[...] * float(MEM_DQK ** -0.5))
    head_of_lane = lax.broadcasted_iota(jnp.int32, qn.shape, 1) // MEM_DQK
    km = km_ref[...]

    def gate(j):
        z = _dot(hb, wgate_ref[:, j * D:(j + 1) * D]) + bgate_ref[:, j * D:(j + 1) * D]
        return 1.0 / (1.0 + jnp.exp(-z))

    y = gate(0) * omla_ref[...].astype(F32) + gate(1) * ogla_ref[...].astype(F32)
    g_mem = gate(2)
    parts = []
    for h in range(MEM_HEADS):
        qh = jnp.where(head_of_lane == h, qn, 0.0).astype(BF16)
        s = lax.dot_general(qh, km, NT_DIMS, preferred_element_type=F32)
        p = jnp.exp(s - jnp.max(s, axis=-1, keepdims=True))
        o_h = _dot(p.astype(BF16), vm_ref[:, h * MEM_DV:(h + 1) * MEM_DV])
        parts.append(o_h / jnp.sum(p, axis=-1, keepdims=True))
    y = y + g_mem * jnp.concatenate(parts, axis=-1)
    o_ref[...] = x_ref[...] + _dot(y.astype(BF16), wo_ref[...])


def _merge(x2, hb, o_mla, o_gla, km, vm, wqm, gqm, seg, wgate, bgate, wo, *, seq, mem_len, tm):
    T, D = x2.shape
    per_batch = seq // tm
    row = lambda w: pl.BlockSpec((tm, w), lambda i: (i, 0))
    mem_blk = lambda w: pl.BlockSpec((mem_len, w), lambda i: (i // per_batch, 0))
    return pl.pallas_call(
        _merge_kernel,
        grid=(T // tm,),
        in_specs=[row(D), row(D), row(D), row(D), mem_blk(km.shape[1]), mem_blk(vm.shape[1])]
        + [_const_spec(w.shape) for w in (wqm, gqm, seg, wgate, bgate, wo)],
        out_specs=row(D),
        out_shape=jax.ShapeDtypeStruct((T, D), F32),
        compiler_params=_params("parallel"),
        name="merge",
    )(x2, hb, o_mla, o_gla, km, vm, wqm, gqm, seg, wgate, bgate, wo)


def _ffn_kernel(x_ref, g_ref, wup_ref, wdn_ref, o_ref, *, ff_chunk):
    x = x_ref[...]
    hb = _rms(x, g_ref[...]).astype(BF16)
    acc = x
    for c in range(wup_ref.shape[1] // ff_chunk):
        u = jnp.maximum(_dot(hb, wup_ref[:, c * ff_chunk:(c + 1) * ff_chunk]), 0.0)
        acc = acc + _dot((u * u).astype(BF16), wdn_ref[c * ff_chunk:(c + 1) * ff_chunk, :])
    o_ref[...] = acc


def _ffn(x1, g_ffn, wup, wdn, *, tm, ff_chunk):
    T, D = x1.shape
    row = pl.BlockSpec((tm, D), lambda i: (i, 0))
    return pl.pallas_call(
        functools.partial(_ffn_kernel, ff_chunk=ff_chunk),
        grid=(T // tm,),
        in_specs=[row, _const_spec(g_ffn.shape), _const_spec(wup.shape), _const_spec(wdn.shape)],
        out_specs=row,
        out_shape=jax.ShapeDtypeStruct((T, D), F32),
        compiler_params=_params("parallel"),
        name="ffn",
    )(x1, g_ffn, wup, wdn)


def _pad_cols(w, n):
    return jnp.pad(w, ((0, 0), (0, n - w.shape[-1])))


def _layer(x, mem, positions, g_mix, w_in, b_gate, g_ckv, w_ukv, g_q_nope, g_k_nope, g_q_rope, g_k_rope,
           w_gla_gate, b_gla_gate, g_gla_out, g_mem, w_mem_kv, g_q_mem, g_k_mem, w_o, g_ffn, w_up, w_down):
    B, S, D = x.shape
    M = mem.shape[1]
    T = B * S
    rank = g_ckv.shape[0]
    row = lambda g: g.reshape(1, -1).astype(F32)

    sizes = (MLA_HEADS * (MLA_NOPE + MLA_ROPE), rank, MLA_ROPE, GLA_HEADS * GLA_DK, GLA_HEADS * GLA_DK,
             GLA_HEADS * GLA_DV, GLA_GATE_RANK, GLA_HEADS * GLA_DV, MEM_HEADS * MEM_DQK, N_BRANCHES * D)
    offs = np.concatenate([[0], np.cumsum(sizes)])
    (w_q, w_ckv, w_kr, w_gq, w_gk, w_gv, w_ga, w_gr, w_qm, w_gate) = [
        w_in[:, offs[i]:offs[i + 1]] for i in range(len(sizes))]

    tq = min(512, S)
    w_ukv3 = w_ukv.reshape(rank, MLA_HEADS, MLA_NOPE + MLA_V)
    wuk = w_ukv3[:, :, :MLA_NOPE].reshape(rank, -1).astype(BF16)
    wuvt = w_ukv3[:, :, MLA_NOPE:].reshape(rank, -1).T.astype(BF16)
    q_scale = float((MLA_NOPE + MLA_ROPE) ** -0.5 * np.log2(np.e))
    lanes_of = lambda g, scale=1.0: jnp.broadcast_to((g.astype(F32) * scale)[:, None], (g.shape[0], tq))

    inv = 1.0 / (ROPE_THETA ** (jnp.arange(0, MLA_ROPE, 2, dtype=F32) / MLA_ROPE))
    ang = inv[:, None] * positions.astype(F32).reshape(1, T)

    x2 = x.reshape(T, D)
    hb, qt, k, vt = _mla_prep(
        x2, row(g_mix), w_q.T.astype(BF16), jnp.concatenate([w_ckv, w_kr], axis=1).T.astype(BF16), wuvt,
        w_ckv.astype(BF16), wuk,
        jnp.cos(ang), jnp.sin(ang), lanes_of(g_q_nope, q_scale), lanes_of(g_q_rope, q_scale), lanes_of(g_ckv),
        lanes_of(g_k_rope), row(g_ckv), row(g_k_nope), tm=tq)
    o_mla = _mla_attn(qt, k, vt, batch=B, seq=S, tq=tq, heads=2)

    nmat = jnp.asarray(np.tile(_gla_decay_matrix(), (1, 2)), BF16)
    masks = jnp.asarray(_gla_level_masks())
    wg = jnp.pad(w_gla_gate, ((0, LANES - GLA_GATE_RANK), (0, 0))).astype(BF16)
    qg, kg, vg, la, sr = _gla_proj(hb, w_gq.astype(BF16), w_gk.astype(BF16), w_gv.astype(BF16),
                                   _pad_cols(w_ga, LANES).astype(BF16), w_gr.astype(BF16), wg,
                                   row(b_gla_gate), tm=min(1024, S))
    o_gla = _gla_rec(qg, kg, vg, la, sr, row(g_gla_out), nmat, masks, batch=B, seq=S, ts=min(1024, S), group=16)

    nqk = MEM_HEADS * MEM_DQK
    seg = jnp.asarray(np.kron(np.eye(MEM_HEADS), np.ones((MEM_DQK, MEM_DQK))).astype(np.float32), BF16)
    km, vm = _mem_kv(mem.reshape(B * M, D), row(g_mem), w_mem_kv.astype(BF16),
                     jnp.tile(row(g_k_mem), (1, MEM_HEADS)), seg, batch=B, mem_len=M)
    x1 = _merge(x2, hb, o_mla, o_gla, km, vm, w_qm.astype(BF16), jnp.tile(row(g_q_mem), (1, MEM_HEADS)), seg,
                w_gate.astype(BF16), row(b_gate), w_o.astype(BF16), seq=S, mem_len=M, tm=min(512, S))
    out = _ffn(x1, row(g_ffn), w_up.astype(BF16), w_down.astype(BF16), tm=min(512, S), ff_chunk=1024)
    return out.reshape(B, S, D)


def kernel(x, mem, positions, g_mix, w_in, b_gate, g_ckv, w_ukv, g_q_nope, g_k_nope, g_q_rope, g_k_rope,
           w_gla_gate, b_gla_gate, g_gla_out, g_mem, w_mem_kv, g_q_mem, g_k_mem, w_o, g_ffn, w_up, w_down):
    for l in range(g_mix.shape[0]):
        x = _layer(x, mem, positions, g_mix[l], w_in[l], b_gate[l], g_ckv[l], w_ukv[l], g_q_nope[l],
                   g_k_nope[l], g_q_rope[l], g_k_rope[l], w_gla_gate[l], b_gla_gate[l], g_gla_out[l],
                   g_mem[l], w_mem_kv[l], g_q_mem[l], g_k_mem[l], w_o[l], g_ffn[l], w_up[l], w_down[l])
    return x
```

```python
import functools

import jax
import jax.numpy as jnp
import numpy as np
from jax import lax
from jax.experimental import pallas as pl
from jax.experimental.pallas import tpu as pltpu

F32 = jnp.float32
BF16 = jnp.bfloat16

EPS = 1e-6
ROPE_THETA = 10000.0
LANES = 128

MLA_HEADS, MLA_NOPE, MLA_ROPE, MLA_V = 8, 128, 64, 128
MLA_HEAD_PAD = 2 * LANES
MLA_VT_ROWS = MLA_V + 16
GLA_HEADS, GLA_DK, GLA_DV = 4, 128, 256
GLA_GATE_RANK, GLA_GATE_NORMALIZER, GLA_CHUNK, GLA_SUB = 16, 16.0, 64, 16
MEM_HEADS, MEM_DQK, MEM_DV = 4, 64, 256
N_BRANCHES = 3

VMEM_LIMIT = 48 * 1024 * 1024

NT_DIMS = (((1,), (1,)), ((), ()))
TN_DIMS = (((0,), (0,)), ((), ()))


def _params(*sem):
    return pltpu.CompilerParams(dimension_semantics=sem, vmem_limit_bytes=VMEM_LIMIT)


def _rms(t, g, n=None):
    n = t.shape[-1] if n is None else n
    ss = jnp.sum(t * t, axis=-1, keepdims=True) * (1.0 / n)
    return t * lax.rsqrt(ss + EPS) * g


def _dot(a, b):
    return jnp.dot(a, b, preferred_element_type=F32)


def _split(a_f32):
    hi = a_f32.astype(BF16)
    return hi, (a_f32 - hi.astype(F32)).astype(BF16)


def _dot_split(a_f32, b_bf16):
    hi, lo = _split(a_f32)
    return _dot(hi, b_bf16) + _dot(lo, b_bf16)


def _const_spec(shape):
    return pl.BlockSpec(shape, lambda *_: (0,) * len(shape))


def _rms_rows(t, g):
    ss = jnp.sum(t * t, axis=0, keepdims=True) * (1.0 / t.shape[0])
    return t * lax.rsqrt(ss + EPS) * g


def _rope_rows(r, cos, sin):
    half = r.shape[0] // 2
    t1, t2 = r[:half], r[half:]
    return t1 * cos - t2 * sin, t1 * sin + t2 * cos


def _mla_prep_kernel(x_ref, gmix_ref, wqt_ref, wlatt_ref, wuvt_ref, wckv_ref, wuk_ref,
                     cost_ref, sint_ref, gqnt_ref, gqrt_ref, gckvt_ref, gkrt_ref,
                     gckv_ref, gkn_ref, hb_ref, qt_ref, k_ref, vt_ref):
    h = _rms(x_ref[...], gmix_ref[...])
    hb = h.astype(BF16)
    hb_ref[...] = hb
    hbt = h.T.astype(BF16)
    tm = hbt.shape[1]
    half = MLA_ROPE // 2
    dq = MLA_NOPE + MLA_ROPE

    cos, sin = cost_ref[...], sint_ref[...]
    rank = gckv_ref.shape[-1]
    latt = _dot(wlatt_ref[...], hbt)

    ckv = _rms(_dot(hb, wckv_ref[...]), gckv_ref[...]).astype(BF16)
    kn = _dot(ckv, wuk_ref[...])
    kr1, kr2 = _rope_rows(_rms_rows(latt[rank:], gkrt_ref[...]), cos, sin)
    kpe = jnp.concatenate([kr1, kr2, jnp.zeros((LANES - MLA_ROPE, tm), F32)], axis=0).T.astype(BF16)
    gkn = gkn_ref[...]
    for hd in range(MLA_HEADS):
        c0 = hd * MLA_HEAD_PAD
        k_ref[:, c0:c0 + LANES] = _rms(kn[:, hd * LANES:(hd + 1) * LANES], gkn).astype(BF16)
        k_ref[:, c0 + LANES:c0 + 2 * LANES] = kpe

    ckvt = _rms_rows(latt[:rank], gckvt_ref[...]).astype(BF16)
    vt = _dot(wuvt_ref[...], ckvt)
    pad_row = lax.broadcasted_iota(jnp.int32, (MLA_VT_ROWS - MLA_V, tm), 0)
    ones_rows = jnp.where(pad_row == 0, 1.0, 0.0).astype(BF16)
    for hd in range(MLA_HEADS):
        vt_ref[hd, :MLA_V, :] = vt[hd * MLA_V:(hd + 1) * MLA_V].astype(BF16)
        vt_ref[hd, MLA_V:, :] = ones_rows

    gqn, gqr = gqnt_ref[...], gqrt_ref[...]
    zeros = jnp.zeros((MLA_HEAD_PAD - dq, tm), BF16)
    for hd in range(MLA_HEADS):
        t = _dot(wqt_ref[hd * dq:(hd + 1) * dq, :], hbt)
        qt_ref[hd, :MLA_NOPE, :] = _rms_rows(t[:MLA_NOPE], gqn).astype(BF16)
        r1, r2 = _rope_rows(_rms_rows(t[MLA_NOPE:], gqr), cos, sin)
        qt_ref[hd, MLA_NOPE:MLA_NOPE + half, :] = r1.astype(BF16)
        qt_ref[hd, MLA_NOPE + half:dq, :] = r2.astype(BF16)
        qt_ref[hd, dq:, :] = zeros


def _mla_prep(x2, g_mix, wqt, wlatt, wuvt, wckv, wuk, cos_tt, sin_tt, gqnt, gqrt, gckvt, gkrt,
              gckv, gkn, *, tm):
    T, D = x2.shape
    HP = MLA_HEADS * MLA_HEAD_PAD
    row = lambda w: pl.BlockSpec((tm, w), lambda i: (i, 0))
    col = lambda r: pl.BlockSpec((r, tm), lambda i: (0, i))
    consts = (g_mix, wqt, wlatt, wuvt, wckv, wuk)
    gains = (gqnt, gqrt, gckvt, gkrt, gckv, gkn)
    return pl.pallas_call(
        _mla_prep_kernel,
        grid=(T // tm,),
        in_specs=[row(D)] + [_const_spec(c.shape) for c in consts]
        + [col(cos_tt.shape[0]), col(sin_tt.shape[0])]
        + [_const_spec(g.shape) for g in gains],
        out_specs=[row(D), pl.BlockSpec((MLA_HEADS, None, MLA_HEAD_PAD, tm), lambda i: (0, i, 0, 0)), row(HP),
                   pl.BlockSpec((MLA_HEADS, None, MLA_VT_ROWS, tm), lambda i: (0, i, 0, 0))],
        out_shape=[jax.ShapeDtypeStruct((T, D), BF16),
                   jax.ShapeDtypeStruct((MLA_HEADS, T // tm, MLA_HEAD_PAD, tm), BF16),
                   jax.ShapeDtypeStruct((T, HP), BF16),
                   jax.ShapeDtypeStruct((MLA_HEADS, T // tm, MLA_VT_ROWS, tm), BF16)],
        compiler_params=_params("parallel"),
        name="mla_prep",
    )(x2, *consts, cos_tt, sin_tt, *gains)


ATTN_TILES_PER_STEP = 4


def _mla_attn_kernel(qt_ref, qt_next_ref, k_ref, vt_ref, o_ref, s_a, s_b, s_c, max_a, max_b, max_c,
                     *stat_refs, tq, heads, nq):
    quad = pl.program_id(2)
    hs = range(heads)
    buf_a, buf_b, buf_c = (s_a, max_a), (s_b, max_b), (s_c, max_c)
    stats_of = [(stat_refs[2 * r], stat_refs[2 * r + 1]) for r in range(ATTN_TILES_PER_STEP)]

    def step(prod=None, cons=None):
        if prod is not None:
            jp, q_ref, q_tile, (sp_ref, mp_ref) = prod
            r0 = pl.multiple_of(jp * tq, tq)
            for h in hs:
                q = q_ref[h] if q_tile is None else q_ref[h, q_tile]
                s = _dot(k_ref[pl.ds(r0, tq), h * MLA_HEAD_PAD:(h + 1) * MLA_HEAD_PAD], q)
                sp_ref[h] = s
                mp_ref[h] = jnp.max(s, axis=0, keepdims=True)
        if cons is not None:
            jc, (sc_ref, mc_ref), diagonal, (m_ref, acc_ref) = cons
            for h in hs:
                s = sc_ref[h]
                if diagonal:
                    kpos = lax.broadcasted_iota(jnp.int32, s.shape, 0)
                    qpos = lax.broadcasted_iota(jnp.int32, s.shape, 1)
                    s = jnp.where(kpos <= qpos, s, -1e30)
                    s_max = jnp.max(s, axis=0, keepdims=True)
                else:
                    s_max = mc_ref[h]
                m = m_ref[h]
                m_new = jnp.maximum(m, s_max)
                alpha = jnp.exp2(m - m_new)
                p = jnp.exp2(s - m_new)
                m_ref[h] = m_new
                acc_ref[h] = alpha * acc_ref[h] + _dot(vt_ref[h, jc], p.astype(BF16))

    def start(stats):
        m_ref, acc_ref = stats
        m_ref[...] = jnp.full(m_ref.shape, -1e30, F32)
        acc_ref[...] = jnp.zeros(acc_ref.shape, F32)

    def write_out(stats, tile):
        for h in hs:
            acc = stats[1][h]
            o_ref[tile * tq:(tile + 1) * tq, h * MLA_V:(h + 1) * MLA_V] = (
                acc[:MLA_V] / acc[MLA_V:MLA_V + 1]).T.astype(o_ref.dtype)

    def two_blocks(t, q_tile, stats):
        j = 2 * t
        step(prod=(j + 1, qt_ref, q_tile, buf_b), cons=(j, buf_a, False, stats))
        step(prod=(j + 2, qt_ref, q_tile, buf_a), cons=(j + 1, buf_b, False, stats))

    def run_tile(i, r, trips, leftover, diag_buf, next_prod):
        stats = stats_of[r]
        step(prod=(0, qt_ref, r, buf_a), cons=(i, diag_buf, True, stats))
        if not (isinstance(trips, int) and trips == 0):
            def four_blocks(t, carry):
                two_blocks(2 * t, r, stats)
                two_blocks(2 * t + 1, r, stats)
                return carry

            lax.fori_loop(0, trips, four_blocks, 0)
        if leftover:
            two_blocks(2 * trips, r, stats)
        if r % 2 == 0:
            step(prod=(i - 1, qt_ref, r, buf_b), cons=(i - 2, buf_a, False, stats))
            step(prod=next_prod, cons=(i - 1, buf_b, False, stats))
        else:
            step(prod=next_prod, cons=(i - 1, buf_a, False, stats))
        write_out(stats, r)

    first = ATTN_TILES_PER_STEP * quad
    ahead = (jnp.minimum(first + ATTN_TILES_PER_STEP, nq - 1), qt_next_ref, None, buf_c)
    diag_of = lambda r: (first + r, qt_ref, r, buf_c)
    for stats in stats_of:
        start(stats)

    @pl.when(quad == 0)
    def _():
        step(prod=(0, qt_ref, 0, buf_c))
        step(prod=(1, qt_ref, 1, buf_b), cons=(0, buf_c, True, stats_of[0]))
        write_out(stats_of[0], 0)
        run_tile(1, 1, 0, 0, buf_b, diag_of(2))
        run_tile(2, 2, 0, 0, buf_c, diag_of(3))
        run_tile(3, 3, 0, 1, buf_c, ahead)

    @pl.when(quad > 0)
    def _():
        run_tile(first, 0, quad - 1, 1, buf_c, diag_of(1))
        run_tile(first + 1, 1, quad, 0, buf_c, diag_of(2))
        run_tile(first + 2, 2, quad, 0, buf_c, diag_of(3))
        run_tile(first + 3, 3, quad, 1, buf_c, ahead)


def _mla_attn(qt, k, vt, *, batch, seq, tq, heads):
    T = k.shape[0]
    nq = seq // tq
    per = ATTN_TILES_PER_STEP
    assert nq % per == 0, "q tiles are processed in groups of ATTN_TILES_PER_STEP"
    steps = nq // per
    score = pltpu.VMEM((heads, tq, tq), F32)
    stat = pltpu.VMEM((heads, 1, tq), F32)
    acc = pltpu.VMEM((heads, MLA_VT_ROWS, tq), F32)
    return pl.pallas_call(
        functools.partial(_mla_attn_kernel, tq=tq, heads=heads, nq=nq),
        grid=(batch, MLA_HEADS // heads, steps),
        in_specs=[pl.BlockSpec((heads, per, MLA_HEAD_PAD, tq), lambda b, h, s: (h, b * steps + s, 0, 0)),
                  pl.BlockSpec((heads, None, MLA_HEAD_PAD, tq),
                               lambda b, h, s: (h, b * nq + jnp.minimum(per * s + per, nq - 1), 0, 0)),
                  pl.BlockSpec((seq, heads * MLA_HEAD_PAD), lambda b, h, s: (b, h)),
                  pl.BlockSpec((heads, nq, MLA_VT_ROWS, tq), lambda b, h, s: (h, b, 0, 0))],
        out_specs=pl.BlockSpec((per * tq, heads * MLA_V), lambda b, h, s: (b * steps + s, h)),
        out_shape=jax.ShapeDtypeStruct((T, MLA_HEADS * MLA_V), BF16),
        scratch_shapes=[score, score, score, stat, stat, stat] + [stat, acc] * per,
        compiler_params=_params("parallel", "parallel", "arbitrary"),
        name="mla_attn",
    )(qt, qt, k, vt)


def _gla_proj_kernel(hb_ref, wq_ref, wk_ref, wv_ref, wa_ref, wr_ref, wg_ref, bg_ref,
                     q_ref, k_ref, v_ref, la_ref, sr_ref):
    hb = hb_ref[...]
    a = _dot(hb, wa_ref[...])
    z = _dot(a.astype(BF16), wg_ref[...]) + bg_ref[...]
    log_sig = jnp.minimum(z, 0.0) - jnp.log1p(jnp.exp(-jnp.abs(z)))
    la_ref[...] = log_sig * (1.0 / GLA_GATE_NORMALIZER)
    r = _dot(hb, wr_ref[...])
    sr_ref[...] = (r / (1.0 + jnp.exp(-r))).astype(BF16)
    v_ref[...] = _dot(hb, wv_ref[...]).astype(BF16)
    q_ref[...] = (_dot(hb, wq_ref[...]) * float(GLA_DK ** -0.5)).astype(BF16)
    k_ref[...] = _dot(hb, wk_ref[...]).astype(BF16)


def _gla_proj(hb, wq, wk, wv, wa, wr, wg, bg, *, tm):
    T, D = hb.shape
    nk, nv = wq.shape[1], wv.shape[1]
    row = lambda w: pl.BlockSpec((tm, w), lambda i: (i, 0))
    return pl.pallas_call(
        _gla_proj_kernel,
        grid=(T // tm,),
        in_specs=[row(D)] + [_const_spec(w.shape) for w in (wq, wk, wv, wa, wr, wg, bg)],
        out_specs=[row(nk), row(nk), row(nv), row(nk), row(nv)],
        out_shape=[jax.ShapeDtypeStruct((T, nk), BF16), jax.ShapeDtypeStruct((T, nk), BF16),
                   jax.ShapeDtypeStruct((T, nv), BF16), jax.ShapeDtypeStruct((T, nk), F32),
                   jax.ShapeDtypeStruct((T, nv), BF16)],
        compiler_params=_params("parallel"),
        name="gla_proj",
    )(hb, wq, wk, wv, wa, wr, wg, bg)


GLA_LEVELS = tuple(GLA_CHUNK >> (i + 1) for i in range(GLA_CHUNK.bit_length() - 1))


def _gla_decay_matrix():
    r = np.arange(GLA_CHUNK)
    groups = [r[None, :] <= r[:, None], r[None, :] > r[:, None]]
    for s in GLA_LEVELS:
        mid = (r // (2 * s)) * (2 * s) + s
        upper = (r & s) != 0
        up = (r[None, :] > mid[:, None]) & (r[None, :] <= r[:, None])
        lo = (r[None, :] > r[:, None]) & (r[None, :] <= mid[:, None])
        groups.append(np.where(upper[:, None], up, lo))
    return np.concatenate(groups, axis=0).astype(np.float32)


def _gla_level_masks():
    r = np.arange(GLA_CHUNK)
    x = r[:, None] ^ r[None, :]
    lower = r[None, :] < r[:, None]
    return np.stack([(lower & (x >= s) & (x < 2 * s)) for s in GLA_LEVELS]).astype(np.float32)


def _gla_group(qs, ks, vs, las, st, nmat, masks_ref):
    C, n = GLA_CHUNK, len(qs)
    la2 = jnp.concatenate([jnp.concatenate(_split(la), axis=0) for la in las], axis=1)
    e_all = jnp.exp(_dot(nmat, la2))
    e = [e_all[:, u * GLA_DK:(u + 1) * GLA_DK] for u in range(n)]

    row = lax.broadcasted_iota(jnp.int32, qs[0].shape, 0)
    a = [jnp.zeros((C, C), F32) for _ in range(n)]
    for lvl, s in enumerate(GLA_LEVELS):
        upper = (row & s) != 0
        mask = masks_ref[lvl]
        for u in range(n):
            t = (jnp.where(upper, qs[u], ks[u]) * e[u][(2 + lvl) * C:(3 + lvl) * C]).astype(BF16)
            a[u] = a[u] + lax.dot_general(t, t, NT_DIMS, preferred_element_type=F32) * mask
    eye = (lax.broadcasted_iota(jnp.int32, (C, C), 0) == lax.broadcasted_iota(jnp.int32, (C, C), 1))
    o_intra, upd = [], []
    for u in range(n):
        a_u = jnp.where(eye, jnp.sum(qs[u] * ks[u], axis=-1, keepdims=True), a[u])
        o_intra.append(_dot(a_u.astype(BF16), vs[u]))
        k_dec = (ks[u] * e[u][C:2 * C]).astype(BF16)
        upd.append(lax.dot_general(k_dec, vs[u], TN_DIMS, preferred_element_type=F32))
    outs = []
    for u in range(n):
        eb = e[u][:C]
        outs.append(o_intra[u] + _dot((qs[u] * eb).astype(BF16), st.astype(BF16)))
        decay = jnp.broadcast_to(eb[C - 1:C, :], (GLA_DK, GLA_DK)).T
        st = st * jnp.concatenate([decay] * (GLA_DV // GLA_DK), axis=1) + upd[u]
    return outs, st


def _gla_rec_kernel(q_ref, k_ref, v_ref, la_ref, sr_ref, g_ref, nmat_ref, masks_ref, o_ref, st_ref, *,
                    ts, group):
    @pl.when(pl.program_id(2) == 0)
    def _():
        st_ref[...] = jnp.zeros_like(st_ref)

    g = g_ref[...]
    nmat = nmat_ref[...]
    span = group * GLA_CHUNK

    def body(c, _):
        base = pl.multiple_of(c * span, span)
        rows = [pl.ds(base + u * GLA_CHUNK, GLA_CHUNK) for u in range(group)]
        outs, st = _gla_group([q_ref[r, :].astype(F32) for r in rows], [k_ref[r, :].astype(F32) for r in rows],
                              [v_ref[r, :] for r in rows], [la_ref[r, :] for r in rows], st_ref[...],
                              nmat, masks_ref)
        st_ref[...] = st
        for r, o in zip(rows, outs):
            o_ref[r, :] = (_rms(o, g) * sr_ref[r, :].astype(F32)).astype(o_ref.dtype)
        return 0

    lax.fori_loop(0, ts // span, body, 0)


def _gla_rec(q, k, v, la, sr, g_out, nmat, masks, *, batch, seq, ts, group):
    T = q.shape[0]
    ns = seq // ts
    blk = lambda w: pl.BlockSpec((ts, w), lambda b, h, i: (b * ns + i, h))
    return pl.pallas_call(
        functools.partial(_gla_rec_kernel, ts=ts, group=group),
        grid=(batch, GLA_HEADS, ns),
        in_specs=[blk(GLA_DK), blk(GLA_DK), blk(GLA_DV), blk(GLA_DK), blk(GLA_DV),
                  _const_spec(g_out.shape), _const_spec(nmat.shape), _const_spec(masks.shape)],
        out_specs=blk(GLA_DV),
        out_shape=jax.ShapeDtypeStruct((T, GLA_HEADS * GLA_DV), BF16),
        scratch_shapes=[pltpu.VMEM((GLA_DK, GLA_DV), F32)],
        compiler_params=_params("parallel", "parallel", "arbitrary"),
        name="gla_rec",
    )(q, k, v, la, sr, g_out, nmat, masks)


def _mem_kv_kernel(mem_ref, g_ref, w_ref, gk_ref, seg_ref, k_ref, v_ref):
    kv = _dot(_rms(mem_ref[...], g_ref[...]).astype(BF16), w_ref[...])
    nk = MEM_HEADS * MEM_DQK
    k = kv[:, :nk]
    ss = _dot_split(k * k, seg_ref[...]) * (1.0 / MEM_DQK)
    k_ref[...] = (k * lax.rsqrt(ss + EPS) * gk_ref[...]).astype(BF16)
    v_ref[...] = kv[:, nk:].astype(BF16)


def _mem_kv(mem2, g_mem, w, gk, seg, *, batch, mem_len):
    D = mem2.shape[1]
    nk, nv = MEM_HEADS * MEM_DQK, MEM_HEADS * MEM_DV
    row = lambda w_: pl.BlockSpec((mem_len, w_), lambda b: (b, 0))
    return pl.pallas_call(
        _mem_kv_kernel,
        grid=(batch,),
        in_specs=[row(D), _const_spec(g_mem.shape), _const_spec(w.shape), _const_spec(gk.shape),
                  _const_spec(seg.shape)],
        out_specs=[row(nk), row(nv)],
        out_shape=[jax.ShapeDtypeStruct((batch * mem_len, nk), BF16),
                   jax.ShapeDtypeStruct((batch * mem_len, nv), BF16)],
        compiler_params=_params("parallel"),
        name="mem_kv",
    )(mem2, g_mem, w, gk, seg)


def _merge_kernel(x_ref, hb_ref, omla_ref, ogla_ref, km_ref, vm_ref, wqm_ref, gqm_ref, seg_ref,
                  wgate_ref, bgate_ref, wo_ref, o_ref):
    hb = hb_ref[...]
    D = x_ref.shape[1]
    qm = _dot(hb, wqm_ref[...])
    ss = _dot_split(qm * qm, seg_ref[...]) * (1.0 / MEM_DQK)
    qn = qm * lax.rsqrt(ss + EPS) * (gqm_ref[...] * float(MEM_DQK ** -0.5))
    head_of_lane = lax.broadcasted_iota(jnp.int32, qn.shape, 1) // MEM_DQK
    km = km_ref[...]

    def gate(j):
        z = _dot(hb, wgate_ref[:, j * D:(j + 1) * D]) + bgate_ref[:, j * D:(j + 1) * D]
        return 1.0 / (1.0 + jnp.exp(-z))

    y = gate(0) * omla_ref[...].astype(F32) + gate(1) * ogla_ref[...].astype(F32)
    g_mem = gate(2)
    parts = []
    for h in range(MEM_HEADS):
        qh = jnp.where(head_of_lane == h, qn, 0.0).astype(BF16)
        s = lax.dot_general(qh, km, NT_DIMS, preferred_element_type=F32)
        p = jnp.exp(s - jnp.max(s, axis=-1, keepdims=True))
        o_h = _dot(p.astype(BF16), vm_ref[:, h * MEM_DV:(h + 1) * MEM_DV])
        parts.append(o_h / jnp.sum(p, axis=-1, keepdims=True))
    y = y + g_mem * jnp.concatenate(parts, axis=-1)
    o_ref[...] = x_ref[...] + _dot(y.astype(BF16), wo_ref[...])


def _merge(x2, hb, o_mla, o_gla, km, vm, wqm, gqm, seg, wgate, bgate, wo, *, seq, mem_len, tm):
    T, D = x2.shape
    per_batch = seq // tm
    row = lambda w: pl.BlockSpec((tm, w), lambda i: (i, 0))
    mem_blk = lambda w: pl.BlockSpec((mem_len, w), lambda i: (i // per_batch, 0))
    return pl.pallas_call(
        _merge_kernel,
        grid=(T // tm,),
        in_specs=[row(D), row(D), row(D), row(D), mem_blk(km.shape[1]), mem_blk(vm.shape[1])]
        + [_const_spec(w.shape) for w in (wqm, gqm, seg, wgate, bgate, wo)],
        out_specs=row(D),
        out_shape=jax.ShapeDtypeStruct((T, D), F32),
        compiler_params=_params("parallel"),
        name="merge",
    )(x2, hb, o_mla, o_gla, km, vm, wqm, gqm, seg, wgate, bgate, wo)


def _ffn_kernel(x_ref, g_ref, wup_ref, wdn_ref, o_ref, *, ff_chunk):
    x = x_ref[...]
    hb = _rms(x, g_ref[...]).astype(BF16)
    acc = x
    for c in range(wup_ref.shape[1] // ff_chunk):
        u = jnp.maximum(_dot(hb, wup_ref[:, c * ff_chunk:(c + 1) * ff_chunk]), 0.0)
        acc = acc + _dot((u * u).astype(BF16), wdn_ref[c * ff_chunk:(c + 1) * ff_chunk, :])
    o_ref[...] = acc


def _ffn(x1, g_ffn, wup, wdn, *, tm, ff_chunk):
    T, D = x1.shape
    row = pl.BlockSpec((tm, D), lambda i: (i, 0))
    return pl.pallas_call(
        functools.partial(_ffn_kernel, ff_chunk=ff_chunk),
        grid=(T // tm,),
        in_specs=[row, _const_spec(g_ffn.shape), _const_spec(wup.shape), _const_spec(wdn.shape)],
        out_specs=row,
        out_shape=jax.ShapeDtypeStruct((T, D), F32),
        compiler_params=_params("parallel"),
        name="ffn",
    )(x1, g_ffn, wup, wdn)


def _pad_cols(w, n):
    return jnp.pad(w, ((0, 0), (0, n - w.shape[-1])))


def _layer(x, mem, positions, g_mix, w_in, b_gate, g_ckv, w_ukv, g_q_nope, g_k_nope, g_q_rope, g_k_rope,
           w_gla_gate, b_gla_gate, g_gla_out, g_mem, w_mem_kv, g_q_mem, g_k_mem, w_o, g_ffn, w_up, w_down):
    B, S, D = x.shape
    M = mem.shape[1]
    T = B * S
    rank = g_ckv.shape[0]
    row = lambda g: g.reshape(1, -1).astype(F32)

    sizes = (MLA_HEADS * (MLA_NOPE + MLA_ROPE), rank, MLA_ROPE, GLA_HEADS * GLA_DK, GLA_HEADS * GLA_DK,
             GLA_HEADS * GLA_DV, GLA_GATE_RANK, GLA_HEADS * GLA_DV, MEM_HEADS * MEM_DQK, N_BRANCHES * D)
    offs = np.concatenate([[0], np.cumsum(sizes)])
    (w_q, w_ckv, w_kr, w_gq, w_gk, w_gv, w_ga, w_gr, w_qm, w_gate) = [
        w_in[:, offs[i]:offs[i + 1]] for i in range(len(sizes))]

    tq = min(512, S)
    w_ukv3 = w_ukv.reshape(rank, MLA_HEADS, MLA_NOPE + MLA_V)
    wuk = w_ukv3[:, :, :MLA_NOPE].reshape(rank, -1).astype(BF16)
    wuvt = w_ukv3[:, :, MLA_NOPE:].reshape(rank, -1).T.astype(BF16)
    q_scale = float((MLA_NOPE + MLA_ROPE) ** -0.5 * np.log2(np.e))
    lanes_of = lambda g, scale=1.0: jnp.broadcast_to((g.astype(F32) * scale)[:, None], (g.shape[0], tq))

    inv = 1.0 / (ROPE_THETA ** (jnp.arange(0, MLA_ROPE, 2, dtype=F32) / MLA_ROPE))
    ang = inv[:, None] * positions.astype(F32).reshape(1, T)

    x2 = x.reshape(T, D)
    hb, qt, k, vt = _mla_prep(
        x2, row(g_mix), w_q.T.astype(BF16), jnp.concatenate([w_ckv, w_kr], axis=1).T.astype(BF16), wuvt,
        w_ckv.astype(BF16), wuk,
        jnp.cos(ang), jnp.sin(ang), lanes_of(g_q_nope, q_scale), lanes_of(g_q_rope, q_scale), lanes_of(g_ckv),
        lanes_of(g_k_rope), row(g_ckv), row(g_k_nope), tm=tq)
    o_mla = _mla_attn(qt, k, vt, batch=B, seq=S, tq=tq, heads=2)

    nmat = jnp.asarray(np.tile(_gla_decay_matrix(), (1, 2)), BF16)
    masks = jnp.asarray(_gla_level_masks())
    wg = jnp.pad(w_gla_gate, ((0, LANES - GLA_GATE_RANK), (0, 0))).astype(BF16)
    qg, kg, vg, la, sr = _gla_proj(hb, w_gq.astype(BF16), w_gk.astype(BF16), w_gv.astype(BF16),
                                   _pad_cols(w_ga, LANES).astype(BF16), w_gr.astype(BF16), wg,
                                   row(b_gla_gate), tm=min(1024, S))
    o_gla = _gla_rec(qg, kg, vg, la, sr, row(g_gla_out), nmat, masks, batch=B, seq=S, ts=min(1024, S), group=16)

    nqk = MEM_HEADS * MEM_DQK
    seg = jnp.asarray(np.kron(np.eye(MEM_HEADS), np.ones((MEM_DQK, MEM_DQK))).astype(np.float32), BF16)
    km, vm = _mem_kv(mem.reshape(B * M, D), row(g_mem), w_mem_kv.astype(BF16),
                     jnp.tile(row(g_k_mem), (1, MEM_HEADS)), seg, batch=B, mem_len=M)
    x1 = _merge(x2, hb, o_mla, o_gla, km, vm, w_qm.astype(BF16), jnp.tile(row(g_q_mem), (1, MEM_HEADS)), seg,
                w_gate.astype(BF16), row(b_gate), w_o.astype(BF16), seq=S, mem_len=M, tm=min(512, S))
    out = _ffn(x1, row(g_ffn), w_up.astype(BF16), w_down.astype(BF16), tm=min(512, S), ff_chunk=1024)
    return out.reshape(B, S, D)


def kernel(x, mem, positions, g_mix, w_in, b_gate, g_ckv, w_ukv, g_q_nope, g_k_nope, g_q_rope, g_k_rope,
           w_gla_gate, b_gla_gate, g_gla_out, g_mem, w_mem_kv, g_q_mem, g_k_mem, w_o, g_ffn, w_up, w_down):
    for l in range(g_mix.shape[0]):
        x = _layer(x, mem, positions, g_mix[l], w_in[l], b_gate[l], g_ckv[l], w_ukv[l], g_q_nope[l],
                   g_k_nope[l], g_q_rope[l], g_k_rope[l], w_gla_gate[l], b_gla_gate[l], g_gla_out[l],
                   g_mem[l], w_mem_kv[l], g_q_mem[l], g_k_mem[l], w_o[l], g_ffn[l], w_up[l], w_down[l])
    return x
```

```python
import functools

import jax
import jax.numpy as jnp
import numpy as np
from jax import lax
from jax.experimental import pallas as pl
from jax.experimental.pallas import tpu as pltpu

F32 = jnp.float32
BF16 = jnp.bfloat16

EPS = 1e-6
ROPE_THETA = 10000.0
LANES = 128
BF16_SUBLANES = 16

MLA_HEADS, MLA_NOPE, MLA_ROPE, MLA_V = 8, 128, 64, 128
MLA_HEAD_PAD = 2 * LANES
MLA_VT_ROWS = MLA_V + BF16_SUBLANES
GLA_HEADS, GLA_DK, GLA_DV = 4, 128, 256
GLA_GATE_RANK, GLA_GATE_NORMALIZER, GLA_CHUNK = 16, 16.0, 64
MEM_HEADS, MEM_DQK, MEM_DV = 4, 64, 256
N_BRANCHES = 3

VMEM_LIMIT = 48 * 1024 * 1024

ATTN_TILE = 512
ATTN_HEADS_PER_STEP = 2
GLA_PROJ_TILE = 1024
GLA_STEP_TOKENS = 1024
MERGE_TILE = 512
FFN_TILE = 512
FFN_CHUNK = 1024

NT_DIMS = (((1,), (1,)), ((), ()))
TN_DIMS = (((0,), (0,)), ((), ()))


def _params(*sem):
    return pltpu.CompilerParams(dimension_semantics=sem, vmem_limit_bytes=VMEM_LIMIT)


def _rms(t, g, n=None):
    n = t.shape[-1] if n is None else n
    ss = jnp.sum(t * t, axis=-1, keepdims=True) * (1.0 / n)
    return t * lax.rsqrt(ss + EPS) * g


def _dot(a, b):
    return jnp.dot(a, b, preferred_element_type=F32)


def _split(a_f32):
    hi = a_f32.astype(BF16)
    return hi, (a_f32 - hi.astype(F32)).astype(BF16)


def _dot_split(a_f32, b_bf16):
    hi, lo = _split(a_f32)
    return _dot(hi, b_bf16) + _dot(lo, b_bf16)


def _const_spec(shape):
    return pl.BlockSpec(shape, lambda *_: (0,) * len(shape))


def _rms_rows(t, g):
    ss = jnp.sum(t * t, axis=0, keepdims=True) * (1.0 / t.shape[0])
    return t * lax.rsqrt(ss + EPS) * g


def _rope_rows(r, cos, sin):
    half = r.shape[0] // 2
    t1, t2 = r[:half], r[half:]
    return t1 * cos - t2 * sin, t1 * sin + t2 * cos


def _mla_prep_kernel(x_ref, gmix_ref, wqt_ref, wlatt_ref, wuvt_ref, wckv_ref, wuk_ref,
                     cost_ref, sint_ref, gqnt_ref, gqrt_ref, gckvt_ref, gkrt_ref,
                     gckv_ref, gkn_ref, hb_ref, qt_ref, k_ref, vt_ref):
    h = _rms(x_ref[...], gmix_ref[...])
    hb = h.astype(BF16)
    hb_ref[...] = hb
    hbt = h.T.astype(BF16)
    tm = hbt.shape[1]
    half = MLA_ROPE // 2
    dq = MLA_NOPE + MLA_ROPE

    cos, sin = cost_ref[...], sint_ref[...]
    rank = gckv_ref.shape[-1]
    gqn, gqr = gqnt_ref[...], gqrt_ref[...]
    zeros = jnp.zeros((MLA_HEAD_PAD - dq, tm), BF16)

    def q_heads(heads):
        for hd in heads:
            t = _dot(wqt_ref[hd * dq:(hd + 1) * dq, :], hbt)
            qt_ref[hd, :MLA_NOPE, :] = _rms_rows(t[:MLA_NOPE], gqn).astype(BF16)
            r1, r2 = _rope_rows(_rms_rows(t[MLA_NOPE:], gqr), cos, sin)
            qt_ref[hd, MLA_NOPE:MLA_NOPE + half, :] = r1.astype(BF16)
            qt_ref[hd, MLA_NOPE + half:dq, :] = r2.astype(BF16)
            qt_ref[hd, dq:, :] = zeros

    ckv_raw = _dot(hb, wckv_ref[...])
    latt = _dot(wlatt_ref[...], hbt)
    q_heads(range(0, MLA_HEADS // 2))

    ckv = _rms(ckv_raw, gckv_ref[...]).astype(BF16)
    kn = _dot(ckv, wuk_ref[...])
    ckvt = _rms_rows(latt[:rank], gckvt_ref[...]).astype(BF16)
    vt = _dot(wuvt_ref[...], ckvt)
    q_heads(range(MLA_HEADS // 2, MLA_HEADS))

    kr1, kr2 = _rope_rows(_rms_rows(latt[rank:], gkrt_ref[...]), cos, sin)
    kpe = jnp.concatenate([kr1, kr2, jnp.zeros((LANES - MLA_ROPE, tm), F32)], axis=0).T.astype(BF16)
    gkn = gkn_ref[...]
    pad_row = lax.broadcasted_iota(jnp.int32, (MLA_VT_ROWS - MLA_V, tm), 0)
    ones_rows = jnp.where(pad_row == 0, 1.0, 0.0).astype(BF16)
    for hd in range(MLA_HEADS):
        c0 = hd * MLA_HEAD_PAD
        k_ref[:, c0:c0 + LANES] = _rms(kn[:, hd * LANES:(hd + 1) * LANES], gkn).astype(BF16)
        k_ref[:, c0 + LANES:c0 + 2 * LANES] = kpe
        vt_ref[hd, :MLA_V, :] = vt[hd * MLA_V:(hd + 1) * MLA_V].astype(BF16)
        vt_ref[hd, MLA_V:, :] = ones_rows


def _mla_prep(x2, g_mix, wqt, wlatt, wuvt, wckv, wuk, cos_tt, sin_tt, gqnt, gqrt, gckvt, gkrt,
              gckv, gkn, *, tm):
    T, D = x2.shape
    HP = MLA_HEADS * MLA_HEAD_PAD
    row = lambda w: pl.BlockSpec((tm, w), lambda i: (i, 0))
    col = lambda r: pl.BlockSpec((r, tm), lambda i: (0, i))
    consts = (g_mix, wqt, wlatt, wuvt, wckv, wuk)
    gains = (gqnt, gqrt, gckvt, gkrt, gckv, gkn)
    return pl.pallas_call(
        _mla_prep_kernel,
        grid=(T // tm,),
        in_specs=[row(D)] + [_const_spec(c.shape) for c in consts]
        + [col(cos_tt.shape[0]), col(sin_tt.shape[0])]
        + [_const_spec(g.shape) for g in gains],
        out_specs=[row(D), pl.BlockSpec((MLA_HEADS, None, MLA_HEAD_PAD, tm), lambda i: (0, i, 0, 0)), row(HP),
                   pl.BlockSpec((MLA_HEADS, None, MLA_VT_ROWS, tm), lambda i: (0, i, 0, 0))],
        out_shape=[jax.ShapeDtypeStruct((T, D), BF16),
                   jax.ShapeDtypeStruct((MLA_HEADS, T // tm, MLA_HEAD_PAD, tm), BF16),
                   jax.ShapeDtypeStruct((T, HP), BF16),
                   jax.ShapeDtypeStruct((MLA_HEADS, T // tm, MLA_VT_ROWS, tm), BF16)],
        compiler_params=_params("parallel"),
        name="mla_prep",
    )(x2, *consts, cos_tt, sin_tt, *gains)


ATTN_TILES_PER_STEP = 4


def _mla_attn_kernel(qt_ref, qt_next_ref, k_ref, vt_ref, o_ref, s_a, s_b, s_c, max_a, max_b, max_c,
                     *stat_refs, tq, heads, nq):
    quad = pl.program_id(2)
    hs = range(heads)
    buf_a, buf_b, buf_c = (s_a, max_a), (s_b, max_b), (s_c, max_c)
    stats_of = [(stat_refs[2 * r], stat_refs[2 * r + 1]) for r in range(ATTN_TILES_PER_STEP)]

    def step(prod=None, cons=None):
        if prod is not None:
            jp, q_ref, q_tile, (sp_ref, mp_ref) = prod
            r0 = pl.multiple_of(jp * tq, tq)
            for h in hs:
                q = q_ref[h] if q_tile is None else q_ref[h, q_tile]
                s = _dot(k_ref[pl.ds(r0, tq), h * MLA_HEAD_PAD:(h + 1) * MLA_HEAD_PAD], q)
                sp_ref[h] = s
                mp_ref[h] = jnp.max(s, axis=0, keepdims=True)
        if cons is not None:
            jc, (sc_ref, mc_ref), diagonal, (m_ref, acc_ref) = cons
            for h in hs:
                s = sc_ref[h]
                if diagonal:
                    kpos = lax.broadcasted_iota(jnp.int32, s.shape, 0)
                    qpos = lax.broadcasted_iota(jnp.int32, s.shape, 1)
                    s = jnp.where(kpos <= qpos, s, -1e30)
                    s_max = jnp.max(s, axis=0, keepdims=True)
                else:
                    s_max = mc_ref[h]
                m = m_ref[h]
                m_new = jnp.maximum(m, s_max)
                alpha = jnp.exp2(m - m_new)
                p = jnp.exp2(s - m_new)
                m_ref[h] = m_new
                acc_ref[h] = alpha * acc_ref[h] + _dot(vt_ref[h, jc], p.astype(BF16))

    def start(stats):
        m_ref, acc_ref = stats
        m_ref[...] = jnp.full(m_ref.shape, -1e30, F32)
        acc_ref[...] = jnp.zeros(acc_ref.shape, F32)

    def write_out(stats, tile):
        for h in hs:
            acc = stats[1][h]
            o_ref[tile * tq:(tile + 1) * tq, h * MLA_V:(h + 1) * MLA_V] = (
                acc[:MLA_V] / acc[MLA_V:MLA_V + 1]).T.astype(o_ref.dtype)

    def two_blocks(t, q_tile, stats):
        j = 2 * t
        step(prod=(j + 1, qt_ref, q_tile, buf_b), cons=(j, buf_a, False, stats))
        step(prod=(j + 2, qt_ref, q_tile, buf_a), cons=(j + 1, buf_b, False, stats))

    def run_tile(i, r, trips, leftover, diag_buf, next_prod):
        stats = stats_of[r]
        step(prod=(0, qt_ref, r, buf_a), cons=(i, diag_buf, True, stats))
        if not (isinstance(trips, int) and trips == 0):
            def four_blocks(t, carry):
                two_blocks(2 * t, r, stats)
                two_blocks(2 * t + 1, r, stats)
                return carry

            lax.fori_loop(0, trips, four_blocks, 0)
        if leftover:
            two_blocks(2 * trips, r, stats)
        if r % 2 == 0:
            step(prod=(i - 1, qt_ref, r, buf_b), cons=(i - 2, buf_a, False, stats))
            step(prod=next_prod, cons=(i - 1, buf_b, False, stats))
        else:
            step(prod=next_prod, cons=(i - 1, buf_a, False, stats))
        write_out(stats, r)

    first = ATTN_TILES_PER_STEP * quad
    ahead = (jnp.minimum(first + ATTN_TILES_PER_STEP, nq - 1), qt_next_ref, None, buf_c)
    diag_of = lambda r: (first + r, qt_ref, r, buf_c)
    for stats in stats_of:
        start(stats)

    @pl.when(quad == 0)
    def _():
        step(prod=(0, qt_ref, 0, buf_c))
        step(prod=(1, qt_ref, 1, buf_b), cons=(0, buf_c, True, stats_of[0]))
        write_out(stats_of[0], 0)
        run_tile(1, 1, 0, 0, buf_b, diag_of(2))
        run_tile(2, 2, 0, 0, buf_c, diag_of(3))
        run_tile(3, 3, 0, 1, buf_c, ahead)

    @pl.when(quad > 0)
    def _():
        run_tile(first, 0, quad - 1, 1, buf_c, diag_of(1))
        run_tile(first + 1, 1, quad, 0, buf_c, diag_of(2))
        run_tile(first + 2, 2, quad, 0, buf_c, diag_of(3))
        run_tile(first + 3, 3, quad, 1, buf_c, ahead)


def _mla_attn(qt, k, vt, *, batch, seq, tq, heads):
    T = k.shape[0]
    nq = seq // tq
    per = ATTN_TILES_PER_STEP
    assert nq % per == 0, "q tiles are processed in groups of ATTN_TILES_PER_STEP"
    steps = nq // per
    score = pltpu.VMEM((heads, tq, tq), F32)
    stat = pltpu.VMEM((heads, 1, tq), F32)
    acc = pltpu.VMEM((heads, MLA_VT_ROWS, tq), F32)
    return pl.pallas_call(
        functools.partial(_mla_attn_kernel, tq=tq, heads=heads, nq=nq),
        grid=(batch, MLA_HEADS // heads, steps),
        in_specs=[pl.BlockSpec((heads, per, MLA_HEAD_PAD, tq), lambda b, h, s: (h, b * steps + s, 0, 0)),
                  pl.BlockSpec((heads, None, MLA_HEAD_PAD, tq),
                               lambda b, h, s: (h, b * nq + jnp.minimum(per * s + per, nq - 1), 0, 0)),
                  pl.BlockSpec((seq, heads * MLA_HEAD_PAD), lambda b, h, s: (b, h)),
                  pl.BlockSpec((heads, nq, MLA_VT_ROWS, tq), lambda b, h, s: (h, b, 0, 0))],
        out_specs=pl.BlockSpec((per * tq, heads * MLA_V), lambda b, h, s: (b * steps + s, h)),
        out_shape=jax.ShapeDtypeStruct((T, MLA_HEADS * MLA_V), BF16),
        scratch_shapes=[score, score, score, stat, stat, stat] + [stat, acc] * per,
        compiler_params=_params("parallel", "parallel", "arbitrary"),
        name="mla_attn",
    )(qt, qt, k, vt)


def _gla_proj_kernel(hb_ref, wq_ref, wk_ref, wv_ref, wa_ref, wr_ref, wg_ref, bg_ref,
                     q_ref, k_ref, v_ref, la_ref, sr_ref):
    hb = hb_ref[...]
    a = _dot(hb, wa_ref[...])
    r = _dot(hb, wr_ref[...])
    z = _dot(a.astype(BF16), wg_ref[...]) + bg_ref[...]
    log_sig = jnp.minimum(z, 0.0) - jnp.log1p(jnp.exp(-jnp.abs(z)))
    la_ref[...] = log_sig * (1.0 / GLA_GATE_NORMALIZER)
    sr_ref[...] = (r / (1.0 + jnp.exp(-r))).astype(BF16)
    v_ref[...] = _dot(hb, wv_ref[...]).astype(BF16)
    q_ref[...] = (_dot(hb, wq_ref[...]) * float(GLA_DK ** -0.5)).astype(BF16)
    k_ref[...] = _dot(hb, wk_ref[...]).astype(BF16)


def _gla_proj(hb, wq, wk, wv, wa, wr, wg, bg, *, tm):
    T, D = hb.shape
    nk, nv = wq.shape[1], wv.shape[1]
    row = lambda w: pl.BlockSpec((tm, w), lambda i: (i, 0))
    return pl.pallas_call(
        _gla_proj_kernel,
        grid=(T // tm,),
        in_specs=[row(D)] + [_const_spec(w.shape) for w in (wq, wk, wv, wa, wr, wg, bg)],
        out_specs=[row(nk), row(nk), row(nv), row(nk), row(nv)],
        out_shape=[jax.ShapeDtypeStruct((T, nk), BF16), jax.ShapeDtypeStruct((T, nk), BF16),
                   jax.ShapeDtypeStruct((T, nv), BF16), jax.ShapeDtypeStruct((T, nk), F32),
                   jax.ShapeDtypeStruct((T, nv), BF16)],
        compiler_params=_params("parallel"),
        name="gla_proj",
    )(hb, wq, wk, wv, wa, wr, wg, bg)


GLA_LEVELS = tuple(GLA_CHUNK >> (i + 1) for i in range(GLA_CHUNK.bit_length() - 1))


def _gla_decay_matrix():
    r = np.arange(GLA_CHUNK)
    groups = [r[None, :] <= r[:, None], r[None, :] > r[:, None]]
    for s in GLA_LEVELS:
        mid = (r // (2 * s)) * (2 * s) + s
        upper = (r & s) != 0
        up = (r[None, :] > mid[:, None]) & (r[None, :] <= r[:, None])
        lo = (r[None, :] > r[:, None]) & (r[None, :] <= mid[:, None])
        groups.append(np.where(upper[:, None], up, lo))
    return np.concatenate(groups, axis=0).astype(np.float32)


def _gla_level_masks():
    r = np.arange(GLA_CHUNK)
    x = r[:, None] ^ r[None, :]
    lower = r[None, :] < r[:, None]
    return np.stack([(lower & (x >= s) & (x < 2 * s)) for s in GLA_LEVELS]).astype(np.float32)


def _gla_group(qs, ks, vs, las, st, nmat, masks_ref):
    C, n = GLA_CHUNK, len(qs)
    la2 = jnp.concatenate([jnp.concatenate(_split(la), axis=0) for la in las], axis=1)
    e_all = jnp.exp(_dot(nmat, la2))
    e = [e_all[:, u * GLA_DK:(u + 1) * GLA_DK] for u in range(n)]

    row = lax.broadcasted_iota(jnp.int32, qs[0].shape, 0)
    a = [jnp.zeros((C, C), F32) for _ in range(n)]
    for lvl, s in enumerate(GLA_LEVELS):
        upper = (row & s) != 0
        mask = masks_ref[lvl]
        for u in range(n):
            t = (jnp.where(upper, qs[u], ks[u]) * e[u][(2 + lvl) * C:(3 + lvl) * C]).astype(BF16)
            a[u] = a[u] + lax.dot_general(t, t, NT_DIMS, preferred_element_type=F32) * mask
    eye = (lax.broadcasted_iota(jnp.int32, (C, C), 0) == lax.broadcasted_iota(jnp.int32, (C, C), 1))
    o_intra, upd = [], []
    for u in range(n):
        a_u = jnp.where(eye, jnp.sum(qs[u] * ks[u], axis=-1, keepdims=True), a[u])
        o_intra.append(_dot(a_u.astype(BF16), vs[u]))
        k_dec = (ks[u] * e[u][C:2 * C]).astype(BF16)
        upd.append(lax.dot_general(k_dec, vs[u], TN_DIMS, preferred_element_type=F32))
    outs = []
    for u in range(n):
        eb = e[u][:C]
        outs.append(o_intra[u] + _dot((qs[u] * eb).astype(BF16), st.astype(BF16)))
        decay = jnp.broadcast_to(eb[C - 1:C, :], (GLA_DK, GLA_DK)).T
        st = st * jnp.concatenate([decay] * (GLA_DV // GLA_DK), axis=1) + upd[u]
    return outs, st


def _gla_rec_kernel(q_ref, k_ref, v_ref, la_ref, sr_ref, g_ref, nmat_ref, masks_ref, o_ref, st_ref, *,
                    ts, group):
    @pl.when(pl.program_id(2) == 0)
    def _():
        st_ref[...] = jnp.zeros_like(st_ref)

    g = g_ref[...]
    nmat = nmat_ref[...]
    span = group * GLA_CHUNK

    def body(c, _):
        base = pl.multiple_of(c * span, span)
        rows = [pl.ds(base + u * GLA_CHUNK, GLA_CHUNK) for u in range(group)]
        outs, st = _gla_group([q_ref[r, :].astype(F32) for r in rows], [k_ref[r, :].astype(F32) for r in rows],
                              [v_ref[r, :] for r in rows], [la_ref[r, :] for r in rows], st_ref[...],
                              nmat, masks_ref)
        st_ref[...] = st
        for r, o in zip(rows, outs):
            o_ref[r, :] = (_rms(o, g) * sr_ref[r, :].astype(F32)).astype(o_ref.dtype)
        return 0

    lax.fori_loop(0, ts // span, body, 0)


def _gla_rec(q, k, v, la, sr, g_out, nmat, masks, *, batch, seq, ts, group):
    T = q.shape[0]
    ns = seq // ts
    blk = lambda w: pl.BlockSpec((ts, w), lambda b, h, i: (b * ns + i, h))
    return pl.pallas_call(
        functools.partial(_gla_rec_kernel, ts=ts, group=group),
        grid=(batch, GLA_HEADS, ns),
        in_specs=[blk(GLA_DK), blk(GLA_DK), blk(GLA_DV), blk(GLA_DK), blk(GLA_DV),
                  _const_spec(g_out.shape), _const_spec(nmat.shape), _const_spec(masks.shape)],
        out_specs=blk(GLA_DV),
        out_shape=jax.ShapeDtypeStruct((T, GLA_HEADS * GLA_DV), BF16),
        scratch_shapes=[pltpu.VMEM((GLA_DK, GLA_DV), F32)],
        compiler_params=_params("parallel", "parallel", "arbitrary"),
        name="gla_rec",
    )(q, k, v, la, sr, g_out, nmat, masks)


def _mem_kv_kernel(mem_ref, g_ref, w_ref, gk_ref, seg_ref, k_ref, v_ref):
    kv = _dot(_rms(mem_ref[...], g_ref[...]).astype(BF16), w_ref[...])
    nk = MEM_HEADS * MEM_DQK
    k = kv[:, :nk]
    ss = _dot_split(k * k, seg_ref[...]) * (1.0 / MEM_DQK)
    k_ref[...] = (k * lax.rsqrt(ss + EPS) * gk_ref[...]).astype(BF16)
    v_ref[...] = kv[:, nk:].astype(BF16)


def _mem_kv(mem2, g_mem, w, gk, seg, *, batch, mem_len):
    D = mem2.shape[1]
    nk, nv = MEM_HEADS * MEM_DQK, MEM_HEADS * MEM_DV
    row = lambda w_: pl.BlockSpec((mem_len, w_), lambda b: (b, 0))
    return pl.pallas_call(
        _mem_kv_kernel,
        grid=(batch,),
        in_specs=[row(D), _const_spec(g_mem.shape), _const_spec(w.shape), _const_spec(gk.shape),
                  _const_spec(seg.shape)],
        out_specs=[row(nk), row(nv)],
        out_shape=[jax.ShapeDtypeStruct((batch * mem_len, nk), BF16),
                   jax.ShapeDtypeStruct((batch * mem_len, nv), BF16)],
        compiler_params=_params("parallel"),
        name="mem_kv",
    )(mem2, g_mem, w, gk, seg)


def _merge_kernel(x_ref, hb_ref, omla_ref, ogla_ref, km_ref, vm_ref, wqm_ref, gqm_ref, seg_ref,
                  wgate_ref, bgate_ref, wo_ref, o_ref):
    hb = hb_ref[...]
    D = x_ref.shape[1]
    km = km_ref[...]

    def gate(j):
        z = _dot(hb, wgate_ref[:, j * D:(j + 1) * D]) + bgate_ref[:, j * D:(j + 1) * D]
        return 1.0 / (1.0 + jnp.exp(-z))

    qm = _dot(hb, wqm_ref[...])
    y = gate(0) * omla_ref[...].astype(F32)
    ss = _dot_split(qm * qm, seg_ref[...]) * (1.0 / MEM_DQK)
    qn = qm * lax.rsqrt(ss + EPS) * (gqm_ref[...] * float(MEM_DQK ** -0.5))
    head_of_lane = lax.broadcasted_iota(jnp.int32, qn.shape, 1) // MEM_DQK
    y = y + gate(1) * ogla_ref[...].astype(F32)
    probs = []
    for h in range(MEM_HEADS):
        qh = jnp.where(head_of_lane == h, qn, 0.0).astype(BF16)
        s = lax.dot_general(qh, km, NT_DIMS, preferred_element_type=F32)
        probs.append(jnp.exp(s - jnp.max(s, axis=-1, keepdims=True)))
    g_mem = gate(2)
    parts = []
    for h, p in enumerate(probs):
        o_h = _dot(p.astype(BF16), vm_ref[:, h * MEM_DV:(h + 1) * MEM_DV])
        parts.append(o_h / jnp.sum(p, axis=-1, keepdims=True))
    y = y + g_mem * jnp.concatenate(parts, axis=-1)
    o_ref[...] = x_ref[...] + _dot(y.astype(BF16), wo_ref[...])


def _merge(x2, hb, o_mla, o_gla, km, vm, wqm, gqm, seg, wgate, bgate, wo, *, seq, mem_len, tm):
    T, D = x2.shape
    per_batch = seq // tm
    row = lambda w: pl.BlockSpec((tm, w), lambda i: (i, 0))
    mem_blk = lambda w: pl.BlockSpec((mem_len, w), lambda i: (i // per_batch, 0))
    return pl.pallas_call(
        _merge_kernel,
        grid=(T // tm,),
        in_specs=[row(D), row(D), row(D), row(D), mem_blk(km.shape[1]), mem_blk(vm.shape[1])]
        + [_const_spec(w.shape) for w in (wqm, gqm, seg, wgate, bgate, wo)],
        out_specs=row(D),
        out_shape=jax.ShapeDtypeStruct((T, D), F32),
        compiler_params=_params("parallel"),
        name="merge",
    )(x2, hb, o_mla, o_gla, km, vm, wqm, gqm, seg, wgate, bgate, wo)


def _ffn_kernel(x_ref, g_ref, wup_ref, wdn_ref, o_ref, *, ff_chunk):
    x = x_ref[...]
    hb = _rms(x, g_ref[...]).astype(BF16)
    acc = x
    for c in range(wup_ref.shape[1] // ff_chunk):
        u = jnp.maximum(_dot(hb, wup_ref[:, c * ff_chunk:(c + 1) * ff_chunk]), 0.0)
        acc = acc + _dot((u * u).astype(BF16), wdn_ref[c * ff_chunk:(c + 1) * ff_chunk, :])
    o_ref[...] = acc


def _ffn(x1, g_ffn, wup, wdn, *, tm, ff_chunk):
    T, D = x1.shape
    row = pl.BlockSpec((tm, D), lambda i: (i, 0))
    return pl.pallas_call(
        functools.partial(_ffn_kernel, ff_chunk=ff_chunk),
        grid=(T // tm,),
        in_specs=[row, _const_spec(g_ffn.shape), _const_spec(wup.shape), _const_spec(wdn.shape)],
        out_specs=row,
        out_shape=jax.ShapeDtypeStruct((T, D), F32),
        compiler_params=_params("parallel"),
        name="ffn",
    )(x1, g_ffn, wup, wdn)


def _pad_cols(w, n):
    return jnp.pad(w, ((0, 0), (0, n - w.shape[-1])))


def _layer(x, mem, positions, g_mix, w_in, b_gate, g_ckv, w_ukv, g_q_nope, g_k_nope, g_q_rope, g_k_rope,
           w_gla_gate, b_gla_gate, g_gla_out, g_mem, w_mem_kv, g_q_mem, g_k_mem, w_o, g_ffn, w_up, w_down):
    B, S, D = x.shape
    M = mem.shape[1]
    T = B * S
    rank = g_ckv.shape[0]
    row = lambda g: g.reshape(1, -1).astype(F32)

    sizes = (MLA_HEADS * (MLA_NOPE + MLA_ROPE), rank, MLA_ROPE, GLA_HEADS * GLA_DK, GLA_HEADS * GLA_DK,
             GLA_HEADS * GLA_DV, GLA_GATE_RANK, GLA_HEADS * GLA_DV, MEM_HEADS * MEM_DQK, N_BRANCHES * D)
    offs = np.concatenate([[0], np.cumsum(sizes)])
    (w_q, w_ckv, w_kr, w_gq, w_gk, w_gv, w_ga, w_gr, w_qm, w_gate) = [
        w_in[:, offs[i]:offs[i + 1]] for i in range(len(sizes))]

    tq = min(ATTN_TILE, S)
    w_ukv3 = w_ukv.reshape(rank, MLA_HEADS, MLA_NOPE + MLA_V)
    wuk = w_ukv3[:, :, :MLA_NOPE].reshape(rank, -1).astype(BF16)
    wuvt = w_ukv3[:, :, MLA_NOPE:].reshape(rank, -1).T.astype(BF16)
    q_scale = float((MLA_NOPE + MLA_ROPE) ** -0.5 * np.log2(np.e))
    lanes_of = lambda g, scale=1.0: jnp.broadcast_to((g.astype(F32) * scale)[:, None], (g.shape[0], tq))

    inv = 1.0 / (ROPE_THETA ** (jnp.arange(0, MLA_ROPE, 2, dtype=F32) / MLA_ROPE))
    ang = inv[:, None] * positions.astype(F32).reshape(1, T)

    x2 = x.reshape(T, D)
    hb, qt, k, vt = _mla_prep(
        x2, row(g_mix), w_q.T.astype(BF16), jnp.concatenate([w_ckv, w_kr], axis=1).T.astype(BF16), wuvt,
        w_ckv.astype(BF16), wuk,
        jnp.cos(ang), jnp.sin(ang), lanes_of(g_q_nope, q_scale), lanes_of(g_q_rope, q_scale), lanes_of(g_ckv),
        lanes_of(g_k_rope), row(g_ckv), row(g_k_nope), tm=tq)
    o_mla = _mla_attn(qt, k, vt, batch=B, seq=S, tq=tq, heads=ATTN_HEADS_PER_STEP)

    nmat = jnp.asarray(np.tile(_gla_decay_matrix(), (1, 2)), BF16)
    masks = jnp.asarray(_gla_level_masks())
    wg = jnp.pad(w_gla_gate, ((0, LANES - GLA_GATE_RANK), (0, 0))).astype(BF16)
    qg, kg, vg, la, sr = _gla_proj(hb, w_gq.astype(BF16), w_gk.astype(BF16), w_gv.astype(BF16),
                                   _pad_cols(w_ga, LANES).astype(BF16), w_gr.astype(BF16), wg,
                                   row(b_gla_gate), tm=min(GLA_PROJ_TILE, S))
    ts = min(GLA_STEP_TOKENS, S)
    o_gla = _gla_rec(qg, kg, vg, la, sr, row(g_gla_out), nmat, masks, batch=B, seq=S, ts=ts,
                     group=ts // GLA_CHUNK)

    seg = jnp.asarray(np.kron(np.eye(MEM_HEADS), np.ones((MEM_DQK, MEM_DQK))).astype(np.float32), BF16)
    km, vm = _mem_kv(mem.reshape(B * M, D), row(g_mem), w_mem_kv.astype(BF16),
                     jnp.tile(row(g_k_mem), (1, MEM_HEADS)), seg, batch=B, mem_len=M)
    x1 = _merge(x2, hb, o_mla, o_gla, km, vm, w_qm.astype(BF16), jnp.tile(row(g_q_mem), (1, MEM_HEADS)), seg,
                w_gate.astype(BF16), row(b_gate), w_o.astype(BF16), seq=S, mem_len=M, tm=min(MERGE_TILE, S))
    out = _ffn(x1, row(g_ffn), w_up.astype(BF16), w_down.astype(BF16), tm=min(FFN_TILE, S), ff_chunk=FFN_CHUNK)
    return out.reshape(B, S, D)


def kernel(x, mem, positions, g_mix, w_in, b_gate, g_ckv, w_ukv, g_q_nope, g_k_nope, g_q_rope, g_k_rope,
           w_gla_gate, b_gla_gate, g_gla_out, g_mem, w_mem_kv, g_q_mem, g_k_mem, w_o, g_ffn, w_up, w_down):
    for l in range(g_mix.shape[0]):
        x = _layer(x, mem, positions, g_mix[l], w_in[l], b_gate[l], g_ckv[l], w_ukv[l], g_q_nope[l],
                   g_k_nope[l], g_q_rope[l], g_k_rope[l], w_gla_gate[l], b_gla_gate[l], g_gla_out[l],
                   g_mem[l], w_mem_kv[l], g_q_mem[l], g_k_mem[l], w_o[l], g_ffn[l], w_up[l], w_down[l])
    return x
```

```python
import functools

import jax
import jax.numpy as jnp
import numpy as np
from jax import lax
from jax.experimental import pallas as pl
from jax.experimental.pallas import tpu as pltpu

F32 = jnp.float32
BF16 = jnp.bfloat16

EPS = 1e-6
ROPE_THETA = 10000.0
LANES = 128
BF16_SUBLANES = 16

MLA_HEADS, MLA_NOPE, MLA_ROPE, MLA_V = 8, 128, 64, 128
MLA_HEAD_PAD = 2 * LANES
MLA_VT_ROWS = MLA_V + BF16_SUBLANES
GLA_HEADS, GLA_DK, GLA_DV = 4, 128, 256
GLA_GATE_RANK, GLA_GATE_NORMALIZER, GLA_CHUNK = 16, 16.0, 64
MEM_HEADS, MEM_DQK, MEM_DV = 4, 64, 256
N_BRANCHES = 3

VMEM_LIMIT = 48 * 1024 * 1024

ATTN_TILE = 512
ATTN_HEADS_PER_STEP = 2
GLA_PROJ_TILE = 1024
GLA_STEP_TOKENS = 1024
MERGE_TILE = 512
FFN_TILE = 512
FFN_CHUNK = 1024

NT_DIMS = (((1,), (1,)), ((), ()))
TN_DIMS = (((0,), (0,)), ((), ()))


def _params(*sem):
    return pltpu.CompilerParams(dimension_semantics=sem, vmem_limit_bytes=VMEM_LIMIT)


def _rms(t, g, n=None):
    n = t.shape[-1] if n is None else n
    ss = jnp.sum(t * t, axis=-1, keepdims=True) * (1.0 / n)
    return t * lax.rsqrt(ss + EPS) * g


def _dot(a, b):
    return jnp.dot(a, b, preferred_element_type=F32)


def _split(a_f32):
    hi = a_f32.astype(BF16)
    return hi, (a_f32 - hi.astype(F32)).astype(BF16)


def _dot_split(a_f32, b_bf16):
    hi, lo = _split(a_f32)
    return _dot(hi, b_bf16) + _dot(lo, b_bf16)


def _const_spec(shape):
    return pl.BlockSpec(shape, lambda *_: (0,) * len(shape))


def _rms_rows(t, g):
    ss = jnp.sum(t * t, axis=0, keepdims=True) * (1.0 / t.shape[0])
    return t * lax.rsqrt(ss + EPS) * g


def _rope_rows(r, cos, sin):
    half = r.shape[0] // 2
    t1, t2 = r[:half], r[half:]
    return t1 * cos - t2 * sin, t1 * sin + t2 * cos


def _mla_prep_kernel(x_ref, gmix_ref, wqt_ref, wlatt_ref, wuvt_ref, wckv_ref, wuk_ref,
                     cost_ref, sint_ref, gqnt_ref, gqrt_ref, gckvt_ref, gkrt_ref,
                     gckv_ref, gkn_ref, hb_ref, qt_ref, k_ref, vt_ref):
    h = _rms(x_ref[...], gmix_ref[...])
    hb = h.astype(BF16)
    hb_ref[...] = hb
    hbt = h.T.astype(BF16)
    tm = hbt.shape[1]
    half = MLA_ROPE // 2
    dq = MLA_NOPE + MLA_ROPE

    cos, sin = cost_ref[...], sint_ref[...]
    rank = gckv_ref.shape[-1]
    gqn, gqr = gqnt_ref[...], gqrt_ref[...]
    zeros = jnp.zeros((MLA_HEAD_PAD - dq, tm), BF16)

    def q_heads(heads):
        for hd in heads:
            t = _dot(wqt_ref[hd * dq:(hd + 1) * dq, :], hbt)
            qt_ref[hd, :MLA_NOPE, :] = _rms_rows(t[:MLA_NOPE], gqn).astype(BF16)
            r1, r2 = _rope_rows(_rms_rows(t[MLA_NOPE:], gqr), cos, sin)
            qt_ref[hd, MLA_NOPE:MLA_NOPE + half, :] = r1.astype(BF16)
            qt_ref[hd, MLA_NOPE + half:dq, :] = r2.astype(BF16)
            qt_ref[hd, dq:, :] = zeros

    ckv_raw = _dot(hb, wckv_ref[...])
    latt = _dot(wlatt_ref[...], hbt)
    q_heads(range(0, MLA_HEADS // 2))

    ckv = _rms(ckv_raw, gckv_ref[...]).astype(BF16)
    kn = _dot(ckv, wuk_ref[...])
    ckvt = _rms_rows(latt[:rank], gckvt_ref[...]).astype(BF16)
    vt = _dot(wuvt_ref[...], ckvt)
    q_heads(range(MLA_HEADS // 2, MLA_HEADS))

    kr1, kr2 = _rope_rows(_rms_rows(latt[rank:], gkrt_ref[...]), cos, sin)
    kpe = jnp.concatenate([kr1, kr2, jnp.zeros((LANES - MLA_ROPE, tm), F32)], axis=0).T.astype(BF16)
    gkn = gkn_ref[...]
    pad_row = lax.broadcasted_iota(jnp.int32, (MLA_VT_ROWS - MLA_V, tm), 0)
    ones_rows = jnp.where(pad_row == 0, 1.0, 0.0).astype(BF16)
    for hd in range(MLA_HEADS):
        c0 = hd * MLA_HEAD_PAD
        k_ref[:, c0:c0 + LANES] = _rms(kn[:, hd * LANES:(hd + 1) * LANES], gkn).astype(BF16)
        k_ref[:, c0 + LANES:c0 + 2 * LANES] = kpe
        vt_ref[hd, :MLA_V, :] = vt[hd * MLA_V:(hd + 1) * MLA_V].astype(BF16)
        vt_ref[hd, MLA_V:, :] = ones_rows


def _mla_prep(x2, g_mix, wqt, wlatt, wuvt, wckv, wuk, cos_tt, sin_tt, gqnt, gqrt, gckvt, gkrt,
              gckv, gkn, *, tm):
    T, D = x2.shape
    HP = MLA_HEADS * MLA_HEAD_PAD
    row = lambda w: pl.BlockSpec((tm, w), lambda i: (i, 0))
    col = lambda r: pl.BlockSpec((r, tm), lambda i: (0, i))
    consts = (g_mix, wqt, wlatt, wuvt, wckv, wuk)
    gains = (gqnt, gqrt, gckvt, gkrt, gckv, gkn)
    return pl.pallas_call(
        _mla_prep_kernel,
        grid=(T // tm,),
        in_specs=[row(D)] + [_const_spec(c.shape) for c in consts]
        + [col(cos_tt.shape[0]), col(sin_tt.shape[0])]
        + [_const_spec(g.shape) for g in gains],
        out_specs=[row(D), pl.BlockSpec((MLA_HEADS, None, MLA_HEAD_PAD, tm), lambda i: (0, i, 0, 0)), row(HP),
                   pl.BlockSpec((MLA_HEADS, None, MLA_VT_ROWS, tm), lambda i: (0, i, 0, 0))],
        out_shape=[jax.ShapeDtypeStruct((T, D), BF16),
                   jax.ShapeDtypeStruct((MLA_HEADS, T // tm, MLA_HEAD_PAD, tm), BF16),
                   jax.ShapeDtypeStruct((T, HP), BF16),
                   jax.ShapeDtypeStruct((MLA_HEADS, T // tm, MLA_VT_ROWS, tm), BF16)],
        compiler_params=_params("parallel"),
        name="mla_prep",
    )(x2, *consts, cos_tt, sin_tt, *gains)


ATTN_TILES_PER_STEP = 4


def _mla_attn_kernel(qt_ref, qt_next_ref, k_ref, vt_ref, o_ref, s_a, s_b, s_c, max_a, max_b, max_c,
                     *stat_refs, tq, heads, nq):
    quad = pl.program_id(2)
    hs = range(heads)
    buf_a, buf_b, buf_c = (s_a, max_a), (s_b, max_b), (s_c, max_c)
    stats_of = [(stat_refs[2 * r], stat_refs[2 * r + 1]) for r in range(ATTN_TILES_PER_STEP)]

    def step(prod=None, cons=None):
        if prod is not None:
            jp, q_ref, q_tile, (sp_ref, mp_ref) = prod
            r0 = pl.multiple_of(jp * tq, tq)
            for h in hs:
                q = q_ref[h] if q_tile is None else q_ref[h, q_tile]
                s = _dot(k_ref[pl.ds(r0, tq), h * MLA_HEAD_PAD:(h + 1) * MLA_HEAD_PAD], q)
                sp_ref[h] = s
                mp_ref[h] = jnp.max(s, axis=0, keepdims=True)
        if cons is not None:
            jc, (sc_ref, mc_ref), diagonal, (m_ref, acc_ref) = cons
            for h in hs:
                s = sc_ref[h]
                if diagonal:
                    kpos = lax.broadcasted_iota(jnp.int32, s.shape, 0)
                    qpos = lax.broadcasted_iota(jnp.int32, s.shape, 1)
                    s = jnp.where(kpos <= qpos, s, -1e30)
                    m_new = jnp.max(s, axis=0, keepdims=True)
                else:
                    m = m_ref[h]
                    m_new = jnp.maximum(m, mc_ref[h])
                p = jnp.exp2(s - m_new)
                m_ref[h] = m_new
                pv = _dot(vt_ref[h, jc], p.astype(BF16))
                acc_ref[h] = pv if diagonal else jnp.exp2(m - m_new) * acc_ref[h] + pv

    def write_out(stats, tile):
        for h in hs:
            acc = stats[1][h]
            o_ref[tile * tq:(tile + 1) * tq, h * MLA_V:(h + 1) * MLA_V] = (
                acc[:MLA_V] / acc[MLA_V:MLA_V + 1]).T.astype(o_ref.dtype)

    def two_blocks(t, q_tile, stats):
        j = 2 * t
        step(prod=(j + 1, qt_ref, q_tile, buf_b), cons=(j, buf_a, False, stats))
        step(prod=(j + 2, qt_ref, q_tile, buf_a), cons=(j + 1, buf_b, False, stats))

    def run_tile(i, r, trips, leftover, diag_buf, next_prod):
        stats = stats_of[r]
        step(prod=(0, qt_ref, r, buf_a), cons=(i, diag_buf, True, stats))
        if not (isinstance(trips, int) and trips == 0):
            def four_blocks(t, carry):
                two_blocks(2 * t, r, stats)
                two_blocks(2 * t + 1, r, stats)
                return carry

            lax.fori_loop(0, trips, four_blocks, 0)
        if leftover:
            two_blocks(2 * trips, r, stats)
        if r % 2 == 0:
            step(prod=(i - 1, qt_ref, r, buf_b), cons=(i - 2, buf_a, False, stats))
            step(prod=next_prod, cons=(i - 1, buf_b, False, stats))
        else:
            step(prod=next_prod, cons=(i - 1, buf_a, False, stats))
        write_out(stats, r)

    first = ATTN_TILES_PER_STEP * quad
    ahead = (jnp.minimum(first + ATTN_TILES_PER_STEP, nq - 1), qt_next_ref, None, buf_c)
    diag_of = lambda r: (first + r, qt_ref, r, buf_c)

    @pl.when(quad == 0)
    def _():
        step(prod=(0, qt_ref, 0, buf_c))
        step(prod=(1, qt_ref, 1, buf_b), cons=(0, buf_c, True, stats_of[0]))
        write_out(stats_of[0], 0)
        run_tile(1, 1, 0, 0, buf_b, diag_of(2))
        run_tile(2, 2, 0, 0, buf_c, diag_of(3))
        run_tile(3, 3, 0, 1, buf_c, ahead)

    @pl.when(quad > 0)
    def _():
        run_tile(first, 0, quad - 1, 1, buf_c, diag_of(1))
        run_tile(first + 1, 1, quad, 0, buf_c, diag_of(2))
        run_tile(first + 2, 2, quad, 0, buf_c, diag_of(3))
        run_tile(first + 3, 3, quad, 1, buf_c, ahead)


def _mla_attn(qt, k, vt, *, batch, seq, tq, heads):
    T = k.shape[0]
    nq = seq // tq
    per = ATTN_TILES_PER_STEP
    assert nq % per == 0, "q tiles are processed in groups of ATTN_TILES_PER_STEP"
    steps = nq // per
    score = pltpu.VMEM((heads, tq, tq), F32)
    stat = pltpu.VMEM((heads, 1, tq), F32)
    acc = pltpu.VMEM((heads, MLA_VT_ROWS, tq), F32)
    return pl.pallas_call(
        functools.partial(_mla_attn_kernel, tq=tq, heads=heads, nq=nq),
        grid=(batch, MLA_HEADS // heads, steps),
        in_specs=[pl.BlockSpec((heads, per, MLA_HEAD_PAD, tq), lambda b, h, s: (h, b * steps + s, 0, 0)),
                  pl.BlockSpec((heads, None, MLA_HEAD_PAD, tq),
                               lambda b, h, s: (h, b * nq + jnp.minimum(per * s + per, nq - 1), 0, 0)),
                  pl.BlockSpec((seq, heads * MLA_HEAD_PAD), lambda b, h, s: (b, h)),
                  pl.BlockSpec((heads, nq, MLA_VT_ROWS, tq), lambda b, h, s: (h, b, 0, 0))],
        out_specs=pl.BlockSpec((per * tq, heads * MLA_V), lambda b, h, s: (b * steps + s, h)),
        out_shape=jax.ShapeDtypeStruct((T, MLA_HEADS * MLA_V), BF16),
        scratch_shapes=[score, score, score, stat, stat, stat] + [stat, acc] * per,
        compiler_params=_params("parallel", "parallel", "arbitrary"),
        name="mla_attn",
    )(qt, qt, k, vt)


def _gla_proj_kernel(hb_ref, wq_ref, wk_ref, wv_ref, wa_ref, wr_ref, wg_ref, bg_ref,
                     q_ref, k_ref, v_ref, la_ref, sr_ref):
    hb = hb_ref[...]
    a = _dot(hb, wa_ref[...])
    r = _dot(hb, wr_ref[...])
    z = _dot(a.astype(BF16), wg_ref[...]) + bg_ref[...]
    log_sig = jnp.minimum(z, 0.0) - jnp.log1p(jnp.exp(-jnp.abs(z)))
    la_ref[...] = log_sig * (1.0 / GLA_GATE_NORMALIZER)
    sr_ref[...] = (r / (1.0 + jnp.exp(-r))).astype(BF16)
    v_ref[...] = _dot(hb, wv_ref[...]).astype(BF16)
    q_ref[...] = (_dot(hb, wq_ref[...]) * float(GLA_DK ** -0.5)).astype(BF16)
    k_ref[...] = _dot(hb, wk_ref[...]).astype(BF16)


def _gla_proj(hb, wq, wk, wv, wa, wr, wg, bg, *, tm):
    T, D = hb.shape
    nk, nv = wq.shape[1], wv.shape[1]
    row = lambda w: pl.BlockSpec((tm, w), lambda i: (i, 0))
    return pl.pallas_call(
        _gla_proj_kernel,
        grid=(T // tm,),
        in_specs=[row(D)] + [_const_spec(w.shape) for w in (wq, wk, wv, wa, wr, wg, bg)],
        out_specs=[row(nk), row(nk), row(nv), row(nk), row(nv)],
        out_shape=[jax.ShapeDtypeStruct((T, nk), BF16), jax.ShapeDtypeStruct((T, nk), BF16),
                   jax.ShapeDtypeStruct((T, nv), BF16), jax.ShapeDtypeStruct((T, nk), F32),
                   jax.ShapeDtypeStruct((T, nv), BF16)],
        compiler_params=_params("parallel"),
        name="gla_proj",
    )(hb, wq, wk, wv, wa, wr, wg, bg)


GLA_LEVELS = tuple(GLA_CHUNK >> (i + 1) for i in range(GLA_CHUNK.bit_length() - 1))


def _gla_decay_matrix():
    r = np.arange(GLA_CHUNK)
    groups = [r[None, :] <= r[:, None], r[None, :] > r[:, None]]
    for s in GLA_LEVELS:
        mid = (r // (2 * s)) * (2 * s) + s
        upper = (r & s) != 0
        up = (r[None, :] > mid[:, None]) & (r[None, :] <= r[:, None])
        lo = (r[None, :] > r[:, None]) & (r[None, :] <= mid[:, None])
        groups.append(np.where(upper[:, None], up, lo))
    return np.concatenate(groups, axis=0).astype(np.float32)


def _gla_level_masks():
    r = np.arange(GLA_CHUNK)
    x = r[:, None] ^ r[None, :]
    lower = r[None, :] < r[:, None]
    return np.stack([(lower & (x >= s) & (x < 2 * s)) for s in GLA_LEVELS]).astype(np.float32)


def _gla_group(qs, ks, vs, las, st, nmat, masks_ref):
    C, n = GLA_CHUNK, len(qs)
    la2 = jnp.concatenate([jnp.concatenate(_split(la), axis=0) for la in las], axis=1)
    e_all = jnp.exp(_dot(nmat, la2))
    e = [e_all[:, u * GLA_DK:(u + 1) * GLA_DK] for u in range(n)]

    row = lax.broadcasted_iota(jnp.int32, qs[0].shape, 0)
    a = [jnp.zeros((C, C), F32) for _ in range(n)]
    for lvl, s in enumerate(GLA_LEVELS):
        upper = (row & s) != 0
        mask = masks_ref[lvl]
        for u in range(n):
            t = (jnp.where(upper, qs[u], ks[u]) * e[u][(2 + lvl) * C:(3 + lvl) * C]).astype(BF16)
            a[u] = a[u] + lax.dot_general(t, t, NT_DIMS, preferred_element_type=F32) * mask
    eye = (lax.broadcasted_iota(jnp.int32, (C, C), 0) == lax.broadcasted_iota(jnp.int32, (C, C), 1))
    o_intra, upd = [], []
    for u in range(n):
        a_u = jnp.where(eye, jnp.sum(qs[u] * ks[u], axis=-1, keepdims=True), a[u])
        o_intra.append(_dot(a_u.astype(BF16), vs[u]))
        k_dec = (ks[u] * e[u][C:2 * C]).astype(BF16)
        upd.append(lax.dot_general(k_dec, vs[u], TN_DIMS, preferred_element_type=F32))
    outs = []
    for u in range(n):
        eb = e[u][:C]
        outs.append(o_intra[u] + _dot((qs[u] * eb).astype(BF16), st.astype(BF16)))
        decay = jnp.broadcast_to(eb[C - 1:C, :], (GLA_DK, GLA_DK)).T
        st = st * jnp.concatenate([decay] * (GLA_DV // GLA_DK), axis=1) + upd[u]
    return outs, st


def _gla_rec_kernel(q_ref, k_ref, v_ref, la_ref, sr_ref, g_ref, nmat_ref, masks_ref, o_ref, st_ref, *,
                    ts, group):
    @pl.when(pl.program_id(2) == 0)
    def _():
        st_ref[...] = jnp.zeros_like(st_ref)

    g = g_ref[...]
    nmat = nmat_ref[...]
    span = group * GLA_CHUNK

    def body(c, _):
        base = pl.multiple_of(c * span, span)
        rows = [pl.ds(base + u * GLA_CHUNK, GLA_CHUNK) for u in range(group)]
        outs, st = _gla_group([q_ref[r, :].astype(F32) for r in rows], [k_ref[r, :].astype(F32) for r in rows],
                              [v_ref[r, :] for r in rows], [la_ref[r, :] for r in rows], st_ref[...],
                              nmat, masks_ref)
        st_ref[...] = st
        for r, o in zip(rows, outs):
            o_ref[r, :] = (_rms(o, g) * sr_ref[r, :].astype(F32)).astype(o_ref.dtype)
        return 0

    lax.fori_loop(0, ts // span, body, 0)


def _gla_rec(q, k, v, la, sr, g_out, nmat, masks, *, batch, seq, ts, group):
    T = q.shape[0]
    ns = seq // ts
    blk = lambda w: pl.BlockSpec((ts, w), lambda b, h, i: (b * ns + i, h))
    return pl.pallas_call(
        functools.partial(_gla_rec_kernel, ts=ts, group=group),
        grid=(batch, GLA_HEADS, ns),
        in_specs=[blk(GLA_DK), blk(GLA_DK), blk(GLA_DV), blk(GLA_DK), blk(GLA_DV),
                  _const_spec(g_out.shape), _const_spec(nmat.shape), _const_spec(masks.shape)],
        out_specs=blk(GLA_DV),
        out_shape=jax.ShapeDtypeStruct((T, GLA_HEADS * GLA_DV), BF16),
        scratch_shapes=[pltpu.VMEM((GLA_DK, GLA_DV), F32)],
        compiler_params=_params("parallel", "parallel", "arbitrary"),
        name="gla_rec",
    )(q, k, v, la, sr, g_out, nmat, masks)


def _mem_kv_kernel(mem_ref, g_ref, w_ref, gk_ref, seg_ref, k_ref, v_ref):
    kv = _dot(_rms(mem_ref[...], g_ref[...]).astype(BF16), w_ref[...])
    nk = MEM_HEADS * MEM_DQK
    k = kv[:, :nk]
    ss = _dot_split(k * k, seg_ref[...]) * (1.0 / MEM_DQK)
    k_ref[...] = (k * lax.rsqrt(ss + EPS) * gk_ref[...]).astype(BF16)
    v_ref[...] = kv[:, nk:].astype(BF16)


def _mem_kv(mem2, g_mem, w, gk, seg, *, batch, mem_len):
    D = mem2.shape[1]
    nk, nv = MEM_HEADS * MEM_DQK, MEM_HEADS * MEM_DV
    row = lambda w_: pl.BlockSpec((mem_len, w_), lambda b: (b, 0))
    return pl.pallas_call(
        _mem_kv_kernel,
        grid=(batch,),
        in_specs=[row(D), _const_spec(g_mem.shape), _const_spec(w.shape), _const_spec(gk.shape),
                  _const_spec(seg.shape)],
        out_specs=[row(nk), row(nv)],
        out_shape=[jax.ShapeDtypeStruct((batch * mem_len, nk), BF16),
                   jax.ShapeDtypeStruct((batch * mem_len, nv), BF16)],
        compiler_params=_params("parallel"),
        name="mem_kv",
    )(mem2, g_mem, w, gk, seg)


def _merge_kernel(x_ref, hb_ref, omla_ref, ogla_ref, km_ref, vm_ref, wqm_ref, gqm_ref, seg_ref,
                  wgate_ref, bgate_ref, wo_ref, o_ref):
    hb = hb_ref[...]
    D = x_ref.shape[1]
    km = km_ref[...]

    def gate(j):
        z = _dot(hb, wgate_ref[:, j * D:(j + 1) * D]) + bgate_ref[:, j * D:(j + 1) * D]
        return 1.0 / (1.0 + jnp.exp(-z))

    qm = _dot(hb, wqm_ref[...])
    y = gate(0) * omla_ref[...].astype(F32)
    ss = _dot_split(qm * qm, seg_ref[...]) * (1.0 / MEM_DQK)
    qn = qm * lax.rsqrt(ss + EPS) * (gqm_ref[...] * float(MEM_DQK ** -0.5))
    head_of_lane = lax.broadcasted_iota(jnp.int32, qn.shape, 1) // MEM_DQK
    y = y + gate(1) * ogla_ref[...].astype(F32)
    probs = []
    for h in range(MEM_HEADS):
        qh = jnp.where(head_of_lane == h, qn, 0.0).astype(BF16)
        s = lax.dot_general(qh, km, NT_DIMS, preferred_element_type=F32)
        probs.append(jnp.exp(s - jnp.max(s, axis=-1, keepdims=True)))
    g_mem = gate(2)
    parts = []
    for h, p in enumerate(probs):
        o_h = _dot(p.astype(BF16), vm_ref[:, h * MEM_DV:(h + 1) * MEM_DV])
        parts.append(o_h / jnp.sum(p, axis=-1, keepdims=True))
    y = y + g_mem * jnp.concatenate(parts, axis=-1)
    o_ref[...] = x_ref[...] + _dot(y.astype(BF16), wo_ref[...])


def _merge(x2, hb, o_mla, o_gla, km, vm, wqm, gqm, seg, wgate, bgate, wo, *, seq, mem_len, tm):
    T, D = x2.shape
    per_batch = seq // tm
    row = lambda w: pl.BlockSpec((tm, w), lambda i: (i, 0))
    mem_blk = lambda w: pl.BlockSpec((mem_len, w), lambda i: (i // per_batch, 0))
    return pl.pallas_call(
        _merge_kernel,
        grid=(T // tm,),
        in_specs=[row(D), row(D), row(D), row(D), mem_blk(km.shape[1]), mem_blk(vm.shape[1])]
        + [_const_spec(w.shape) for w in (wqm, gqm, seg, wgate, bgate, wo)],
        out_specs=row(D),
        out_shape=jax.ShapeDtypeStruct((T, D), F32),
        compiler_params=_params("parallel"),
        name="merge",
    )(x2, hb, o_mla, o_gla, km, vm, wqm, gqm, seg, wgate, bgate, wo)


def _ffn_kernel(x_ref, g_ref, wup_ref, wdn_ref, o_ref, *, ff_chunk):
    x = x_ref[...]
    hb = _rms(x, g_ref[...]).astype(BF16)
    acc = x
    for c in range(wup_ref.shape[1] // ff_chunk):
        u = jnp.maximum(_dot(hb, wup_ref[:, c * ff_chunk:(c + 1) * ff_chunk]), 0.0)
        acc = acc + _dot((u * u).astype(BF16), wdn_ref[c * ff_chunk:(c + 1) * ff_chunk, :])
    o_ref[...] = acc


def _ffn(x1, g_ffn, wup, wdn, *, tm, ff_chunk):
    T, D = x1.shape
    row = pl.BlockSpec((tm, D), lambda i: (i, 0))
    return pl.pallas_call(
        functools.partial(_ffn_kernel, ff_chunk=ff_chunk),
        grid=(T // tm,),
        in_specs=[row, _const_spec(g_ffn.shape), _const_spec(wup.shape), _const_spec(wdn.shape)],
        out_specs=row,
        out_shape=jax.ShapeDtypeStruct((T, D), F32),
        compiler_params=_params("parallel"),
        name="ffn",
    )(x1, g_ffn, wup, wdn)


def _pad_cols(w, n):
    return jnp.pad(w, ((0, 0), (0, n - w.shape[-1])))


def _layer(x, mem, positions, g_mix, w_in, b_gate, g_ckv, w_ukv, g_q_nope, g_k_nope, g_q_rope, g_k_rope,
           w_gla_gate, b_gla_gate, g_gla_out, g_mem, w_mem_kv, g_q_mem, g_k_mem, w_o, g_ffn, w_up, w_down):
    B, S, D = x.shape
    M = mem.shape[1]
    T = B * S
    rank = g_ckv.shape[0]
    row = lambda g: g.reshape(1, -1).astype(F32)

    sizes = (MLA_HEADS * (MLA_NOPE + MLA_ROPE), rank, MLA_ROPE, GLA_HEADS * GLA_DK, GLA_HEADS * GLA_DK,
             GLA_HEADS * GLA_DV, GLA_GATE_RANK, GLA_HEADS * GLA_DV, MEM_HEADS * MEM_DQK, N_BRANCHES * D)
    offs = np.concatenate([[0], np.cumsum(sizes)])
    (w_q, w_ckv, w_kr, w_gq, w_gk, w_gv, w_ga, w_gr, w_qm, w_gate) = [
        w_in[:, offs[i]:offs[i + 1]] for i in range(len(sizes))]

    tq = min(ATTN_TILE, S)
    w_ukv3 = w_ukv.reshape(rank, MLA_HEADS, MLA_NOPE + MLA_V)
    wuk = w_ukv3[:, :, :MLA_NOPE].reshape(rank, -1).astype(BF16)
    wuvt = w_ukv3[:, :, MLA_NOPE:].reshape(rank, -1).T.astype(BF16)
    q_scale = float((MLA_NOPE + MLA_ROPE) ** -0.5 * np.log2(np.e))
    lanes_of = lambda g, scale=1.0: jnp.broadcast_to((g.astype(F32) * scale)[:, None], (g.shape[0], tq))

    inv = 1.0 / (ROPE_THETA ** (jnp.arange(0, MLA_ROPE, 2, dtype=F32) / MLA_ROPE))
    ang = inv[:, None] * positions.astype(F32).reshape(1, T)

    x2 = x.reshape(T, D)
    hb, qt, k, vt = _mla_prep(
        x2, row(g_mix), w_q.T.astype(BF16), jnp.concatenate([w_ckv, w_kr], axis=1).T.astype(BF16), wuvt,
        w_ckv.astype(BF16), wuk,
        jnp.cos(ang), jnp.sin(ang), lanes_of(g_q_nope, q_scale), lanes_of(g_q_rope, q_scale), lanes_of(g_ckv),
        lanes_of(g_k_rope), row(g_ckv), row(g_k_nope), tm=tq)
    o_mla = _mla_attn(qt, k, vt, batch=B, seq=S, tq=tq, heads=ATTN_HEADS_PER_STEP)

    nmat = jnp.asarray(np.tile(_gla_decay_matrix(), (1, 2)), BF16)
    masks = jnp.asarray(_gla_level_masks())
    wg = jnp.pad(w_gla_gate, ((0, LANES - GLA_GATE_RANK), (0, 0))).astype(BF16)
    qg, kg, vg, la, sr = _gla_proj(hb, w_gq.astype(BF16), w_gk.astype(BF16), w_gv.astype(BF16),
                                   _pad_cols(w_ga, LANES).astype(BF16), w_gr.astype(BF16), wg,
                                   row(b_gla_gate), tm=min(GLA_PROJ_TILE, S))
    ts = min(GLA_STEP_TOKENS, S)
    o_gla = _gla_rec(qg, kg, vg, la, sr, row(g_gla_out), nmat, masks, batch=B, seq=S, ts=ts,
                     group=ts // GLA_CHUNK)

    seg = jnp.asarray(np.kron(np.eye(MEM_HEADS), np.ones((MEM_DQK, MEM_DQK))).astype(np.float32), BF16)
    km, vm = _mem_kv(mem.reshape(B * M, D), row(g_mem), w_mem_kv.astype(BF16),
                     jnp.tile(row(g_k_mem), (1, MEM_HEADS)), seg, batch=B, mem_len=M)
    x1 = _merge(x2, hb, o_mla, o_gla, km, vm, w_qm.astype(BF16), jnp.tile(row(g_q_mem), (1, MEM_HEADS)), seg,
                w_gate.astype(BF16), row(b_gate), w_o.astype(BF16), seq=S, mem_len=M, tm=min(MERGE_TILE, S))
    out = _ffn(x1, row(g_ffn), w_up.astype(BF16), w_down.astype(BF16), tm=min(FFN_TILE, S), ff_chunk=FFN_CHUNK)
    return out.reshape(B, S, D)


def kernel(x, mem, positions, g_mix, w_in, b_gate, g_ckv, w_ukv, g_q_nope, g_k_nope, g_q_rope, g_k_rope,
           w_gla_gate, b_gla_gate, g_gla_out, g_mem, w_mem_kv, g_q_mem, g_k_mem, w_o, g_ffn, w_up, w_down):
    for l in range(g_mix.shape[0]):
        x = _layer(x, mem, positions, g_mix[l], w_in[l], b_gate[l], g_ckv[l], w_ukv[l], g_q_nope[l],
                   g_k_nope[l], g_q_rope[l], g_k_rope[l], w_gla_gate[l], b_gla_gate[l], g_gla_out[l],
                   g_mem[l], w_mem_kv[l], g_q_mem[l], g_k_mem[l], w_o[l], g_ffn[l], w_up[l], w_down[l])
    return x
```

```python
import functools

import jax
import jax.numpy as jnp
import numpy as np
from jax import lax
from jax.experimental import pallas as pl
from jax.experimental.pallas import tpu as pltpu

F32 = jnp.float32
BF16 = jnp.bfloat16

EPS = 1e-6
ROPE_THETA = 10000.0
LANES = 128
BF16_SUBLANES = 16

MLA_HEADS, MLA_NOPE, MLA_ROPE, MLA_V = 8, 128, 64, 128
MLA_HEAD_PAD = 2 * LANES
MLA_VT_ROWS = MLA_V + BF16_SUBLANES
GLA_HEADS, GLA_DK, GLA_DV = 4, 128, 256
GLA_GATE_RANK, GLA_GATE_NORMALIZER, GLA_CHUNK = 16, 16.0, 64
MEM_HEADS, MEM_DQK, MEM_DV = 4, 64, 256
N_BRANCHES = 3

VMEM_LIMIT = 48 * 1024 * 1024

ATTN_TILE = 512
ATTN_HEADS_PER_STEP = 2
GLA_PROJ_TILE = 1024
GLA_STEP_TOKENS = 1024
MERGE_TILE = 512
FFN_TILE = 512
FFN_CHUNK = 1024

NT_DIMS = (((1,), (1,)), ((), ()))
TN_DIMS = (((0,), (0,)), ((), ()))


def _params(*sem):
    return pltpu.CompilerParams(dimension_semantics=sem, vmem_limit_bytes=VMEM_LIMIT)


def _rms(t, g, n=None):
    n = t.shape[-1] if n is None else n
    ss = jnp.sum(t * t, axis=-1, keepdims=True) * (1.0 / n)
    return t * lax.rsqrt(ss + EPS) * g


def _dot(a, b):
    return jnp.dot(a, b, preferred_element_type=F32)


def _split(a_f32):
    hi = a_f32.astype(BF16)
    return hi, (a_f32 - hi.astype(F32)).astype(BF16)


def _dot_split(a_f32, b_bf16):
    hi, lo = _split(a_f32)
    return _dot(hi, b_bf16) + _dot(lo, b_bf16)


def _const_spec(shape):
    return pl.BlockSpec(shape, lambda *_: (0,) * len(shape))


def _rms_rows(t, g):
    ss = jnp.sum(t * t, axis=0, keepdims=True) * (1.0 / t.shape[0])
    return t * lax.rsqrt(ss + EPS) * g


def _rope_rows(r, cos, sin):
    half = r.shape[0] // 2
    t1, t2 = r[:half], r[half:]
    return t1 * cos - t2 * sin, t1 * sin + t2 * cos


def _mla_prep_kernel(x_ref, gmix_ref, wqt_ref, wlatt_ref, wuvt_ref, wckv_ref, wuk_ref,
                     cost_ref, sint_ref, gqnt_ref, gqrt_ref, gckvt_ref, gkrt_ref,
                     gckv_ref, gkn_ref, hb_ref, qt_ref, k_ref, vt_ref):
    h = _rms(x_ref[...], gmix_ref[...])
    hb = h.astype(BF16)
    hb_ref[...] = hb
    hbt = h.T.astype(BF16)
    tm = hbt.shape[1]
    half = MLA_ROPE // 2
    dq = MLA_NOPE + MLA_ROPE

    cos, sin = cost_ref[...], sint_ref[...]
    rank = gckv_ref.shape[-1]
    gqn, gqr = gqnt_ref[...], gqrt_ref[...]
    zeros = jnp.zeros((MLA_HEAD_PAD - dq, tm), BF16)

    def q_heads(heads):
        for hd in heads:
            t = _dot(wqt_ref[hd * dq:(hd + 1) * dq, :], hbt)
            qt_ref[hd, :MLA_NOPE, :] = _rms_rows(t[:MLA_NOPE], gqn).astype(BF16)
            r1, r2 = _rope_rows(_rms_rows(t[MLA_NOPE:], gqr), cos, sin)
            qt_ref[hd, MLA_NOPE:MLA_NOPE + half, :] = r1.astype(BF16)
            qt_ref[hd, MLA_NOPE + half:dq, :] = r2.astype(BF16)
            qt_ref[hd, dq:, :] = zeros

    ckv_raw = _dot(hb, wckv_ref[...])
    latt = _dot(wlatt_ref[...], hbt)
    q_heads(range(0, MLA_HEADS // 2))

    ckv = _rms(ckv_raw, gckv_ref[...]).astype(BF16)
    kn = _dot(ckv, wuk_ref[...])
    ckvt = _rms_rows(latt[:rank], gckvt_ref[...]).astype(BF16)
    vt = _dot(wuvt_ref[...], ckvt)
    q_heads(range(MLA_HEADS // 2, MLA_HEADS))

    kr1, kr2 = _rope_rows(_rms_rows(latt[rank:], gkrt_ref[...]), cos, sin)
    kpe = jnp.concatenate([kr1, kr2, jnp.zeros((LANES - MLA_ROPE, tm), F32)], axis=0).T.astype(BF16)
    gkn = gkn_ref[...]
    pad_row = lax.broadcasted_iota(jnp.int32, (MLA_VT_ROWS - MLA_V, tm), 0)
    ones_rows = jnp.where(pad_row == 0, 1.0, 0.0).astype(BF16)
    for hd in range(MLA_HEADS):
        c0 = hd * MLA_HEAD_PAD
        k_ref[:, c0:c0 + LANES] = _rms(kn[:, hd * LANES:(hd + 1) * LANES], gkn).astype(BF16)
        k_ref[:, c0 + LANES:c0 + 2 * LANES] = kpe
        vt_ref[hd, :MLA_V, :] = vt[hd * MLA_V:(hd + 1) * MLA_V].astype(BF16)
        vt_ref[hd, MLA_V:, :] = ones_rows


def _mla_prep(x2, g_mix, wqt, wlatt, wuvt, wckv, wuk, cos_tt, sin_tt, gqnt, gqrt, gckvt, gkrt,
              gckv, gkn, *, tm):
    T, D = x2.shape
    HP = MLA_HEADS * MLA_HEAD_PAD
    row = lambda w: pl.BlockSpec((tm, w), lambda i: (i, 0))
    col = lambda r: pl.BlockSpec((r, tm), lambda i: (0, i))
    consts = (g_mix, wqt, wlatt, wuvt, wckv, wuk)
    gains = (gqnt, gqrt, gckvt, gkrt, gckv, gkn)
    return pl.pallas_call(
        _mla_prep_kernel,
        grid=(T // tm,),
        in_specs=[row(D)] + [_const_spec(c.shape) for c in consts]
        + [col(cos_tt.shape[0]), col(sin_tt.shape[0])]
        + [_const_spec(g.shape) for g in gains],
        out_specs=[row(D), pl.BlockSpec((MLA_HEADS, None, MLA_HEAD_PAD, tm), lambda i: (0, i, 0, 0)), row(HP),
                   pl.BlockSpec((MLA_HEADS, None, MLA_VT_ROWS, tm), lambda i: (0, i, 0, 0))],
        out_shape=[jax.ShapeDtypeStruct((T, D), BF16),
                   jax.ShapeDtypeStruct((MLA_HEADS, T // tm, MLA_HEAD_PAD, tm), BF16),
                   jax.ShapeDtypeStruct((T, HP), BF16),
                   jax.ShapeDtypeStruct((MLA_HEADS, T // tm, MLA_VT_ROWS, tm), BF16)],
        compiler_params=_params("parallel"),
        name="mla_prep",
    )(x2, *consts, cos_tt, sin_tt, *gains)


ATTN_TILES_PER_STEP = 4


def _mla_attn_kernel(qt_ref, qt_next_ref, k_ref, vt_ref, o_ref, s_a, s_b, s_c, max_a, max_b, max_c,
                     *stat_refs, tq, heads, nq):
    quad = pl.program_id(2)
    hs = range(heads)
    buf_a, buf_b, buf_c = (s_a, max_a), (s_b, max_b), (s_c, max_c)
    stats_of = [(stat_refs[2 * r], stat_refs[2 * r + 1]) for r in range(ATTN_TILES_PER_STEP)]

    def step(prod=None, cons=None):
        half = tq // 2
        if prod is not None:
            jp, q_ref, q_tile, (sp_ref, mp_ref), diagonal = prod
            r0 = pl.multiple_of(jp * tq, tq)
            for h in hs:
                cols = slice(h * MLA_HEAD_PAD, (h + 1) * MLA_HEAD_PAD)
                q = q_ref[h] if q_tile is None else q_ref[h, q_tile]
                if diagonal:
                    sp_ref[h, :half, :] = _dot(k_ref[pl.ds(r0, half), cols], q)
                    sp_ref[h, half:, half:] = _dot(k_ref[pl.ds(r0 + half, half), cols], q[:, half:])
                else:
                    s = _dot(k_ref[pl.ds(r0, tq), cols], q)
                    sp_ref[h] = s
                    mp_ref[h] = jnp.max(s, axis=0, keepdims=True)
        if cons is not None:
            jc, (sc_ref, mc_ref), diagonal, (m_ref, acc_ref) = cons
            for h in hs:
                if diagonal:
                    causal = lambda s: jnp.where(lax.broadcasted_iota(jnp.int32, s.shape, 0)
                                                 <= lax.broadcasted_iota(jnp.int32, s.shape, 1), s, -1e30)
                    top = causal(sc_ref[h, :half, :])
                    low = causal(sc_ref[h, half:, half:])
                    m_top = jnp.max(top, axis=0, keepdims=True)
                    m_new = jnp.concatenate(
                        [m_top[:, :half], jnp.maximum(m_top[:, half:], jnp.max(low, axis=0, keepdims=True))], axis=1)
                    pv = _dot(vt_ref[h, jc, :, :half], jnp.exp2(top - m_new).astype(BF16))
                    pv_low = _dot(vt_ref[h, jc, :, half:], jnp.exp2(low - m_new[:, half:]).astype(BF16))
                    acc_ref[h] = jnp.concatenate([pv[:, :half], pv[:, half:] + pv_low], axis=1)
                else:
                    m = m_ref[h]
                    m_new = jnp.maximum(m, mc_ref[h])
                    pv = _dot(vt_ref[h, jc], jnp.exp2(sc_ref[h] - m_new).astype(BF16))
                    acc_ref[h] = jnp.exp2(m - m_new) * acc_ref[h] + pv
                m_ref[h] = m_new

    def write_out(stats, tile):
        for h in hs:
            acc = stats[1][h]
            o_ref[tile * tq:(tile + 1) * tq, h * MLA_V:(h + 1) * MLA_V] = (
                acc[:MLA_V] / acc[MLA_V:MLA_V + 1]).T.astype(o_ref.dtype)

    def two_blocks(t, q_tile, stats):
        j = 2 * t
        step(prod=(j + 1, qt_ref, q_tile, buf_b, False), cons=(j, buf_a, False, stats))
        step(prod=(j + 2, qt_ref, q_tile, buf_a, False), cons=(j + 1, buf_b, False, stats))

    def run_tile(i, r, trips, leftover, diag_buf, next_prod):
        stats = stats_of[r]
        step(prod=(0, qt_ref, r, buf_a, False), cons=(i, diag_buf, True, stats))
        if not (isinstance(trips, int) and trips == 0):
            def four_blocks(t, carry):
                two_blocks(2 * t, r, stats)
                two_blocks(2 * t + 1, r, stats)
                return carry

            lax.fori_loop(0, trips, four_blocks, 0)
        if leftover:
            two_blocks(2 * trips, r, stats)
        if r % 2 == 0:
            step(prod=(i - 1, qt_ref, r, buf_b, False), cons=(i - 2, buf_a, False, stats))
            step(prod=next_prod, cons=(i - 1, buf_b, False, stats))
        else:
            step(prod=next_prod, cons=(i - 1, buf_a, False, stats))
        write_out(stats, r)

    first = ATTN_TILES_PER_STEP * quad
    ahead = (jnp.minimum(first + ATTN_TILES_PER_STEP, nq - 1), qt_next_ref, None, buf_c, True)
    diag_of = lambda r: (first + r, qt_ref, r, buf_c, True)

    @pl.when(quad == 0)
    def _():
        step(prod=(0, qt_ref, 0, buf_c, True))
        step(prod=(1, qt_ref, 1, buf_b, True), cons=(0, buf_c, True, stats_of[0]))
        write_out(stats_of[0], 0)
        run_tile(1, 1, 0, 0, buf_b, diag_of(2))
        run_tile(2, 2, 0, 0, buf_c, diag_of(3))
        run_tile(3, 3, 0, 1, buf_c, ahead)

    @pl.when(quad > 0)
    def _():
        run_tile(first, 0, quad - 1, 1, buf_c, diag_of(1))
        run_tile(first + 1, 1, quad, 0, buf_c, diag_of(2))
        run_tile(first + 2, 2, quad, 0, buf_c, diag_of(3))
        run_tile(first + 3, 3, quad, 1, buf_c, ahead)


def _mla_attn(qt, k, vt, *, batch, seq, tq, heads):
    T = k.shape[0]
    nq = seq // tq
    per = ATTN_TILES_PER_STEP
    assert nq % per == 0, "q tiles are processed in groups of ATTN_TILES_PER_STEP"
    steps = nq // per
    score = pltpu.VMEM((heads, tq, tq), F32)
    stat = pltpu.VMEM((heads, 1, tq), F32)
    acc = pltpu.VMEM((heads, MLA_VT_ROWS, tq), F32)
    return pl.pallas_call(
        functools.partial(_mla_attn_kernel, tq=tq, heads=heads, nq=nq),
        grid=(batch, MLA_HEADS // heads, steps),
        in_specs=[pl.BlockSpec((heads, per, MLA_HEAD_PAD, tq), lambda b, h, s: (h, b * steps + s, 0, 0)),
                  pl.BlockSpec((heads, None, MLA_HEAD_PAD, tq),
                               lambda b, h, s: (h, b * nq + jnp.minimum(per * s + per, nq - 1), 0, 0)),
                  pl.BlockSpec((seq, heads * MLA_HEAD_PAD), lambda b, h, s: (b, h)),
                  pl.BlockSpec((heads, nq, MLA_VT_ROWS, tq), lambda b, h, s: (h, b, 0, 0))],
        out_specs=pl.BlockSpec((per * tq, heads * MLA_V), lambda b, h, s: (b * steps + s, h)),
        out_shape=jax.ShapeDtypeStruct((T, MLA_HEADS * MLA_V), BF16),
        scratch_shapes=[score, score, score, stat, stat, stat] + [stat, acc] * per,
        compiler_params=_params("parallel", "parallel", "arbitrary"),
        name="mla_attn",
    )(qt, qt, k, vt)


def _gla_proj_kernel(hb_ref, wq_ref, wk_ref, wv_ref, wa_ref, wr_ref, wg_ref, bg_ref,
                     q_ref, k_ref, v_ref, la_ref, sr_ref):
    hb = hb_ref[...]
    a = _dot(hb, wa_ref[...])
    r = _dot(hb, wr_ref[...])
    z = _dot(a.astype(BF16), wg_ref[...]) + bg_ref[...]
    log_sig = jnp.minimum(z, 0.0) - jnp.log1p(jnp.exp(-jnp.abs(z)))
    la_ref[...] = log_sig * (1.0 / GLA_GATE_NORMALIZER)
    sr_ref[...] = (r / (1.0 + jnp.exp(-r))).astype(BF16)
    v_ref[...] = _dot(hb, wv_ref[...]).astype(BF16)
    q_ref[...] = (_dot(hb, wq_ref[...]) * float(GLA_DK ** -0.5)).astype(BF16)
    k_ref[...] = _dot(hb, wk_ref[...]).astype(BF16)


def _gla_proj(hb, wq, wk, wv, wa, wr, wg, bg, *, tm):
    T, D = hb.shape
    nk, nv = wq.shape[1], wv.shape[1]
    row = lambda w: pl.BlockSpec((tm, w), lambda i: (i, 0))
    return pl.pallas_call(
        _gla_proj_kernel,
        grid=(T // tm,),
        in_specs=[row(D)] + [_const_spec(w.shape) for w in (wq, wk, wv, wa, wr, wg, bg)],
        out_specs=[row(nk), row(nk), row(nv), row(nk), row(nv)],
        out_shape=[jax.ShapeDtypeStruct((T, nk), BF16), jax.ShapeDtypeStruct((T, nk), BF16),
                   jax.ShapeDtypeStruct((T, nv), BF16), jax.ShapeDtypeStruct((T, nk), F32),
                   jax.ShapeDtypeStruct((T, nv), BF16)],
        compiler_params=_params("parallel"),
        name="gla_proj",
    )(hb, wq, wk, wv, wa, wr, wg, bg)


GLA_LEVELS = tuple(GLA_CHUNK >> (i + 1) for i in range(GLA_CHUNK.bit_length() - 1))


def _gla_decay_matrix():
    r = np.arange(GLA_CHUNK)
    groups = [r[None, :] <= r[:, None], r[None, :] > r[:, None]]
    for s in GLA_LEVELS:
        mid = (r // (2 * s)) * (2 * s) + s
        upper = (r & s) != 0
        up = (r[None, :] > mid[:, None]) & (r[None, :] <= r[:, None])
        lo = (r[None, :] > r[:, None]) & (r[None, :] <= mid[:, None])
        groups.append(np.where(upper[:, None], up, lo))
    return np.concatenate(groups, axis=0).astype(np.float32)


def _gla_level_masks():
    r = np.arange(GLA_CHUNK)
    x = r[:, None] ^ r[None, :]
    lower = r[None, :] < r[:, None]
    return np.stack([(lower & (x >= s) & (x < 2 * s)) for s in GLA_LEVELS]).astype(np.float32)


def _gla_group(qs, ks, vs, las, st, nmat, masks_ref):
    C, n = GLA_CHUNK, len(qs)
    la2 = jnp.concatenate([jnp.concatenate(_split(la), axis=0) for la in las], axis=1)
    e_all = jnp.exp(_dot(nmat, la2))
    e = [e_all[:, u * GLA_DK:(u + 1) * GLA_DK] for u in range(n)]

    row = lax.broadcasted_iota(jnp.int32, qs[0].shape, 0)
    a = [jnp.zeros((C, C), F32) for _ in range(n)]
    for lvl, s in enumerate(GLA_LEVELS):
        upper = (row & s) != 0
        mask = masks_ref[lvl]
        for u in range(n):
            t = (jnp.where(upper, qs[u], ks[u]) * e[u][(2 + lvl) * C:(3 + lvl) * C]).astype(BF16)
            a[u] = a[u] + lax.dot_general(t, t, NT_DIMS, preferred_element_type=F32) * mask
    eye = (lax.broadcasted_iota(jnp.int32, (C, C), 0) == lax.broadcasted_iota(jnp.int32, (C, C), 1))
    o_intra, upd = [], []
    for u in range(n):
        a_u = jnp.where(eye, jnp.sum(qs[u] * ks[u], axis=-1, keepdims=True), a[u])
        o_intra.append(_dot(a_u.astype(BF16), vs[u]))
        k_dec = (ks[u] * e[u][C:2 * C]).astype(BF16)
        upd.append(lax.dot_general(k_dec, vs[u], TN_DIMS, preferred_element_type=F32))
    outs = []
    for u in range(n):
        eb = e[u][:C]
        outs.append(o_intra[u] + _dot((qs[u] * eb).astype(BF16), st.astype(BF16)))
        decay = jnp.broadcast_to(eb[C - 1:C, :], (GLA_DK, GLA_DK)).T
        st = st * jnp.concatenate([decay] * (GLA_DV // GLA_DK), axis=1) + upd[u]
    return outs, st


def _gla_rec_kernel(q_ref, k_ref, v_ref, la_ref, sr_ref, g_ref, nmat_ref, masks_ref, o_ref, st_ref, *,
                    ts, group):
    @pl.when(pl.program_id(2) == 0)
    def _():
        st_ref[...] = jnp.zeros_like(st_ref)

    g = g_ref[...]
    nmat = nmat_ref[...]
    span = group * GLA_CHUNK

    def body(c, _):
        base = pl.multiple_of(c * span, span)
        rows = [pl.ds(base + u * GLA_CHUNK, GLA_CHUNK) for u in range(group)]
        outs, st = _gla_group([q_ref[r, :].astype(F32) for r in rows], [k_ref[r, :].astype(F32) for r in rows],
                              [v_ref[r, :] for r in rows], [la_ref[r, :] for r in rows], st_ref[...],
                              nmat, masks_ref)
        st_ref[...] = st
        for r, o in zip(rows, outs):
            o_ref[r, :] = (_rms(o, g) * sr_ref[r, :].astype(F32)).astype(o_ref.dtype)
        return 0

    lax.fori_loop(0, ts // span, body, 0)


def _gla_rec(q, k, v, la, sr, g_out, nmat, masks, *, batch, seq, ts, group):
    T = q.shape[0]
    ns = seq // ts
    blk = lambda w: pl.BlockSpec((ts, w), lambda b, h, i: (b * ns + i, h))
    return pl.pallas_call(
        functools.partial(_gla_rec_kernel, ts=ts, group=group),
        grid=(batch, GLA_HEADS, ns),
        in_specs=[blk(GLA_DK), blk(GLA_DK), blk(GLA_DV), blk(GLA_DK), blk(GLA_DV),
                  _const_spec(g_out.shape), _const_spec(nmat.shape), _const_spec(masks.shape)],
        out_specs=blk(GLA_DV),
        out_shape=jax.ShapeDtypeStruct((T, GLA_HEADS * GLA_DV), BF16),
        scratch_shapes=[pltpu.VMEM((GLA_DK, GLA_DV), F32)],
        compiler_params=_params("parallel", "parallel", "arbitrary"),
        name="gla_rec",
    )(q, k, v, la, sr, g_out, nmat, masks)


def _mem_kv_kernel(mem_ref, g_ref, w_ref, gk_ref, seg_ref, k_ref, v_ref):
    kv = _dot(_rms(mem_ref[...], g_ref[...]).astype(BF16), w_ref[...])
    nk = MEM_HEADS * MEM_DQK
    k = kv[:, :nk]
    ss = _dot_split(k * k, seg_ref[...]) * (1.0 / MEM_DQK)
    k_ref[...] = (k * lax.rsqrt(ss + EPS) * gk_ref[...]).astype(BF16)
    v_ref[...] = kv[:, nk:].astype(BF16)


def _mem_kv(mem2, g_mem, w, gk, seg, *, batch, mem_len):
    D = mem2.shape[1]
    nk, nv = MEM_HEADS * MEM_DQK, MEM_HEADS * MEM_DV
    row = lambda w_: pl.BlockSpec((mem_len, w_), lambda b: (b, 0))
    return pl.pallas_call(
        _mem_kv_kernel,
        grid=(batch,),
        in_specs=[row(D), _const_spec(g_mem.shape), _const_spec(w.shape), _const_spec(gk.shape),
                  _const_spec(seg.shape)],
        out_specs=[row(nk), row(nv)],
        out_shape=[jax.ShapeDtypeStruct((batch * mem_len, nk), BF16),
                   jax.ShapeDtypeStruct((batch * mem_len, nv), BF16)],
        compiler_params=_params("parallel"),
        name="mem_kv",
    )(mem2, g_mem, w, gk, seg)


def _merge_kernel(x_ref, hb_ref, omla_ref, ogla_ref, km_ref, vm_ref, wqm_ref, gqm_ref, seg_ref,
                  wgate_ref, bgate_ref, wo_ref, o_ref):
    hb = hb_ref[...]
    D = x_ref.shape[1]
    km = km_ref[...]

    def gate(j):
        z = _dot(hb, wgate_ref[:, j * D:(j + 1) * D]) + bgate_ref[:, j * D:(j + 1) * D]
        return 1.0 / (1.0 + jnp.exp(-z))

    qm = _dot(hb, wqm_ref[...])
    y = gate(0) * omla_ref[...].astype(F32)
    ss = _dot_split(qm * qm, seg_ref[...]) * (1.0 / MEM_DQK)
    qn = qm * lax.rsqrt(ss + EPS) * (gqm_ref[...] * float(MEM_DQK ** -0.5))
    head_of_lane = lax.broadcasted_iota(jnp.int32, qn.shape, 1) // MEM_DQK
    y = y + gate(1) * ogla_ref[...].astype(F32)
    probs = []
    for h in range(MEM_HEADS):
        qh = jnp.where(head_of_lane == h, qn, 0.0).astype(BF16)
        s = lax.dot_general(qh, km, NT_DIMS, preferred_element_type=F32)
        probs.append(jnp.exp(s - jnp.max(s, axis=-1, keepdims=True)))
    g_mem = gate(2)
    parts = []
    for h, p in enumerate(probs):
        o_h = _dot(p.astype(BF16), vm_ref[:, h * MEM_DV:(h + 1) * MEM_DV])
        parts.append(o_h / jnp.sum(p, axis=-1, keepdims=True))
    y = y + g_mem * jnp.concatenate(parts, axis=-1)
    o_ref[...] = x_ref[...] + _dot(y.astype(BF16), wo_ref[...])


def _merge(x2, hb, o_mla, o_gla, km, vm, wqm, gqm, seg, wgate, bgate, wo, *, seq, mem_len, tm):
    T, D = x2.shape
    per_batch = seq // tm
    row = lambda w: pl.BlockSpec((tm, w), lambda i: (i, 0))
    mem_blk = lambda w: pl.BlockSpec((mem_len, w), lambda i: (i // per_batch, 0))
    return pl.pallas_call(
        _merge_kernel,
        grid=(T // tm,),
        in_specs=[row(D), row(D), row(D), row(D), mem_blk(km.shape[1]), mem_blk(vm.shape[1])]
        + [_const_spec(w.shape) for w in (wqm, gqm, seg, wgate, bgate, wo)],
        out_specs=row(D),
        out_shape=jax.ShapeDtypeStruct((T, D), F32),
        compiler_params=_params("parallel"),
        name="merge",
    )(x2, hb, o_mla, o_gla, km, vm, wqm, gqm, seg, wgate, bgate, wo)


def _ffn_kernel(x_ref, g_ref, wup_ref, wdn_ref, o_ref, *, ff_chunk):
    x = x_ref[...]
    hb = _rms(x, g_ref[...]).astype(BF16)
    acc = x
    for c in range(wup_ref.shape[1] // ff_chunk):
        u = jnp.maximum(_dot(hb, wup_ref[:, c * ff_chunk:(c + 1) * ff_chunk]), 0.0)
        acc = acc + _dot((u * u).astype(BF16), wdn_ref[c * ff_chunk:(c + 1) * ff_chunk, :])
    o_ref[...] = acc


def _ffn(x1, g_ffn, wup, wdn, *, tm, ff_chunk):
    T, D = x1.shape
    row = pl.BlockSpec((tm, D), lambda i: (i, 0))
    return pl.pallas_call(
        functools.partial(_ffn_kernel, ff_chunk=ff_chunk),
        grid=(T // tm,),
        in_specs=[row, _const_spec(g_ffn.shape), _const_spec(wup.shape), _const_spec(wdn.shape)],
        out_specs=row,
        out_shape=jax.ShapeDtypeStruct((T, D), F32),
        compiler_params=_params("parallel"),
        name="ffn",
    )(x1, g_ffn, wup, wdn)


def _pad_cols(w, n):
    return jnp.pad(w, ((0, 0), (0, n - w.shape[-1])))


def _layer(x, mem, positions, g_mix, w_in, b_gate, g_ckv, w_ukv, g_q_nope, g_k_nope, g_q_rope, g_k_rope,
           w_gla_gate, b_gla_gate, g_gla_out, g_mem, w_mem_kv, g_q_mem, g_k_mem, w_o, g_ffn, w_up, w_down):
    B, S, D = x.shape
    M = mem.shape[1]
    T = B * S
    rank = g_ckv.shape[0]
    row = lambda g: g.reshape(1, -1).astype(F32)

    sizes = (MLA_HEADS * (MLA_NOPE + MLA_ROPE), rank, MLA_ROPE, GLA_HEADS * GLA_DK, GLA_HEADS * GLA_DK,
             GLA_HEADS * GLA_DV, GLA_GATE_RANK, GLA_HEADS * GLA_DV, MEM_HEADS * MEM_DQK, N_BRANCHES * D)
    offs = np.concatenate([[0], np.cumsum(sizes)])
    (w_q, w_ckv, w_kr, w_gq, w_gk, w_gv, w_ga, w_gr, w_qm, w_gate) = [
        w_in[:, offs[i]:offs[i + 1]] for i in range(len(sizes))]

    tq = min(ATTN_TILE, S)
    w_ukv3 = w_ukv.reshape(rank, MLA_HEADS, MLA_NOPE + MLA_V)
    wuk = w_ukv3[:, :, :MLA_NOPE].reshape(rank, -1).astype(BF16)
    wuvt = w_ukv3[:, :, MLA_NOPE:].reshape(rank, -1).T.astype(BF16)
    q_scale = float((MLA_NOPE + MLA_ROPE) ** -0.5 * np.log2(np.e))
    lanes_of = lambda g, scale=1.0: jnp.broadcast_to((g.astype(F32) * scale)[:, None], (g.shape[0], tq))

    inv = 1.0 / (ROPE_THETA ** (jnp.arange(0, MLA_ROPE, 2, dtype=F32) / MLA_ROPE))
    ang = inv[:, None] * positions.astype(F32).reshape(1, T)

    x2 = x.reshape(T, D)
    hb, qt, k, vt = _mla_prep(
        x2, row(g_mix), w_q.T.astype(BF16), jnp.concatenate([w_ckv, w_kr], axis=1).T.astype(BF16), wuvt,
        w_ckv.astype(BF16), wuk,
        jnp.cos(ang), jnp.sin(ang), lanes_of(g_q_nope, q_scale), lanes_of(g_q_rope, q_scale), lanes_of(g_ckv),
        lanes_of(g_k_rope), row(g_ckv), row(g_k_nope), tm=tq)
    o_mla = _mla_attn(qt, k, vt, batch=B, seq=S, tq=tq, heads=ATTN_HEADS_PER_STEP)

    nmat = jnp.asarray(np.tile(_gla_decay_matrix(), (1, 2)), BF16)
    masks = jnp.asarray(_gla_level_masks())
    wg = jnp.pad(w_gla_gate, ((0, LANES - GLA_GATE_RANK), (0, 0))).astype(BF16)
    qg, kg, vg, la, sr = _gla_proj(hb, w_gq.astype(BF16), w_gk.astype(BF16), w_gv.astype(BF16),
                                   _pad_cols(w_ga, LANES).astype(BF16), w_gr.astype(BF16), wg,
                                   row(b_gla_gate), tm=min(GLA_PROJ_TILE, S))
    ts = min(GLA_STEP_TOKENS, S)
    o_gla = _gla_rec(qg, kg, vg, la, sr, row(g_gla_out), nmat, masks, batch=B, seq=S, ts=ts,
                     group=ts // GLA_CHUNK)

    seg = jnp.asarray(np.kron(np.eye(MEM_HEADS), np.ones((MEM_DQK, MEM_DQK))).astype(np.float32), BF16)
    km, vm = _mem_kv(mem.reshape(B * M, D), row(g_mem), w_mem_kv.astype(BF16),
                     jnp.tile(row(g_k_mem), (1, MEM_HEADS)), seg, batch=B, mem_len=M)
    x1 = _merge(x2, hb, o_mla, o_gla, km, vm, w_qm.astype(BF16), jnp.tile(row(g_q_mem), (1, MEM_HEADS)), seg,
                w_gate.astype(BF16), row(b_gate), w_o.astype(BF16), seq=S, mem_len=M, tm=min(MERGE_TILE, S))
    out = _ffn(x1, row(g_ffn), w_up.astype(BF16), w_down.astype(BF16), tm=min(FFN_TILE, S), ff_chunk=FFN_CHUNK)
    return out.reshape(B, S, D)


def kernel(x, mem, positions, g_mix, w_in, b_gate, g_ckv, w_ukv, g_q_nope, g_k_nope, g_q_rope, g_k_rope,
           w_gla_gate, b_gla_gate, g_gla_out, g_mem, w_mem_kv, g_q_mem, g_k_mem, w_o, g_ffn, w_up, w_down):
    for l in range(g_mix.shape[0]):
        x = _layer(x, mem, positions, g_mix[l], w_in[l], b_gate[l], g_ckv[l], w_ukv[l], g_q_nope[l],
                   g_k_nope[l], g_q_rope[l], g_k_rope[l], w_gla_gate[l], b_gla_gate[l], g_gla_out[l],
                   g_mem[l], w_mem_kv[l], g_q_mem[l], g_k_mem[l], w_o[l], g_ffn[l], w_up[l], w_down[l])
    return x
```

```python
import functools

import jax
import jax.numpy as jnp
import numpy as np
from jax import lax
from jax.experimental import pallas as pl
from jax.experimental.pallas import tpu as pltpu

F32 = jnp.float32
BF16 = jnp.bfloat16

EPS = 1e-6
ROPE_THETA = 10000.0
LANES = 128
BF16_SUBLANES = 16

MLA_HEADS, MLA_NOPE, MLA_ROPE, MLA_V = 8, 128, 64, 128
MLA_HEAD_PAD = 2 * LANES
MLA_VT_ROWS = MLA_V + BF16_SUBLANES
GLA_HEADS, GLA_DK, GLA_DV = 4, 128, 256
GLA_GATE_RANK, GLA_GATE_NORMALIZER, GLA_CHUNK = 16, 16.0, 64
MEM_HEADS, MEM_DQK, MEM_DV = 4, 64, 256
N_BRANCHES = 3

VMEM_LIMIT = 48 * 1024 * 1024

ATTN_TILE = 512
ATTN_HEADS_PER_STEP = 2
GLA_PROJ_TILE = 1024
GLA_STEP_TOKENS = 1024
MERGE_TILE = 512
FFN_TILE = 1024
FFN_CHUNK = 1024

NT_DIMS = (((1,), (1,)), ((), ()))
TN_DIMS = (((0,), (0,)), ((), ()))


def _params(*sem):
    return pltpu.CompilerParams(dimension_semantics=sem, vmem_limit_bytes=VMEM_LIMIT)


def _rms(t, g, n=None):
    n = t.shape[-1] if n is None else n
    ss = jnp.sum(t * t, axis=-1, keepdims=True) * (1.0 / n)
    return t * lax.rsqrt(ss + EPS) * g


def _dot(a, b):
    return jnp.dot(a, b, preferred_element_type=F32)


def _split(a_f32):
    hi = a_f32.astype(BF16)
    return hi, (a_f32 - hi.astype(F32)).astype(BF16)


def _dot_split(a_f32, b_bf16):
    hi, lo = _split(a_f32)
    return _dot(hi, b_bf16) + _dot(lo, b_bf16)


def _const_spec(shape):
    return pl.BlockSpec(shape, lambda *_: (0,) * len(shape))


def _rms_rows(t, g):
    ss = jnp.sum(t * t, axis=0, keepdims=True) * (1.0 / t.shape[0])
    return t * lax.rsqrt(ss + EPS) * g


def _rope_rows(r, cos, sin):
    half = r.shape[0] // 2
    t1, t2 = r[:half], r[half:]
    return t1 * cos - t2 * sin, t1 * sin + t2 * cos


def _mla_prep_kernel(x_ref, gmix_ref, wqt_ref, wlatt_ref, wuvt_ref, wckv_ref, wuk_ref,
                     cost_ref, sint_ref, gqnt_ref, gqrt_ref, gckvt_ref, gkrt_ref,
                     gckv_ref, gkn_ref, hb_ref, qt_ref, k_ref, vt_ref):
    h = _rms(x_ref[...], gmix_ref[...])
    hb = h.astype(BF16)
    hb_ref[...] = hb
    hbt = h.T.astype(BF16)
    tm = hbt.shape[1]
    half = MLA_ROPE // 2
    dq = MLA_NOPE + MLA_ROPE

    cos, sin = cost_ref[...], sint_ref[...]
    rank = gckv_ref.shape[-1]
    gqn, gqr = gqnt_ref[...], gqrt_ref[...]
    zeros = jnp.zeros((MLA_HEAD_PAD - dq, tm), BF16)

    def q_heads(heads):
        for hd in heads:
            t = _dot(wqt_ref[hd * dq:(hd + 1) * dq, :], hbt)
            qt_ref[hd, :MLA_NOPE, :] = _rms_rows(t[:MLA_NOPE], gqn).astype(BF16)
            r1, r2 = _rope_rows(_rms_rows(t[MLA_NOPE:], gqr), cos, sin)
            qt_ref[hd, MLA_NOPE:MLA_NOPE + half, :] = r1.astype(BF16)
            qt_ref[hd, MLA_NOPE + half:dq, :] = r2.astype(BF16)
            qt_ref[hd, dq:, :] = zeros

    ckv_raw = _dot(hb, wckv_ref[...])
    latt = _dot(wlatt_ref[...], hbt)
    q_heads(range(0, MLA_HEADS // 2))

    ckv = _rms(ckv_raw, gckv_ref[...]).astype(BF16)
    kn = _dot(ckv, wuk_ref[...])
    ckvt = _rms_rows(latt[:rank], gckvt_ref[...]).astype(BF16)
    vt = _dot(wuvt_ref[...], ckvt)
    q_heads(range(MLA_HEADS // 2, MLA_HEADS))

    kr1, kr2 = _rope_rows(_rms_rows(latt[rank:], gkrt_ref[...]), cos, sin)
    kpe = jnp.concatenate([kr1, kr2, jnp.zeros((LANES - MLA_ROPE, tm), F32)], axis=0).T.astype(BF16)
    gkn = gkn_ref[...]
    pad_row = lax.broadcasted_iota(jnp.int32, (MLA_VT_ROWS - MLA_V, tm), 0)
    ones_rows = jnp.where(pad_row == 0, 1.0, 0.0).astype(BF16)
    for hd in range(MLA_HEADS):
        c0 = hd * MLA_HEAD_PAD
        k_ref[:, c0:c0 + LANES] = _rms(kn[:, hd * LANES:(hd + 1) * LANES], gkn).astype(BF16)
        k_ref[:, c0 + LANES:c0 + 2 * LANES] = kpe
        vt_ref[hd, :MLA_V, :] = vt[hd * MLA_V:(hd + 1) * MLA_V].astype(BF16)
        vt_ref[hd, MLA_V:, :] = ones_rows


def _mla_prep(x2, g_mix, wqt, wlatt, wuvt, wckv, wuk, cos_tt, sin_tt, gqnt, gqrt, gckvt, gkrt,
              gckv, gkn, *, tm):
    T, D = x2.shape
    HP = MLA_HEADS * MLA_HEAD_PAD
    row = lambda w: pl.BlockSpec((tm, w), lambda i: (i, 0))
    col = lambda r: pl.BlockSpec((r, tm), lambda i: (0, i))
    consts = (g_mix, wqt, wlatt, wuvt, wckv, wuk)
    gains = (gqnt, gqrt, gckvt, gkrt, gckv, gkn)
    return pl.pallas_call(
        _mla_prep_kernel,
        grid=(T // tm,),
        in_specs=[row(D)] + [_const_spec(c.shape) for c in consts]
        + [col(cos_tt.shape[0]), col(sin_tt.shape[0])]
        + [_const_spec(g.shape) for g in gains],
        out_specs=[row(D), pl.BlockSpec((MLA_HEADS, None, MLA_HEAD_PAD, tm), lambda i: (0, i, 0, 0)), row(HP),
                   pl.BlockSpec((MLA_HEADS, None, MLA_VT_ROWS, tm), lambda i: (0, i, 0, 0))],
        out_shape=[jax.ShapeDtypeStruct((T, D), BF16),
                   jax.ShapeDtypeStruct((MLA_HEADS, T // tm, MLA_HEAD_PAD, tm), BF16),
                   jax.ShapeDtypeStruct((T, HP), BF16),
                   jax.ShapeDtypeStruct((MLA_HEADS, T // tm, MLA_VT_ROWS, tm), BF16)],
        compiler_params=_params("parallel"),
        name="mla_prep",
    )(x2, *consts, cos_tt, sin_tt, *gains)


ATTN_TILES_PER_STEP = 4


def _mla_attn_kernel(qt_ref, qt_next_ref, k_ref, vt_ref, o_ref, s_a, s_b, s_c, max_a, max_b, max_c,
                     *stat_refs, tq, heads, nq):
    quad = pl.program_id(2)
    hs = range(heads)
    buf_a, buf_b, buf_c = (s_a, max_a), (s_b, max_b), (s_c, max_c)
    stats_of = [(stat_refs[2 * r], stat_refs[2 * r + 1]) for r in range(ATTN_TILES_PER_STEP)]

    def step(prod=None, cons=None):
        half = tq // 2
        if prod is not None:
            jp, q_ref, q_tile, (sp_ref, mp_ref), diagonal = prod
            r0 = pl.multiple_of(jp * tq, tq)
            for h in hs:
                cols = slice(h * MLA_HEAD_PAD, (h + 1) * MLA_HEAD_PAD)
                q = q_ref[h] if q_tile is None else q_ref[h, q_tile]
                if diagonal:
                    sp_ref[h, :half, :] = _dot(k_ref[pl.ds(r0, half), cols], q)
                    sp_ref[h, half:, half:] = _dot(k_ref[pl.ds(r0 + half, half), cols], q[:, half:])
                else:
                    s = _dot(k_ref[pl.ds(r0, tq), cols], q)
                    sp_ref[h] = s
                    mp_ref[h] = jnp.max(s, axis=0, keepdims=True)
        if cons is not None:
            jc, (sc_ref, mc_ref), diagonal, (m_ref, acc_ref) = cons
            for h in hs:
                if diagonal:
                    causal = lambda s: jnp.where(lax.broadcasted_iota(jnp.int32, s.shape, 0)
                                                 <= lax.broadcasted_iota(jnp.int32, s.shape, 1), s, -1e30)
                    top = causal(sc_ref[h, :half, :])
                    low = causal(sc_ref[h, half:, half:])
                    m_top = jnp.max(top, axis=0, keepdims=True)
                    m_new = jnp.concatenate(
                        [m_top[:, :half], jnp.maximum(m_top[:, half:], jnp.max(low, axis=0, keepdims=True))], axis=1)
                    pv = _dot(vt_ref[h, jc, :, :half], jnp.exp2(top - m_new).astype(BF16))
                    pv_low = _dot(vt_ref[h, jc, :, half:], jnp.exp2(low - m_new[:, half:]).astype(BF16))
                    acc_ref[h] = jnp.concatenate([pv[:, :half], pv[:, half:] + pv_low], axis=1)
                else:
                    m = m_ref[h]
                    m_new = jnp.maximum(m, mc_ref[h])
                    pv = _dot(vt_ref[h, jc], jnp.exp2(sc_ref[h] - m_new).astype(BF16))
                    acc_ref[h] = jnp.exp2(m - m_new) * acc_ref[h] + pv
                m_ref[h] = m_new

    def write_out(stats, tile):
        for h in hs:
            acc = stats[1][h]
            o_ref[tile * tq:(tile + 1) * tq, h * MLA_V:(h + 1) * MLA_V] = (
                acc[:MLA_V] / acc[MLA_V:MLA_V + 1]).T.astype(o_ref.dtype)

    def two_blocks(t, q_tile, stats):
        j = 2 * t
        step(prod=(j + 1, qt_ref, q_tile, buf_b, False), cons=(j, buf_a, False, stats))
        step(prod=(j + 2, qt_ref, q_tile, buf_a, False), cons=(j + 1, buf_b, False, stats))

    def run_tile(i, r, trips, leftover, diag_buf, next_prod):
        stats = stats_of[r]
        step(prod=(0, qt_ref, r, buf_a, False), cons=(i, diag_buf, True, stats))
        if not (isinstance(trips, int) and trips == 0):
            def four_blocks(t, carry):
                two_blocks(2 * t, r, stats)
                two_blocks(2 * t + 1, r, stats)
                return carry

            lax.fori_loop(0, trips, four_blocks, 0)
        if leftover:
            two_blocks(2 * trips, r, stats)
        if r % 2 == 0:
            step(prod=(i - 1, qt_ref, r, buf_b, False), cons=(i - 2, buf_a, False, stats))
            step(prod=next_prod, cons=(i - 1, buf_b, False, stats))
        else:
            step(prod=next_prod, cons=(i - 1, buf_a, False, stats))
        write_out(stats, r)

    first = ATTN_TILES_PER_STEP * quad
    ahead = (jnp.minimum(first + ATTN_TILES_PER_STEP, nq - 1), qt_next_ref, None, buf_c, True)
    diag_of = lambda r: (first + r, qt_ref, r, buf_c, True)

    @pl.when(quad == 0)
    def _():
        step(prod=(0, qt_ref, 0, buf_c, True))
        step(prod=(1, qt_ref, 1, buf_b, True), cons=(0, buf_c, True, stats_of[0]))
        write_out(stats_of[0], 0)
        run_tile(1, 1, 0, 0, buf_b, diag_of(2))
        run_tile(2, 2, 0, 0, buf_c, diag_of(3))
        run_tile(3, 3, 0, 1, buf_c, ahead)

    @pl.when(quad > 0)
    def _():
        run_tile(first, 0, quad - 1, 1, buf_c, diag_of(1))
        run_tile(first + 1, 1, quad, 0, buf_c, diag_of(2))
        run_tile(first + 2, 2, quad, 0, buf_c, diag_of(3))
        run_tile(first + 3, 3, quad, 1, buf_c, ahead)


def _mla_attn(qt, k, vt, *, batch, seq, tq, heads):
    T = k.shape[0]
    nq = seq // tq
    per = ATTN_TILES_PER_STEP
    assert nq % per == 0, "q tiles are processed in groups of ATTN_TILES_PER_STEP"
    steps = nq // per
    score = pltpu.VMEM((heads, tq, tq), F32)
    stat = pltpu.VMEM((heads, 1, tq), F32)
    acc = pltpu.VMEM((heads, MLA_VT_ROWS, tq), F32)
    return pl.pallas_call(
        functools.partial(_mla_attn_kernel, tq=tq, heads=heads, nq=nq),
        grid=(batch, MLA_HEADS // heads, steps),
        in_specs=[pl.BlockSpec((heads, per, MLA_HEAD_PAD, tq), lambda b, h, s: (h, b * steps + s, 0, 0)),
                  pl.BlockSpec((heads, None, MLA_HEAD_PAD, tq),
                               lambda b, h, s: (h, b * nq + jnp.minimum(per * s + per, nq - 1), 0, 0)),
                  pl.BlockSpec((seq, heads * MLA_HEAD_PAD), lambda b, h, s: (b, h)),
                  pl.BlockSpec((heads, nq, MLA_VT_ROWS, tq), lambda b, h, s: (h, b, 0, 0))],
        out_specs=pl.BlockSpec((per * tq, heads * MLA_V), lambda b, h, s: (b * steps + s, h)),
        out_shape=jax.ShapeDtypeStruct((T, MLA_HEADS * MLA_V), BF16),
        scratch_shapes=[score, score, score, stat, stat, stat] + [stat, acc] * per,
        compiler_params=_params("parallel", "parallel", "arbitrary"),
        name="mla_attn",
    )(qt, qt, k, vt)


def _gla_proj_kernel(hb_ref, wq_ref, wk_ref, wv_ref, wa_ref, wr_ref, wg_ref, bg_ref,
                     q_ref, k_ref, v_ref, la_ref, sr_ref):
    hb = hb_ref[...]
    a = _dot(hb, wa_ref[...])
    r = _dot(hb, wr_ref[...])
    z = _dot(a.astype(BF16), wg_ref[...]) + bg_ref[...]
    log_sig = jnp.minimum(z, 0.0) - jnp.log1p(jnp.exp(-jnp.abs(z)))
    la_ref[...] = log_sig * (1.0 / GLA_GATE_NORMALIZER)
    sr_ref[...] = (r / (1.0 + jnp.exp(-r))).astype(BF16)
    v_ref[...] = _dot(hb, wv_ref[...]).astype(BF16)
    q_ref[...] = (_dot(hb, wq_ref[...]) * float(GLA_DK ** -0.5)).astype(BF16)
    k_ref[...] = _dot(hb, wk_ref[...]).astype(BF16)


def _gla_proj(hb, wq, wk, wv, wa, wr, wg, bg, *, tm):
    T, D = hb.shape
    nk, nv = wq.shape[1], wv.shape[1]
    row = lambda w: pl.BlockSpec((tm, w), lambda i: (i, 0))
    return pl.pallas_call(
        _gla_proj_kernel,
        grid=(T // tm,),
        in_specs=[row(D)] + [_const_spec(w.shape) for w in (wq, wk, wv, wa, wr, wg, bg)],
        out_specs=[row(nk), row(nk), row(nv), row(nk), row(nv)],
        out_shape=[jax.ShapeDtypeStruct((T, nk), BF16), jax.ShapeDtypeStruct((T, nk), BF16),
                   jax.ShapeDtypeStruct((T, nv), BF16), jax.ShapeDtypeStruct((T, nk), F32),
                   jax.ShapeDtypeStruct((T, nv), BF16)],
        compiler_params=_params("parallel"),
        name="gla_proj",
    )(hb, wq, wk, wv, wa, wr, wg, bg)


GLA_LEVELS = tuple(GLA_CHUNK >> (i + 1) for i in range(GLA_CHUNK.bit_length() - 2))


def _gla_decay_matrix():
    r = np.arange(GLA_CHUNK)
    groups = [r[None, :] <= r[:, None], r[None, :] > r[:, None]]
    for s in GLA_LEVELS:
        mid = (r // (2 * s)) * (2 * s) + s
        upper = (r & s) != 0
        up = (r[None, :] > mid[:, None]) & (r[None, :] <= r[:, None])
        lo = (r[None, :] > r[:, None]) & (r[None, :] <= mid[:, None])
        groups.append(np.where(upper[:, None], up, lo))
    return np.concatenate(groups, axis=0).astype(np.float32)


def _gla_level_masks():
    r = np.arange(GLA_CHUNK)
    x = r[:, None] ^ r[None, :]
    lower = r[None, :] < r[:, None]
    return np.stack([(lower & (x >= s) & (x < 2 * s)) for s in GLA_LEVELS]).astype(np.float32)


def _gla_group(qs, ks, vs, las, st, nmat, masks_ref):
    C, n = GLA_CHUNK, len(qs)
    la2 = jnp.concatenate([jnp.concatenate(_split(la), axis=0) for la in las], axis=1)
    e_all = jnp.exp(_dot(nmat, la2))
    e = [e_all[:, u * GLA_DK:(u + 1) * GLA_DK] for u in range(n)]

    row = lax.broadcasted_iota(jnp.int32, qs[0].shape, 0)
    a = [jnp.zeros((C, C), F32) for _ in range(n)]
    for lvl, s in enumerate(GLA_LEVELS):
        upper = (row & s) != 0
        mask = masks_ref[lvl]
        for u in range(n):
            t = (jnp.where(upper, qs[u], ks[u]) * e[u][(2 + lvl) * C:(3 + lvl) * C]).astype(BF16)
            a[u] = a[u] + lax.dot_general(t, t, NT_DIMS, preferred_element_type=F32) * mask
    r_cc = lax.broadcasted_iota(jnp.int32, (C, C), 0)
    c_cc = lax.broadcasted_iota(jnp.int32, (C, C), 1)
    eye = r_cc == c_cc
    below = (r_cc - 1 == c_cc) & ((r_cc & 1) == 1)
    o_intra, upd = [], []
    for u in range(n):
        pair = jnp.sum(qs[u] * jnp.exp(las[u]) * pltpu.roll(ks[u], 1, 0), axis=-1, keepdims=True)
        a_u = jnp.where(below, pair, a[u])
        a_u = jnp.where(eye, jnp.sum(qs[u] * ks[u], axis=-1, keepdims=True), a_u)
        o_intra.append(_dot(a_u.astype(BF16), vs[u]))
        k_dec = (ks[u] * e[u][C:2 * C]).astype(BF16)
        upd.append(lax.dot_general(k_dec, vs[u], TN_DIMS, preferred_element_type=F32))
    outs = []
    for u in range(n):
        eb = e[u][:C]
        outs.append(o_intra[u] + _dot((qs[u] * eb).astype(BF16), st.astype(BF16)))
        decay = jnp.broadcast_to(eb[C - 1:C, :], (GLA_DK, GLA_DK)).T
        st = st * jnp.concatenate([decay] * (GLA_DV // GLA_DK), axis=1) + upd[u]
    return outs, st


def _gla_rec_kernel(q_ref, k_ref, v_ref, la_ref, sr_ref, g_ref, nmat_ref, masks_ref, o_ref, st_ref, *,
                    ts, group):
    @pl.when(pl.program_id(2) == 0)
    def _():
        st_ref[...] = jnp.zeros_like(st_ref)

    g = g_ref[...]
    nmat = nmat_ref[...]
    span = group * GLA_CHUNK

    def body(c, _):
        base = pl.multiple_of(c * span, span)
        rows = [pl.ds(base + u * GLA_CHUNK, GLA_CHUNK) for u in range(group)]
        outs, st = _gla_group([q_ref[r, :].astype(F32) for r in rows], [k_ref[r, :].astype(F32) for r in rows],
                              [v_ref[r, :] for r in rows], [la_ref[r, :] for r in rows], st_ref[...],
                              nmat, masks_ref)
        st_ref[...] = st
        for r, o in zip(rows, outs):
            o_ref[r, :] = (_rms(o, g) * sr_ref[r, :].astype(F32)).astype(o_ref.dtype)
        return 0

    lax.fori_loop(0, ts // span, body, 0)


def _gla_rec(q, k, v, la, sr, g_out, nmat, masks, *, batch, seq, ts, group):
    T = q.shape[0]
    ns = seq // ts
    blk = lambda w: pl.BlockSpec((ts, w), lambda b, h, i: (b * ns + i, h))
    return pl.pallas_call(
        functools.partial(_gla_rec_kernel, ts=ts, group=group),
        grid=(batch, GLA_HEADS, ns),
        in_specs=[blk(GLA_DK), blk(GLA_DK), blk(GLA_DV), blk(GLA_DK), blk(GLA_DV),
                  _const_spec(g_out.shape), _const_spec(nmat.shape), _const_spec(masks.shape)],
        out_specs=blk(GLA_DV),
        out_shape=jax.ShapeDtypeStruct((T, GLA_HEADS * GLA_DV), BF16),
        scratch_shapes=[pltpu.VMEM((GLA_DK, GLA_DV), F32)],
        compiler_params=_params("parallel", "parallel", "arbitrary"),
        name="gla_rec",
    )(q, k, v, la, sr, g_out, nmat, masks)


def _mem_kv_kernel(mem_ref, g_ref, w_ref, gk_ref, seg_ref, k_ref, v_ref):
    kv = _dot(_rms(mem_ref[...], g_ref[...]).astype(BF16), w_ref[...])
    nk = MEM_HEADS * MEM_DQK
    k = kv[:, :nk]
    ss = _dot_split(k * k, seg_ref[...]) * (1.0 / MEM_DQK)
    k_ref[...] = (k * lax.rsqrt(ss + EPS) * gk_ref[...]).astype(BF16)
    v_ref[...] = kv[:, nk:].astype(BF16)


def _mem_kv(mem2, g_mem, w, gk, seg, *, batch, mem_len):
    D = mem2.shape[1]
    nk, nv = MEM_HEADS * MEM_DQK, MEM_HEADS * MEM_DV
    row = lambda w_: pl.BlockSpec((mem_len, w_), lambda b: (b, 0))
    return pl.pallas_call(
        _mem_kv_kernel,
        grid=(batch,),
        in_specs=[row(D), _const_spec(g_mem.shape), _const_spec(w.shape), _const_spec(gk.shape),
                  _const_spec(seg.shape)],
        out_specs=[row(nk), row(nv)],
        out_shape=[jax.ShapeDtypeStruct((batch * mem_len, nk), BF16),
                   jax.ShapeDtypeStruct((batch * mem_len, nv), BF16)],
        compiler_params=_params("parallel"),
        name="mem_kv",
    )(mem2, g_mem, w, gk, seg)


def _merge_kernel(x_ref, hb_ref, omla_ref, ogla_ref, km_ref, vm_ref, wqm_ref, gqm_ref, seg_ref,
                  wgate_ref, bgate_ref, wo_ref, o_ref):
    hb = hb_ref[...]
    D = x_ref.shape[1]
    km = km_ref[...]

    def gate(j):
        z = _dot(hb, wgate_ref[:, j * D:(j + 1) * D]) + bgate_ref[:, j * D:(j + 1) * D]
        return 1.0 / (1.0 + jnp.exp(-z))

    qm = _dot(hb, wqm_ref[...])
    y = gate(0) * omla_ref[...].astype(F32)
    ss = _dot_split(qm * qm, seg_ref[...]) * (1.0 / MEM_DQK)
    qn = qm * lax.rsqrt(ss + EPS) * (gqm_ref[...] * float(MEM_DQK ** -0.5))
    head_of_lane = lax.broadcasted_iota(jnp.int32, qn.shape, 1) // MEM_DQK
    y = y + gate(1) * ogla_ref[...].astype(F32)
    probs = []
    for h in range(MEM_HEADS):
        qh = jnp.where(head_of_lane == h, qn, 0.0).astype(BF16)
        s = lax.dot_general(qh, km, NT_DIMS, preferred_element_type=F32)
        probs.append(jnp.exp(s - jnp.max(s, axis=-1, keepdims=True)))
    g_mem = gate(2)
    parts = []
    for h, p in enumerate(probs):
        o_h = _dot(p.astype(BF16), vm_ref[:, h * MEM_DV:(h + 1) * MEM_DV])
        parts.append(o_h / jnp.sum(p, axis=-1, keepdims=True))
    y = y + g_mem * jnp.concatenate(parts, axis=-1)
    o_ref[...] = x_ref[...] + _dot(y.astype(BF16), wo_ref[...])


def _merge(x2, hb, o_mla, o_gla, km, vm, wqm, gqm, seg, wgate, bgate, wo, *, seq, mem_len, tm):
    T, D = x2.shape
    per_batch = seq // tm
    row = lambda w: pl.BlockSpec((tm, w), lambda i: (i, 0))
    mem_blk = lambda w: pl.BlockSpec((mem_len, w), lambda i: (i // per_batch, 0))
    return pl.pallas_call(
        _merge_kernel,
        grid=(T // tm,),
        in_specs=[row(D), row(D), row(D), row(D), mem_blk(km.shape[1]), mem_blk(vm.shape[1])]
        + [_const_spec(w.shape) for w in (wqm, gqm, seg, wgate, bgate, wo)],
        out_specs=row(D),
        out_shape=jax.ShapeDtypeStruct((T, D), F32),
        compiler_params=_params("parallel"),
        name="merge",
    )(x2, hb, o_mla, o_gla, km, vm, wqm, gqm, seg, wgate, bgate, wo)


def _ffn_kernel(x_ref, g_ref, wup_ref, wdn_ref, o_ref, *, ff_chunk):
    x = x_ref[...]
    hb = _rms(x, g_ref[...]).astype(BF16)
    acc = x
    for c in range(wup_ref.shape[1] // ff_chunk):
        u = jnp.maximum(_dot(hb, wup_ref[:, c * ff_chunk:(c + 1) * ff_chunk]), 0.0)
        acc = acc + _dot((u * u).astype(BF16), wdn_ref[c * ff_chunk:(c + 1) * ff_chunk, :])
    o_ref[...] = acc


def _ffn(x1, g_ffn, wup, wdn, *, tm, ff_chunk):
    T, D = x1.shape
    row = pl.BlockSpec((tm, D), lambda i: (i, 0))
    return pl.pallas_call(
        functools.partial(_ffn_kernel, ff_chunk=ff_chunk),
        grid=(T // tm,),
        in_specs=[row, _const_spec(g_ffn.shape), _const_spec(wup.shape), _const_spec(wdn.shape)],
        out_specs=row,
        out_shape=jax.ShapeDtypeStruct((T, D), F32),
        compiler_params=_params("parallel"),
        name="ffn",
    )(x1, g_ffn, wup, wdn)


def _pad_cols(w, n):
    return jnp.pad(w, ((0, 0), (0, n - w.shape[-1])))


def _layer(x, mem, positions, g_mix, w_in, b_gate, g_ckv, w_ukv, g_q_nope, g_k_nope, g_q_rope, g_k_rope,
           w_gla_gate, b_gla_gate, g_gla_out, g_mem, w_mem_kv, g_q_mem, g_k_mem, w_o, g_ffn, w_up, w_down):
    B, S, D = x.shape
    M = mem.shape[1]
    T = B * S
    rank = g_ckv.shape[0]
    row = lambda g: g.reshape(1, -1).astype(F32)

    sizes = (MLA_HEADS * (MLA_NOPE + MLA_ROPE), rank, MLA_ROPE, GLA_HEADS * GLA_DK, GLA_HEADS * GLA_DK,
             GLA_HEADS * GLA_DV, GLA_GATE_RANK, GLA_HEADS * GLA_DV, MEM_HEADS * MEM_DQK, N_BRANCHES * D)
    offs = np.concatenate([[0], np.cumsum(sizes)])
    (w_q, w_ckv, w_kr, w_gq, w_gk, w_gv, w_ga, w_gr, w_qm, w_gate) = [
        w_in[:, offs[i]:offs[i + 1]] for i in range(len(sizes))]

    tq = min(ATTN_TILE, S)
    w_ukv3 = w_ukv.reshape(rank, MLA_HEADS, MLA_NOPE + MLA_V)
    wuk = w_ukv3[:, :, :MLA_NOPE].reshape(rank, -1).astype(BF16)
    wuvt = w_ukv3[:, :, MLA_NOPE:].reshape(rank, -1).T.astype(BF16)
    q_scale = float((MLA_NOPE + MLA_ROPE) ** -0.5 * np.log2(np.e))
    lanes_of = lambda g, scale=1.0: jnp.broadcast_to((g.astype(F32) * scale)[:, None], (g.shape[0], tq))

    inv = 1.0 / (ROPE_THETA ** (jnp.arange(0, MLA_ROPE, 2, dtype=F32) / MLA_ROPE))
    ang = inv[:, None] * positions.astype(F32).reshape(1, T)

    x2 = x.reshape(T, D)
    hb, qt, k, vt = _mla_prep(
        x2, row(g_mix), w_q.T.astype(BF16), jnp.concatenate([w_ckv, w_kr], axis=1).T.astype(BF16), wuvt,
        w_ckv.astype(BF16), wuk,
        jnp.cos(ang), jnp.sin(ang), lanes_of(g_q_nope, q_scale), lanes_of(g_q_rope, q_scale), lanes_of(g_ckv),
        lanes_of(g_k_rope), row(g_ckv), row(g_k_nope), tm=tq)
    o_mla = _mla_attn(qt, k, vt, batch=B, seq=S, tq=tq, heads=ATTN_HEADS_PER_STEP)

    nmat = jnp.asarray(np.tile(_gla_decay_matrix(), (1, 2)), BF16)
    masks = jnp.asarray(_gla_level_masks())
    wg = jnp.pad(w_gla_gate, ((0, LANES - GLA_GATE_RANK), (0, 0))).astype(BF16)
    qg, kg, vg, la, sr = _gla_proj(hb, w_gq.astype(BF16), w_gk.astype(BF16), w_gv.astype(BF16),
                                   _pad_cols(w_ga, LANES).astype(BF16), w_gr.astype(BF16), wg,
                                   row(b_gla_gate), tm=min(GLA_PROJ_TILE, S))
    ts = min(GLA_STEP_TOKENS, S)
    o_gla = _gla_rec(qg, kg, vg, la, sr, row(g_gla_out), nmat, masks, batch=B, seq=S, ts=ts,
                     group=ts // GLA_CHUNK)

    seg = jnp.asarray(np.kron(np.eye(MEM_HEADS), np.ones((MEM_DQK, MEM_DQK))).astype(np.float32), BF16)
    km, vm = _mem_kv(mem.reshape(B * M, D), row(g_mem), w_mem_kv.astype(BF16),
                     jnp.tile(row(g_k_mem), (1, MEM_HEADS)), seg, batch=B, mem_len=M)
    x1 = _merge(x2, hb, o_mla, o_gla, km, vm, w_qm.astype(BF16), jnp.tile(row(g_q_mem), (1, MEM_HEADS)), seg,
                w_gate.astype(BF16), row(b_gate), w_o.astype(BF16), seq=S, mem_len=M, tm=min(MERGE_TILE, S))
    out = _ffn(x1, row(g_ffn), w_up.astype(BF16), w_down.astype(BF16), tm=min(FFN_TILE, S), ff_chunk=FFN_CHUNK)
    return out.reshape(B, S, D)


def kernel(x, mem, positions, g_mix, w_in, b_gate, g_ckv, w_ukv, g_q_nope, g_k_nope, g_q_rope, g_k_rope,
           w_gla_gate, b_gla_gate, g_gla_out, g_mem, w_mem_kv, g_q_mem, g_k_mem, w_o, g_ffn, w_up, w_down):
    for l in range(g_mix.shape[0]):
        x = _layer(x, mem, positions, g_mix[l], w_in[l], b_gate[l], g_ckv[l], w_ukv[l], g_q_nope[l],
                   g_k_nope[l], g_q_rope[l], g_k_rope[l], w_gla_gate[l], b_gla_gate[l], g_gla_out[l],
                   g_mem[l], w_mem_kv[l], g_q_mem[l], g_k_mem[l], w_o[l], g_ffn[l], w_up[l], w_down[l])
    return x
```

```python
import functools

import jax
import jax.numpy as jnp
import numpy as np
from jax import lax
from jax.experimental import pallas as pl
from jax.experimental.pallas import tpu as pltpu

F32 = jnp.float32
BF16 = jnp.bfloat16

EPS = 1e-6
ROPE_THETA = 10000.0
LANES = 128
BF16_SUBLANES = 16

MLA_HEADS, MLA_NOPE, MLA_ROPE, MLA_V = 8, 128, 64, 128
MLA_HEAD_PAD = 2 * LANES
MLA_VT_ROWS = MLA_V + BF16_SUBLANES
GLA_HEADS, GLA_DK, GLA_DV = 4, 128, 256
GLA_GATE_RANK, GLA_GATE_NORMALIZER, GLA_CHUNK = 16, 16.0, 64
MEM_HEADS, MEM_DQK, MEM_DV = 4, 64, 256
N_BRANCHES = 3

VMEM_LIMIT = 48 * 1024 * 1024

ATTN_TILE = 512
ATTN_HEADS_PER_STEP = 2
GLA_PROJ_TILE = 1024
GLA_STEP_TOKENS = 1024
MERGE_TILE = 512
FFN_TILE = 1024
FFN_CHUNK = 1024

NT_DIMS = (((1,), (1,)), ((), ()))
TN_DIMS = (((0,), (0,)), ((), ()))


def _params(*sem):
    return pltpu.CompilerParams(dimension_semantics=sem, vmem_limit_bytes=VMEM_LIMIT)


def _rms(t, g, n=None):
    n = t.shape[-1] if n is None else n
    ss = jnp.sum(t * t, axis=-1, keepdims=True) * (1.0 / n)
    return t * lax.rsqrt(ss + EPS) * g


def _dot(a, b):
    return jnp.dot(a, b, preferred_element_type=F32)


def _split(a_f32):
    hi = a_f32.astype(BF16)
    return hi, (a_f32 - hi.astype(F32)).astype(BF16)


def _dot_split(a_f32, b_bf16):
    hi, lo = _split(a_f32)
    return _dot(hi, b_bf16) + _dot(lo, b_bf16)


def _const_spec(shape):
    return pl.BlockSpec(shape, lambda *_: (0,) * len(shape))


def _slab_spec(a, steps):
    rows = a.shape[0] // steps
    assert rows * steps == a.shape[0] and rows % BF16_SUBLANES == 0
    return pl.BlockSpec((rows, a.shape[1]), lambda i: (i, 0))


def _rms_rows(t, g):
    ss = jnp.sum(t * t, axis=0, keepdims=True) * (1.0 / t.shape[0])
    return t * lax.rsqrt(ss + EPS) * g


def _rope_rows(r, cos, sin):
    half = r.shape[0] // 2
    t1, t2 = r[:half], r[half:]
    return t1 * cos - t2 * sin, t1 * sin + t2 * cos


def _mla_prep_kernel(x_ref, gmix_ref, wqt_ref, wlatt_ref, wuvt_ref, wckv_ref, wuk_ref,
                     cost_ref, sint_ref, gqnt_ref, gqrt_ref, gckvt_ref, gkrt_ref,
                     gckv_ref, gkn_ref, hb_ref, qt_ref, k_ref, vt_ref):
    h = _rms(x_ref[...], gmix_ref[...])
    hb = h.astype(BF16)
    hb_ref[...] = hb
    hbt = h.T.astype(BF16)
    tm = hbt.shape[1]
    half = MLA_ROPE // 2
    dq = MLA_NOPE + MLA_ROPE

    cos, sin = cost_ref[...], sint_ref[...]
    rank = gckv_ref.shape[-1]
    gqn, gqr = gqnt_ref[...], gqrt_ref[...]
    zeros = jnp.zeros((MLA_HEAD_PAD - dq, tm), BF16)

    def q_heads(heads):
        for hd in heads:
            t = _dot(wqt_ref[hd * dq:(hd + 1) * dq, :], hbt)
            qt_ref[hd, :MLA_NOPE, :] = _rms_rows(t[:MLA_NOPE], gqn).astype(BF16)
            r1, r2 = _rope_rows(_rms_rows(t[MLA_NOPE:], gqr), cos, sin)
            qt_ref[hd, MLA_NOPE:MLA_NOPE + half, :] = r1.astype(BF16)
            qt_ref[hd, MLA_NOPE + half:dq, :] = r2.astype(BF16)
            qt_ref[hd, dq:, :] = zeros

    ckv_raw = _dot(hb, wckv_ref[...])
    latt = _dot(wlatt_ref[...], hbt)
    q_heads(range(0, MLA_HEADS // 2))

    ckv = _rms(ckv_raw, gckv_ref[...]).astype(BF16)
    kn = _dot(ckv, wuk_ref[...])
    ckvt = _rms_rows(latt[:rank], gckvt_ref[...]).astype(BF16)
    vt = _dot(wuvt_ref[...], ckvt)
    q_heads(range(MLA_HEADS // 2, MLA_HEADS))

    kr1, kr2 = _rope_rows(_rms_rows(latt[rank:], gkrt_ref[...]), cos, sin)
    kpe = jnp.concatenate([kr1, kr2, jnp.zeros((LANES - MLA_ROPE, tm), F32)], axis=0).T.astype(BF16)
    gkn = gkn_ref[...]
    pad_row = lax.broadcasted_iota(jnp.int32, (MLA_VT_ROWS - MLA_V, tm), 0)
    ones_rows = jnp.where(pad_row == 0, 1.0, 0.0).astype(BF16)
    for hd in range(MLA_HEADS):
        c0 = hd * MLA_HEAD_PAD
        k_ref[:, c0:c0 + LANES] = _rms(kn[:, hd * LANES:(hd + 1) * LANES], gkn).astype(BF16)
        k_ref[:, c0 + LANES:c0 + 2 * LANES] = kpe
        vt_ref[hd, :MLA_V, :] = vt[hd * MLA_V:(hd + 1) * MLA_V].astype(BF16)
        vt_ref[hd, MLA_V:, :] = ones_rows


def _mla_prep(x2, g_mix, wqt, wlatt, wuvt, wckv, wuk, cos_tt, sin_tt, gqnt, gqrt, gckvt, gkrt,
              gckv, gkn, *, tm):
    T, D = x2.shape
    HP = MLA_HEADS * MLA_HEAD_PAD
    row = lambda w: pl.BlockSpec((tm, w), lambda i: (i, 0))
    col = lambda r: pl.BlockSpec((r, tm), lambda i: (0, i))
    consts = (g_mix, wqt, wlatt, wuvt, wckv, wuk)
    gains = (gqnt, gqrt, gckvt, gkrt, gckv, gkn)
    return pl.pallas_call(
        _mla_prep_kernel,
        grid=(T // tm,),
        in_specs=[row(D)] + [_const_spec(c.shape) for c in consts]
        + [col(cos_tt.shape[0]), col(sin_tt.shape[0])]
        + [_const_spec(g.shape) for g in gains],
        out_specs=[row(D), pl.BlockSpec((MLA_HEADS, None, MLA_HEAD_PAD, tm), lambda i: (0, i, 0, 0)), row(HP),
                   pl.BlockSpec((MLA_HEADS, None, MLA_VT_ROWS, tm), lambda i: (0, i, 0, 0))],
        out_shape=[jax.ShapeDtypeStruct((T, D), BF16),
                   jax.ShapeDtypeStruct((MLA_HEADS, T // tm, MLA_HEAD_PAD, tm), BF16),
                   jax.ShapeDtypeStruct((T, HP), BF16),
                   jax.ShapeDtypeStruct((MLA_HEADS, T // tm, MLA_VT_ROWS, tm), BF16)],
        compiler_params=_params("parallel"),
        name="mla_prep",
    )(x2, *consts, cos_tt, sin_tt, *gains)


ATTN_TILES_PER_STEP = 4


def _mla_attn_kernel(qt_ref, qt_next_ref, k_ref, vt_ref, o_ref, s_a, s_b, s_c, max_a, max_b, max_c,
                     *stat_refs, tq, heads, nq):
    quad = pl.program_id(2)
    hs = range(heads)
    buf_a, buf_b, buf_c = (s_a, max_a), (s_b, max_b), (s_c, max_c)
    stats_of = [(stat_refs[2 * r], stat_refs[2 * r + 1]) for r in range(ATTN_TILES_PER_STEP)]

    def step(prod=None, cons=None):
        half = tq // 2
        if prod is not None:
            jp, q_ref, q_tile, (sp_ref, mp_ref), diagonal = prod
            r0 = pl.multiple_of(jp * tq, tq)
            for h in hs:
                cols = slice(h * MLA_HEAD_PAD, (h + 1) * MLA_HEAD_PAD)
                q = q_ref[h] if q_tile is None else q_ref[h, q_tile]
                if diagonal:
                    sp_ref[h, :half, :] = _dot(k_ref[pl.ds(r0, half), cols], q)
                    sp_ref[h, half:, half:] = _dot(k_ref[pl.ds(r0 + half, half), cols], q[:, half:])
                else:
                    s = _dot(k_ref[pl.ds(r0, tq), cols], q)
                    sp_ref[h] = s
                    mp_ref[h] = jnp.max(s, axis=0, keepdims=True)
        if cons is not None:
            jc, (sc_ref, mc_ref), diagonal, (m_ref, acc_ref) = cons
            for h in hs:
                if diagonal:
                    causal = lambda s: jnp.where(lax.broadcasted_iota(jnp.int32, s.shape, 0)
                                                 <= lax.broadcasted_iota(jnp.int32, s.shape, 1), s, -1e30)
                    top = causal(sc_ref[h, :half, :])
                    low = causal(sc_ref[h, half:, half:])
                    m_top = jnp.max(top, axis=0, keepdims=True)
                    m_new = jnp.concatenate(
                        [m_top[:, :half], jnp.maximum(m_top[:, half:], jnp.max(low, axis=0, keepdims=True))], axis=1)
                    pv = _dot(vt_ref[h, jc, :, :half], jnp.exp2(top - m_new).astype(BF16))
                    pv_low = _dot(vt_ref[h, jc, :, half:], jnp.exp2(low - m_new[:, half:]).astype(BF16))
                    acc_ref[h] = jnp.concatenate([pv[:, :half], pv[:, half:] + pv_low], axis=1)
                else:
                    m = m_ref[h]
                    m_new = jnp.maximum(m, mc_ref[h])
                    pv = _dot(vt_ref[h, jc], jnp.exp2(sc_ref[h] - m_new).astype(BF16))
                    acc_ref[h] = jnp.exp2(m - m_new) * acc_ref[h] + pv
                m_ref[h] = m_new

    def write_out(stats, tile):
        for h in hs:
            acc = stats[1][h]
            o_ref[tile * tq:(tile + 1) * tq, h * MLA_V:(h + 1) * MLA_V] = (
                acc[:MLA_V] / acc[MLA_V:MLA_V + 1]).T.astype(o_ref.dtype)

    def two_blocks(t, q_tile, stats):
        j = 2 * t
        step(prod=(j + 1, qt_ref, q_tile, buf_b, False), cons=(j, buf_a, False, stats))
        step(prod=(j + 2, qt_ref, q_tile, buf_a, False), cons=(j + 1, buf_b, False, stats))

    def run_tile(i, r, trips, leftover, diag_buf, next_prod):
        stats = stats_of[r]
        step(prod=(0, qt_ref, r, buf_a, False), cons=(i, diag_buf, True, stats))
        if not (isinstance(trips, int) and trips == 0):
            def four_blocks(t, carry):
                two_blocks(2 * t, r, stats)
                two_blocks(2 * t + 1, r, stats)
                return carry

            lax.fori_loop(0, trips, four_blocks, 0)
        if leftover:
            two_blocks(2 * trips, r, stats)
        if r % 2 == 0:
            step(prod=(i - 1, qt_ref, r, buf_b, False), cons=(i - 2, buf_a, False, stats))
            step(prod=next_prod, cons=(i - 1, buf_b, False, stats))
        else:
            step(prod=next_prod, cons=(i - 1, buf_a, False, stats))
        write_out(stats, r)

    first = ATTN_TILES_PER_STEP * quad
    ahead = (jnp.minimum(first + ATTN_TILES_PER_STEP, nq - 1), qt_next_ref, None, buf_c, True)
    diag_of = lambda r: (first + r, qt_ref, r, buf_c, True)

    @pl.when(quad == 0)
    def _():
        step(prod=(0, qt_ref, 0, buf_c, True))
        step(prod=(1, qt_ref, 1, buf_b, True), cons=(0, buf_c, True, stats_of[0]))
        write_out(stats_of[0], 0)
        run_tile(1, 1, 0, 0, buf_b, diag_of(2))
        run_tile(2, 2, 0, 0, buf_c, diag_of(3))
        run_tile(3, 3, 0, 1, buf_c, ahead)

    @pl.when(quad > 0)
    def _():
        run_tile(first, 0, quad - 1, 1, buf_c, diag_of(1))
        run_tile(first + 1, 1, quad, 0, buf_c, diag_of(2))
        run_tile(first + 2, 2, quad, 0, buf_c, diag_of(3))
        run_tile(first + 3, 3, quad, 1, buf_c, ahead)


def _mla_attn(qt, k, vt, *, batch, seq, tq, heads):
    T = k.shape[0]
    nq = seq // tq
    per = ATTN_TILES_PER_STEP
    assert nq % per == 0, "q tiles are processed in groups of ATTN_TILES_PER_STEP"
    steps = nq // per
    score = pltpu.VMEM((heads, tq, tq), F32)
    stat = pltpu.VMEM((heads, 1, tq), F32)
    acc = pltpu.VMEM((heads, MLA_VT_ROWS, tq), F32)
    return pl.pallas_call(
        functools.partial(_mla_attn_kernel, tq=tq, heads=heads, nq=nq),
        grid=(batch, MLA_HEADS // heads, steps),
        in_specs=[pl.BlockSpec((heads, per, MLA_HEAD_PAD, tq), lambda b, h, s: (h, b * steps + s, 0, 0)),
                  pl.BlockSpec((heads, None, MLA_HEAD_PAD, tq),
                               lambda b, h, s: (h, b * nq + jnp.minimum(per * s + per, nq - 1), 0, 0)),
                  pl.BlockSpec((seq, heads * MLA_HEAD_PAD), lambda b, h, s: (b, h)),
                  pl.BlockSpec((heads, nq, MLA_VT_ROWS, tq), lambda b, h, s: (h, b, 0, 0))],
        out_specs=pl.BlockSpec((per * tq, heads * MLA_V), lambda b, h, s: (b * steps + s, h)),
        out_shape=jax.ShapeDtypeStruct((T, MLA_HEADS * MLA_V), BF16),
        scratch_shapes=[score, score, score, stat, stat, stat] + [stat, acc] * per,
        compiler_params=_params("parallel", "parallel", "arbitrary"),
        name="mla_attn",
    )(qt, qt, k, vt)


def _gla_proj_kernel(hb_ref, wq_ref, wk_ref, wv_ref, wa_ref, wr_ref, wg_ref, bg_ref,
                     q_ref, k_ref, v_ref, la_ref, sr_ref):
    hb = hb_ref[...]
    a = _dot(hb, wa_ref[...])
    r = _dot(hb, wr_ref[...])
    z = _dot(a.astype(BF16), wg_ref[...]) + bg_ref[...]
    log_sig = jnp.minimum(z, 0.0) - jnp.log1p(jnp.exp(-jnp.abs(z)))
    la_ref[...] = log_sig * (1.0 / GLA_GATE_NORMALIZER)
    sr_ref[...] = (r / (1.0 + jnp.exp(-r))).astype(BF16)
    v_ref[...] = _dot(hb, wv_ref[...]).astype(BF16)
    q_ref[...] = (_dot(hb, wq_ref[...]) * float(GLA_DK ** -0.5)).astype(BF16)
    k_ref[...] = _dot(hb, wk_ref[...]).astype(BF16)


def _gla_proj(hb, wq, wk, wv, wa, wr, wg, bg, *, tm):
    T, D = hb.shape
    nk, nv = wq.shape[1], wv.shape[1]
    row = lambda w: pl.BlockSpec((tm, w), lambda i: (i, 0))
    return pl.pallas_call(
        _gla_proj_kernel,
        grid=(T // tm,),
        in_specs=[row(D)] + [_const_spec(w.shape) for w in (wq, wk, wv, wa, wr, wg, bg)],
        out_specs=[row(nk), row(nk), row(nv), row(nk), row(nv)],
        out_shape=[jax.ShapeDtypeStruct((T, nk), BF16), jax.ShapeDtypeStruct((T, nk), BF16),
                   jax.ShapeDtypeStruct((T, nv), BF16), jax.ShapeDtypeStruct((T, nk), F32),
                   jax.ShapeDtypeStruct((T, nv), BF16)],
        compiler_params=_params("parallel"),
        name="gla_proj",
    )(hb, wq, wk, wv, wa, wr, wg, bg)


GLA_LEVELS = tuple(GLA_CHUNK >> (i + 1) for i in range(GLA_CHUNK.bit_length() - 2))


def _gla_decay_matrix():
    r = np.arange(GLA_CHUNK)
    groups = [r[None, :] <= r[:, None], r[None, :] > r[:, None]]
    for s in GLA_LEVELS:
        mid = (r // (2 * s)) * (2 * s) + s
        upper = (r & s) != 0
        up = (r[None, :] > mid[:, None]) & (r[None, :] <= r[:, None])
        lo = (r[None, :] > r[:, None]) & (r[None, :] <= mid[:, None])
        groups.append(np.where(upper[:, None], up, lo))
    return np.concatenate(groups, axis=0).astype(np.float32)


def _gla_level_masks():
    r = np.arange(GLA_CHUNK)
    x = r[:, None] ^ r[None, :]
    lower = r[None, :] < r[:, None]
    return np.stack([(lower & (x >= s) & (x < 2 * s)) for s in GLA_LEVELS]).astype(np.float32)


def _gla_group(qs, ks, vs, las, st, nmat, masks_ref):
    C, n = GLA_CHUNK, len(qs)
    la2 = jnp.concatenate([jnp.concatenate(_split(la), axis=0) for la in las], axis=1)
    e_all = jnp.exp(_dot(nmat, la2))
    e = [e_all[:, u * GLA_DK:(u + 1) * GLA_DK] for u in range(n)]

    row = lax.broadcasted_iota(jnp.int32, qs[0].shape, 0)
    a = [jnp.zeros((C, C), F32) for _ in range(n)]
    for lvl, s in enumerate(GLA_LEVELS):
        upper = (row & s) != 0
        mask = masks_ref[lvl]
        for u in range(n):
            t = (jnp.where(upper, qs[u], ks[u]) * e[u][(2 + lvl) * C:(3 + lvl) * C]).astype(BF16)
            a[u] = a[u] + lax.dot_general(t, t, NT_DIMS, preferred_element_type=F32) * mask
    r_cc = lax.broadcasted_iota(jnp.int32, (C, C), 0)
    c_cc = lax.broadcasted_iota(jnp.int32, (C, C), 1)
    eye = r_cc == c_cc
    below = (r_cc - 1 == c_cc) & ((r_cc & 1) == 1)
    o_intra, upd = [], []
    for u in range(n):
        pair = jnp.sum(qs[u] * jnp.exp(las[u]) * pltpu.roll(ks[u], 1, 0), axis=-1, keepdims=True)
        a_u = jnp.where(below, pair, a[u])
        a_u = jnp.where(eye, jnp.sum(qs[u] * ks[u], axis=-1, keepdims=True), a_u)
        o_intra.append(_dot(a_u.astype(BF16), vs[u]))
        k_dec = (ks[u] * e[u][C:2 * C]).astype(BF16)
        upd.append(lax.dot_general(k_dec, vs[u], TN_DIMS, preferred_element_type=F32))
    outs = []
    for u in range(n):
        eb = e[u][:C]
        outs.append(o_intra[u] + _dot((qs[u] * eb).astype(BF16), st.astype(BF16)))
        decay = jnp.broadcast_to(eb[C - 1:C, :], (GLA_DK, GLA_DK)).T
        st = st * jnp.concatenate([decay] * (GLA_DV // GLA_DK), axis=1) + upd[u]
    return outs, st


def _gla_rec_kernel(q_ref, k_ref, v_ref, la_ref, sr_ref, g_ref, nmat_ref, masks_ref, o_ref, st_ref, *,
                    ts, group):
    @pl.when(pl.program_id(2) == 0)
    def _():
        st_ref[...] = jnp.zeros_like(st_ref)

    g = g_ref[...]
    nmat = nmat_ref[...]
    span = group * GLA_CHUNK

    def body(c, _):
        base = pl.multiple_of(c * span, span)
        rows = [pl.ds(base + u * GLA_CHUNK, GLA_CHUNK) for u in range(group)]
        outs, st = _gla_group([q_ref[r, :].astype(F32) for r in rows], [k_ref[r, :].astype(F32) for r in rows],
                              [v_ref[r, :] for r in rows], [la_ref[r, :] for r in rows], st_ref[...],
                              nmat, masks_ref)
        st_ref[...] = st
        for r, o in zip(rows, outs):
            o_ref[r, :] = (_rms(o, g) * sr_ref[r, :].astype(F32)).astype(o_ref.dtype)
        return 0

    lax.fori_loop(0, ts // span, body, 0)


def _gla_rec(q, k, v, la, sr, g_out, nmat, masks, *, batch, seq, ts, group):
    T = q.shape[0]
    ns = seq // ts
    blk = lambda w: pl.BlockSpec((ts, w), lambda b, h, i: (b * ns + i, h))
    return pl.pallas_call(
        functools.partial(_gla_rec_kernel, ts=ts, group=group),
        grid=(batch, GLA_HEADS, ns),
        in_specs=[blk(GLA_DK), blk(GLA_DK), blk(GLA_DV), blk(GLA_DK), blk(GLA_DV),
                  _const_spec(g_out.shape), _const_spec(nmat.shape), _const_spec(masks.shape)],
        out_specs=blk(GLA_DV),
        out_shape=jax.ShapeDtypeStruct((T, GLA_HEADS * GLA_DV), BF16),
        scratch_shapes=[pltpu.VMEM((GLA_DK, GLA_DV), F32)],
        compiler_params=_params("parallel", "parallel", "arbitrary"),
        name="gla_rec",
    )(q, k, v, la, sr, g_out, nmat, masks)


def _mem_kv_kernel(mem_ref, g_ref, w_ref, gk_ref, seg_ref, k_ref, v_ref):
    kv = _dot(_rms(mem_ref[...], g_ref[...]).astype(BF16), w_ref[...])
    nk = MEM_HEADS * MEM_DQK
    k = kv[:, :nk]
    ss = _dot_split(k * k, seg_ref[...]) * (1.0 / MEM_DQK)
    k_ref[...] = (k * lax.rsqrt(ss + EPS) * gk_ref[...]).astype(BF16)
    v_ref[...] = kv[:, nk:].astype(BF16)


def _mem_kv(mem2, g_mem, w, gk, seg, *, batch, mem_len):
    D = mem2.shape[1]
    nk, nv = MEM_HEADS * MEM_DQK, MEM_HEADS * MEM_DV
    row = lambda w_: pl.BlockSpec((mem_len, w_), lambda b: (b, 0))
    return pl.pallas_call(
        _mem_kv_kernel,
        grid=(batch,),
        in_specs=[row(D), _const_spec(g_mem.shape), _const_spec(w.shape), _const_spec(gk.shape),
                  _const_spec(seg.shape)],
        out_specs=[row(nk), row(nv)],
        out_shape=[jax.ShapeDtypeStruct((batch * mem_len, nk), BF16),
                   jax.ShapeDtypeStruct((batch * mem_len, nv), BF16)],
        compiler_params=_params("parallel"),
        name="mem_kv",
    )(mem2, g_mem, w, gk, seg)


def _merge_kernel(x_ref, hb_ref, omla_ref, ogla_ref, km_ref, vm_ref, wqm_ref, gqm_ref, seg_ref,
                  wgate_ref, bgate_ref, wo_ref, wup_ref, wdn_ref, o_ref, wupb_ref, wdnb_ref):
    wupb_ref[...] = wup_ref[...].astype(BF16)
    wdnb_ref[...] = wdn_ref[...].astype(BF16)
    hb = hb_ref[...]
    D = x_ref.shape[1]
    km = km_ref[...]

    def gate(j):
        z = _dot(hb, wgate_ref[:, j * D:(j + 1) * D]) + bgate_ref[:, j * D:(j + 1) * D]
        return 1.0 / (1.0 + jnp.exp(-z))

    qm = _dot(hb, wqm_ref[...])
    y = gate(0) * omla_ref[...].astype(F32)
    ss = _dot_split(qm * qm, seg_ref[...]) * (1.0 / MEM_DQK)
    qn = qm * lax.rsqrt(ss + EPS) * (gqm_ref[...] * float(MEM_DQK ** -0.5))
    head_of_lane = lax.broadcasted_iota(jnp.int32, qn.shape, 1) // MEM_DQK
    y = y + gate(1) * ogla_ref[...].astype(F32)
    probs = []
    for h in range(MEM_HEADS):
        qh = jnp.where(head_of_lane == h, qn, 0.0).astype(BF16)
        s = lax.dot_general(qh, km, NT_DIMS, preferred_element_type=F32)
        probs.append(jnp.exp(s - jnp.max(s, axis=-1, keepdims=True)))
    g_mem = gate(2)
    parts = []
    for h, p in enumerate(probs):
        o_h = _dot(p.astype(BF16), vm_ref[:, h * MEM_DV:(h + 1) * MEM_DV])
        parts.append(o_h / jnp.sum(p, axis=-1, keepdims=True))
    y = y + g_mem * jnp.concatenate(parts, axis=-1)
    o_ref[...] = x_ref[...] + _dot(y.astype(BF16), wo_ref[...])


def _merge(x2, hb, o_mla, o_gla, km, vm, wqm, gqm, seg, wgate, bgate, wo, w_up, w_down, *, seq, mem_len, tm):
    T, D = x2.shape
    per_batch = seq // tm
    row = lambda w: pl.BlockSpec((tm, w), lambda i: (i, 0))
    mem_blk = lambda w: pl.BlockSpec((mem_len, w), lambda i: (i // per_batch, 0))
    return pl.pallas_call(
        _merge_kernel,
        grid=(T // tm,),
        in_specs=[row(D), row(D), row(D), row(D), mem_blk(km.shape[1]), mem_blk(vm.shape[1])]
        + [_const_spec(w.shape) for w in (wqm, gqm, seg, wgate, bgate, wo)]
        + [_slab_spec(w_up, T // tm), _slab_spec(w_down, T // tm)],
        out_specs=[row(D), _slab_spec(w_up, T // tm), _slab_spec(w_down, T // tm)],
        out_shape=[jax.ShapeDtypeStruct((T, D), F32), jax.ShapeDtypeStruct(w_up.shape, BF16),
                   jax.ShapeDtypeStruct(w_down.shape, BF16)],
        compiler_params=_params("parallel"),
        name="merge",
    )(x2, hb, o_mla, o_gla, km, vm, wqm, gqm, seg, wgate, bgate, wo, w_up, w_down)


def _ffn_kernel(x_ref, g_ref, wup_ref, wdn_ref, o_ref, *, ff_chunk):
    x = x_ref[...]
    hb = _rms(x, g_ref[...]).astype(BF16)
    acc = x
    for c in range(wup_ref.shape[1] // ff_chunk):
        u = jnp.maximum(_dot(hb, wup_ref[:, c * ff_chunk:(c + 1) * ff_chunk]), 0.0)
        acc = acc + _dot((u * u).astype(BF16), wdn_ref[c * ff_chunk:(c + 1) * ff_chunk, :])
    o_ref[...] = acc


def _ffn(x1, g_ffn, wup, wdn, *, tm, ff_chunk):
    T, D = x1.shape
    row = pl.BlockSpec((tm, D), lambda i: (i, 0))
    return pl.pallas_call(
        functools.partial(_ffn_kernel, ff_chunk=ff_chunk),
        grid=(T // tm,),
        in_specs=[row, _const_spec(g_ffn.shape), _const_spec(wup.shape), _const_spec(wdn.shape)],
        out_specs=row,
        out_shape=jax.ShapeDtypeStruct((T, D), F32),
        compiler_params=_params("parallel"),
        name="ffn",
    )(x1, g_ffn, wup, wdn)


def _pad_cols(w, n):
    return jnp.pad(w, ((0, 0), (0, n - w.shape[-1])))


def _layer(x, mem, positions, g_mix, w_in, b_gate, g_ckv, w_ukv, g_q_nope, g_k_nope, g_q_rope, g_k_rope,
           w_gla_gate, b_gla_gate, g_gla_out, g_mem, w_mem_kv, g_q_mem, g_k_mem, w_o, g_ffn, w_up, w_down):
    B, S, D = x.shape
    M = mem.shape[1]
    T = B * S
    rank = g_ckv.shape[0]
    row = lambda g: g.reshape(1, -1).astype(F32)

    sizes = (MLA_HEADS * (MLA_NOPE + MLA_ROPE), rank, MLA_ROPE, GLA_HEADS * GLA_DK, GLA_HEADS * GLA_DK,
             GLA_HEADS * GLA_DV, GLA_GATE_RANK, GLA_HEADS * GLA_DV, MEM_HEADS * MEM_DQK, N_BRANCHES * D)
    offs = np.concatenate([[0], np.cumsum(sizes)])
    (w_q, w_ckv, w_kr, w_gq, w_gk, w_gv, w_ga, w_gr, w_qm, w_gate) = [
        w_in[:, offs[i]:offs[i + 1]] for i in range(len(sizes))]

    tq = min(ATTN_TILE, S)
    w_ukv3 = w_ukv.reshape(rank, MLA_HEADS, MLA_NOPE + MLA_V)
    wuk = w_ukv3[:, :, :MLA_NOPE].reshape(rank, -1).astype(BF16)
    wuvt = w_ukv3[:, :, MLA_NOPE:].reshape(rank, -1).T.astype(BF16)
    q_scale = float((MLA_NOPE + MLA_ROPE) ** -0.5 * np.log2(np.e))
    lanes_of = lambda g, scale=1.0: jnp.broadcast_to((g.astype(F32) * scale)[:, None], (g.shape[0], tq))

    inv = 1.0 / (ROPE_THETA ** (jnp.arange(0, MLA_ROPE, 2, dtype=F32) / MLA_ROPE))
    ang = inv[:, None] * positions.astype(F32).reshape(1, T)

    x2 = x.reshape(T, D)
    hb, qt, k, vt = _mla_prep(
        x2, row(g_mix), w_q.T.astype(BF16), jnp.concatenate([w_ckv, w_kr], axis=1).T.astype(BF16), wuvt,
        w_ckv.astype(BF16), wuk,
        jnp.cos(ang), jnp.sin(ang), lanes_of(g_q_nope, q_scale), lanes_of(g_q_rope, q_scale), lanes_of(g_ckv),
        lanes_of(g_k_rope), row(g_ckv), row(g_k_nope), tm=tq)
    o_mla = _mla_attn(qt, k, vt, batch=B, seq=S, tq=tq, heads=ATTN_HEADS_PER_STEP)

    nmat = jnp.asarray(np.tile(_gla_decay_matrix(), (1, 2)), BF16)
    masks = jnp.asarray(_gla_level_masks())
    wg = jnp.pad(w_gla_gate, ((0, LANES - GLA_GATE_RANK), (0, 0))).astype(BF16)
    qg, kg, vg, la, sr = _gla_proj(hb, w_gq.astype(BF16), w_gk.astype(BF16), w_gv.astype(BF16),
                                   _pad_cols(w_ga, LANES).astype(BF16), w_gr.astype(BF16), wg,
                                   row(b_gla_gate), tm=min(GLA_PROJ_TILE, S))
    ts = min(GLA_STEP_TOKENS, S)
    o_gla = _gla_rec(qg, kg, vg, la, sr, row(g_gla_out), nmat, masks, batch=B, seq=S, ts=ts,
                     group=ts // GLA_CHUNK)

    seg = jnp.asarray(np.kron(np.eye(MEM_HEADS), np.ones((MEM_DQK, MEM_DQK))).astype(np.float32), BF16)
    km, vm = _mem_kv(mem.reshape(B * M, D), row(g_mem), w_mem_kv.astype(BF16),
                     jnp.tile(row(g_k_mem), (1, MEM_HEADS)), seg, batch=B, mem_len=M)
    x1, w_up_b, w_down_b = _merge(x2, hb, o_mla, o_gla, km, vm, w_qm.astype(BF16),
                                  jnp.tile(row(g_q_mem), (1, MEM_HEADS)), seg, w_gate.astype(BF16), row(b_gate),
                                  w_o.astype(BF16), w_up, w_down, seq=S, mem_len=M,
                                  tm=min(MERGE_TILE, S))
    out = _ffn(x1, row(g_ffn), w_up_b, w_down_b, tm=min(FFN_TILE, S), ff_chunk=FFN_CHUNK)
    return out.reshape(B, S, D)


def kernel(x, mem, positions, g_mix, w_in, b_gate, g_ckv, w_ukv, g_q_nope, g_k_nope, g_q_rope, g_k_rope,
           w_gla_gate, b_gla_gate, g_gla_out, g_mem, w_mem_kv, g_q_mem, g_k_mem, w_o, g_ffn, w_up, w_down):
    for l in range(g_mix.shape[0]):
        x = _layer(x, mem, positions, g_mix[l], w_in[l], b_gate[l], g_ckv[l], w_ukv[l], g_q_nope[l],
                   g_k_nope[l], g_q_rope[l], g_k_rope[l], w_gla_gate[l], b_gla_gate[l], g_gla_out[l],
                   g_mem[l], w_mem_kv[l], g_q_mem[l], g_k_mem[l], w_o[l], g_ffn[l], w_up[l], w_down[l])
    return x
```

```python
import functools

import jax
import jax.numpy as jnp
import numpy as np
from jax import lax
from jax.experimental import pallas as pl
from jax.experimental.pallas import tpu as pltpu

F32 = jnp.float32
BF16 = jnp.bfloat16

EPS = 1e-6
ROPE_THETA = 10000.0
LANES = 128
BF16_SUBLANES = 16

MLA_HEADS, MLA_NOPE, MLA_ROPE, MLA_V = 8, 128, 64, 128
MLA_HEAD_PAD = 2 * LANES
MLA_VT_ROWS = MLA_V + BF16_SUBLANES
GLA_HEADS, GLA_DK, GLA_DV = 4, 128, 256
GLA_GATE_RANK, GLA_GATE_NORMALIZER, GLA_CHUNK = 16, 16.0, 64
MEM_HEADS, MEM_DQK, MEM_DV = 4, 64, 256
N_BRANCHES = 3

VMEM_LIMIT = 48 * 1024 * 1024

ATTN_TILE = 512
ATTN_HEADS_PER_STEP = 2
GLA_PROJ_TILE = 1024
GLA_STEP_TOKENS = 1024
MERGE_TILE = 512
FFN_TILE = 1024
FFN_CHUNK = 1024

NT_DIMS = (((1,), (1,)), ((), ()))
TN_DIMS = (((0,), (0,)), ((), ()))


def _params(*sem):
    return pltpu.CompilerParams(dimension_semantics=sem, vmem_limit_bytes=VMEM_LIMIT)


def _rms(t, g, n=None):
    n = t.shape[-1] if n is None else n
    ss = jnp.sum(t * t, axis=-1, keepdims=True) * (1.0 / n)
    return t * lax.rsqrt(ss + EPS) * g


def _dot(a, b):
    return jnp.dot(a, b, preferred_element_type=F32)


def _split(a_f32):
    hi = a_f32.astype(BF16)
    return hi, (a_f32 - hi.astype(F32)).astype(BF16)


def _dot_split(a_f32, b_bf16):
    hi, lo = _split(a_f32)
    return _dot(hi, b_bf16) + _dot(lo, b_bf16)


def _const_spec(shape):
    return pl.BlockSpec(shape, lambda *_: (0,) * len(shape))


def _slab_spec(a, steps):
    rows = a.shape[0] // steps
    assert rows * steps == a.shape[0] and rows % BF16_SUBLANES == 0
    return pl.BlockSpec((rows, a.shape[1]), lambda i: (i, 0))


def _rms_rows(t, g):
    ss = jnp.sum(t * t, axis=0, keepdims=True) * (1.0 / t.shape[0])
    return t * lax.rsqrt(ss + EPS) * g


def _rope_rows(r, cos, sin):
    half = r.shape[0] // 2
    t1, t2 = r[:half], r[half:]
    return t1 * cos - t2 * sin, t1 * sin + t2 * cos


def _mla_prep_kernel(x_ref, gmix_ref, wqt_ref, wlatt_ref, wuvt_ref, wckv_ref, wuk_ref,
                     cost_ref, sint_ref, gqnt_ref, gqrt_ref, gckvt_ref, gkrt_ref,
                     gckv_ref, gkn_ref, hb_ref, qt_ref, k_ref, vt_ref, ga_ref):
    h = _rms(x_ref[...], gmix_ref[...])
    hb = h.astype(BF16)
    hb_ref[...] = hb
    hbt = h.T.astype(BF16)
    tm = hbt.shape[1]
    half = MLA_ROPE // 2
    dq = MLA_NOPE + MLA_ROPE

    cos, sin = cost_ref[...], sint_ref[...]
    rank = gckv_ref.shape[-1]
    gqn, gqr = gqnt_ref[...], gqrt_ref[...]
    zeros = jnp.zeros((MLA_HEAD_PAD - dq, tm), BF16)

    def q_heads(heads):
        for hd in heads:
            t = _dot(wqt_ref[hd * dq:(hd + 1) * dq, :], hbt)
            qt_ref[hd, :MLA_NOPE, :] = _rms_rows(t[:MLA_NOPE], gqn).astype(BF16)
            r1, r2 = _rope_rows(_rms_rows(t[MLA_NOPE:], gqr), cos, sin)
            qt_ref[hd, MLA_NOPE:MLA_NOPE + half, :] = r1.astype(BF16)
            qt_ref[hd, MLA_NOPE + half:dq, :] = r2.astype(BF16)
            qt_ref[hd, dq:, :] = zeros

    ckv_raw = _dot(hb, wckv_ref[...])
    latt = _dot(wlatt_ref[...], hbt)
    q_heads(range(0, MLA_HEADS // 2))
    ga_t = latt[rank + MLA_ROPE:]
    ga_ref[...] = jnp.concatenate([ga_t, jnp.zeros((LANES - ga_t.shape[0], tm), F32)], axis=0).T.astype(BF16)

    ckv = _rms(ckv_raw, gckv_ref[...]).astype(BF16)
    kn = _dot(ckv, wuk_ref[...])
    ckvt = _rms_rows(latt[:rank], gckvt_ref[...]).astype(BF16)
    vt = _dot(wuvt_ref[...], ckvt)
    q_heads(range(MLA_HEADS // 2, MLA_HEADS))

    kr1, kr2 = _rope_rows(_rms_rows(latt[rank:rank + MLA_ROPE], gkrt_ref[...]), cos, sin)
    kpe = jnp.concatenate([kr1, kr2, jnp.zeros((LANES - MLA_ROPE, tm), F32)], axis=0).T.astype(BF16)
    gkn = gkn_ref[...]
    pad_row = lax.broadcasted_iota(jnp.int32, (MLA_VT_ROWS - MLA_V, tm), 0)
    ones_rows = jnp.where(pad_row == 0, 1.0, 0.0).astype(BF16)
    for hd in range(MLA_HEADS):
        c0 = hd * MLA_HEAD_PAD
        k_ref[:, c0:c0 + LANES] = _rms(kn[:, hd * LANES:(hd + 1) * LANES], gkn).astype(BF16)
        k_ref[:, c0 + LANES:c0 + 2 * LANES] = kpe
        vt_ref[hd, :MLA_V, :] = vt[hd * MLA_V:(hd + 1) * MLA_V].astype(BF16)
        vt_ref[hd, MLA_V:, :] = ones_rows


def _mla_prep(x2, g_mix, wqt, wlatt, wuvt, wckv, wuk, cos_tt, sin_tt, gqnt, gqrt, gckvt, gkrt,
              gckv, gkn, *, tm):
    T, D = x2.shape
    HP = MLA_HEADS * MLA_HEAD_PAD
    row = lambda w: pl.BlockSpec((tm, w), lambda i: (i, 0))
    col = lambda r: pl.BlockSpec((r, tm), lambda i: (0, i))
    consts = (g_mix, wqt, wlatt, wuvt, wckv, wuk)
    gains = (gqnt, gqrt, gckvt, gkrt, gckv, gkn)
    return pl.pallas_call(
        _mla_prep_kernel,
        grid=(T // tm,),
        in_specs=[row(D)] + [_const_spec(c.shape) for c in consts]
        + [col(cos_tt.shape[0]), col(sin_tt.shape[0])]
        + [_const_spec(g.shape) for g in gains],
        out_specs=[row(D), pl.BlockSpec((MLA_HEADS, None, MLA_HEAD_PAD, tm), lambda i: (0, i, 0, 0)), row(HP),
                   pl.BlockSpec((MLA_HEADS, None, MLA_VT_ROWS, tm), lambda i: (0, i, 0, 0)), row(LANES)],
        out_shape=[jax.ShapeDtypeStruct((T, D), BF16),
                   jax.ShapeDtypeStruct((MLA_HEADS, T // tm, MLA_HEAD_PAD, tm), BF16),
                   jax.ShapeDtypeStruct((T, HP), BF16),
                   jax.ShapeDtypeStruct((MLA_HEADS, T // tm, MLA_VT_ROWS, tm), BF16),
                   jax.ShapeDtypeStruct((T, LANES), BF16)],
        compiler_params=_params("parallel"),
        name="mla_prep",
    )(x2, *consts, cos_tt, sin_tt, *gains)


ATTN_TILES_PER_STEP = 4


def _mla_attn_kernel(qt_ref, qt_next_ref, k_ref, vt_ref, o_ref, s_a, s_b, s_c, max_a, max_b, max_c,
                     *stat_refs, tq, heads, nq):
    quad = pl.program_id(2)
    hs = range(heads)
    buf_a, buf_b, buf_c = (s_a, max_a), (s_b, max_b), (s_c, max_c)
    stats_of = [(stat_refs[2 * r], stat_refs[2 * r + 1]) for r in range(ATTN_TILES_PER_STEP)]

    def step(prod=None, cons=None):
        half = tq // 2
        if prod is not None:
            jp, q_ref, q_tile, (sp_ref, mp_ref), diagonal = prod
            r0 = pl.multiple_of(jp * tq, tq)
            for h in hs:
                cols = slice(h * MLA_HEAD_PAD, (h + 1) * MLA_HEAD_PAD)
                q = q_ref[h] if q_tile is None else q_ref[h, q_tile]
                if diagonal:
                    sp_ref[h, :half, :] = _dot(k_ref[pl.ds(r0, half), cols], q)
                    sp_ref[h, half:, half:] = _dot(k_ref[pl.ds(r0 + half, half), cols], q[:, half:])
                else:
                    s = _dot(k_ref[pl.ds(r0, tq), cols], q)
                    sp_ref[h] = s
                    mp_ref[h] = jnp.max(s, axis=0, keepdims=True)
        if cons is not None:
            jc, (sc_ref, mc_ref), diagonal, (m_ref, acc_ref) = cons
            for h in hs:
                if diagonal:
                    causal = lambda s: jnp.where(lax.broadcasted_iota(jnp.int32, s.shape, 0)
                                                 <= lax.broadcasted_iota(jnp.int32, s.shape, 1), s, -1e30)
                    top = causal(sc_ref[h, :half, :])
                    low = causal(sc_ref[h, half:, half:])
                    m_top = jnp.max(top, axis=0, keepdims=True)
                    m_new = jnp.concatenate(
                        [m_top[:, :half], jnp.maximum(m_top[:, half:], jnp.max(low, axis=0, keepdims=True))], axis=1)
                    pv = _dot(vt_ref[h, jc, :, :half], jnp.exp2(top - m_new).astype(BF16))
                    pv_low = _dot(vt_ref[h, jc, :, half:], jnp.exp2(low - m_new[:, half:]).astype(BF16))
                    acc_ref[h] = jnp.concatenate([pv[:, :half], pv[:, half:] + pv_low], axis=1)
                else:
                    m = m_ref[h]
                    m_new = jnp.maximum(m, mc_ref[h])
                    pv = _dot(vt_ref[h, jc], jnp.exp2(sc_ref[h] - m_new).astype(BF16))
                    acc_ref[h] = jnp.exp2(m - m_new) * acc_ref[h] + pv
                m_ref[h] = m_new

    def write_out(stats, tile):
        for h in hs:
            acc = stats[1][h]
            o_ref[tile * tq:(tile + 1) * tq, h * MLA_V:(h + 1) * MLA_V] = (
                acc[:MLA_V] / acc[MLA_V:MLA_V + 1]).T.astype(o_ref.dtype)

    def two_blocks(t, q_tile, stats):
        j = 2 * t
        step(prod=(j + 1, qt_ref, q_tile, buf_b, False), cons=(j, buf_a, False, stats))
        step(prod=(j + 2, qt_ref, q_tile, buf_a, False), cons=(j + 1, buf_b, False, stats))

    def run_tile(i, r, trips, leftover, diag_buf, next_prod):
        stats = stats_of[r]
        step(prod=(0, qt_ref, r, buf_a, False), cons=(i, diag_buf, True, stats))
        if not (isinstance(trips, int) and trips == 0):
            def four_blocks(t, carry):
                two_blocks(2 * t, r, stats)
                two_blocks(2 * t + 1, r, stats)
                return carry

            lax.fori_loop(0, trips, four_blocks, 0)
        if leftover:
            two_blocks(2 * trips, r, stats)
        if r % 2 == 0:
            step(prod=(i - 1, qt_ref, r, buf_b, False), cons=(i - 2, buf_a, False, stats))
            step(prod=next_prod, cons=(i - 1, buf_b, False, stats))
        else:
            step(prod=next_prod, cons=(i - 1, buf_a, False, stats))
        write_out(stats, r)

    first = ATTN_TILES_PER_STEP * quad
    ahead = (jnp.minimum(first + ATTN_TILES_PER_STEP, nq - 1), qt_next_ref, None, buf_c, True)
    diag_of = lambda r: (first + r, qt_ref, r, buf_c, True)

    @pl.when(quad == 0)
    def _():
        step(prod=(0, qt_ref, 0, buf_c, True))
        step(prod=(1, qt_ref, 1, buf_b, True), cons=(0, buf_c, True, stats_of[0]))
        write_out(stats_of[0], 0)
        run_tile(1, 1, 0, 0, buf_b, diag_of(2))
        run_tile(2, 2, 0, 0, buf_c, diag_of(3))
        run_tile(3, 3, 0, 1, buf_c, ahead)

    @pl.when(quad > 0)
    def _():
        run_tile(first, 0, quad - 1, 1, buf_c, diag_of(1))
        run_tile(first + 1, 1, quad, 0, buf_c, diag_of(2))
        run_tile(first + 2, 2, quad, 0, buf_c, diag_of(3))
        run_tile(first + 3, 3, quad, 1, buf_c, ahead)


def _mla_attn(qt, k, vt, *, batch, seq, tq, heads):
    T = k.shape[0]
    nq = seq // tq
    per = ATTN_TILES_PER_STEP
    assert nq % per == 0, "q tiles are processed in groups of ATTN_TILES_PER_STEP"
    steps = nq // per
    score = pltpu.VMEM((heads, tq, tq), F32)
    stat = pltpu.VMEM((heads, 1, tq), F32)
    acc = pltpu.VMEM((heads, MLA_VT_ROWS, tq), F32)
    return pl.pallas_call(
        functools.partial(_mla_attn_kernel, tq=tq, heads=heads, nq=nq),
        grid=(batch, MLA_HEADS // heads, steps),
        in_specs=[pl.BlockSpec((heads, per, MLA_HEAD_PAD, tq), lambda b, h, s: (h, b * steps + s, 0, 0)),
                  pl.BlockSpec((heads, None, MLA_HEAD_PAD, tq),
                               lambda b, h, s: (h, b * nq + jnp.minimum(per * s + per, nq - 1), 0, 0)),
                  pl.BlockSpec((seq, heads * MLA_HEAD_PAD), lambda b, h, s: (b, h)),
                  pl.BlockSpec((heads, nq, MLA_VT_ROWS, tq), lambda b, h, s: (h, b, 0, 0))],
        out_specs=pl.BlockSpec((per * tq, heads * MLA_V), lambda b, h, s: (b * steps + s, h)),
        out_shape=jax.ShapeDtypeStruct((T, MLA_HEADS * MLA_V), BF16),
        scratch_shapes=[score, score, score, stat, stat, stat] + [stat, acc] * per,
        compiler_params=_params("parallel", "parallel", "arbitrary"),
        name="mla_attn",
    )(qt, qt, k, vt)


def _gla_proj_kernel(hb_ref, ga_ref, wq_ref, wk_ref, wv_ref, wr_ref, wg_ref, bg_ref,
                     q_ref, k_ref, v_ref, la_ref, sr_ref):
    hb = hb_ref[...]
    z = _dot(ga_ref[...], wg_ref[...]) + bg_ref[...]
    r = _dot(hb, wr_ref[...])
    log_sig = jnp.minimum(z, 0.0) - jnp.log1p(jnp.exp(-jnp.abs(z)))
    la_ref[...] = log_sig * (1.0 / GLA_GATE_NORMALIZER)
    sr_ref[...] = (r / (1.0 + jnp.exp(-r))).astype(BF16)
    v_ref[...] = _dot(hb, wv_ref[...]).astype(BF16)
    q_ref[...] = (_dot(hb, wq_ref[...]) * float(GLA_DK ** -0.5)).astype(BF16)
    k_ref[...] = _dot(hb, wk_ref[...]).astype(BF16)


def _gla_proj(hb, ga, wq, wk, wv, wr, wg, bg, *, tm):
    T, D = hb.shape
    nk, nv = wq.shape[1], wv.shape[1]
    row = lambda w: pl.BlockSpec((tm, w), lambda i: (i, 0))
    return pl.pallas_call(
        _gla_proj_kernel,
        grid=(T // tm,),
        in_specs=[row(D), row(ga.shape[1])] + [_const_spec(w.shape) for w in (wq, wk, wv, wr, wg, bg)],
        out_specs=[row(nk), row(nk), row(nv), row(nk), row(nv)],
        out_shape=[jax.ShapeDtypeStruct((T, nk), BF16), jax.ShapeDtypeStruct((T, nk), BF16),
                   jax.ShapeDtypeStruct((T, nv), BF16), jax.ShapeDtypeStruct((T, nk), F32),
                   jax.ShapeDtypeStruct((T, nv), BF16)],
        compiler_params=_params("parallel"),
        name="gla_proj",
    )(hb, ga, wq, wk, wv, wr, wg, bg)


GLA_LEVELS = tuple(GLA_CHUNK >> (i + 1) for i in range(GLA_CHUNK.bit_length() - 2))


def _gla_decay_matrix():
    r = np.arange(GLA_CHUNK)
    groups = [r[None, :] <= r[:, None], r[None, :] > r[:, None]]
    for s in GLA_LEVELS:
        mid = (r // (2 * s)) * (2 * s) + s
        upper = (r & s) != 0
        up = (r[None, :] > mid[:, None]) & (r[None, :] <= r[:, None])
        lo = (r[None, :] > r[:, None]) & (r[None, :] <= mid[:, None])
        groups.append(np.where(upper[:, None], up, lo))
    return np.concatenate(groups, axis=0).astype(np.float32)


def _gla_level_masks():
    r = np.arange(GLA_CHUNK)
    x = r[:, None] ^ r[None, :]
    lower = r[None, :] < r[:, None]
    return np.stack([(lower & (x >= s) & (x < 2 * s)) for s in GLA_LEVELS]).astype(np.float32)


def _gla_group(qs, ks, vs, las, st, nmat, masks_ref):
    C, n = GLA_CHUNK, len(qs)
    la2 = jnp.concatenate([jnp.concatenate(_split(la), axis=0) for la in las], axis=1)
    e_all = jnp.exp(_dot(nmat, la2))
    e = [e_all[:, u * GLA_DK:(u + 1) * GLA_DK] for u in range(n)]

    row = lax.broadcasted_iota(jnp.int32, qs[0].shape, 0)
    a = [jnp.zeros((C, C), F32) for _ in range(n)]
    for lvl, s in enumerate(GLA_LEVELS):
        upper = (row & s) != 0
        mask = masks_ref[lvl]
        for u in range(n):
            t = (jnp.where(upper, qs[u], ks[u]) * e[u][(2 + lvl) * C:(3 + lvl) * C]).astype(BF16)
            a[u] = a[u] + lax.dot_general(t, t, NT_DIMS, preferred_element_type=F32) * mask
    r_cc = lax.broadcasted_iota(jnp.int32, (C, C), 0)
    c_cc = lax.broadcasted_iota(jnp.int32, (C, C), 1)
    eye = r_cc == c_cc
    below = (r_cc - 1 == c_cc) & ((r_cc & 1) == 1)
    o_intra, upd = [], []
    for u in range(n):
        pair = jnp.sum(qs[u] * jnp.exp(las[u]) * pltpu.roll(ks[u], 1, 0), axis=-1, keepdims=True)
        a_u = jnp.where(below, pair, a[u])
        a_u = jnp.where(eye, jnp.sum(qs[u] * ks[u], axis=-1, keepdims=True), a_u)
        o_intra.append(_dot(a_u.astype(BF16), vs[u]))
        k_dec = (ks[u] * e[u][C:2 * C]).astype(BF16)
        upd.append(lax.dot_general(k_dec, vs[u], TN_DIMS, preferred_element_type=F32))
    outs = []
    for u in range(n):
        eb = e[u][:C]
        outs.append(o_intra[u] + _dot((qs[u] * eb).astype(BF16), st.astype(BF16)))
        decay = jnp.broadcast_to(eb[C - 1:C, :], (GLA_DK, GLA_DK)).T
        st = st * jnp.concatenate([decay] * (GLA_DV // GLA_DK), axis=1) + upd[u]
    return outs, st


def _gla_rec_kernel(q_ref, k_ref, v_ref, la_ref, sr_ref, g_ref, nmat_ref, masks_ref, o_ref, st_ref, *,
                    ts, group):
    @pl.when(pl.program_id(2) == 0)
    def _():
        st_ref[...] = jnp.zeros_like(st_ref)

    g = g_ref[...]
    nmat = nmat_ref[...]
    span = group * GLA_CHUNK

    def body(c, _):
        base = pl.multiple_of(c * span, span)
        rows = [pl.ds(base + u * GLA_CHUNK, GLA_CHUNK) for u in range(group)]
        outs, st = _gla_group([q_ref[r, :].astype(F32) for r in rows], [k_ref[r, :].astype(F32) for r in rows],
                              [v_ref[r, :] for r in rows], [la_ref[r, :] for r in rows], st_ref[...],
                              nmat, masks_ref)
        st_ref[...] = st
        for r, o in zip(rows, outs):
            o_ref[r, :] = (_rms(o, g) * sr_ref[r, :].astype(F32)).astype(o_ref.dtype)
        return 0

    lax.fori_loop(0, ts // span, body, 0)


def _gla_rec(q, k, v, la, sr, g_out, nmat, masks, *, batch, seq, ts, group):
    T = q.shape[0]
    ns = seq // ts
    blk = lambda w: pl.BlockSpec((ts, w), lambda b, h, i: (b * ns + i, h))
    return pl.pallas_call(
        functools.partial(_gla_rec_kernel, ts=ts, group=group),
        grid=(batch, GLA_HEADS, ns),
        in_specs=[blk(GLA_DK), blk(GLA_DK), blk(GLA_DV), blk(GLA_DK), blk(GLA_DV),
                  _const_spec(g_out.shape), _const_spec(nmat.shape), _const_spec(masks.shape)],
        out_specs=blk(GLA_DV),
        out_shape=jax.ShapeDtypeStruct((T, GLA_HEADS * GLA_DV), BF16),
        scratch_shapes=[pltpu.VMEM((GLA_DK, GLA_DV), F32)],
        compiler_params=_params("parallel", "parallel", "arbitrary"),
        name="gla_rec",
    )(q, k, v, la, sr, g_out, nmat, masks)


def _mem_kv_kernel(mem_ref, g_ref, w_ref, gk_ref, seg_ref, k_ref, v_ref):
    kv = _dot(_rms(mem_ref[...], g_ref[...]).astype(BF16), w_ref[...])
    nk = MEM_HEADS * MEM_DQK
    k = kv[:, :nk]
    ss = _dot_split(k * k, seg_ref[...]) * (1.0 / MEM_DQK)
    k_ref[...] = (k * lax.rsqrt(ss + EPS) * gk_ref[...]).astype(BF16)
    v_ref[...] = kv[:, nk:].astype(BF16)


def _mem_kv(mem2, g_mem, w, gk, seg, *, batch, mem_len):
    D = mem2.shape[1]
    nk, nv = MEM_HEADS * MEM_DQK, MEM_HEADS * MEM_DV
    row = lambda w_: pl.BlockSpec((mem_len, w_), lambda b: (b, 0))
    return pl.pallas_call(
        _mem_kv_kernel,
        grid=(batch,),
        in_specs=[row(D), _const_spec(g_mem.shape), _const_spec(w.shape), _const_spec(gk.shape),
                  _const_spec(seg.shape)],
        out_specs=[row(nk), row(nv)],
        out_shape=[jax.ShapeDtypeStruct((batch * mem_len, nk), BF16),
                   jax.ShapeDtypeStruct((batch * mem_len, nv), BF16)],
        compiler_params=_params("parallel"),
        name="mem_kv",
    )(mem2, g_mem, w, gk, seg)


def _merge_kernel(x_ref, hb_ref, omla_ref, ogla_ref, km_ref, vm_ref, wqm_ref, gqm_ref, seg_ref,
                  wgate_ref, bgate_ref, wo_ref, wup_ref, wdn_ref, o_ref, wupb_ref, wdnb_ref):
    wupb_ref[...] = wup_ref[...].astype(BF16)
    wdnb_ref[...] = wdn_ref[...].astype(BF16)
    hb = hb_ref[...]
    D = x_ref.shape[1]
    km = km_ref[...]

    def gate(j):
        z = _dot(hb, wgate_ref[:, j * D:(j + 1) * D]) + bgate_ref[:, j * D:(j + 1) * D]
        return 1.0 / (1.0 + jnp.exp(-z))

    qm = _dot(hb, wqm_ref[...])
    y = gate(0) * omla_ref[...].astype(F32)
    ss = _dot_split(qm * qm, seg_ref[...]) * (1.0 / MEM_DQK)
    qn = qm * lax.rsqrt(ss + EPS) * (gqm_ref[...] * float(MEM_DQK ** -0.5))
    head_of_lane = lax.broadcasted_iota(jnp.int32, qn.shape, 1) // MEM_DQK
    y = y + gate(1) * ogla_ref[...].astype(F32)
    probs = []
    for h in range(MEM_HEADS):
        qh = jnp.where(head_of_lane == h, qn, 0.0).astype(BF16)
        s = lax.dot_general(qh, km, NT_DIMS, preferred_element_type=F32)
        probs.append(jnp.exp(s - jnp.max(s, axis=-1, keepdims=True)))
    g_mem = gate(2)
    parts = []
    for h, p in enumerate(probs):
        o_h = _dot(p.astype(BF16), vm_ref[:, h * MEM_DV:(h + 1) * MEM_DV])
        parts.append(o_h / jnp.sum(p, axis=-1, keepdims=True))
    y = y + g_mem * jnp.concatenate(parts, axis=-1)
    o_ref[...] = x_ref[...] + _dot(y.astype(BF16), wo_ref[...])


def _merge(x2, hb, o_mla, o_gla, km, vm, wqm, gqm, seg, wgate, bgate, wo, w_up, w_down, *, seq, mem_len, tm):
    T, D = x2.shape
    per_batch = seq // tm
    row = lambda w: pl.BlockSpec((tm, w), lambda i: (i, 0))
    mem_blk = lambda w: pl.BlockSpec((mem_len, w), lambda i: (i // per_batch, 0))
    return pl.pallas_call(
        _merge_kernel,
        grid=(T // tm,),
        in_specs=[row(D), row(D), row(D), row(D), mem_blk(km.shape[1]), mem_blk(vm.shape[1])]
        + [_const_spec(w.shape) for w in (wqm, gqm, seg, wgate, bgate, wo)]
        + [_slab_spec(w_up, T // tm), _slab_spec(w_down, T // tm)],
        out_specs=[row(D), _slab_spec(w_up, T // tm), _slab_spec(w_down, T // tm)],
        out_shape=[jax.ShapeDtypeStruct((T, D), F32), jax.ShapeDtypeStruct(w_up.shape, BF16),
                   jax.ShapeDtypeStruct(w_down.shape, BF16)],
        compiler_params=_params("parallel"),
        name="merge",
    )(x2, hb, o_mla, o_gla, km, vm, wqm, gqm, seg, wgate, bgate, wo, w_up, w_down)


def _ffn_kernel(x_ref, g_ref, wup_ref, wdn_ref, o_ref, *, ff_chunk):
    x = x_ref[...]
    hb = _rms(x, g_ref[...]).astype(BF16)
    acc = x
    for c in range(wup_ref.shape[1] // ff_chunk):
        u = jnp.maximum(_dot(hb, wup_ref[:, c * ff_chunk:(c + 1) * ff_chunk]), 0.0)
        acc = acc + _dot((u * u).astype(BF16), wdn_ref[c * ff_chunk:(c + 1) * ff_chunk, :])
    o_ref[...] = acc


def _ffn(x1, g_ffn, wup, wdn, *, tm, ff_chunk):
    T, D = x1.shape
    row = pl.BlockSpec((tm, D), lambda i: (i, 0))
    return pl.pallas_call(
        functools.partial(_ffn_kernel, ff_chunk=ff_chunk),
        grid=(T // tm,),
        in_specs=[row, _const_spec(g_ffn.shape), _const_spec(wup.shape), _const_spec(wdn.shape)],
        out_specs=row,
        out_shape=jax.ShapeDtypeStruct((T, D), F32),
        compiler_params=_params("parallel"),
        name="ffn",
    )(x1, g_ffn, wup, wdn)


def _pad_cols(w, n):
    return jnp.pad(w, ((0, 0), (0, n - w.shape[-1])))


def _layer(x, mem, positions, g_mix, w_in, b_gate, g_ckv, w_ukv, g_q_nope, g_k_nope, g_q_rope, g_k_rope,
           w_gla_gate, b_gla_gate, g_gla_out, g_mem, w_mem_kv, g_q_mem, g_k_mem, w_o, g_ffn, w_up, w_down):
    B, S, D = x.shape
    M = mem.shape[1]
    T = B * S
    rank = g_ckv.shape[0]
    row = lambda g: g.reshape(1, -1).astype(F32)

    sizes = (MLA_HEADS * (MLA_NOPE + MLA_ROPE), rank, MLA_ROPE, GLA_HEADS * GLA_DK, GLA_HEADS * GLA_DK,
             GLA_HEADS * GLA_DV, GLA_GATE_RANK, GLA_HEADS * GLA_DV, MEM_HEADS * MEM_DQK, N_BRANCHES * D)
    offs = np.concatenate([[0], np.cumsum(sizes)])
    (w_q, w_ckv, w_kr, w_gq, w_gk, w_gv, w_ga, w_gr, w_qm, w_gate) = [
        w_in[:, offs[i]:offs[i + 1]] for i in range(len(sizes))]

    tq = min(ATTN_TILE, S)
    w_ukv3 = w_ukv.reshape(rank, MLA_HEADS, MLA_NOPE + MLA_V)
    wuk = w_ukv3[:, :, :MLA_NOPE].reshape(rank, -1).astype(BF16)
    wuvt = w_ukv3[:, :, MLA_NOPE:].reshape(rank, -1).T.astype(BF16)
    q_scale = float((MLA_NOPE + MLA_ROPE) ** -0.5 * np.log2(np.e))
    lanes_of = lambda g, scale=1.0: jnp.broadcast_to((g.astype(F32) * scale)[:, None], (g.shape[0], tq))

    inv = 1.0 / (ROPE_THETA ** (jnp.arange(0, MLA_ROPE, 2, dtype=F32) / MLA_ROPE))
    ang = inv[:, None] * positions.astype(F32).reshape(1, T)

    x2 = x.reshape(T, D)
    hb, qt, k, vt, ga = _mla_prep(
        x2, row(g_mix), w_q.T.astype(BF16), jnp.concatenate([w_ckv, w_kr, w_ga], axis=1).T.astype(BF16), wuvt,
        w_ckv.astype(BF16), wuk,
        jnp.cos(ang), jnp.sin(ang), lanes_of(g_q_nope, q_scale), lanes_of(g_q_rope, q_scale), lanes_of(g_ckv),
        lanes_of(g_k_rope), row(g_ckv), row(g_k_nope), tm=tq)
    o_mla = _mla_attn(qt, k, vt, batch=B, seq=S, tq=tq, heads=ATTN_HEADS_PER_STEP)

    nmat = jnp.asarray(np.tile(_gla_decay_matrix(), (1, 2)), BF16)
    masks = jnp.asarray(_gla_level_masks())
    wg = jnp.pad(w_gla_gate, ((0, LANES - GLA_GATE_RANK), (0, 0))).astype(BF16)
    qg, kg, vg, la, sr = _gla_proj(hb, ga, w_gq.astype(BF16), w_gk.astype(BF16), w_gv.astype(BF16),
                                   w_gr.astype(BF16), wg, row(b_gla_gate), tm=min(GLA_PROJ_TILE, S))
    ts = min(GLA_STEP_TOKENS, S)
    o_gla = _gla_rec(qg, kg, vg, la, sr, row(g_gla_out), nmat, masks, batch=B, seq=S, ts=ts,
                     group=ts // GLA_CHUNK)

    seg = jnp.asarray(np.kron(np.eye(MEM_HEADS), np.ones((MEM_DQK, MEM_DQK))).astype(np.float32), BF16)
    km, vm = _mem_kv(mem.reshape(B * M, D), row(g_mem), w_mem_kv.astype(BF16),
                     jnp.tile(row(g_k_mem), (1, MEM_HEADS)), seg, batch=B, mem_len=M)
    x1, w_up_b, w_down_b = _merge(x2, hb, o_mla, o_gla, km, vm, w_qm.astype(BF16),
                                  jnp.tile(row(g_q_mem), (1, MEM_HEADS)), seg, w_gate.astype(BF16), row(b_gate),
                                  w_o.astype(BF16), w_up, w_down, seq=S, mem_len=M,
                                  tm=min(MERGE_TILE, S))
    out = _ffn(x1, row(g_ffn), w_up_b, w_down_b, tm=min(FFN_TILE, S), ff_chunk=FFN_CHUNK)
    return out.reshape(B, S, D)


def kernel(x, mem, positions, g_mix, w_in, b_gate, g_ckv, w_ukv, g_q_nope, g_k_nope, g_q_rope, g_k_rope,
           w_gla_gate, b_gla_gate, g_gla_out, g_mem, w_mem_kv, g_q_mem, g_k_mem, w_o, g_ffn, w_up, w_down):
    for l in range(g_mix.shape[0]):
        x = _layer(x, mem, positions, g_mix[l], w_in[l], b_gate[l], g_ckv[l], w_ukv[l], g_q_nope[l],
                   g_k_nope[l], g_q_rope[l], g_k_rope[l], w_gla_gate[l], b_gla_gate[l], g_gla_out[l],
                   g_mem[l], w_mem_kv[l], g_q_mem[l], g_k_mem[l], w_o[l], g_ffn[l], w_up[l], w_down[l])
    return x
```

```python
import functools

import jax
import jax.numpy as jnp
import numpy as np
from jax import lax
from jax.experimental import pallas as pl
from jax.experimental.pallas import tpu as pltpu

F32 = jnp.float32
BF16 = jnp.bfloat16

EPS = 1e-6
ROPE_THETA = 10000.0
LANES = 128
BF16_SUBLANES = 16

MLA_HEADS, MLA_NOPE, MLA_ROPE, MLA_V = 8, 128, 64, 128
MLA_HEAD_PAD = 2 * LANES
MLA_VT_ROWS = MLA_V + BF16_SUBLANES
GLA_HEADS, GLA_DK, GLA_DV = 4, 128, 256
GLA_GATE_RANK, GLA_GATE_NORMALIZER, GLA_CHUNK = 16, 16.0, 64
MEM_HEADS, MEM_DQK, MEM_DV = 4, 64, 256
N_BRANCHES = 3

VMEM_LIMIT = 48 * 1024 * 1024

ATTN_TILE = 512
ATTN_HEADS_PER_STEP = 2
GLA_PROJ_TILE = 1024
GLA_STEP_TOKENS = 2048
MERGE_TILE = 512
FFN_TILE = 1024
FFN_CHUNK = 1024

NT_DIMS = (((1,), (1,)), ((), ()))
TN_DIMS = (((0,), (0,)), ((), ()))


def _params(*sem):
    return pltpu.CompilerParams(dimension_semantics=sem, vmem_limit_bytes=VMEM_LIMIT)


def _rms(t, g, n=None):
    n = t.shape[-1] if n is None else n
    ss = jnp.sum(t * t, axis=-1, keepdims=True) * (1.0 / n)
    return t * lax.rsqrt(ss + EPS) * g


def _dot(a, b):
    return jnp.dot(a, b, preferred_element_type=F32)


def _split(a_f32):
    hi = a_f32.astype(BF16)
    return hi, (a_f32 - hi.astype(F32)).astype(BF16)


def _dot_split(a_f32, b_bf16):
    hi, lo = _split(a_f32)
    return _dot(hi, b_bf16) + _dot(lo, b_bf16)


def _const_spec(shape):
    return pl.BlockSpec(shape, lambda *_: (0,) * len(shape))


def _slab_spec(a, steps):
    rows = a.shape[0] // steps
    assert rows * steps == a.shape[0] and rows % BF16_SUBLANES == 0
    return pl.BlockSpec((rows, a.shape[1]), lambda i: (i, 0))


def _rms_rows(t, g):
    ss = jnp.sum(t * t, axis=0, keepdims=True) * (1.0 / t.shape[0])
    return t * lax.rsqrt(ss + EPS) * g


def _rope_rows(r, cos, sin):
    half = r.shape[0] // 2
    t1, t2 = r[:half], r[half:]
    return t1 * cos - t2 * sin, t1 * sin + t2 * cos


def _mla_prep_kernel(x_ref, gmix_ref, wqt_ref, wlatt_ref, wuvt_ref, wckv_ref, wuk_ref,
                     cost_ref, sint_ref, gqnt_ref, gqrt_ref, gckvt_ref, gkrt_ref,
                     gckv_ref, gkn_ref, hb_ref, qt_ref, k_ref, vt_ref, ga_ref):
    h = _rms(x_ref[...], gmix_ref[...])
    hb = h.astype(BF16)
    hb_ref[...] = hb
    hbt = h.T.astype(BF16)
    tm = hbt.shape[1]
    half = MLA_ROPE // 2
    dq = MLA_NOPE + MLA_ROPE

    cos, sin = cost_ref[...], sint_ref[...]
    rank = gckv_ref.shape[-1]
    gqn, gqr = gqnt_ref[...], gqrt_ref[...]
    zeros = jnp.zeros((MLA_HEAD_PAD - dq, tm), BF16)

    def q_heads(heads):
        for hd in heads:
            t = _dot(wqt_ref[hd * dq:(hd + 1) * dq, :], hbt)
            qt_ref[hd, :MLA_NOPE, :] = _rms_rows(t[:MLA_NOPE], gqn).astype(BF16)
            r1, r2 = _rope_rows(_rms_rows(t[MLA_NOPE:], gqr), cos, sin)
            qt_ref[hd, MLA_NOPE:MLA_NOPE + half, :] = r1.astype(BF16)
            qt_ref[hd, MLA_NOPE + half:dq, :] = r2.astype(BF16)
            qt_ref[hd, dq:, :] = zeros

    ckv_raw = _dot(hb, wckv_ref[...])
    latt = _dot(wlatt_ref[...], hbt)
    q_heads(range(0, MLA_HEADS // 2))
    ga_t = latt[rank + MLA_ROPE:]
    ga_ref[...] = jnp.concatenate([ga_t, jnp.zeros((LANES - ga_t.shape[0], tm), F32)], axis=0).T.astype(BF16)

    ckv = _rms(ckv_raw, gckv_ref[...]).astype(BF16)
    kn = _dot(ckv, wuk_ref[...])
    ckvt = _rms_rows(latt[:rank], gckvt_ref[...]).astype(BF16)
    vt = _dot(wuvt_ref[...], ckvt)
    q_heads(range(MLA_HEADS // 2, MLA_HEADS))

    kr1, kr2 = _rope_rows(_rms_rows(latt[rank:rank + MLA_ROPE], gkrt_ref[...]), cos, sin)
    kpe = jnp.concatenate([kr1, kr2, jnp.zeros((LANES - MLA_ROPE, tm), F32)], axis=0).T.astype(BF16)
    gkn = gkn_ref[...]
    pad_row = lax.broadcasted_iota(jnp.int32, (MLA_VT_ROWS - MLA_V, tm), 0)
    ones_rows = jnp.where(pad_row == 0, 1.0, 0.0).astype(BF16)
    for hd in range(MLA_HEADS):
        c0 = hd * MLA_HEAD_PAD
        k_ref[:, c0:c0 + LANES] = _rms(kn[:, hd * LANES:(hd + 1) * LANES], gkn).astype(BF16)
        k_ref[:, c0 + LANES:c0 + 2 * LANES] = kpe
        vt_ref[hd, :MLA_V, :] = vt[hd * MLA_V:(hd + 1) * MLA_V].astype(BF16)
        vt_ref[hd, MLA_V:, :] = ones_rows


def _mla_prep(x2, g_mix, wqt, wlatt, wuvt, wckv, wuk, cos_tt, sin_tt, gqnt, gqrt, gckvt, gkrt,
              gckv, gkn, *, tm):
    T, D = x2.shape
    HP = MLA_HEADS * MLA_HEAD_PAD
    row = lambda w: pl.BlockSpec((tm, w), lambda i: (i, 0))
    col = lambda r: pl.BlockSpec((r, tm), lambda i: (0, i))
    consts = (g_mix, wqt, wlatt, wuvt, wckv, wuk)
    gains = (gqnt, gqrt, gckvt, gkrt, gckv, gkn)
    return pl.pallas_call(
        _mla_prep_kernel,
        grid=(T // tm,),
        in_specs=[row(D)] + [_const_spec(c.shape) for c in consts]
        + [col(cos_tt.shape[0]), col(sin_tt.shape[0])]
        + [_const_spec(g.shape) for g in gains],
        out_specs=[row(D), pl.BlockSpec((MLA_HEADS, None, MLA_HEAD_PAD, tm), lambda i: (0, i, 0, 0)), row(HP),
                   pl.BlockSpec((MLA_HEADS, None, MLA_VT_ROWS, tm), lambda i: (0, i, 0, 0)), row(LANES)],
        out_shape=[jax.ShapeDtypeStruct((T, D), BF16),
                   jax.ShapeDtypeStruct((MLA_HEADS, T // tm, MLA_HEAD_PAD, tm), BF16),
                   jax.ShapeDtypeStruct((T, HP), BF16),
                   jax.ShapeDtypeStruct((MLA_HEADS, T // tm, MLA_VT_ROWS, tm), BF16),
                   jax.ShapeDtypeStruct((T, LANES), BF16)],
        compiler_params=_params("parallel"),
        name="mla_prep",
    )(x2, *consts, cos_tt, sin_tt, *gains)


ATTN_TILES_PER_STEP = 4


def _mla_attn_kernel(qt_ref, qt_next_ref, k_ref, vt_ref, o_ref, s_a, s_b, s_c, max_a, max_b, max_c,
                     *stat_refs, tq, heads, nq):
    quad = pl.program_id(2)
    hs = range(heads)
    buf_a, buf_b, buf_c = (s_a, max_a), (s_b, max_b), (s_c, max_c)
    stats_of = [(stat_refs[2 * r], stat_refs[2 * r + 1]) for r in range(ATTN_TILES_PER_STEP)]

    def step(prod=None, cons=None):
        half = tq // 2
        if prod is not None:
            jp, q_ref, q_tile, (sp_ref, mp_ref), diagonal = prod
            r0 = pl.multiple_of(jp * tq, tq)
            for h in hs:
                cols = slice(h * MLA_HEAD_PAD, (h + 1) * MLA_HEAD_PAD)
                q = q_ref[h] if q_tile is None else q_ref[h, q_tile]
                if diagonal:
                    sp_ref[h, :half, :] = _dot(k_ref[pl.ds(r0, half), cols], q)
                    sp_ref[h, half:, half:] = _dot(k_ref[pl.ds(r0 + half, half), cols], q[:, half:])
                else:
                    s = _dot(k_ref[pl.ds(r0, tq), cols], q)
                    sp_ref[h] = s
                    mp_ref[h] = jnp.max(s, axis=0, keepdims=True)
        if cons is not None:
            jc, (sc_ref, mc_ref), diagonal, (m_ref, acc_ref) = cons
            for h in hs:
                if diagonal:
                    causal = lambda s: jnp.where(lax.broadcasted_iota(jnp.int32, s.shape, 0)
                                                 <= lax.broadcasted_iota(jnp.int32, s.shape, 1), s, -1e30)
                    top = causal(sc_ref[h, :half, :])
                    low = causal(sc_ref[h, half:, half:])
                    m_top = jnp.max(top, axis=0, keepdims=True)
                    m_new = jnp.concatenate(
                        [m_top[:, :half], jnp.maximum(m_top[:, half:], jnp.max(low, axis=0, keepdims=True))], axis=1)
                    pv = _dot(vt_ref[h, jc, :, :half], jnp.exp2(top - m_new).astype(BF16))
                    pv_low = _dot(vt_ref[h, jc, :, half:], jnp.exp2(low - m_new[:, half:]).astype(BF16))
                    acc_ref[h] = jnp.concatenate([pv[:, :half], pv[:, half:] + pv_low], axis=1)
                else:
                    m = m_ref[h]
                    m_new = jnp.maximum(m, mc_ref[h])
                    pv = _dot(vt_ref[h, jc], jnp.exp2(sc_ref[h] - m_new).astype(BF16))
                    acc_ref[h] = jnp.exp2(m - m_new) * acc_ref[h] + pv
                m_ref[h] = m_new

    def write_out(stats, tile):
        for h in hs:
            acc = stats[1][h]
            o_ref[tile * tq:(tile + 1) * tq, h * MLA_V:(h + 1) * MLA_V] = (
                acc[:MLA_V] / acc[MLA_V:MLA_V + 1]).T.astype(o_ref.dtype)

    def two_blocks(t, q_tile, stats):
        j = 2 * t
        step(prod=(j + 1, qt_ref, q_tile, buf_b, False), cons=(j, buf_a, False, stats))
        step(prod=(j + 2, qt_ref, q_tile, buf_a, False), cons=(j + 1, buf_b, False, stats))

    def run_tile(i, r, trips, leftover, diag_buf, next_prod):
        stats = stats_of[r]
        step(prod=(0, qt_ref, r, buf_a, False), cons=(i, diag_buf, True, stats))
        if not (isinstance(trips, int) and trips == 0):
            def four_blocks(t, carry):
                two_blocks(2 * t, r, stats)
                two_blocks(2 * t + 1, r, stats)
                return carry

            lax.fori_loop(0, trips, four_blocks, 0)
        if leftover:
            two_blocks(2 * trips, r, stats)
        if r % 2 == 0:
            step(prod=(i - 1, qt_ref, r, buf_b, False), cons=(i - 2, buf_a, False, stats))
            step(prod=next_prod, cons=(i - 1, buf_b, False, stats))
        else:
            step(prod=next_prod, cons=(i - 1, buf_a, False, stats))
        write_out(stats, r)

    first = ATTN_TILES_PER_STEP * quad
    ahead = (jnp.minimum(first + ATTN_TILES_PER_STEP, nq - 1), qt_next_ref, None, buf_c, True)
    diag_of = lambda r: (first + r, qt_ref, r, buf_c, True)

    @pl.when(quad == 0)
    def _():
        step(prod=(0, qt_ref, 0, buf_c, True))
        step(prod=(1, qt_ref, 1, buf_b, True), cons=(0, buf_c, True, stats_of[0]))
        write_out(stats_of[0], 0)
        run_tile(1, 1, 0, 0, buf_b, diag_of(2))
        run_tile(2, 2, 0, 0, buf_c, diag_of(3))
        run_tile(3, 3, 0, 1, buf_c, ahead)

    @pl.when(quad > 0)
    def _():
        run_tile(first, 0, quad - 1, 1, buf_c, diag_of(1))
        run_tile(first + 1, 1, quad, 0, buf_c, diag_of(2))
        run_tile(first + 2, 2, quad, 0, buf_c, diag_of(3))
        run_tile(first + 3, 3, quad, 1, buf_c, ahead)


def _mla_attn(qt, k, vt, *, batch, seq, tq, heads):
    T = k.shape[0]
    nq = seq // tq
    per = ATTN_TILES_PER_STEP
    assert nq % per == 0, "q tiles are processed in groups of ATTN_TILES_PER_STEP"
    steps = nq // per
    score = pltpu.VMEM((heads, tq, tq), F32)
    stat = pltpu.VMEM((heads, 1, tq), F32)
    acc = pltpu.VMEM((heads, MLA_VT_ROWS, tq), F32)
    return pl.pallas_call(
        functools.partial(_mla_attn_kernel, tq=tq, heads=heads, nq=nq),
        grid=(batch, MLA_HEADS // heads, steps),
        in_specs=[pl.BlockSpec((heads, per, MLA_HEAD_PAD, tq), lambda b, h, s: (h, b * steps + s, 0, 0)),
                  pl.BlockSpec((heads, None, MLA_HEAD_PAD, tq),
                               lambda b, h, s: (h, b * nq + jnp.minimum(per * s + per, nq - 1), 0, 0)),
                  pl.BlockSpec((seq, heads * MLA_HEAD_PAD), lambda b, h, s: (b, h)),
                  pl.BlockSpec((heads, nq, MLA_VT_ROWS, tq), lambda b, h, s: (h, b, 0, 0))],
        out_specs=pl.BlockSpec((per * tq, heads * MLA_V), lambda b, h, s: (b * steps + s, h)),
        out_shape=jax.ShapeDtypeStruct((T, MLA_HEADS * MLA_V), BF16),
        scratch_shapes=[score, score, score, stat, stat, stat] + [stat, acc] * per,
        compiler_params=_params("parallel", "parallel", "arbitrary"),
        name="mla_attn",
    )(qt, qt, k, vt)


def _gla_proj_kernel(hb_ref, ga_ref, wq_ref, wk_ref, wv_ref, wr_ref, wg_ref, bg_ref,
                     q_ref, k_ref, v_ref, la_ref, sr_ref):
    hb = hb_ref[...]
    z = _dot(ga_ref[...], wg_ref[...]) + bg_ref[...]
    r = _dot(hb, wr_ref[...])
    log_sig = jnp.minimum(z, 0.0) - jnp.log1p(jnp.exp(-jnp.abs(z)))
    la_ref[...] = log_sig * (1.0 / GLA_GATE_NORMALIZER)
    sr_ref[...] = (r / (1.0 + jnp.exp(-r))).astype(BF16)
    v_ref[...] = _dot(hb, wv_ref[...]).astype(BF16)
    q_ref[...] = (_dot(hb, wq_ref[...]) * float(GLA_DK ** -0.5)).astype(BF16)
    k_ref[...] = _dot(hb, wk_ref[...]).astype(BF16)


def _gla_proj(hb, ga, wq, wk, wv, wr, wg, bg, *, tm):
    T, D = hb.shape
    nk, nv = wq.shape[1], wv.shape[1]
    row = lambda w: pl.BlockSpec((tm, w), lambda i: (i, 0))
    return pl.pallas_call(
        _gla_proj_kernel,
        grid=(T // tm,),
        in_specs=[row(D), row(ga.shape[1])] + [_const_spec(w.shape) for w in (wq, wk, wv, wr, wg, bg)],
        out_specs=[row(nk), row(nk), row(nv), row(nk), row(nv)],
        out_shape=[jax.ShapeDtypeStruct((T, nk), BF16), jax.ShapeDtypeStruct((T, nk), BF16),
                   jax.ShapeDtypeStruct((T, nv), BF16), jax.ShapeDtypeStruct((T, nk), F32),
                   jax.ShapeDtypeStruct((T, nv), BF16)],
        compiler_params=_params("parallel"),
        name="gla_proj",
    )(hb, ga, wq, wk, wv, wr, wg, bg)


GLA_LEVELS = tuple(GLA_CHUNK >> (i + 1) for i in range(GLA_CHUNK.bit_length() - 2))


def _gla_decay_matrix():
    r = np.arange(GLA_CHUNK)
    groups = [r[None, :] <= r[:, None], r[None, :] > r[:, None]]
    for s in GLA_LEVELS:
        mid = (r // (2 * s)) * (2 * s) + s
        upper = (r & s) != 0
        up = (r[None, :] > mid[:, None]) & (r[None, :] <= r[:, None])
        lo = (r[None, :] > r[:, None]) & (r[None, :] <= mid[:, None])
        groups.append(np.where(upper[:, None], up, lo))
    return np.concatenate(groups, axis=0).astype(np.float32)


def _gla_level_masks():
    r = np.arange(GLA_CHUNK)
    x = r[:, None] ^ r[None, :]
    lower = r[None, :] < r[:, None]
    return np.stack([(lower & (x >= s) & (x < 2 * s)) for s in GLA_LEVELS]).astype(np.float32)


def _gla_group(qs, ks, vs, las, st, nmat, masks_ref):
    C, n = GLA_CHUNK, len(qs)
    la2 = jnp.concatenate([jnp.concatenate(_split(la), axis=0) for la in las], axis=1)
    e_all = jnp.exp(_dot(nmat, la2))
    e = [e_all[:, u * GLA_DK:(u + 1) * GLA_DK] for u in range(n)]

    row = lax.broadcasted_iota(jnp.int32, qs[0].shape, 0)
    a = [jnp.zeros((C, C), F32) for _ in range(n)]
    for lvl, s in enumerate(GLA_LEVELS):
        upper = (row & s) != 0
        mask = masks_ref[lvl]
        for u in range(n):
            t = (jnp.where(upper, qs[u], ks[u]) * e[u][(2 + lvl) * C:(3 + lvl) * C]).astype(BF16)
            a[u] = a[u] + lax.dot_general(t, t, NT_DIMS, preferred_element_type=F32) * mask
    r_cc = lax.broadcasted_iota(jnp.int32, (C, C), 0)
    c_cc = lax.broadcasted_iota(jnp.int32, (C, C), 1)
    eye = r_cc == c_cc
    below = (r_cc - 1 == c_cc) & ((r_cc & 1) == 1)
    o_intra, upd = [], []
    for u in range(n):
        pair = jnp.sum(qs[u] * jnp.exp(las[u]) * pltpu.roll(ks[u], 1, 0), axis=-1, keepdims=True)
        a_u = jnp.where(below, pair, a[u])
        a_u = jnp.where(eye, jnp.sum(qs[u] * ks[u], axis=-1, keepdims=True), a_u)
        o_intra.append(_dot(a_u.astype(BF16), vs[u]))
        k_dec = (ks[u] * e[u][C:2 * C]).astype(BF16)
        upd.append(lax.dot_general(k_dec, vs[u], TN_DIMS, preferred_element_type=F32))
    outs = []
    for u in range(n):
        eb = e[u][:C]
        outs.append(o_intra[u] + _dot((qs[u] * eb).astype(BF16), st.astype(BF16)))
        decay = jnp.broadcast_to(eb[C - 1:C, :], (GLA_DK, GLA_DK)).T
        st = st * jnp.concatenate([decay] * (GLA_DV // GLA_DK), axis=1) + upd[u]
    return outs, st


def _gla_rec_kernel(q_ref, k_ref, v_ref, la_ref, sr_ref, g_ref, nmat_ref, masks_ref, o_ref, st_ref, *,
                    ts, group):
    @pl.when(pl.program_id(2) == 0)
    def _():
        st_ref[...] = jnp.zeros_like(st_ref)

    g = g_ref[...]
    nmat = nmat_ref[...]
    span = group * GLA_CHUNK

    def body(c, _):
        base = pl.multiple_of(c * span, span)
        rows = [pl.ds(base + u * GLA_CHUNK, GLA_CHUNK) for u in range(group)]
        outs, st = _gla_group([q_ref[r, :].astype(F32) for r in rows], [k_ref[r, :].astype(F32) for r in rows],
                              [v_ref[r, :] for r in rows], [la_ref[r, :] for r in rows], st_ref[...],
                              nmat, masks_ref)
        st_ref[...] = st
        for r, o in zip(rows, outs):
            o_ref[r, :] = (_rms(o, g) * sr_ref[r, :].astype(F32)).astype(o_ref.dtype)
        return 0

    lax.fori_loop(0, ts // span, body, 0)


def _gla_rec(q, k, v, la, sr, g_out, nmat, masks, *, batch, seq, ts, group):
    T = q.shape[0]
    ns = seq // ts
    blk = lambda w: pl.BlockSpec((ts, w), lambda b, h, i: (b * ns + i, h))
    return pl.pallas_call(
        functools.partial(_gla_rec_kernel, ts=ts, group=group),
        grid=(batch, GLA_HEADS, ns),
        in_specs=[blk(GLA_DK), blk(GLA_DK), blk(GLA_DV), blk(GLA_DK), blk(GLA_DV),
                  _const_spec(g_out.shape), _const_spec(nmat.shape), _const_spec(masks.shape)],
        out_specs=blk(GLA_DV),
        out_shape=jax.ShapeDtypeStruct((T, GLA_HEADS * GLA_DV), BF16),
        scratch_shapes=[pltpu.VMEM((GLA_DK, GLA_DV), F32)],
        compiler_params=_params("parallel", "parallel", "arbitrary"),
        name="gla_rec",
    )(q, k, v, la, sr, g_out, nmat, masks)


def _mem_kv_kernel(mem_ref, g_ref, w_ref, gk_ref, seg_ref, k_ref, v_ref):
    kv = _dot(_rms(mem_ref[...], g_ref[...]).astype(BF16), w_ref[...])
    nk = MEM_HEADS * MEM_DQK
    k = kv[:, :nk]
    ss = _dot_split(k * k, seg_ref[...]) * (1.0 / MEM_DQK)
    k_ref[...] = (k * lax.rsqrt(ss + EPS) * gk_ref[...]).astype(BF16)
    v_ref[...] = kv[:, nk:].astype(BF16)


def _mem_kv(mem2, g_mem, w, gk, seg, *, batch, mem_len):
    D = mem2.shape[1]
    nk, nv = MEM_HEADS * MEM_DQK, MEM_HEADS * MEM_DV
    row = lambda w_: pl.BlockSpec((mem_len, w_), lambda b: (b, 0))
    return pl.pallas_call(
        _mem_kv_kernel,
        grid=(batch,),
        in_specs=[row(D), _const_spec(g_mem.shape), _const_spec(w.shape), _const_spec(gk.shape),
                  _const_spec(seg.shape)],
        out_specs=[row(nk), row(nv)],
        out_shape=[jax.ShapeDtypeStruct((batch * mem_len, nk), BF16),
                   jax.ShapeDtypeStruct((batch * mem_len, nv), BF16)],
        compiler_params=_params("parallel"),
        name="mem_kv",
    )(mem2, g_mem, w, gk, seg)


def _merge_kernel(x_ref, hb_ref, omla_ref, ogla_ref, km_ref, vm_ref, wqm_ref, gqm_ref, seg_ref,
                  wgate_ref, bgate_ref, wo_ref, wup_ref, wdn_ref, o_ref, wupb_ref, wdnb_ref):
    wupb_ref[...] = wup_ref[...].astype(BF16)
    wdnb_ref[...] = wdn_ref[...].astype(BF16)
    hb = hb_ref[...]
    D = x_ref.shape[1]
    km = km_ref[...]

    def gate(j):
        z = _dot(hb, wgate_ref[:, j * D:(j + 1) * D]) + bgate_ref[:, j * D:(j + 1) * D]
        return 1.0 / (1.0 + jnp.exp(-z))

    qm = _dot(hb, wqm_ref[...])
    y = gate(0) * omla_ref[...].astype(F32)
    ss = _dot_split(qm * qm, seg_ref[...]) * (1.0 / MEM_DQK)
    qn = qm * lax.rsqrt(ss + EPS) * (gqm_ref[...] * float(MEM_DQK ** -0.5))
    head_of_lane = lax.broadcasted_iota(jnp.int32, qn.shape, 1) // MEM_DQK
    y = y + gate(1) * ogla_ref[...].astype(F32)
    probs = []
    for h in range(MEM_HEADS):
        qh = jnp.where(head_of_lane == h, qn, 0.0).astype(BF16)
        s = lax.dot_general(qh, km, NT_DIMS, preferred_element_type=F32)
        probs.append(jnp.exp(s - jnp.max(s, axis=-1, keepdims=True)))
    g_mem = gate(2)
    parts = []
    for h, p in enumerate(probs):
        o_h = _dot(p.astype(BF16), vm_ref[:, h * MEM_DV:(h + 1) * MEM_DV])
        parts.append(o_h / jnp.sum(p, axis=-1, keepdims=True))
    y = y + g_mem * jnp.concatenate(parts, axis=-1)
    o_ref[...] = x_ref[...] + _dot(y.astype(BF16), wo_ref[...])


def _merge(x2, hb, o_mla, o_gla, km, vm, wqm, gqm, seg, wgate, bgate, wo, w_up, w_down, *, seq, mem_len, tm):
    T, D = x2.shape
    per_batch = seq // tm
    row = lambda w: pl.BlockSpec((tm, w), lambda i: (i, 0))
    mem_blk = lambda w: pl.BlockSpec((mem_len, w), lambda i: (i // per_batch, 0))
    return pl.pallas_call(
        _merge_kernel,
        grid=(T // tm,),
        in_specs=[row(D), row(D), row(D), row(D), mem_blk(km.shape[1]), mem_blk(vm.shape[1])]
        + [_const_spec(w.shape) for w in (wqm, gqm, seg, wgate, bgate, wo)]
        + [_slab_spec(w_up, T // tm), _slab_spec(w_down, T // tm)],
        out_specs=[row(D), _slab_spec(w_up, T // tm), _slab_spec(w_down, T // tm)],
        out_shape=[jax.ShapeDtypeStruct((T, D), F32), jax.ShapeDtypeStruct(w_up.shape, BF16),
                   jax.ShapeDtypeStruct(w_down.shape, BF16)],
        compiler_params=_params("parallel"),
        name="merge",
    )(x2, hb, o_mla, o_gla, km, vm, wqm, gqm, seg, wgate, bgate, wo, w_up, w_down)


def _ffn_kernel(x_ref, g_ref, wup_ref, wdn_ref, o_ref, *, ff_chunk):
    x = x_ref[...]
    hb = _rms(x, g_ref[...]).astype(BF16)
    acc = x
    for c in range(wup_ref.shape[1] // ff_chunk):
        u = jnp.maximum(_dot(hb, wup_ref[:, c * ff_chunk:(c + 1) * ff_chunk]), 0.0)
        acc = acc + _dot((u * u).astype(BF16), wdn_ref[c * ff_chunk:(c + 1) * ff_chunk, :])
    o_ref[...] = acc


def _ffn(x1, g_ffn, wup, wdn, *, tm, ff_chunk):
    T, D = x1.shape
    row = pl.BlockSpec((tm, D), lambda i: (i, 0))
    return pl.pallas_call(
        functools.partial(_ffn_kernel, ff_chunk=ff_chunk),
        grid=(T // tm,),
        in_specs=[row, _const_spec(g_ffn.shape), _const_spec(wup.shape), _const_spec(wdn.shape)],
        out_specs=row,
        out_shape=jax.ShapeDtypeStruct((T, D), F32),
        compiler_params=_params("parallel"),
        name="ffn",
    )(x1, g_ffn, wup, wdn)


def _layer(x, mem, positions, g_mix, w_in, b_gate, g_ckv, w_ukv, g_q_nope, g_k_nope, g_q_rope, g_k_rope,
           w_gla_gate, b_gla_gate, g_gla_out, g_mem, w_mem_kv, g_q_mem, g_k_mem, w_o, g_ffn, w_up, w_down):
    B, S, D = x.shape
    M = mem.shape[1]
    T = B * S
    rank = g_ckv.shape[0]
    row = lambda g: g.reshape(1, -1).astype(F32)

    sizes = (MLA_HEADS * (MLA_NOPE + MLA_ROPE), rank, MLA_ROPE, GLA_HEADS * GLA_DK, GLA_HEADS * GLA_DK,
             GLA_HEADS * GLA_DV, GLA_GATE_RANK, GLA_HEADS * GLA_DV, MEM_HEADS * MEM_DQK, N_BRANCHES * D)
    offs = np.concatenate([[0], np.cumsum(sizes)])
    (w_q, w_ckv, w_kr, w_gq, w_gk, w_gv, w_ga, w_gr, w_qm, w_gate) = [
        w_in[:, offs[i]:offs[i + 1]] for i in range(len(sizes))]

    tq = min(ATTN_TILE, S)
    w_ukv3 = w_ukv.reshape(rank, MLA_HEADS, MLA_NOPE + MLA_V)
    wuk = w_ukv3[:, :, :MLA_NOPE].reshape(rank, -1).astype(BF16)
    wuvt = w_ukv3[:, :, MLA_NOPE:].reshape(rank, -1).T.astype(BF16)
    q_scale = float((MLA_NOPE + MLA_ROPE) ** -0.5 * np.log2(np.e))
    lanes_of = lambda g, scale=1.0: jnp.broadcast_to((g.astype(F32) * scale)[:, None], (g.shape[0], tq))

    inv = 1.0 / (ROPE_THETA ** (jnp.arange(0, MLA_ROPE, 2, dtype=F32) / MLA_ROPE))
    ang = inv[:, None] * positions.astype(F32).reshape(1, T)

    x2 = x.reshape(T, D)
    hb, qt, k, vt, ga = _mla_prep(
        x2, row(g_mix), w_q.T.astype(BF16), jnp.concatenate([w_ckv, w_kr, w_ga], axis=1).T.astype(BF16), wuvt,
        w_ckv.astype(BF16), wuk,
        jnp.cos(ang), jnp.sin(ang), lanes_of(g_q_nope, q_scale), lanes_of(g_q_rope, q_scale), lanes_of(g_ckv),
        lanes_of(g_k_rope), row(g_ckv), row(g_k_nope), tm=tq)
    o_mla = _mla_attn(qt, k, vt, batch=B, seq=S, tq=tq, heads=ATTN_HEADS_PER_STEP)

    nmat = jnp.asarray(np.tile(_gla_decay_matrix(), (1, 2)), BF16)
    masks = jnp.asarray(_gla_level_masks())
    wg = jnp.pad(w_gla_gate, ((0, LANES - GLA_GATE_RANK), (0, 0))).astype(BF16)
    qg, kg, vg, la, sr = _gla_proj(hb, ga, w_gq.astype(BF16), w_gk.astype(BF16), w_gv.astype(BF16),
                                   w_gr.astype(BF16), wg, row(b_gla_gate), tm=min(GLA_PROJ_TILE, S))
    ts = min(GLA_STEP_TOKENS, S)
    o_gla = _gla_rec(qg, kg, vg, la, sr, row(g_gla_out), nmat, masks, batch=B, seq=S, ts=ts,
                     group=ts // GLA_CHUNK)

    seg = jnp.asarray(np.kron(np.eye(MEM_HEADS), np.ones((MEM_DQK, MEM_DQK))).astype(np.float32), BF16)
    km, vm = _mem_kv(mem.reshape(B * M, D), row(g_mem), w_mem_kv.astype(BF16),
                     jnp.tile(row(g_k_mem), (1, MEM_HEADS)), seg, batch=B, mem_len=M)
    x1, w_up_b, w_down_b = _merge(x2, hb, o_mla, o_gla, km, vm, w_qm.astype(BF16),
                                  jnp.tile(row(g_q_mem), (1, MEM_HEADS)), seg, w_gate.astype(BF16), row(b_gate),
                                  w_o.astype(BF16), w_up, w_down, seq=S, mem_len=M,
                                  tm=min(MERGE_TILE, S))
    out = _ffn(x1, row(g_ffn), w_up_b, w_down_b, tm=min(FFN_TILE, S), ff_chunk=FFN_CHUNK)
    return out.reshape(B, S, D)


def kernel(x, mem, positions, g_mix, w_in, b_gate, g_ckv, w_ukv, g_q_nope, g_k_nope, g_q_rope, g_k_rope,
           w_gla_gate, b_gla_gate, g_gla_out, g_mem, w_mem_kv, g_q_mem, g_k_mem, w_o, g_ffn, w_up, w_down):
    for l in range(g_mix.shape[0]):
        x = _layer(x, mem, positions, g_mix[l], w_in[l], b_gate[l], g_ckv[l], w_ukv[l], g_q_nope[l],
                   g_k_nope[l], g_q_rope[l], g_k_rope[l], w_gla_gate[l], b_gla_gate[l], g_gla_out[l],
                   g_mem[l], w_mem_kv[l], g_q_mem[l], g_k_mem[l], w_o[l], g_ffn[l], w_up[l], w_down[l])
    return x
```

```python
import functools

import jax
import jax.numpy as jnp
import numpy as np
from jax import lax
from jax.experimental import pallas as pl
from jax.experimental.pallas import tpu as pltpu

F32 = jnp.float32
BF16 = jnp.bfloat16

EPS = 1e-6
ROPE_THETA = 10000.0
LANES = 128
BF16_SUBLANES = 16

MLA_HEADS, MLA_NOPE, MLA_ROPE, MLA_V = 8, 128, 64, 128
MLA_HEAD_PAD = 2 * LANES
MLA_VT_ROWS = MLA_V + BF16_SUBLANES
GLA_HEADS, GLA_DK, GLA_DV = 4, 128, 256
GLA_GATE_RANK, GLA_GATE_NORMALIZER, GLA_CHUNK = 16, 16.0, 64
MEM_HEADS, MEM_DQK, MEM_DV = 4, 64, 256
N_BRANCHES = 3

VMEM_LIMIT = 48 * 1024 * 1024

ATTN_TILE = 512
ATTN_HEADS_PER_STEP = 2
GLA_PROJ_TILE = 1024
GLA_STEP_TOKENS = 4096
MERGE_TILE = 512
FFN_TILE = 1024
FFN_CHUNK = 1024

NT_DIMS = (((1,), (1,)), ((), ()))
TN_DIMS = (((0,), (0,)), ((), ()))


def _params(*sem):
    return pltpu.CompilerParams(dimension_semantics=sem, vmem_limit_bytes=VMEM_LIMIT)


def _rms(t, g, n=None):
    n = t.shape[-1] if n is None else n
    ss = jnp.sum(t * t, axis=-1, keepdims=True) * (1.0 / n)
    return t * lax.rsqrt(ss + EPS) * g


def _dot(a, b):
    return jnp.dot(a, b, preferred_element_type=F32)


def _split(a_f32):
    hi = a_f32.astype(BF16)
    return hi, (a_f32 - hi.astype(F32)).astype(BF16)


def _dot_split(a_f32, b_bf16):
    hi, lo = _split(a_f32)
    return _dot(hi, b_bf16) + _dot(lo, b_bf16)


def _const_spec(shape):
    return pl.BlockSpec(shape, lambda *_: (0,) * len(shape))


def _slab_spec(a, steps):
    rows = a.shape[0] // steps
    assert rows * steps == a.shape[0] and rows % BF16_SUBLANES == 0
    return pl.BlockSpec((rows, a.shape[1]), lambda i: (i, 0))


def _rms_rows(t, g):
    ss = jnp.sum(t * t, axis=0, keepdims=True) * (1.0 / t.shape[0])
    return t * lax.rsqrt(ss + EPS) * g


def _rope_rows(r, cos, sin):
    half = r.shape[0] // 2
    t1, t2 = r[:half], r[half:]
    return t1 * cos - t2 * sin, t1 * sin + t2 * cos


def _mla_prep_kernel(x_ref, gmix_ref, wqt_ref, wlatt_ref, wuvt_ref, wckv_ref, wuk_ref,
                     cost_ref, sint_ref, gqnt_ref, gqrt_ref, gckvt_ref, gkrt_ref,
                     gckv_ref, gkn_ref, hb_ref, qt_ref, k_ref, vt_ref, ga_ref):
    h = _rms(x_ref[...], gmix_ref[...])
    hb = h.astype(BF16)
    hb_ref[...] = hb
    hbt = h.T.astype(BF16)
    tm = hbt.shape[1]
    half = MLA_ROPE // 2
    dq = MLA_NOPE + MLA_ROPE

    cos, sin = cost_ref[...], sint_ref[...]
    rank = gckv_ref.shape[-1]
    gqn, gqr = gqnt_ref[...], gqrt_ref[...]
    zeros = jnp.zeros((MLA_HEAD_PAD - dq, tm), BF16)

    def q_heads(heads):
        for hd in heads:
            t = _dot(wqt_ref[hd * dq:(hd + 1) * dq, :], hbt)
            qt_ref[hd, :MLA_NOPE, :] = _rms_rows(t[:MLA_NOPE], gqn).astype(BF16)
            r1, r2 = _rope_rows(_rms_rows(t[MLA_NOPE:], gqr), cos, sin)
            qt_ref[hd, MLA_NOPE:MLA_NOPE + half, :] = r1.astype(BF16)
            qt_ref[hd, MLA_NOPE + half:dq, :] = r2.astype(BF16)
            qt_ref[hd, dq:, :] = zeros

    ckv_raw = _dot(hb, wckv_ref[...])
    latt = _dot(wlatt_ref[...], hbt)
    q_heads(range(0, MLA_HEADS // 2))
    ga_t = latt[rank + MLA_ROPE:]
    ga_ref[...] = jnp.concatenate([ga_t, jnp.zeros((LANES - ga_t.shape[0], tm), F32)], axis=0).T.astype(BF16)

    ckv = _rms(ckv_raw, gckv_ref[...]).astype(BF16)
    kn = _dot(ckv, wuk_ref[...])
    ckvt = _rms_rows(latt[:rank], gckvt_ref[...]).astype(BF16)
    vt = _dot(wuvt_ref[...], ckvt)
    q_heads(range(MLA_HEADS // 2, MLA_HEADS))

    kr1, kr2 = _rope_rows(_rms_rows(latt[rank:rank + MLA_ROPE], gkrt_ref[...]), cos, sin)
    kpe = jnp.concatenate([kr1, kr2, jnp.zeros((LANES - MLA_ROPE, tm), F32)], axis=0).T.astype(BF16)
    gkn = gkn_ref[...]
    pad_row = lax.broadcasted_iota(jnp.int32, (MLA_VT_ROWS - MLA_V, tm), 0)
    ones_rows = jnp.where(pad_row == 0, 1.0, 0.0).astype(BF16)
    for hd in range(MLA_HEADS):
        c0 = hd * MLA_HEAD_PAD
        k_ref[:, c0:c0 + LANES] = _rms(kn[:, hd * LANES:(hd + 1) * LANES], gkn).astype(BF16)
        k_ref[:, c0 + LANES:c0 + 2 * LANES] = kpe
        vt_ref[hd, :MLA_V, :] = vt[hd * MLA_V:(hd + 1) * MLA_V].astype(BF16)
        vt_ref[hd, MLA_V:, :] = ones_rows


def _mla_prep(x2, g_mix, wqt, wlatt, wuvt, wckv, wuk, cos_tt, sin_tt, gqnt, gqrt, gckvt, gkrt,
              gckv, gkn, *, tm):
    T, D = x2.shape
    HP = MLA_HEADS * MLA_HEAD_PAD
    row = lambda w: pl.BlockSpec((tm, w), lambda i: (i, 0))
    col = lambda r: pl.BlockSpec((r, tm), lambda i: (0, i))
    consts = (g_mix, wqt, wlatt, wuvt, wckv, wuk)
    gains = (gqnt, gqrt, gckvt, gkrt, gckv, gkn)
    return pl.pallas_call(
        _mla_prep_kernel,
        grid=(T // tm,),
        in_specs=[row(D)] + [_const_spec(c.shape) for c in consts]
        + [col(cos_tt.shape[0]), col(sin_tt.shape[0])]
        + [_const_spec(g.shape) for g in gains],
        out_specs=[row(D), pl.BlockSpec((MLA_HEADS, None, MLA_HEAD_PAD, tm), lambda i: (0, i, 0, 0)), row(HP),
                   pl.BlockSpec((MLA_HEADS, None, MLA_VT_ROWS, tm), lambda i: (0, i, 0, 0)), row(LANES)],
        out_shape=[jax.ShapeDtypeStruct((T, D), BF16),
                   jax.ShapeDtypeStruct((MLA_HEADS, T // tm, MLA_HEAD_PAD, tm), BF16),
                   jax.ShapeDtypeStruct((T, HP), BF16),
                   jax.ShapeDtypeStruct((MLA_HEADS, T // tm, MLA_VT_ROWS, tm), BF16),
                   jax.ShapeDtypeStruct((T, LANES), BF16)],
        compiler_params=_params("parallel"),
        name="mla_prep",
    )(x2, *consts, cos_tt, sin_tt, *gains)


ATTN_TILES_PER_STEP = 4


def _mla_attn_kernel(qt_ref, qt_next_ref, k_ref, vt_ref, o_ref, s_a, s_b, s_c, max_a, max_b, max_c,
                     *stat_refs, tq, heads, nq):
    quad = pl.program_id(2)
    hs = range(heads)
    buf_a, buf_b, buf_c = (s_a, max_a), (s_b, max_b), (s_c, max_c)
    stats_of = [(stat_refs[2 * r], stat_refs[2 * r + 1]) for r in range(ATTN_TILES_PER_STEP)]

    def step(prod=None, cons=None):
        half = tq // 2
        if prod is not None:
            jp, q_ref, q_tile, (sp_ref, mp_ref), diagonal = prod
            r0 = pl.multiple_of(jp * tq, tq)
            for h in hs:
                cols = slice(h * MLA_HEAD_PAD, (h + 1) * MLA_HEAD_PAD)
                q = q_ref[h] if q_tile is None else q_ref[h, q_tile]
                if diagonal:
                    sp_ref[h, :half, :] = _dot(k_ref[pl.ds(r0, half), cols], q)
                    sp_ref[h, half:, half:] = _dot(k_ref[pl.ds(r0 + half, half), cols], q[:, half:])
                else:
                    s = _dot(k_ref[pl.ds(r0, tq), cols], q)
                    sp_ref[h] = s
                    mp_ref[h] = jnp.max(s, axis=0, keepdims=True)
        if cons is not None:
            jc, (sc_ref, mc_ref), diagonal, (m_ref, acc_ref) = cons
            for h in hs:
                if diagonal:
                    causal = lambda s: jnp.where(lax.broadcasted_iota(jnp.int32, s.shape, 0)
                                                 <= lax.broadcasted_iota(jnp.int32, s.shape, 1), s, -1e30)
                    top = causal(sc_ref[h, :half, :])
                    low = causal(sc_ref[h, half:, half:])
                    m_top = jnp.max(top, axis=0, keepdims=True)
                    m_new = jnp.concatenate(
                        [m_top[:, :half], jnp.maximum(m_top[:, half:], jnp.max(low, axis=0, keepdims=True))], axis=1)
                    pv = _dot(vt_ref[h, jc, :, :half], jnp.exp2(top - m_new).astype(BF16))
                    pv_low = _dot(vt_ref[h, jc, :, half:], jnp.exp2(low - m_new[:, half:]).astype(BF16))
                    acc_ref[h] = jnp.concatenate([pv[:, :half], pv[:, half:] + pv_low], axis=1)
                else:
                    m = m_ref[h]
                    m_new = jnp.maximum(m, mc_ref[h])
                    pv = _dot(vt_ref[h, jc], jnp.exp2(sc_ref[h] - m_new).astype(BF16))
                    acc_ref[h] = jnp.exp2(m - m_new) * acc_ref[h] + pv
                m_ref[h] = m_new

    def write_out(stats, tile):
        for h in hs:
            acc = stats[1][h]
            o_ref[tile * tq:(tile + 1) * tq, h * MLA_V:(h + 1) * MLA_V] = (
                acc[:MLA_V] / acc[MLA_V:MLA_V + 1]).T.astype(o_ref.dtype)

    def two_blocks(t, q_tile, stats):
        j = 2 * t
        step(prod=(j + 1, qt_ref, q_tile, buf_b, False), cons=(j, buf_a, False, stats))
        step(prod=(j + 2, qt_ref, q_tile, buf_a, False), cons=(j + 1, buf_b, False, stats))

    def run_tile(i, r, trips, leftover, diag_buf, next_prod):
        stats = stats_of[r]
        step(prod=(0, qt_ref, r, buf_a, False), cons=(i, diag_buf, True, stats))
        if not (isinstance(trips, int) and trips == 0):
            def four_blocks(t, carry):
                two_blocks(2 * t, r, stats)
                two_blocks(2 * t + 1, r, stats)
                return carry

            lax.fori_loop(0, trips, four_blocks, 0)
        if leftover:
            two_blocks(2 * trips, r, stats)
        if r % 2 == 0:
            step(prod=(i - 1, qt_ref, r, buf_b, False), cons=(i - 2, buf_a, False, stats))
            step(prod=next_prod, cons=(i - 1, buf_b, False, stats))
        else:
            step(prod=next_prod, cons=(i - 1, buf_a, False, stats))
        write_out(stats, r)

    first = ATTN_TILES_PER_STEP * quad
    ahead = (jnp.minimum(first + ATTN_TILES_PER_STEP, nq - 1), qt_next_ref, None, buf_c, True)
    diag_of = lambda r: (first + r, qt_ref, r, buf_c, True)

    @pl.when(quad == 0)
    def _():
        step(prod=(0, qt_ref, 0, buf_c, True))
        step(prod=(1, qt_ref, 1, buf_b, True), cons=(0, buf_c, True, stats_of[0]))
        write_out(stats_of[0], 0)
        run_tile(1, 1, 0, 0, buf_b, diag_of(2))
        run_tile(2, 2, 0, 0, buf_c, diag_of(3))
        run_tile(3, 3, 0, 1, buf_c, ahead)

    @pl.when(quad > 0)
    def _():
        run_tile(first, 0, quad - 1, 1, buf_c, diag_of(1))
        run_tile(first + 1, 1, quad, 0, buf_c, diag_of(2))
        run_tile(first + 2, 2, quad, 0, buf_c, diag_of(3))
        run_tile(first + 3, 3, quad, 1, buf_c, ahead)


def _mla_attn(qt, k, vt, *, batch, seq, tq, heads):
    T = k.shape[0]
    nq = seq // tq
    per = ATTN_TILES_PER_STEP
    assert nq % per == 0, "q tiles are processed in groups of ATTN_TILES_PER_STEP"
    steps = nq // per
    score = pltpu.VMEM((heads, tq, tq), F32)
    stat = pltpu.VMEM((heads, 1, tq), F32)
    acc = pltpu.VMEM((heads, MLA_VT_ROWS, tq), F32)
    return pl.pallas_call(
        functools.partial(_mla_attn_kernel, tq=tq, heads=heads, nq=nq),
        grid=(batch, MLA_HEADS // heads, steps),
        in_specs=[pl.BlockSpec((heads, per, MLA_HEAD_PAD, tq), lambda b, h, s: (h, b * steps + s, 0, 0)),
                  pl.BlockSpec((heads, None, MLA_HEAD_PAD, tq),
                               lambda b, h, s: (h, b * nq + jnp.minimum(per * s + per, nq - 1), 0, 0)),
                  pl.BlockSpec((seq, heads * MLA_HEAD_PAD), lambda b, h, s: (b, h)),
                  pl.BlockSpec((heads, nq, MLA_VT_ROWS, tq), lambda b, h, s: (h, b, 0, 0))],
        out_specs=pl.BlockSpec((per * tq, heads * MLA_V), lambda b, h, s: (b * steps + s, h)),
        out_shape=jax.ShapeDtypeStruct((T, MLA_HEADS * MLA_V), BF16),
        scratch_shapes=[score, score, score, stat, stat, stat] + [stat, acc] * per,
        compiler_params=_params("parallel", "parallel", "arbitrary"),
        name="mla_attn",
    )(qt, qt, k, vt)


def _gla_proj_kernel(hb_ref, ga_ref, wq_ref, wk_ref, wv_ref, wr_ref, wg_ref, bg_ref,
                     q_ref, k_ref, v_ref, la_ref, sr_ref):
    hb = hb_ref[...]
    z = _dot(ga_ref[...], wg_ref[...]) + bg_ref[...]
    r = _dot(hb, wr_ref[...])
    log_sig = jnp.minimum(z, 0.0) - jnp.log1p(jnp.exp(-jnp.abs(z)))
    la_ref[...] = log_sig * (1.0 / GLA_GATE_NORMALIZER)
    sr_ref[...] = (r / (1.0 + jnp.exp(-r))).astype(BF16)
    v_ref[...] = _dot(hb, wv_ref[...]).astype(BF16)
    q_ref[...] = (_dot(hb, wq_ref[...]) * float(GLA_DK ** -0.5)).astype(BF16)
    k_ref[...] = _dot(hb, wk_ref[...]).astype(BF16)


def _gla_proj(hb, ga, wq, wk, wv, wr, wg, bg, *, tm):
    T, D = hb.shape
    nk, nv = wq.shape[1], wv.shape[1]
    row = lambda w: pl.BlockSpec((tm, w), lambda i: (i, 0))
    return pl.pallas_call(
        _gla_proj_kernel,
        grid=(T // tm,),
        in_specs=[row(D), row(ga.shape[1])] + [_const_spec(w.shape) for w in (wq, wk, wv, wr, wg, bg)],
        out_specs=[row(nk), row(nk), row(nv), row(nk), row(nv)],
        out_shape=[jax.ShapeDtypeStruct((T, nk), BF16), jax.ShapeDtypeStruct((T, nk), BF16),
                   jax.ShapeDtypeStruct((T, nv), BF16), jax.ShapeDtypeStruct((T, nk), F32),
                   jax.ShapeDtypeStruct((T, nv), BF16)],
        compiler_params=_params("parallel"),
        name="gla_proj",
    )(hb, ga, wq, wk, wv, wr, wg, bg)


GLA_LEVELS = tuple(GLA_CHUNK >> (i + 1) for i in range(GLA_CHUNK.bit_length() - 2))


def _gla_decay_matrix():
    r = np.arange(GLA_CHUNK)
    groups = [r[None, :] <= r[:, None], r[None, :] > r[:, None]]
    for s in GLA_LEVELS:
        mid = (r // (2 * s)) * (2 * s) + s
        upper = (r & s) != 0
        up = (r[None, :] > mid[:, None]) & (r[None, :] <= r[:, None])
        lo = (r[None, :] > r[:, None]) & (r[None, :] <= mid[:, None])
        groups.append(np.where(upper[:, None], up, lo))
    return np.concatenate(groups, axis=0).astype(np.float32)


def _gla_level_masks():
    r = np.arange(GLA_CHUNK)
    x = r[:, None] ^ r[None, :]
    lower = r[None, :] < r[:, None]
    return np.stack([(lower & (x >= s) & (x < 2 * s)) for s in GLA_LEVELS]).astype(np.float32)


def _gla_group(qs, ks, vs, las, st, nmat, masks_ref):
    C, n = GLA_CHUNK, len(qs)
    la2 = jnp.concatenate([jnp.concatenate(_split(la), axis=0) for la in las], axis=1)
    e_all = jnp.exp(_dot(nmat, la2))
    e = [e_all[:, u * GLA_DK:(u + 1) * GLA_DK] for u in range(n)]

    row = lax.broadcasted_iota(jnp.int32, qs[0].shape, 0)
    a = [jnp.zeros((C, C), F32) for _ in range(n)]
    for lvl, s in enumerate(GLA_LEVELS):
        upper = (row & s) != 0
        mask = masks_ref[lvl]
        for u in range(n):
            t = (jnp.where(upper, qs[u], ks[u]) * e[u][(2 + lvl) * C:(3 + lvl) * C]).astype(BF16)
            a[u] = a[u] + lax.dot_general(t, t, NT_DIMS, preferred_element_type=F32) * mask
    r_cc = lax.broadcasted_iota(jnp.int32, (C, C), 0)
    c_cc = lax.broadcasted_iota(jnp.int32, (C, C), 1)
    eye = r_cc == c_cc
    below = (r_cc - 1 == c_cc) & ((r_cc & 1) == 1)
    o_intra, upd = [], []
    for u in range(n):
        pair = jnp.sum(qs[u] * jnp.exp(las[u]) * pltpu.roll(ks[u], 1, 0), axis=-1, keepdims=True)
        a_u = jnp.where(below, pair, a[u])
        a_u = jnp.where(eye, jnp.sum(qs[u] * ks[u], axis=-1, keepdims=True), a_u)
        o_intra.append(_dot(a_u.astype(BF16), vs[u]))
        k_dec = (ks[u] * e[u][C:2 * C]).astype(BF16)
        upd.append(lax.dot_general(k_dec, vs[u], TN_DIMS, preferred_element_type=F32))
    outs = []
    for u in range(n):
        eb = e[u][:C]
        outs.append(o_intra[u] + _dot((qs[u] * eb).astype(BF16), st.astype(BF16)))
        decay = jnp.broadcast_to(eb[C - 1:C, :], (GLA_DK, GLA_DK)).T
        st = st * jnp.concatenate([decay] * (GLA_DV // GLA_DK), axis=1) + upd[u]
    return outs, st


def _gla_rec_kernel(q_ref, k_ref, v_ref, la_ref, sr_ref, g_ref, nmat_ref, masks_ref, o_ref, st_ref, *,
                    ts, group):
    @pl.when(pl.program_id(2) == 0)
    def _():
        st_ref[...] = jnp.zeros_like(st_ref)

    g = g_ref[...]
    nmat = nmat_ref[...]
    span = group * GLA_CHUNK

    def body(c, _):
        base = pl.multiple_of(c * span, span)
        rows = [pl.ds(base + u * GLA_CHUNK, GLA_CHUNK) for u in range(group)]
        outs, st = _gla_group([q_ref[r, :].astype(F32) for r in rows], [k_ref[r, :].astype(F32) for r in rows],
                              [v_ref[r, :] for r in rows], [la_ref[r, :] for r in rows], st_ref[...],
                              nmat, masks_ref)
        st_ref[...] = st
        for r, o in zip(rows, outs):
            o_ref[r, :] = (_rms(o, g) * sr_ref[r, :].astype(F32)).astype(o_ref.dtype)
        return 0

    lax.fori_loop(0, ts // span, body, 0)


def _gla_rec(q, k, v, la, sr, g_out, nmat, masks, *, batch, seq, ts, group):
    T = q.shape[0]
    ns = seq // ts
    blk = lambda w: pl.BlockSpec((ts, w), lambda b, h, i: (b * ns + i, h))
    return pl.pallas_call(
        functools.partial(_gla_rec_kernel, ts=ts, group=group),
        grid=(batch, GLA_HEADS, ns),
        in_specs=[blk(GLA_DK), blk(GLA_DK), blk(GLA_DV), blk(GLA_DK), blk(GLA_DV),
                  _const_spec(g_out.shape), _const_spec(nmat.shape), _const_spec(masks.shape)],
        out_specs=blk(GLA_DV),
        out_shape=jax.ShapeDtypeStruct((T, GLA_HEADS * GLA_DV), BF16),
        scratch_shapes=[pltpu.VMEM((GLA_DK, GLA_DV), F32)],
        compiler_params=_params("parallel", "parallel", "arbitrary"),
        name="gla_rec",
    )(q, k, v, la, sr, g_out, nmat, masks)


def _mem_kv_kernel(mem_ref, g_ref, w_ref, gk_ref, seg_ref, k_ref, v_ref):
    kv = _dot(_rms(mem_ref[...], g_ref[...]).astype(BF16), w_ref[...])
    nk = MEM_HEADS * MEM_DQK
    k = kv[:, :nk]
    ss = _dot_split(k * k, seg_ref[...]) * (1.0 / MEM_DQK)
    k_ref[...] = (k * lax.rsqrt(ss + EPS) * gk_ref[...]).astype(BF16)
    v_ref[...] = kv[:, nk:].astype(BF16)


def _mem_kv(mem2, g_mem, w, gk, seg, *, batch, mem_len):
    D = mem2.shape[1]
    nk, nv = MEM_HEADS * MEM_DQK, MEM_HEADS * MEM_DV
    row = lambda w_: pl.BlockSpec((mem_len, w_), lambda b: (b, 0))
    return pl.pallas_call(
        _mem_kv_kernel,
        grid=(batch,),
        in_specs=[row(D), _const_spec(g_mem.shape), _const_spec(w.shape), _const_spec(gk.shape),
                  _const_spec(seg.shape)],
        out_specs=[row(nk), row(nv)],
        out_shape=[jax.ShapeDtypeStruct((batch * mem_len, nk), BF16),
                   jax.ShapeDtypeStruct((batch * mem_len, nv), BF16)],
        compiler_params=_params("parallel"),
        name="mem_kv",
    )(mem2, g_mem, w, gk, seg)


def _merge_kernel(x_ref, hb_ref, omla_ref, ogla_ref, km_ref, vm_ref, wqm_ref, gqm_ref, seg_ref,
                  wgate_ref, bgate_ref, wo_ref, wup_ref, wdn_ref, o_ref, wupb_ref, wdnb_ref):
    wupb_ref[...] = wup_ref[...].astype(BF16)
    wdnb_ref[...] = wdn_ref[...].astype(BF16)
    hb = hb_ref[...]
    D = x_ref.shape[1]
    km = km_ref[...]

    def gate(j):
        z = _dot(hb, wgate_ref[:, j * D:(j + 1) * D]) + bgate_ref[:, j * D:(j + 1) * D]
        return 1.0 / (1.0 + jnp.exp(-z))

    qm = _dot(hb, wqm_ref[...])
    y = gate(0) * omla_ref[...].astype(F32)
    ss = _dot_split(qm * qm, seg_ref[...]) * (1.0 / MEM_DQK)
    qn = qm * lax.rsqrt(ss + EPS) * (gqm_ref[...] * float(MEM_DQK ** -0.5))
    head_of_lane = lax.broadcasted_iota(jnp.int32, qn.shape, 1) // MEM_DQK
    y = y + gate(1) * ogla_ref[...].astype(F32)
    probs = []
    for h in range(MEM_HEADS):
        qh = jnp.where(head_of_lane == h, qn, 0.0).astype(BF16)
        s = lax.dot_general(qh, km, NT_DIMS, preferred_element_type=F32)
        probs.append(jnp.exp(s - jnp.max(s, axis=-1, keepdims=True)))
    g_mem = gate(2)
    parts = []
    for h, p in enumerate(probs):
        o_h = _dot(p.astype(BF16), vm_ref[:, h * MEM_DV:(h + 1) * MEM_DV])
        parts.append(o_h / jnp.sum(p, axis=-1, keepdims=True))
    y = y + g_mem * jnp.concatenate(parts, axis=-1)
    o_ref[...] = x_ref[...] + _dot(y.astype(BF16), wo_ref[...])


def _merge(x2, hb, o_mla, o_gla, km, vm, wqm, gqm, seg, wgate, bgate, wo, w_up, w_down, *, seq, mem_len, tm):
    T, D = x2.shape
    per_batch = seq // tm
    row = lambda w: pl.BlockSpec((tm, w), lambda i: (i, 0))
    mem_blk = lambda w: pl.BlockSpec((mem_len, w), lambda i: (i // per_batch, 0))
    return pl.pallas_call(
        _merge_kernel,
        grid=(T // tm,),
        in_specs=[row(D), row(D), row(D), row(D), mem_blk(km.shape[1]), mem_blk(vm.shape[1])]
        + [_const_spec(w.shape) for w in (wqm, gqm, seg, wgate, bgate, wo)]
        + [_slab_spec(w_up, T // tm), _slab_spec(w_down, T // tm)],
        out_specs=[row(D), _slab_spec(w_up, T // tm), _slab_spec(w_down, T // tm)],
        out_shape=[jax.ShapeDtypeStruct((T, D), F32), jax.ShapeDtypeStruct(w_up.shape, BF16),
                   jax.ShapeDtypeStruct(w_down.shape, BF16)],
        compiler_params=_params("parallel"),
        name="merge",
    )(x2, hb, o_mla, o_gla, km, vm, wqm, gqm, seg, wgate, bgate, wo, w_up, w_down)


def _ffn_kernel(x_ref, g_ref, wup_ref, wdn_ref, o_ref, *, ff_chunk):
    x = x_ref[...]
    hb = _rms(x, g_ref[...]).astype(BF16)
    acc = x
    for c in range(wup_ref.shape[1] // ff_chunk):
        u = jnp.maximum(_dot(hb, wup_ref[:, c * ff_chunk:(c + 1) * ff_chunk]), 0.0)
        acc = acc + _dot((u * u).astype(BF16), wdn_ref[c * ff_chunk:(c + 1) * ff_chunk, :])
    o_ref[...] = acc


def _ffn(x1, g_ffn, wup, wdn, *, tm, ff_chunk):
    T, D = x1.shape
    row = pl.BlockSpec((tm, D), lambda i: (i, 0))
    return pl.pallas_call(
        functools.partial(_ffn_kernel, ff_chunk=ff_chunk),
        grid=(T // tm,),
        in_specs=[row, _const_spec(g_ffn.shape), _const_spec(wup.shape), _const_spec(wdn.shape)],
        out_specs=row,
        out_shape=jax.ShapeDtypeStruct((T, D), F32),
        compiler_params=_params("parallel"),
        name="ffn",
    )(x1, g_ffn, wup, wdn)


def _layer(x, mem, positions, g_mix, w_in, b_gate, g_ckv, w_ukv, g_q_nope, g_k_nope, g_q_rope, g_k_rope,
           w_gla_gate, b_gla_gate, g_gla_out, g_mem, w_mem_kv, g_q_mem, g_k_mem, w_o, g_ffn, w_up, w_down):
    B, S, D = x.shape
    M = mem.shape[1]
    T = B * S
    rank = g_ckv.shape[0]
    row = lambda g: g.reshape(1, -1).astype(F32)

    sizes = (MLA_HEADS * (MLA_NOPE + MLA_ROPE), rank, MLA_ROPE, GLA_HEADS * GLA_DK, GLA_HEADS * GLA_DK,
             GLA_HEADS * GLA_DV, GLA_GATE_RANK, GLA_HEADS * GLA_DV, MEM_HEADS * MEM_DQK, N_BRANCHES * D)
    offs = np.concatenate([[0], np.cumsum(sizes)])
    (w_q, w_ckv, w_kr, w_gq, w_gk, w_gv, w_ga, w_gr, w_qm, w_gate) = [
        w_in[:, offs[i]:offs[i + 1]] for i in range(len(sizes))]

    tq = min(ATTN_TILE, S)
    w_ukv3 = w_ukv.reshape(rank, MLA_HEADS, MLA_NOPE + MLA_V)
    wuk = w_ukv3[:, :, :MLA_NOPE].reshape(rank, -1).astype(BF16)
    wuvt = w_ukv3[:, :, MLA_NOPE:].reshape(rank, -1).T.astype(BF16)
    q_scale = float((MLA_NOPE + MLA_ROPE) ** -0.5 * np.log2(np.e))
    lanes_of = lambda g, scale=1.0: jnp.broadcast_to((g.astype(F32) * scale)[:, None], (g.shape[0], tq))

    inv = 1.0 / (ROPE_THETA ** (jnp.arange(0, MLA_ROPE, 2, dtype=F32) / MLA_ROPE))
    ang = inv[:, None] * positions.astype(F32).reshape(1, T)

    x2 = x.reshape(T, D)
    hb, qt, k, vt, ga = _mla_prep(
        x2, row(g_mix), w_q.T.astype(BF16), jnp.concatenate([w_ckv, w_kr, w_ga], axis=1).T.astype(BF16), wuvt,
        w_ckv.astype(BF16), wuk,
        jnp.cos(ang), jnp.sin(ang), lanes_of(g_q_nope, q_scale), lanes_of(g_q_rope, q_scale), lanes_of(g_ckv),
        lanes_of(g_k_rope), row(g_ckv), row(g_k_nope), tm=tq)
    o_mla = _mla_attn(qt, k, vt, batch=B, seq=S, tq=tq, heads=ATTN_HEADS_PER_STEP)

    nmat = jnp.asarray(np.tile(_gla_decay_matrix(), (1, 2)), BF16)
    masks = jnp.asarray(_gla_level_masks())
    wg = jnp.pad(w_gla_gate, ((0, LANES - GLA_GATE_RANK), (0, 0))).astype(BF16)
    qg, kg, vg, la, sr = _gla_proj(hb, ga, w_gq.astype(BF16), w_gk.astype(BF16), w_gv.astype(BF16),
                                   w_gr.astype(BF16), wg, row(b_gla_gate), tm=min(GLA_PROJ_TILE, S))
    ts = min(GLA_STEP_TOKENS, S)
    o_gla = _gla_rec(qg, kg, vg, la, sr, row(g_gla_out), nmat, masks, batch=B, seq=S, ts=ts,
                     group=ts // GLA_CHUNK)

    seg = jnp.asarray(np.kron(np.eye(MEM_HEADS), np.ones((MEM_DQK, MEM_DQK))).astype(np.float32), BF16)
    km, vm = _mem_kv(mem.reshape(B * M, D), row(g_mem), w_mem_kv.astype(BF16),
                     jnp.tile(row(g_k_mem), (1, MEM_HEADS)), seg, batch=B, mem_len=M)
    x1, w_up_b, w_down_b = _merge(x2, hb, o_mla, o_gla, km, vm, w_qm.astype(BF16),
                                  jnp.tile(row(g_q_mem), (1, MEM_HEADS)), seg, w_gate.astype(BF16), row(b_gate),
                                  w_o.astype(BF16), w_up, w_down, seq=S, mem_len=M,
                                  tm=min(MERGE_TILE, S))
    out = _ffn(x1, row(g_ffn), w_up_b, w_down_b, tm=min(FFN_TILE, S), ff_chunk=FFN_CHUNK)
    return out.reshape(B, S, D)


def kernel(x, mem, positions, g_mix, w_in, b_gate, g_ckv, w_ukv, g_q_nope, g_k_nope, g_q_rope, g_k_rope,
           w_gla_gate, b_gla_gate, g_gla_out, g_mem, w_mem_kv, g_q_mem, g_k_mem, w_o, g_ffn, w_up, w_down):
    for l in range(g_mix.shape[0]):
        x = _layer(x, mem, positions, g_mix[l], w_in[l], b_gate[l], g_ckv[l], w_ukv[l], g_q_nope[l],
                   g_k_nope[l], g_q_rope[l], g_k_rope[l], w_gla_gate[l], b_gla_gate[l], g_gla_out[l],
                   g_mem[l], w_mem_kv[l], g_q_mem[l], g_k_mem[l], w_o[l], g_ffn[l], w_up[l], w_down[l])
    return x
```

```python
import functools

import jax
import jax.numpy as jnp
import numpy as np
from jax import lax
from jax.experimental import pallas as pl
from jax.experimental.pallas import tpu as pltpu

F32 = jnp.float32
BF16 = jnp.bfloat16

EPS = 1e-6
ROPE_THETA = 10000.0
LANES = 128
BF16_SUBLANES = 16

MLA_HEADS, MLA_NOPE, MLA_ROPE, MLA_V = 8, 128, 64, 128
MLA_HEAD_PAD = 2 * LANES
MLA_VT_ROWS = MLA_V + BF16_SUBLANES
GLA_HEADS, GLA_DK, GLA_DV = 4, 128, 256
GLA_GATE_RANK, GLA_GATE_NORMALIZER, GLA_CHUNK = 16, 16.0, 64
MEM_HEADS, MEM_DQK, MEM_DV = 4, 64, 256
N_BRANCHES = 3

VMEM_LIMIT = 48 * 1024 * 1024

ATTN_TILE = 512
ATTN_HEADS_PER_STEP = 2
GLA_PROJ_TILE = 1024
GLA_STEP_TOKENS = 2048
MERGE_TILE = 512
FFN_TILE = 1024
FFN_CHUNK = 1024

NT_DIMS = (((1,), (1,)), ((), ()))
TN_DIMS = (((0,), (0,)), ((), ()))


def _params(*sem):
    return pltpu.CompilerParams(dimension_semantics=sem, vmem_limit_bytes=VMEM_LIMIT)


def _rms(t, g, n=None):
    n = t.shape[-1] if n is None else n
    ss = jnp.sum(t * t, axis=-1, keepdims=True) * (1.0 / n)
    return t * lax.rsqrt(ss + EPS) * g


def _dot(a, b):
    return jnp.dot(a, b, preferred_element_type=F32)


def _split(a_f32):
    hi = a_f32.astype(BF16)
    return hi, (a_f32 - hi.astype(F32)).astype(BF16)


def _dot_split(a_f32, b_bf16):
    hi, lo = _split(a_f32)
    return _dot(hi, b_bf16) + _dot(lo, b_bf16)


def _const_spec(shape):
    return pl.BlockSpec(shape, lambda *_: (0,) * len(shape))


def _slab_spec(a, steps):
    rows = a.shape[0] // steps
    assert rows * steps == a.shape[0] and rows % BF16_SUBLANES == 0
    return pl.BlockSpec((rows, a.shape[1]), lambda i: (i, 0))


def _rms_rows(t, g):
    ss = jnp.sum(t * t, axis=0, keepdims=True) * (1.0 / t.shape[0])
    return t * lax.rsqrt(ss + EPS) * g


def _rope_rows(r, cos, sin):
    half = r.shape[0] // 2
    t1, t2 = r[:half], r[half:]
    return t1 * cos - t2 * sin, t1 * sin + t2 * cos


def _mla_prep_kernel(x_ref, gmix_ref, wqt_ref, wlatt_ref, wuvt_ref, wckv_ref, wuk_ref,
                     cost_ref, sint_ref, gqnt_ref, gqrt_ref, gckvt_ref, gkrt_ref,
                     gckv_ref, gkn_ref, hb_ref, qt_ref, k_ref, vt_ref, ga_ref):
    h = _rms(x_ref[...], gmix_ref[...])
    hb = h.astype(BF16)
    hb_ref[...] = hb
    hbt = h.T.astype(BF16)
    tm = hbt.shape[1]
    half = MLA_ROPE // 2
    dq = MLA_NOPE + MLA_ROPE

    cos, sin = cost_ref[...], sint_ref[...]
    rank = gckv_ref.shape[-1]
    gqn, gqr = gqnt_ref[...], gqrt_ref[...]

    def q_heads(heads):
        for hd in heads:
            t = _dot(wqt_ref[hd * dq:(hd + 1) * dq, :], hbt)
            qt_ref[hd, :MLA_NOPE, :] = _rms_rows(t[:MLA_NOPE], gqn).astype(BF16)
            r1, r2 = _rope_rows(_rms_rows(t[MLA_NOPE:], gqr), cos, sin)
            qt_ref[hd, MLA_NOPE:MLA_NOPE + half, :] = r1.astype(BF16)
            qt_ref[hd, MLA_NOPE + half:, :] = r2.astype(BF16)

    ckv_raw = _dot(hb, wckv_ref[...])
    latt = _dot(wlatt_ref[...], hbt)
    q_heads(range(0, MLA_HEADS // 2))
    ga_t = latt[rank + MLA_ROPE:]
    ga_ref[...] = jnp.concatenate([ga_t, jnp.zeros((LANES - ga_t.shape[0], tm), F32)], axis=0).T.astype(BF16)

    ckv = _rms(ckv_raw, gckv_ref[...]).astype(BF16)
    kn = _dot(ckv, wuk_ref[...])
    ckvt = _rms_rows(latt[:rank], gckvt_ref[...]).astype(BF16)
    vt = _dot(wuvt_ref[...], ckvt)
    q_heads(range(MLA_HEADS // 2, MLA_HEADS))

    kr1, kr2 = _rope_rows(_rms_rows(latt[rank:rank + MLA_ROPE], gkrt_ref[...]), cos, sin)
    kpe = jnp.concatenate([kr1, kr2, jnp.zeros((LANES - MLA_ROPE, tm), F32)], axis=0).T.astype(BF16)
    gkn = gkn_ref[...]
    pad_row = lax.broadcasted_iota(jnp.int32, (MLA_VT_ROWS - MLA_V, tm), 0)
    ones_rows = jnp.where(pad_row == 0, 1.0, 0.0).astype(BF16)
    for hd in range(MLA_HEADS):
        c0 = hd * MLA_HEAD_PAD
        k_ref[:, c0:c0 + LANES] = _rms(kn[:, hd * LANES:(hd + 1) * LANES], gkn).astype(BF16)
        k_ref[:, c0 + LANES:c0 + 2 * LANES] = kpe
        vt_ref[hd, :MLA_V, :] = vt[hd * MLA_V:(hd + 1) * MLA_V].astype(BF16)
        vt_ref[hd, MLA_V:, :] = ones_rows


def _mla_prep(x2, g_mix, wqt, wlatt, wuvt, wckv, wuk, cos_tt, sin_tt, gqnt, gqrt, gckvt, gkrt,
              gckv, gkn, *, tm):
    T, D = x2.shape
    HP = MLA_HEADS * MLA_HEAD_PAD
    dq = MLA_NOPE + MLA_ROPE
    row = lambda w: pl.BlockSpec((tm, w), lambda i: (i, 0))
    table = lambda a: pl.BlockSpec((None,) + a.shape[1:], lambda i: (i, 0, 0))
    consts = (g_mix, wqt, wlatt, wuvt, wckv, wuk)
    gains = (gqnt, gqrt, gckvt, gkrt, gckv, gkn)
    return pl.pallas_call(
        _mla_prep_kernel,
        grid=(T // tm,),
        in_specs=[row(D)] + [_const_spec(c.shape) for c in consts]
        + [table(cos_tt), table(sin_tt)]
        + [_const_spec(g.shape) for g in gains],
        out_specs=[row(D), pl.BlockSpec((MLA_HEADS, None, dq, tm), lambda i: (0, i, 0, 0)), row(HP),
                   pl.BlockSpec((MLA_HEADS, None, MLA_VT_ROWS, tm), lambda i: (0, i, 0, 0)), row(LANES)],
        out_shape=[jax.ShapeDtypeStruct((T, D), BF16),
                   jax.ShapeDtypeStruct((MLA_HEADS, T // tm, dq, tm), BF16),
                   jax.ShapeDtypeStruct((T, HP), BF16),
                   jax.ShapeDtypeStruct((MLA_HEADS, T // tm, MLA_VT_ROWS, tm), BF16),
                   jax.ShapeDtypeStruct((T, LANES), BF16)],
        compiler_params=_params("parallel"),
        name="mla_prep",
    )(x2, *consts, cos_tt, sin_tt, *gains)


ATTN_TILES_PER_STEP = 4


def _mla_attn_kernel(qt_ref, qt_next_ref, k_ref, vt_ref, o_ref, s_a, s_b, s_c, max_a, max_b, max_c,
                     *stat_refs, tq, heads, nq):
    quad = pl.program_id(2)
    hs = range(heads)
    buf_a, buf_b, buf_c = (s_a, max_a), (s_b, max_b), (s_c, max_c)
    stats_of = [(stat_refs[2 * r], stat_refs[2 * r + 1]) for r in range(ATTN_TILES_PER_STEP)]

    def step(prod=None, cons=None):
        half = tq // 2
        if prod is not None:
            jp, q_ref, q_tile, (sp_ref, mp_ref), diagonal = prod
            r0 = pl.multiple_of(jp * tq, tq)
            for h in hs:
                cols = slice(h * MLA_HEAD_PAD, (h + 1) * MLA_HEAD_PAD)
                q = q_ref[h] if q_tile is None else q_ref[h, q_tile]
                q = jnp.concatenate([q, jnp.zeros((MLA_HEAD_PAD - q.shape[0], tq), q.dtype)], axis=0)
                if diagonal:
                    sp_ref[h, :half, :] = _dot(k_ref[pl.ds(r0, half), cols], q)
                    sp_ref[h, half:, half:] = _dot(k_ref[pl.ds(r0 + half, half), cols], q[:, half:])
                else:
                    s = _dot(k_ref[pl.ds(r0, tq), cols], q)
                    sp_ref[h] = s
                    mp_ref[h] = jnp.max(s, axis=0, keepdims=True)
        if cons is not None:
            jc, (sc_ref, mc_ref), diagonal, (m_ref, acc_ref) = cons
            for h in hs:
                if diagonal:
                    causal = lambda s: jnp.where(lax.broadcasted_iota(jnp.int32, s.shape, 0)
                                                 <= lax.broadcasted_iota(jnp.int32, s.shape, 1), s, -1e30)
                    top = causal(sc_ref[h, :half, :])
                    low = causal(sc_ref[h, half:, half:])
                    m_top = jnp.max(top, axis=0, keepdims=True)
                    m_new = jnp.concatenate(
                        [m_top[:, :half], jnp.maximum(m_top[:, half:], jnp.max(low, axis=0, keepdims=True))], axis=1)
                    pv = _dot(vt_ref[h, jc, :, :half], jnp.exp2(top - m_new).astype(BF16))
                    pv_low = _dot(vt_ref[h, jc, :, half:], jnp.exp2(low - m_new[:, half:]).astype(BF16))
                    acc_ref[h] = jnp.concatenate([pv[:, :half], pv[:, half:] + pv_low], axis=1)
                else:
                    m = m_ref[h]
                    m_new = jnp.maximum(m, mc_ref[h])
                    pv = _dot(vt_ref[h, jc], jnp.exp2(sc_ref[h] - m_new).astype(BF16))
                    acc_ref[h] = jnp.exp2(m - m_new) * acc_ref[h] + pv
                m_ref[h] = m_new

    def write_out(stats, tile):
        for h in hs:
            acc = stats[1][h]
            o_ref[tile * tq:(tile + 1) * tq, h * MLA_V:(h + 1) * MLA_V] = (
                acc[:MLA_V] / acc[MLA_V:MLA_V + 1]).T.astype(o_ref.dtype)

    def two_blocks(t, q_tile, stats):
        j = 2 * t
        step(prod=(j + 1, qt_ref, q_tile, buf_b, False), cons=(j, buf_a, False, stats))
        step(prod=(j + 2, qt_ref, q_tile, buf_a, False), cons=(j + 1, buf_b, False, stats))

    def run_tile(i, r, trips, leftover, diag_buf, next_prod):
        stats = stats_of[r]
        step(prod=(0, qt_ref, r, buf_a, False), cons=(i, diag_buf, True, stats))
        if not (isinstance(trips, int) and trips == 0):
            def four_blocks(t, carry):
                two_blocks(2 * t, r, stats)
                two_blocks(2 * t + 1, r, stats)
                return carry

            lax.fori_loop(0, trips, four_blocks, 0)
        if leftover:
            two_blocks(2 * trips, r, stats)
        if r % 2 == 0:
            step(prod=(i - 1, qt_ref, r, buf_b, False), cons=(i - 2, buf_a, False, stats))
            step(prod=next_prod, cons=(i - 1, buf_b, False, stats))
        else:
            step(prod=next_prod, cons=(i - 1, buf_a, False, stats))
        write_out(stats, r)

    first = ATTN_TILES_PER_STEP * quad
    ahead = (jnp.minimum(first + ATTN_TILES_PER_STEP, nq - 1), qt_next_ref, None, buf_c, True)
    diag_of = lambda r: (first + r, qt_ref, r, buf_c, True)

    @pl.when(quad == 0)
    def _():
        step(prod=(0, qt_ref, 0, buf_c, True))
        step(prod=(1, qt_ref, 1, buf_b, True), cons=(0, buf_c, True, stats_of[0]))
        write_out(stats_of[0], 0)
        run_tile(1, 1, 0, 0, buf_b, diag_of(2))
        run_tile(2, 2, 0, 0, buf_c, diag_of(3))
        run_tile(3, 3, 0, 1, buf_c, ahead)

    @pl.when(quad > 0)
    def _():
        run_tile(first, 0, quad - 1, 1, buf_c, diag_of(1))
        run_tile(first + 1, 1, quad, 0, buf_c, diag_of(2))
        run_tile(first + 2, 2, quad, 0, buf_c, diag_of(3))
        run_tile(first + 3, 3, quad, 1, buf_c, ahead)


def _mla_attn(qt, k, vt, *, batch, seq, tq, heads):
    T = k.shape[0]
    nq = seq // tq
    per = ATTN_TILES_PER_STEP
    assert nq % per == 0, "q tiles are processed in groups of ATTN_TILES_PER_STEP"
    steps = nq // per
    score = pltpu.VMEM((heads, tq, tq), F32)
    stat = pltpu.VMEM((heads, 1, tq), F32)
    acc = pltpu.VMEM((heads, MLA_VT_ROWS, tq), F32)
    return pl.pallas_call(
        functools.partial(_mla_attn_kernel, tq=tq, heads=heads, nq=nq),
        grid=(batch, MLA_HEADS // heads, steps),
        in_specs=[pl.BlockSpec((heads, per, qt.shape[2], tq), lambda b, h, s: (h, b * steps + s, 0, 0)),
                  pl.BlockSpec((heads, None, qt.shape[2], tq),
                               lambda b, h, s: (h, b * nq + jnp.minimum(per * s + per, nq - 1), 0, 0)),
                  pl.BlockSpec((seq, heads * MLA_HEAD_PAD), lambda b, h, s: (b, h)),
                  pl.BlockSpec((heads, nq, MLA_VT_ROWS, tq), lambda b, h, s: (h, b, 0, 0))],
        out_specs=pl.BlockSpec((per * tq, heads * MLA_V), lambda b, h, s: (b * steps + s, h)),
        out_shape=jax.ShapeDtypeStruct((T, MLA_HEADS * MLA_V), BF16),
        scratch_shapes=[score, score, score, stat, stat, stat] + [stat, acc] * per,
        compiler_params=_params("parallel", "parallel", "arbitrary"),
        name="mla_attn",
    )(qt, qt, k, vt)


def _gla_proj_kernel(hb_ref, ga_ref, wq_ref, wk_ref, wv_ref, wr_ref, wg_ref, bg_ref,
                     q_ref, k_ref, v_ref, la_ref, sr_ref):
    hb = hb_ref[...]
    z = _dot(ga_ref[...], wg_ref[...]) + bg_ref[...]
    r = _dot(hb, wr_ref[...])
    log_sig = jnp.minimum(z, 0.0) - jnp.log1p(jnp.exp(-jnp.abs(z)))
    la_ref[...] = log_sig * (1.0 / GLA_GATE_NORMALIZER)
    sr_ref[...] = (r / (1.0 + jnp.exp(-r))).astype(BF16)
    v_ref[...] = _dot(hb, wv_ref[...]).astype(BF16)
    q_ref[...] = (_dot(hb, wq_ref[...]) * float(GLA_DK ** -0.5)).astype(BF16)
    k_ref[...] = _dot(hb, wk_ref[...]).astype(BF16)


def _gla_proj(hb, ga, wq, wk, wv, wr, wg, bg, *, tm):
    T, D = hb.shape
    nk, nv = wq.shape[1], wv.shape[1]
    row = lambda w: pl.BlockSpec((tm, w), lambda i: (i, 0))
    return pl.pallas_call(
        _gla_proj_kernel,
        grid=(T // tm,),
        in_specs=[row(D), row(ga.shape[1])] + [_const_spec(w.shape) for w in (wq, wk, wv, wr, wg, bg)],
        out_specs=[row(nk), row(nk), row(nv), row(nk), row(nv)],
        out_shape=[jax.ShapeDtypeStruct((T, nk), BF16), jax.ShapeDtypeStruct((T, nk), BF16),
                   jax.ShapeDtypeStruct((T, nv), BF16), jax.ShapeDtypeStruct((T, nk), F32),
                   jax.ShapeDtypeStruct((T, nv), BF16)],
        compiler_params=_params("parallel"),
        name="gla_proj",
    )(hb, ga, wq, wk, wv, wr, wg, bg)


GLA_LEVELS = tuple(GLA_CHUNK >> (i + 1) for i in range(GLA_CHUNK.bit_length() - 2))


def _gla_decay_matrix():
    r = np.arange(GLA_CHUNK)
    groups = [r[None, :] <= r[:, None], r[None, :] > r[:, None]]
    for s in GLA_LEVELS:
        mid = (r // (2 * s)) * (2 * s) + s
        upper = (r & s) != 0
        up = (r[None, :] > mid[:, None]) & (r[None, :] <= r[:, None])
        lo = (r[None, :] > r[:, None]) & (r[None, :] <= mid[:, None])
        groups.append(np.where(upper[:, None], up, lo))
    return np.concatenate(groups, axis=0).astype(np.float32)


def _gla_level_masks():
    r = np.arange(GLA_CHUNK)
    x = r[:, None] ^ r[None, :]
    lower = r[None, :] < r[:, None]
    return np.stack([(lower & (x >= s) & (x < 2 * s)) for s in GLA_LEVELS]).astype(np.float32)


def _gla_group(qs, ks, vs, las, st, nmat, masks_ref):
    C, n = GLA_CHUNK, len(qs)
    la2 = jnp.concatenate([jnp.concatenate(_split(la), axis=0) for la in las], axis=1)
    e_all = jnp.exp(_dot(nmat, la2))
    e = [e_all[:, u * GLA_DK:(u + 1) * GLA_DK] for u in range(n)]

    row = lax.broadcasted_iota(jnp.int32, qs[0].shape, 0)
    a = [jnp.zeros((C, C), F32) for _ in range(n)]
    for lvl, s in enumerate(GLA_LEVELS):
        upper = (row & s) != 0
        mask = masks_ref[lvl]
        for u in range(n):
            t = (jnp.where(upper, qs[u], ks[u]) * e[u][(2 + lvl) * C:(3 + lvl) * C]).astype(BF16)
            a[u] = a[u] + lax.dot_general(t, t, NT_DIMS, preferred_element_type=F32) * mask
    r_cc = lax.broadcasted_iota(jnp.int32, (C, C), 0)
    c_cc = lax.broadcasted_iota(jnp.int32, (C, C), 1)
    eye = r_cc == c_cc
    below = (r_cc - 1 == c_cc) & ((r_cc & 1) == 1)
    o_intra, upd = [], []
    for u in range(n):
        pair = jnp.sum(qs[u] * jnp.exp(las[u]) * pltpu.roll(ks[u], 1, 0), axis=-1, keepdims=True)
        a_u = jnp.where(below, pair, a[u])
        a_u = jnp.where(eye, jnp.sum(qs[u] * ks[u], axis=-1, keepdims=True), a_u)
        o_intra.append(_dot(a_u.astype(BF16), vs[u]))
        k_dec = (ks[u] * e[u][C:2 * C]).astype(BF16)
        upd.append(lax.dot_general(k_dec, vs[u], TN_DIMS, preferred_element_type=F32))
    outs = []
    for u in range(n):
        eb = e[u][:C]
        outs.append(o_intra[u] + _dot((qs[u] * eb).astype(BF16), st.astype(BF16)))
        decay = jnp.broadcast_to(eb[C - 1:C, :], (GLA_DK, GLA_DK)).T
        st = st * jnp.concatenate([decay] * (GLA_DV // GLA_DK), axis=1) + upd[u]
    return outs, st


def _gla_rec_kernel(q_ref, k_ref, v_ref, la_ref, sr_ref, g_ref, nmat_ref, masks_ref, o_ref, st_ref, *,
                    ts, group):
    @pl.when(pl.program_id(2) == 0)
    def _():
        st_ref[...] = jnp.zeros_like(st_ref)

    g = g_ref[...]
    nmat = nmat_ref[...]
    span = group * GLA_CHUNK

    def body(c, _):
        base = pl.multiple_of(c * span, span)
        rows = [pl.ds(base + u * GLA_CHUNK, GLA_CHUNK) for u in range(group)]
        outs, st = _gla_group([q_ref[r, :].astype(F32) for r in rows], [k_ref[r, :].astype(F32) for r in rows],
                              [v_ref[r, :] for r in rows], [la_ref[r, :] for r in rows], st_ref[...],
                              nmat, masks_ref)
        st_ref[...] = st
        for r, o in zip(rows, outs):
            o_ref[r, :] = (_rms(o, g) * sr_ref[r, :].astype(F32)).astype(o_ref.dtype)
        return 0

    lax.fori_loop(0, ts // span, body, 0)


def _gla_rec(q, k, v, la, sr, g_out, nmat, masks, *, batch, seq, ts, group):
    T = q.shape[0]
    ns = seq // ts
    blk = lambda w: pl.BlockSpec((ts, w), lambda b, h, i: (b * ns + i, h))
    return pl.pallas_call(
        functools.partial(_gla_rec_kernel, ts=ts, group=group),
        grid=(batch, GLA_HEADS, ns),
        in_specs=[blk(GLA_DK), blk(GLA_DK), blk(GLA_DV), blk(GLA_DK), blk(GLA_DV),
                  _const_spec(g_out.shape), _const_spec(nmat.shape), _const_spec(masks.shape)],
        out_specs=blk(GLA_DV),
        out_shape=jax.ShapeDtypeStruct((T, GLA_HEADS * GLA_DV), BF16),
        scratch_shapes=[pltpu.VMEM((GLA_DK, GLA_DV), F32)],
        compiler_params=_params("parallel", "parallel", "arbitrary"),
        name="gla_rec",
    )(q, k, v, la, sr, g_out, nmat, masks)


def _mem_kv_kernel(mem_ref, g_ref, w_ref, gk_ref, seg_ref, k_ref, v_ref):
    kv = _dot(_rms(mem_ref[...], g_ref[...]).astype(BF16), w_ref[...])
    nk = MEM_HEADS * MEM_DQK
    k = kv[:, :nk]
    ss = _dot_split(k * k, seg_ref[...]) * (1.0 / MEM_DQK)
    k_ref[...] = (k * lax.rsqrt(ss + EPS) * gk_ref[...]).astype(BF16)
    v_ref[...] = kv[:, nk:].astype(BF16)


def _mem_kv(mem2, g_mem, w, gk, seg, *, batch, mem_len):
    D = mem2.shape[1]
    nk, nv = MEM_HEADS * MEM_DQK, MEM_HEADS * MEM_DV
    row = lambda w_: pl.BlockSpec((mem_len, w_), lambda b: (b, 0))
    return pl.pallas_call(
        _mem_kv_kernel,
        grid=(batch,),
        in_specs=[row(D), _const_spec(g_mem.shape), _const_spec(w.shape), _const_spec(gk.shape),
                  _const_spec(seg.shape)],
        out_specs=[row(nk), row(nv)],
        out_shape=[jax.ShapeDtypeStruct((batch * mem_len, nk), BF16),
                   jax.ShapeDtypeStruct((batch * mem_len, nv), BF16)],
        compiler_params=_params("parallel"),
        name="mem_kv",
    )(mem2, g_mem, w, gk, seg)


def _merge_kernel(x_ref, hb_ref, omla_ref, ogla_ref, km_ref, vm_ref, wqm_ref, gqm_ref, seg_ref,
                  wgate_ref, bgate_ref, wo_ref, wup_ref, wdn_ref, o_ref, wupb_ref, wdnb_ref):
    wupb_ref[...] = wup_ref[...].astype(BF16)
    wdnb_ref[...] = wdn_ref[...].astype(BF16)
    hb = hb_ref[...]
    D = x_ref.shape[1]
    km = km_ref[...]

    def gate(j):
        z = _dot(hb, wgate_ref[:, j * D:(j + 1) * D]) + bgate_ref[:, j * D:(j + 1) * D]
        return 1.0 / (1.0 + jnp.exp(-z))

    qm = _dot(hb, wqm_ref[...])
    y = gate(0) * omla_ref[...].astype(F32)
    ss = _dot_split(qm * qm, seg_ref[...]) * (1.0 / MEM_DQK)
    qn = qm * lax.rsqrt(ss + EPS) * (gqm_ref[...] * float(MEM_DQK ** -0.5))
    head_of_lane = lax.broadcasted_iota(jnp.int32, qn.shape, 1) // MEM_DQK
    y = y + gate(1) * ogla_ref[...].astype(F32)
    probs = []
    for h in range(MEM_HEADS):
        qh = jnp.where(head_of_lane == h, qn, 0.0).astype(BF16)
        s = lax.dot_general(qh, km, NT_DIMS, preferred_element_type=F32)
        probs.append(jnp.exp(s - jnp.max(s, axis=-1, keepdims=True)))
    g_mem = gate(2)
    parts = []
    for h, p in enumerate(probs):
        o_h = _dot(p.astype(BF16), vm_ref[:, h * MEM_DV:(h + 1) * MEM_DV])
        parts.append(o_h / jnp.sum(p, axis=-1, keepdims=True))
    y = y + g_mem * jnp.concatenate(parts, axis=-1)
    o_ref[...] = x_ref[...] + _dot(y.astype(BF16), wo_ref[...])


def _merge(x2, hb, o_mla, o_gla, km, vm, wqm, gqm, seg, wgate, bgate, wo, w_up, w_down, *, seq, mem_len, tm):
    T, D = x2.shape
    per_batch = seq // tm
    row = lambda w: pl.BlockSpec((tm, w), lambda i: (i, 0))
    mem_blk = lambda w: pl.BlockSpec((mem_len, w), lambda i: (i // per_batch, 0))
    return pl.pallas_call(
        _merge_kernel,
        grid=(T // tm,),
        in_specs=[row(D), row(D), row(D), row(D), mem_blk(km.shape[1]), mem_blk(vm.shape[1])]
        + [_const_spec(w.shape) for w in (wqm, gqm, seg, wgate, bgate, wo)]
        + [_slab_spec(w_up, T // tm), _slab_spec(w_down, T // tm)],
        out_specs=[row(D), _slab_spec(w_up, T // tm), _slab_spec(w_down, T // tm)],
        out_shape=[jax.ShapeDtypeStruct((T, D), F32), jax.ShapeDtypeStruct(w_up.shape, BF16),
                   jax.ShapeDtypeStruct(w_down.shape, BF16)],
        compiler_params=_params("parallel"),
        name="merge",
    )(x2, hb, o_mla, o_gla, km, vm, wqm, gqm, seg, wgate, bgate, wo, w_up, w_down)


def _ffn_kernel(x_ref, g_ref, wup_ref, wdn_ref, o_ref, *, ff_chunk):
    x = x_ref[...]
    hb = _rms(x, g_ref[...]).astype(BF16)
    acc = x
    for c in range(wup_ref.shape[1] // ff_chunk):
        u = jnp.maximum(_dot(hb, wup_ref[:, c * ff_chunk:(c + 1) * ff_chunk]), 0.0)
        acc = acc + _dot((u * u).astype(BF16), wdn_ref[c * ff_chunk:(c + 1) * ff_chunk, :])
    o_ref[...] = acc


def _ffn(x1, g_ffn, wup, wdn, *, tm, ff_chunk):
    T, D = x1.shape
    row = pl.BlockSpec((tm, D), lambda i: (i, 0))
    return pl.pallas_call(
        functools.partial(_ffn_kernel, ff_chunk=ff_chunk),
        grid=(T // tm,),
        in_specs=[row, _const_spec(g_ffn.shape), _const_spec(wup.shape), _const_spec(wdn.shape)],
        out_specs=row,
        out_shape=jax.ShapeDtypeStruct((T, D), F32),
        compiler_params=_params("parallel"),
        name="ffn",
    )(x1, g_ffn, wup, wdn)


def _layer(x, mem, positions, g_mix, w_in, b_gate, g_ckv, w_ukv, g_q_nope, g_k_nope, g_q_rope, g_k_rope,
           w_gla_gate, b_gla_gate, g_gla_out, g_mem, w_mem_kv, g_q_mem, g_k_mem, w_o, g_ffn, w_up, w_down):
    B, S, D = x.shape
    M = mem.shape[1]
    T = B * S
    rank = g_ckv.shape[0]
    row = lambda g: g.reshape(1, -1).astype(F32)

    sizes = (MLA_HEADS * (MLA_NOPE + MLA_ROPE), rank, MLA_ROPE, GLA_HEADS * GLA_DK, GLA_HEADS * GLA_DK,
             GLA_HEADS * GLA_DV, GLA_GATE_RANK, GLA_HEADS * GLA_DV, MEM_HEADS * MEM_DQK, N_BRANCHES * D)
    offs = np.concatenate([[0], np.cumsum(sizes)])
    (w_q, w_ckv, w_kr, w_gq, w_gk, w_gv, w_ga, w_gr, w_qm, w_gate) = [
        w_in[:, offs[i]:offs[i + 1]] for i in range(len(sizes))]

    tq = min(ATTN_TILE, S)
    w_ukv3 = w_ukv.reshape(rank, MLA_HEADS, MLA_NOPE + MLA_V)
    wuk = w_ukv3[:, :, :MLA_NOPE].reshape(rank, -1).astype(BF16)
    wuvt = w_ukv3[:, :, MLA_NOPE:].reshape(rank, -1).T.astype(BF16)
    q_scale = float((MLA_NOPE + MLA_ROPE) ** -0.5 * np.log2(np.e))
    lanes_of = lambda g, scale=1.0: jnp.broadcast_to((g.astype(F32) * scale)[:, None], (g.shape[0], tq))

    inv = 1.0 / (ROPE_THETA ** (jnp.arange(0, MLA_ROPE, 2, dtype=F32) / MLA_ROPE))
    ang = inv[None, :, None] * positions.astype(F32).reshape(T // tq, 1, tq)

    x2 = x.reshape(T, D)
    hb, qt, k, vt, ga = _mla_prep(
        x2, row(g_mix), w_q.T.astype(BF16), jnp.concatenate([w_ckv, w_kr, w_ga], axis=1).T.astype(BF16), wuvt,
        w_ckv.astype(BF16), wuk,
        jnp.cos(ang), jnp.sin(ang), lanes_of(g_q_nope, q_scale), lanes_of(g_q_rope, q_scale), lanes_of(g_ckv),
        lanes_of(g_k_rope), row(g_ckv), row(g_k_nope), tm=tq)
    o_mla = _mla_attn(qt, k, vt, batch=B, seq=S, tq=tq, heads=ATTN_HEADS_PER_STEP)

    nmat = jnp.asarray(np.tile(_gla_decay_matrix(), (1, 2)), BF16)
    masks = jnp.asarray(_gla_level_masks())
    wg = jnp.pad(w_gla_gate, ((0, LANES - GLA_GATE_RANK), (0, 0))).astype(BF16)
    qg, kg, vg, la, sr = _gla_proj(hb, ga, w_gq.astype(BF16), w_gk.astype(BF16), w_gv.astype(BF16),
                                   w_gr.astype(BF16), wg, row(b_gla_gate), tm=min(GLA_PROJ_TILE, S))
    ts = min(GLA_STEP_TOKENS, S)
    o_gla = _gla_rec(qg, kg, vg, la, sr, row(g_gla_out), nmat, masks, batch=B, seq=S, ts=ts,
                     group=ts // GLA_CHUNK)

    seg = jnp.asarray(np.kron(np.eye(MEM_HEADS), np.ones((MEM_DQK, MEM_DQK))).astype(np.float32), BF16)
    km, vm = _mem_kv(mem.reshape(B * M, D), row(g_mem), w_mem_kv.astype(BF16),
                     jnp.tile(row(g_k_mem), (1, MEM_HEADS)), seg, batch=B, mem_len=M)
    x1, w_up_b, w_down_b = _merge(x2, hb, o_mla, o_gla, km, vm, w_qm.astype(BF16),
                                  jnp.tile(row(g_q_mem), (1, MEM_HEADS)), seg, w_gate.astype(BF16), row(b_gate),
                                  w_o.astype(BF16), w_up, w_down, seq=S, mem_len=M,
                                  tm=min(MERGE_TILE, S))
    out = _ffn(x1, row(g_ffn), w_up_b, w_down_b, tm=min(FFN_TILE, S), ff_chunk=FFN_CHUNK)
    return out.reshape(B, S, D)


def kernel(x, mem, positions, g_mix, w_in, b_gate, g_ckv, w_ukv, g_q_nope, g_k_nope, g_q_rope, g_k_rope,
           w_gla_gate, b_gla_gate, g_gla_out, g_mem, w_mem_kv, g_q_mem, g_k_mem, w_o, g_ffn, w_up, w_down):
    for l in range(g_mix.shape[0]):
        x = _layer(x, mem, positions, g_mix[l], w_in[l], b_gate[l], g_ckv[l], w_ukv[l], g_q_nope[l],
                   g_k_nope[l], g_q_rope[l], g_k_rope[l], w_gla_gate[l], b_gla_gate[l], g_gla_out[l],
                   g_mem[l], w_mem_kv[l], g_q_mem[l], g_k_mem[l], w_o[l], g_ffn[l], w_up[l], w_down[l])
    return x
```

```python
import functools

import jax
import jax.numpy as jnp
import numpy as np
from jax import lax
from jax.experimental import pallas as pl
from jax.experimental.pallas import tpu as pltpu

F32 = jnp.float32
BF16 = jnp.bfloat16

EPS = 1e-6
ROPE_THETA = 10000.0
LANES = 128
BF16_SUBLANES = 16

MLA_HEADS, MLA_NOPE, MLA_ROPE, MLA_V = 8, 128, 64, 128
MLA_HEAD_PAD = 2 * LANES
MLA_VT_ROWS = MLA_V + BF16_SUBLANES
GLA_HEADS, GLA_DK, GLA_DV = 4, 128, 256
GLA_GATE_RANK, GLA_GATE_NORMALIZER, GLA_CHUNK = 16, 16.0, 64
MEM_HEADS, MEM_DQK, MEM_DV = 4, 64, 256
N_BRANCHES = 3

VMEM_LIMIT = 48 * 1024 * 1024

ATTN_TILE = 512
PREP_TILE = 1024
ATTN_HEADS_PER_STEP = 2
GLA_PROJ_TILE = 1024
GLA_STEP_TOKENS = 2048
MERGE_TILE = 512
FFN_TILE = 1024
FFN_CHUNK = 1024

NT_DIMS = (((1,), (1,)), ((), ()))
TN_DIMS = (((0,), (0,)), ((), ()))


def _params(*sem):
    return pltpu.CompilerParams(dimension_semantics=sem, vmem_limit_bytes=VMEM_LIMIT)


def _rms(t, g, n=None):
    n = t.shape[-1] if n is None else n
    ss = jnp.sum(t * t, axis=-1, keepdims=True) * (1.0 / n)
    return t * lax.rsqrt(ss + EPS) * g


def _dot(a, b):
    return jnp.dot(a, b, preferred_element_type=F32)


def _split(a_f32):
    hi = a_f32.astype(BF16)
    return hi, (a_f32 - hi.astype(F32)).astype(BF16)


def _dot_split(a_f32, b_bf16):
    hi, lo = _split(a_f32)
    return _dot(hi, b_bf16) + _dot(lo, b_bf16)


def _const_spec(shape):
    return pl.BlockSpec(shape, lambda *_: (0,) * len(shape))


def _slab_spec(a, steps):
    rows = a.shape[0] // steps
    assert rows * steps == a.shape[0] and rows % BF16_SUBLANES == 0
    return pl.BlockSpec((rows, a.shape[1]), lambda i: (i, 0))


def _rms_rows(t, g):
    ss = jnp.sum(t * t, axis=0, keepdims=True) * (1.0 / t.shape[0])
    return t * lax.rsqrt(ss + EPS) * g


def _rope_rows(r, cos, sin):
    half = r.shape[0] // 2
    t1, t2 = r[:half], r[half:]
    return t1 * cos - t2 * sin, t1 * sin + t2 * cos


def _mla_prep_kernel(x_ref, *refs, sub):
    weights, (cost_ref, sint_ref), gains = refs[:6], refs[6:8], refs[8:14]
    hb_ref, qt_ref, k_ref, vt_ref, ga_ref = refs[14:]
    for t in range(x_ref.shape[0] // sub):
        rows = pl.ds(t * sub, sub)
        _mla_prep_tile(x_ref.at[rows], *weights, cost_ref.at[t], sint_ref.at[t], *gains,
                       hb_ref.at[rows], qt_ref.at[:, t], k_ref.at[rows], vt_ref.at[:, t], ga_ref.at[rows])


def _mla_prep_tile(x_ref, gmix_ref, wqt_ref, wlatt_ref, wuvt_ref, wckv_ref, wuk_ref,
                   cost_ref, sint_ref, gqnt_ref, gqrt_ref, gckvt_ref, gkrt_ref,
                   gckv_ref, gkn_ref, hb_ref, qt_ref, k_ref, vt_ref, ga_ref):
    h = _rms(x_ref[...], gmix_ref[...])
    hb = h.astype(BF16)
    hb_ref[...] = hb
    hbt = h.T.astype(BF16)
    tm = hbt.shape[1]
    half = MLA_ROPE // 2
    dq = MLA_NOPE + MLA_ROPE

    cos, sin = cost_ref[...], sint_ref[...]
    rank = gckv_ref.shape[-1]
    gqn, gqr = gqnt_ref[...], gqrt_ref[...]

    def q_heads(heads):
        for hd in heads:
            t = _dot(wqt_ref[hd * dq:(hd + 1) * dq, :], hbt)
            qt_ref[hd, :MLA_NOPE, :] = _rms_rows(t[:MLA_NOPE], gqn).astype(BF16)
            r1, r2 = _rope_rows(_rms_rows(t[MLA_NOPE:], gqr), cos, sin)
            qt_ref[hd, MLA_NOPE:MLA_NOPE + half, :] = r1.astype(BF16)
            qt_ref[hd, MLA_NOPE + half:, :] = r2.astype(BF16)

    ckv_raw = _dot(hb, wckv_ref[...])
    latt = _dot(wlatt_ref[...], hbt)
    q_heads(range(0, MLA_HEADS // 2))
    ga_t = latt[rank + MLA_ROPE:]
    ga_ref[...] = jnp.concatenate([ga_t, jnp.zeros((LANES - ga_t.shape[0], tm), F32)], axis=0).T.astype(BF16)

    ckv = _rms(ckv_raw, gckv_ref[...]).astype(BF16)
    kn = _dot(ckv, wuk_ref[...])
    ckvt = _rms_rows(latt[:rank], gckvt_ref[...]).astype(BF16)
    vt = _dot(wuvt_ref[...], ckvt)
    q_heads(range(MLA_HEADS // 2, MLA_HEADS))

    kr1, kr2 = _rope_rows(_rms_rows(latt[rank:rank + MLA_ROPE], gkrt_ref[...]), cos, sin)
    kpe = jnp.concatenate([kr1, kr2, jnp.zeros((LANES - MLA_ROPE, tm), F32)], axis=0).T.astype(BF16)
    gkn = gkn_ref[...]
    pad_row = lax.broadcasted_iota(jnp.int32, (MLA_VT_ROWS - MLA_V, tm), 0)
    ones_rows = jnp.where(pad_row == 0, 1.0, 0.0).astype(BF16)
    for hd in range(MLA_HEADS):
        c0 = hd * MLA_HEAD_PAD
        k_ref[:, c0:c0 + LANES] = _rms(kn[:, hd * LANES:(hd + 1) * LANES], gkn).astype(BF16)
        k_ref[:, c0 + LANES:c0 + 2 * LANES] = kpe
        vt_ref[hd, :MLA_V, :] = vt[hd * MLA_V:(hd + 1) * MLA_V].astype(BF16)
        vt_ref[hd, MLA_V:, :] = ones_rows


def _mla_prep(x2, g_mix, wqt, wlatt, wuvt, wckv, wuk, cos_tt, sin_tt, gqnt, gqrt, gckvt, gkrt,
              gckv, gkn, *, tm, sub):
    T, D = x2.shape
    HP = MLA_HEADS * MLA_HEAD_PAD
    dq = MLA_NOPE + MLA_ROPE
    per = tm // sub
    row = lambda w: pl.BlockSpec((tm, w), lambda i: (i, 0))
    table = lambda a: pl.BlockSpec((per,) + a.shape[1:], lambda i: (i, 0, 0))
    tiles = lambda r: pl.BlockSpec((MLA_HEADS, per, r, sub), lambda i: (0, i, 0, 0))
    consts = (g_mix, wqt, wlatt, wuvt, wckv, wuk)
    gains = (gqnt, gqrt, gckvt, gkrt, gckv, gkn)
    return pl.pallas_call(
        functools.partial(_mla_prep_kernel, sub=sub),
        grid=(T // tm,),
        in_specs=[row(D)] + [_const_spec(c.shape) for c in consts]
        + [table(cos_tt), table(sin_tt)]
        + [_const_spec(g.shape) for g in gains],
        out_specs=[row(D), tiles(dq), row(HP), tiles(MLA_VT_ROWS), row(LANES)],
        out_shape=[jax.ShapeDtypeStruct((T, D), BF16),
                   jax.ShapeDtypeStruct((MLA_HEADS, T // sub, dq, sub), BF16),
                   jax.ShapeDtypeStruct((T, HP), BF16),
                   jax.ShapeDtypeStruct((MLA_HEADS, T // sub, MLA_VT_ROWS, sub), BF16),
                   jax.ShapeDtypeStruct((T, LANES), BF16)],
        compiler_params=_params("parallel"),
        name="mla_prep",
    )(x2, *consts, cos_tt, sin_tt, *gains)


ATTN_TILES_PER_STEP = 4


def _mla_attn_kernel(qt_ref, qt_next_ref, k_ref, vt_ref, o_ref, s_a, s_b, s_c, max_a, max_b, max_c,
                     *stat_refs, tq, heads, nq):
    quad = pl.program_id(2)
    hs = range(heads)
    buf_a, buf_b, buf_c = (s_a, max_a), (s_b, max_b), (s_c, max_c)
    stats_of = [(stat_refs[2 * r], stat_refs[2 * r + 1]) for r in range(ATTN_TILES_PER_STEP)]

    def step(prod=None, cons=None):
        half = tq // 2
        if prod is not None:
            jp, q_ref, q_tile, (sp_ref, mp_ref), diagonal = prod
            r0 = pl.multiple_of(jp * tq, tq)
            for h in hs:
                cols = slice(h * MLA_HEAD_PAD, (h + 1) * MLA_HEAD_PAD)
                q = q_ref[h] if q_tile is None else q_ref[h, q_tile]
                q = jnp.concatenate([q, jnp.zeros((MLA_HEAD_PAD - q.shape[0], tq), q.dtype)], axis=0)
                if diagonal:
                    sp_ref[h, :half, :] = _dot(k_ref[pl.ds(r0, half), cols], q)
                    sp_ref[h, half:, half:] = _dot(k_ref[pl.ds(r0 + half, half), cols], q[:, half:])
                else:
                    s = _dot(k_ref[pl.ds(r0, tq), cols], q)
                    sp_ref[h] = s
                    mp_ref[h] = jnp.max(s, axis=0, keepdims=True)
        if cons is not None:
            jc, (sc_ref, mc_ref), diagonal, (m_ref, acc_ref) = cons
            for h in hs:
                if diagonal:
                    causal = lambda s: jnp.where(lax.broadcasted_iota(jnp.int32, s.shape, 0)
                                                 <= lax.broadcasted_iota(jnp.int32, s.shape, 1), s, -1e30)
                    top = causal(sc_ref[h, :half, :])
                    low = causal(sc_ref[h, half:, half:])
                    m_top = jnp.max(top, axis=0, keepdims=True)
                    m_new = jnp.concatenate(
                        [m_top[:, :half], jnp.maximum(m_top[:, half:], jnp.max(low, axis=0, keepdims=True))], axis=1)
                    pv = _dot(vt_ref[h, jc, :, :half], jnp.exp2(top - m_new).astype(BF16))
                    pv_low = _dot(vt_ref[h, jc, :, half:], jnp.exp2(low - m_new[:, half:]).astype(BF16))
                    acc_ref[h] = jnp.concatenate([pv[:, :half], pv[:, half:] + pv_low], axis=1)
                else:
                    m = m_ref[h]
                    m_new = jnp.maximum(m, mc_ref[h])
                    pv = _dot(vt_ref[h, jc], jnp.exp2(sc_ref[h] - m_new).astype(BF16))
                    acc_ref[h] = jnp.exp2(m - m_new) * acc_ref[h] + pv
                m_ref[h] = m_new

    def write_out(stats, tile):
        for h in hs:
            acc = stats[1][h]
            o_ref[tile * tq:(tile + 1) * tq, h * MLA_V:(h + 1) * MLA_V] = (
                acc[:MLA_V] / acc[MLA_V:MLA_V + 1]).T.astype(o_ref.dtype)

    def two_blocks(t, q_tile, stats):
        j = 2 * t
        step(prod=(j + 1, qt_ref, q_tile, buf_b, False), cons=(j, buf_a, False, stats))
        step(prod=(j + 2, qt_ref, q_tile, buf_a, False), cons=(j + 1, buf_b, False, stats))

    def run_tile(i, r, trips, leftover, diag_buf, next_prod):
        stats = stats_of[r]
        step(prod=(0, qt_ref, r, buf_a, False), cons=(i, diag_buf, True, stats))
        if not (isinstance(trips, int) and trips == 0):
            def four_blocks(t, carry):
                two_blocks(2 * t, r, stats)
                two_blocks(2 * t + 1, r, stats)
                return carry

            lax.fori_loop(0, trips, four_blocks, 0)
        if leftover:
            two_blocks(2 * trips, r, stats)
        if r % 2 == 0:
            step(prod=(i - 1, qt_ref, r, buf_b, False), cons=(i - 2, buf_a, False, stats))
            step(prod=next_prod, cons=(i - 1, buf_b, False, stats))
        else:
            step(prod=next_prod, cons=(i - 1, buf_a, False, stats))
        write_out(stats, r)

    first = ATTN_TILES_PER_STEP * quad
    ahead = (jnp.minimum(first + ATTN_TILES_PER_STEP, nq - 1), qt_next_ref, None, buf_c, True)
    diag_of = lambda r: (first + r, qt_ref, r, buf_c, True)

    @pl.when(quad == 0)
    def _():
        step(prod=(0, qt_ref, 0, buf_c, True))
        step(prod=(1, qt_ref, 1, buf_b, True), cons=(0, buf_c, True, stats_of[0]))
        write_out(stats_of[0], 0)
        run_tile(1, 1, 0, 0, buf_b, diag_of(2))
        run_tile(2, 2, 0, 0, buf_c, diag_of(3))
        run_tile(3, 3, 0, 1, buf_c, ahead)

    @pl.when(quad > 0)
    def _():
        run_tile(first, 0, quad - 1, 1, buf_c, diag_of(1))
        run_tile(first + 1, 1, quad, 0, buf_c, diag_of(2))
        run_tile(first + 2, 2, quad, 0, buf_c, diag_of(3))
        run_tile(first + 3, 3, quad, 1, buf_c, ahead)


def _mla_attn(qt, k, vt, *, batch, seq, tq, heads):
    T = k.shape[0]
    nq = seq // tq
    per = ATTN_TILES_PER_STEP
    assert nq % per == 0, "q tiles are processed in groups of ATTN_TILES_PER_STEP"
    steps = nq // per
    score = pltpu.VMEM((heads, tq, tq), F32)
    stat = pltpu.VMEM((heads, 1, tq), F32)
    acc = pltpu.VMEM((heads, MLA_VT_ROWS, tq), F32)
    return pl.pallas_call(
        functools.partial(_mla_attn_kernel, tq=tq, heads=heads, nq=nq),
        grid=(batch, MLA_HEADS // heads, steps),
        in_specs=[pl.BlockSpec((heads, per, qt.shape[2], tq), lambda b, h, s: (h, b * steps + s, 0, 0)),
                  pl.BlockSpec((heads, None, qt.shape[2], tq),
                               lambda b, h, s: (h, b * nq + jnp.minimum(per * s + per, nq - 1), 0, 0)),
                  pl.BlockSpec((seq, heads * MLA_HEAD_PAD), lambda b, h, s: (b, h)),
                  pl.BlockSpec((heads, nq, MLA_VT_ROWS, tq), lambda b, h, s: (h, b, 0, 0))],
        out_specs=pl.BlockSpec((per * tq, heads * MLA_V), lambda b, h, s: (b * steps + s, h)),
        out_shape=jax.ShapeDtypeStruct((T, MLA_HEADS * MLA_V), BF16),
        scratch_shapes=[score, score, score, stat, stat, stat] + [stat, acc] * per,
        compiler_params=_params("parallel", "parallel", "arbitrary"),
        name="mla_attn",
    )(qt, qt, k, vt)


def _gla_proj_kernel(hb_ref, ga_ref, wq_ref, wk_ref, wv_ref, wr_ref, wg_ref, bg_ref,
                     q_ref, k_ref, v_ref, la_ref, sr_ref):
    hb = hb_ref[...]
    z = _dot(ga_ref[...], wg_ref[...]) + bg_ref[...]
    r = _dot(hb, wr_ref[...])
    log_sig = jnp.minimum(z, 0.0) - jnp.log1p(jnp.exp(-jnp.abs(z)))
    la_ref[...] = log_sig * (1.0 / GLA_GATE_NORMALIZER)
    sr_ref[...] = (r / (1.0 + jnp.exp(-r))).astype(BF16)
    v_ref[...] = _dot(hb, wv_ref[...]).astype(BF16)
    q_ref[...] = (_dot(hb, wq_ref[...]) * float(GLA_DK ** -0.5)).astype(BF16)
    k_ref[...] = _dot(hb, wk_ref[...]).astype(BF16)


def _gla_proj(hb, ga, wq, wk, wv, wr, wg, bg, *, tm):
    T, D = hb.shape
    nk, nv = wq.shape[1], wv.shape[1]
    row = lambda w: pl.BlockSpec((tm, w), lambda i: (i, 0))
    return pl.pallas_call(
        _gla_proj_kernel,
        grid=(T // tm,),
        in_specs=[row(D), row(ga.shape[1])] + [_const_spec(w.shape) for w in (wq, wk, wv, wr, wg, bg)],
        out_specs=[row(nk), row(nk), row(nv), row(nk), row(nv)],
        out_shape=[jax.ShapeDtypeStruct((T, nk), BF16), jax.ShapeDtypeStruct((T, nk), BF16),
                   jax.ShapeDtypeStruct((T, nv), BF16), jax.ShapeDtypeStruct((T, nk), F32),
                   jax.ShapeDtypeStruct((T, nv), BF16)],
        compiler_params=_params("parallel"),
        name="gla_proj",
    )(hb, ga, wq, wk, wv, wr, wg, bg)


GLA_LEVELS = tuple(GLA_CHUNK >> (i + 1) for i in range(GLA_CHUNK.bit_length() - 2))


def _gla_decay_matrix():
    r = np.arange(GLA_CHUNK)
    groups = [r[None, :] <= r[:, None], r[None, :] > r[:, None]]
    for s in GLA_LEVELS:
        mid = (r // (2 * s)) * (2 * s) + s
        upper = (r & s) != 0
        up = (r[None, :] > mid[:, None]) & (r[None, :] <= r[:, None])
        lo = (r[None, :] > r[:, None]) & (r[None, :] <= mid[:, None])
        groups.append(np.where(upper[:, None], up, lo))
    return np.concatenate(groups, axis=0).astype(np.float32)


def _gla_level_masks():
    r = np.arange(GLA_CHUNK)
    x = r[:, None] ^ r[None, :]
    lower = r[None, :] < r[:, None]
    return np.stack([(lower & (x >= s) & (x < 2 * s)) for s in GLA_LEVELS]).astype(np.float32)


def _gla_group(qs, ks, vs, las, st, nmat, masks_ref):
    C, n = GLA_CHUNK, len(qs)
    la2 = jnp.concatenate([jnp.concatenate(_split(la), axis=0) for la in las], axis=1)
    e_all = jnp.exp(_dot(nmat, la2))
    e = [e_all[:, u * GLA_DK:(u + 1) * GLA_DK] for u in range(n)]

    row = lax.broadcasted_iota(jnp.int32, qs[0].shape, 0)
    a = [jnp.zeros((C, C), F32) for _ in range(n)]
    for lvl, s in enumerate(GLA_LEVELS):
        upper = (row & s) != 0
        mask = masks_ref[lvl]
        for u in range(n):
            t = (jnp.where(upper, qs[u], ks[u]) * e[u][(2 + lvl) * C:(3 + lvl) * C]).astype(BF16)
            a[u] = a[u] + lax.dot_general(t, t, NT_DIMS, preferred_element_type=F32) * mask
    r_cc = lax.broadcasted_iota(jnp.int32, (C, C), 0)
    c_cc = lax.broadcasted_iota(jnp.int32, (C, C), 1)
    eye = r_cc == c_cc
    below = (r_cc - 1 == c_cc) & ((r_cc & 1) == 1)
    o_intra, upd = [], []
    for u in range(n):
        pair = jnp.sum(qs[u] * jnp.exp(las[u]) * pltpu.roll(ks[u], 1, 0), axis=-1, keepdims=True)
        a_u = jnp.where(below, pair, a[u])
        a_u = jnp.where(eye, jnp.sum(qs[u] * ks[u], axis=-1, keepdims=True), a_u)
        o_intra.append(_dot(a_u.astype(BF16), vs[u]))
        k_dec = (ks[u] * e[u][C:2 * C]).astype(BF16)
        upd.append(lax.dot_general(k_dec, vs[u], TN_DIMS, preferred_element_type=F32))
    outs = []
    for u in range(n):
        eb = e[u][:C]
        outs.append(o_intra[u] + _dot((qs[u] * eb).astype(BF16), st.astype(BF16)))
        decay = jnp.broadcast_to(eb[C - 1:C, :], (GLA_DK, GLA_DK)).T
        st = st * jnp.concatenate([decay] * (GLA_DV // GLA_DK), axis=1) + upd[u]
    return outs, st


def _gla_rec_kernel(q_ref, k_ref, v_ref, la_ref, sr_ref, g_ref, nmat_ref, masks_ref, o_ref, st_ref, *,
                    ts, group):
    @pl.when(pl.program_id(2) == 0)
    def _():
        st_ref[...] = jnp.zeros_like(st_ref)

    g = g_ref[...]
    nmat = nmat_ref[...]
    span = group * GLA_CHUNK

    def body(c, _):
        base = pl.multiple_of(c * span, span)
        rows = [pl.ds(base + u * GLA_CHUNK, GLA_CHUNK) for u in range(group)]
        outs, st = _gla_group([q_ref[r, :].astype(F32) for r in rows], [k_ref[r, :].astype(F32) for r in rows],
                              [v_ref[r, :] for r in rows], [la_ref[r, :] for r in rows], st_ref[...],
                              nmat, masks_ref)
        st_ref[...] = st
        for r, o in zip(rows, outs):
            o_ref[r, :] = (_rms(o, g) * sr_ref[r, :].astype(F32)).astype(o_ref.dtype)
        return 0

    lax.fori_loop(0, ts // span, body, 0)


def _gla_rec(q, k, v, la, sr, g_out, nmat, masks, *, batch, seq, ts, group):
    T = q.shape[0]
    ns = seq // ts
    blk = lambda w: pl.BlockSpec((ts, w), lambda b, h, i: (b * ns + i, h))
    return pl.pallas_call(
        functools.partial(_gla_rec_kernel, ts=ts, group=group),
        grid=(batch, GLA_HEADS, ns),
        in_specs=[blk(GLA_DK), blk(GLA_DK), blk(GLA_DV), blk(GLA_DK), blk(GLA_DV),
                  _const_spec(g_out.shape), _const_spec(nmat.shape), _const_spec(masks.shape)],
        out_specs=blk(GLA_DV),
        out_shape=jax.ShapeDtypeStruct((T, GLA_HEADS * GLA_DV), BF16),
        scratch_shapes=[pltpu.VMEM((GLA_DK, GLA_DV), F32)],
        compiler_params=_params("parallel", "parallel", "arbitrary"),
        name="gla_rec",
    )(q, k, v, la, sr, g_out, nmat, masks)


def _mem_kv_kernel(mem_ref, g_ref, w_ref, gk_ref, seg_ref, k_ref, v_ref):
    kv = _dot(_rms(mem_ref[...], g_ref[...]).astype(BF16), w_ref[...])
    nk = MEM_HEADS * MEM_DQK
    k = kv[:, :nk]
    ss = _dot_split(k * k, seg_ref[...]) * (1.0 / MEM_DQK)
    k_ref[...] = (k * lax.rsqrt(ss + EPS) * gk_ref[...]).astype(BF16)
    v_ref[...] = kv[:, nk:].astype(BF16)


def _mem_kv(mem2, g_mem, w, gk, seg, *, batch, mem_len):
    D = mem2.shape[1]
    nk, nv = MEM_HEADS * MEM_DQK, MEM_HEADS * MEM_DV
    row = lambda w_: pl.BlockSpec((mem_len, w_), lambda b: (b, 0))
    return pl.pallas_call(
        _mem_kv_kernel,
        grid=(batch,),
        in_specs=[row(D), _const_spec(g_mem.shape), _const_spec(w.shape), _const_spec(gk.shape),
                  _const_spec(seg.shape)],
        out_specs=[row(nk), row(nv)],
        out_shape=[jax.ShapeDtypeStruct((batch * mem_len, nk), BF16),
                   jax.ShapeDtypeStruct((batch * mem_len, nv), BF16)],
        compiler_params=_params("parallel"),
        name="mem_kv",
    )(mem2, g_mem, w, gk, seg)


def _merge_kernel(x_ref, hb_ref, omla_ref, ogla_ref, km_ref, vm_ref, wqm_ref, gqm_ref, seg_ref,
                  wgate_ref, bgate_ref, wo_ref, wup_ref, wdn_ref, o_ref, wupb_ref, wdnb_ref):
    wupb_ref[...] = wup_ref[...].astype(BF16)
    wdnb_ref[...] = wdn_ref[...].astype(BF16)
    hb = hb_ref[...]
    D = x_ref.shape[1]
    km = km_ref[...]

    def gate(j):
        z = _dot(hb, wgate_ref[:, j * D:(j + 1) * D]) + bgate_ref[:, j * D:(j + 1) * D]
        return 1.0 / (1.0 + jnp.exp(-z))

    qm = _dot(hb, wqm_ref[...])
    y = gate(0) * omla_ref[...].astype(F32)
    ss = _dot_split(qm * qm, seg_ref[...]) * (1.0 / MEM_DQK)
    qn = qm * lax.rsqrt(ss + EPS) * (gqm_ref[...] * float(MEM_DQK ** -0.5))
    head_of_lane = lax.broadcasted_iota(jnp.int32, qn.shape, 1) // MEM_DQK
    y = y + gate(1) * ogla_ref[...].astype(F32)
    probs = []
    for h in range(MEM_HEADS):
        qh = jnp.where(head_of_lane == h, qn, 0.0).astype(BF16)
        s = lax.dot_general(qh, km, NT_DIMS, preferred_element_type=F32)
        probs.append(jnp.exp(s - jnp.max(s, axis=-1, keepdims=True)))
    g_mem = gate(2)
    parts = []
    for h, p in enumerate(probs):
        o_h = _dot(p.astype(BF16), vm_ref[:, h * MEM_DV:(h + 1) * MEM_DV])
        parts.append(o_h / jnp.sum(p, axis=-1, keepdims=True))
    y = y + g_mem * jnp.concatenate(parts, axis=-1)
    o_ref[...] = x_ref[...] + _dot(y.astype(BF16), wo_ref[...])


def _merge(x2, hb, o_mla, o_gla, km, vm, wqm, gqm, seg, wgate, bgate, wo, w_up, w_down, *, seq, mem_len, tm):
    T, D = x2.shape
    per_batch = seq // tm
    row = lambda w: pl.BlockSpec((tm, w), lambda i: (i, 0))
    mem_blk = lambda w: pl.BlockSpec((mem_len, w), lambda i: (i // per_batch, 0))
    return pl.pallas_call(
        _merge_kernel,
        grid=(T // tm,),
        in_specs=[row(D), row(D), row(D), row(D), mem_blk(km.shape[1]), mem_blk(vm.shape[1])]
        + [_const_spec(w.shape) for w in (wqm, gqm, seg, wgate, bgate, wo)]
        + [_slab_spec(w_up, T // tm), _slab_spec(w_down, T // tm)],
        out_specs=[row(D), _slab_spec(w_up, T // tm), _slab_spec(w_down, T // tm)],
        out_shape=[jax.ShapeDtypeStruct((T, D), F32), jax.ShapeDtypeStruct(w_up.shape, BF16),
                   jax.ShapeDtypeStruct(w_down.shape, BF16)],
        compiler_params=_params("parallel"),
        name="merge",
    )(x2, hb, o_mla, o_gla, km, vm, wqm, gqm, seg, wgate, bgate, wo, w_up, w_down)


def _ffn_kernel(x_ref, g_ref, wup_ref, wdn_ref, o_ref, *, ff_chunk):
    x = x_ref[...]
    hb = _rms(x, g_ref[...]).astype(BF16)
    acc = x
    for c in range(wup_ref.shape[1] // ff_chunk):
        u = jnp.maximum(_dot(hb, wup_ref[:, c * ff_chunk:(c + 1) * ff_chunk]), 0.0)
        acc = acc + _dot((u * u).astype(BF16), wdn_ref[c * ff_chunk:(c + 1) * ff_chunk, :])
    o_ref[...] = acc


def _ffn(x1, g_ffn, wup, wdn, *, tm, ff_chunk):
    T, D = x1.shape
    row = pl.BlockSpec((tm, D), lambda i: (i, 0))
    return pl.pallas_call(
        functools.partial(_ffn_kernel, ff_chunk=ff_chunk),
        grid=(T // tm,),
        in_specs=[row, _const_spec(g_ffn.shape), _const_spec(wup.shape), _const_spec(wdn.shape)],
        out_specs=row,
        out_shape=jax.ShapeDtypeStruct((T, D), F32),
        compiler_params=_params("parallel"),
        name="ffn",
    )(x1, g_ffn, wup, wdn)


def _layer(x, mem, positions, g_mix, w_in, b_gate, g_ckv, w_ukv, g_q_nope, g_k_nope, g_q_rope, g_k_rope,
           w_gla_gate, b_gla_gate, g_gla_out, g_mem, w_mem_kv, g_q_mem, g_k_mem, w_o, g_ffn, w_up, w_down):
    B, S, D = x.shape
    M = mem.shape[1]
    T = B * S
    rank = g_ckv.shape[0]
    row = lambda g: g.reshape(1, -1).astype(F32)

    sizes = (MLA_HEADS * (MLA_NOPE + MLA_ROPE), rank, MLA_ROPE, GLA_HEADS * GLA_DK, GLA_HEADS * GLA_DK,
             GLA_HEADS * GLA_DV, GLA_GATE_RANK, GLA_HEADS * GLA_DV, MEM_HEADS * MEM_DQK, N_BRANCHES * D)
    offs = np.concatenate([[0], np.cumsum(sizes)])
    (w_q, w_ckv, w_kr, w_gq, w_gk, w_gv, w_ga, w_gr, w_qm, w_gate) = [
        w_in[:, offs[i]:offs[i + 1]] for i in range(len(sizes))]

    tq = min(ATTN_TILE, S)
    w_ukv3 = w_ukv.reshape(rank, MLA_HEADS, MLA_NOPE + MLA_V)
    wuk = w_ukv3[:, :, :MLA_NOPE].reshape(rank, -1).astype(BF16)
    wuvt = w_ukv3[:, :, MLA_NOPE:].reshape(rank, -1).T.astype(BF16)
    q_scale = float((MLA_NOPE + MLA_ROPE) ** -0.5 * np.log2(np.e))
    lanes_of = lambda g, scale=1.0: jnp.broadcast_to((g.astype(F32) * scale)[:, None], (g.shape[0], tq))

    inv = 1.0 / (ROPE_THETA ** (jnp.arange(0, MLA_ROPE, 2, dtype=F32) / MLA_ROPE))
    ang = inv[None, :, None] * positions.astype(F32).reshape(T // tq, 1, tq)

    x2 = x.reshape(T, D)
    hb, qt, k, vt, ga = _mla_prep(
        x2, row(g_mix), w_q.T.astype(BF16), jnp.concatenate([w_ckv, w_kr, w_ga], axis=1).T.astype(BF16), wuvt,
        w_ckv.astype(BF16), wuk,
        jnp.cos(ang), jnp.sin(ang), lanes_of(g_q_nope, q_scale), lanes_of(g_q_rope, q_scale), lanes_of(g_ckv),
        lanes_of(g_k_rope), row(g_ckv), row(g_k_nope), tm=min(PREP_TILE, S), sub=tq)
    o_mla = _mla_attn(qt, k, vt, batch=B, seq=S, tq=tq, heads=ATTN_HEADS_PER_STEP)

    nmat = jnp.asarray(np.tile(_gla_decay_matrix(), (1, 2)), BF16)
    masks = jnp.asarray(_gla_level_masks())
    wg = jnp.pad(w_gla_gate, ((0, LANES - GLA_GATE_RANK), (0, 0))).astype(BF16)
    qg, kg, vg, la, sr = _gla_proj(hb, ga, w_gq.astype(BF16), w_gk.astype(BF16), w_gv.astype(BF16),
                                   w_gr.astype(BF16), wg, row(b_gla_gate), tm=min(GLA_PROJ_TILE, S))
    ts = min(GLA_STEP_TOKENS, S)
    o_gla = _gla_rec(qg, kg, vg, la, sr, row(g_gla_out), nmat, masks, batch=B, seq=S, ts=ts,
                     group=ts // GLA_CHUNK)

    seg = jnp.asarray(np.kron(np.eye(MEM_HEADS), np.ones((MEM_DQK, MEM_DQK))).astype(np.float32), BF16)
    km, vm = _mem_kv(mem.reshape(B * M, D), row(g_mem), w_mem_kv.astype(BF16),
                     jnp.tile(row(g_k_mem), (1, MEM_HEADS)), seg, batch=B, mem_len=M)
    x1, w_up_b, w_down_b = _merge(x2, hb, o_mla, o_gla, km, vm, w_qm.astype(BF16),
                                  jnp.tile(row(g_q_mem), (1, MEM_HEADS)), seg, w_gate.astype(BF16), row(b_gate),
                                  w_o.astype(BF16), w_up, w_down, seq=S, mem_len=M,
                                  tm=min(MERGE_TILE, S))
    out = _ffn(x1, row(g_ffn), w_up_b, w_down_b, tm=min(FFN_TILE, S), ff_chunk=FFN_CHUNK)
    return out.reshape(B, S, D)


def kernel(x, mem, positions, g_mix, w_in, b_gate, g_ckv, w_ukv, g_q_nope, g_k_nope, g_q_rope, g_k_rope,
           w_gla_gate, b_gla_gate, g_gla_out, g_mem, w_mem_kv, g_q_mem, g_k_mem, w_o, g_ffn, w_up, w_down):
    for l in range(g_mix.shape[0]):
        x = _layer(x, mem, positions, g_mix[l], w_in[l], b_gate[l], g_ckv[l], w_ukv[l], g_q_nope[l],
                   g_k_nope[l], g_q_rope[l], g_k_rope[l], w_gla_gate[l], b_gla_gate[l], g_gla_out[l],
                   g_mem[l], w_mem_kv[l], g_q_mem[l], g_k_mem[l], w_o[l], g_ffn[l], w_up[l], w_down[l])
    return x
```
